```python
import math
import jax, jax.numpy as jnp
from jax import lax
import numpy as np

D_MODEL = 4096
BATCH = 2
SEQ = 8192
DEPTH = 1

ATTN_HEAD_DIM = 128
ATTN_WIDTH = D_MODEL // 2
ATTN_HEADS = ATTN_WIDTH // ATTN_HEAD_DIM
Q_BLOCK = 128
SSM_GROUP = 16
SSM_WIDTH = D_MODEL // 4
SSM_GROUPS = SSM_WIDTH // SSM_GROUP
SSM_STATE = 64
DT_MIN = 1e-3
DT_MAX = 1e-1
N_BRANCH = 2
OFF_Q = 0
OFF_K = ATTN_WIDTH
OFF_V = 2 * ATTN_WIDTH
OFF_F = 3 * ATTN_WIDTH
OFF_U = OFF_F + ATTN_HEADS
OFF_G = OFF_U + SSM_WIDTH
IN_WIDTH = OFF_G + N_BRANCH * D_MODEL
MEM_LEN = 256
XATTN_HEADS = 4
XATTN_HEAD_DIM = 256
XATTN_WIDTH = XATTN_HEADS * XATTN_HEAD_DIM
D_FF = 4 * D_MODEL
EPS = 1e-6
NEG_INF = -1e30

kernel_name = "hybrid_fox_s5_gated_block"


def rmsnorm(x, g):
    xf = x.astype(jnp.float32)
    r = lax.rsqrt(jnp.mean(xf * xf, axis=-1, keepdims=True) + EPS)
    return (xf * r).astype(x.dtype) * g


def forgetting_attention(q, k, v, log_f):
    bsz, seq, heads, dh = q.shape
    c = jnp.cumsum(log_f, axis=1).transpose(0, 2, 1)
    qh = q.transpose(0, 2, 1, 3)
    kh = k.transpose(0, 2, 1, 3)
    vh = v.transpose(0, 2, 1, 3)
    k_pos = jnp.arange(seq)
    scale = dh ** -0.5

    def block(i):
        start = i * Q_BLOCK
        qb = lax.dynamic_slice_in_dim(qh, start, Q_BLOCK, axis=2)
        cb = lax.dynamic_slice_in_dim(c, start, Q_BLOCK, axis=2)
        s = jnp.einsum('bhqd,bhkd->bhqk', qb, kh).astype(jnp.float32) * scale
        s = s + cb[..., :, None] - c[..., None, :]
        q_pos = start + jnp.arange(Q_BLOCK)
        s = jnp.where(k_pos[None, :] <= q_pos[:, None], s, NEG_INF)
        p = jax.nn.softmax(s, axis=-1).astype(vh.dtype)
        return jnp.einsum('bhqk,bhkd->bhqd', p, vh)

    out = lax.map(block, jnp.arange(seq // Q_BLOCK))
    return out.transpose(1, 0, 3, 2, 4).reshape(bsz, seq, heads * dh)


def _complex_linear_combine(e1, e2):
    a1r, a1i, b1r, b1i = e1
    a2r, a2i, b2r, b2i = e2
    ar = a1r * a2r - a1i * a2i
    ai = a1r * a2i + a1i * a2r
    br = a2r * b1r - a2i * b1i + b2r
    bi = a2r * b1i + a2i * b1r + b2i
    return (ar, ai, br, bi)


def s5_grouped(u, A_re, A_im, log_dt, B_re, B_im, C_re, C_im, D_skip):
    bsz, seq, _ = u.shape
    ug = u.reshape(bsz, seq, SSM_GROUPS, SSM_GROUP).astype(jnp.float32)
    dt = jnp.exp(log_dt.astype(jnp.float32))[:, None]
    a_re = A_re.astype(jnp.float32)
    a_im = A_im.astype(jnp.float32)
    mag = jnp.exp(dt * a_re)
    ang = dt * a_im
    lb_re = mag * jnp.cos(ang)
    lb_im = mag * jnp.sin(ang)
    den = a_re * a_re + a_im * a_im
    nr = lb_re - 1.0
    ni = lb_im
    f_re = (nr * a_re + ni * a_im) / den
    f_im = (ni * a_re - nr * a_im) / den
    br = B_re.astype(jnp.float32)
    bi = B_im.astype(jnp.float32)
    bb_re = f_re[..., None] * br - f_im[..., None] * bi
    bb_im = f_re[..., None] * bi + f_im[..., None] * br
    bu_re = jnp.einsum('bsgh,gph->bsgp', ug, bb_re)
    bu_im = jnp.einsum('bsgh,gph->bsgp', ug, bb_im)
    al_re = jnp.broadcast_to(lb_re, bu_re.shape)
    al_im = jnp.broadcast_to(lb_im, bu_im.shape)
    _, _, xs_re, xs_im = lax.associative_scan(
        _complex_linear_combine, (al_re, al_im, bu_re, bu_im), axis=1)
    y = (jnp.einsum('bsgp,ghp->bsgh', xs_re, C_re.astype(jnp.float32))
         - jnp.einsum('bsgp,ghp->bsgh', xs_im, C_im.astype(jnp.float32))
         + D_skip.astype(jnp.float32) * ug)
    return y.reshape(bsz, seq, SSM_WIDTH).astype(u.dtype)


def memory_cross_attention(h, m, wq, wk, wv, wo):
    bsz, seq, _ = h.shape
    q = (h @ wq).reshape(bsz, seq, XATTN_HEADS, XATTN_HEAD_DIM)
    k = (m @ wk).reshape(bsz, m.shape[1], XATTN_HEADS, XATTN_HEAD_DIM)
    v = (m @ wv).reshape(bsz, m.shape[1], XATTN_HEADS, XATTN_HEAD_DIM)
    s = jnp.einsum('bqhd,bkhd->bhqk', q, k).astype(jnp.float32) * (XATTN_HEAD_DIM ** -0.5)
    p = jax.nn.softmax(s, axis=-1).astype(v.dtype)
    o = jnp.einsum('bhqk,bkhd->bqhd', p, v).reshape(bsz, seq, XATTN_WIDTH)
    return o @ wo


def setup_inputs(seed: int = 0) -> dict:
    key = jax.random.key(seed)
    ks = jax.random.split(key, 32)

    def nrm(k, shape, scale):
        return jax.random.normal(k, shape, jnp.float32) * scale

    def gain(k):
        return 1.0 + 0.02 * jax.random.normal(k, (DEPTH, D_MODEL), jnp.float32)

    L = DEPTH
    n_idx = jnp.arange(SSM_STATE, dtype=jnp.float32)
    return {
        "x": nrm(ks[0], (BATCH, SEQ, D_MODEL), 1.0),
        "mem": nrm(ks[1], (BATCH, MEM_LEN, D_MODEL), 1.0),
        "g_mix": gain(ks[2]),
        "w_in": nrm(ks[3], (L, D_MODEL, IN_WIDTH), D_MODEL ** -0.5),
        "b_f": jax.random.uniform(ks[4], (L, ATTN_HEADS), jnp.float32, 1.0, 6.0),
        "b_gate": nrm(ks[5], (L, N_BRANCH * D_MODEL), 0.02),
        "A_re": -0.5 + nrm(ks[6], (L, SSM_GROUPS, SSM_STATE), 0.01),
        "A_im": math.pi * n_idx + nrm(ks[7], (L, SSM_GROUPS, SSM_STATE), 0.01),
        "log_dt": jax.random.uniform(ks[8], (L, SSM_GROUPS), jnp.float32,
                                     math.log(DT_MIN), math.log(DT_MAX)),
        "B_re": nrm(ks[9], (L, SSM_GROUPS, SSM_STATE, SSM_GROUP), (2 * SSM_GROUP) ** -0.5),
        "B_im": nrm(ks[10], (L, SSM_GROUPS, SSM_STATE, SSM_GROUP), (2 * SSM_GROUP) ** -0.5),
        "C_re": nrm(ks[11], (L, SSM_GROUPS, SSM_GROUP, SSM_STATE), (2 * SSM_STATE) ** -0.5),
        "C_im": nrm(ks[12], (L, SSM_GROUPS, SSM_GROUP, SSM_STATE), (2 * SSM_STATE) ** -0.5),
        "D_skip": nrm(ks[13], (L, SSM_GROUPS, SSM_GROUP), 1.0),
        "w_glu": nrm(ks[14], (L, SSM_WIDTH, SSM_WIDTH), SSM_WIDTH ** -0.5),
        "b_glu": nrm(ks[15], (L, SSM_WIDTH), 0.02),
        "w_attn_up": nrm(ks[16], (L, ATTN_WIDTH, D_MODEL), ATTN_WIDTH ** -0.5),
        "w_ssm_up": nrm(ks[17], (L, SSM_WIDTH, D_MODEL), SSM_WIDTH ** -0.5),
        "w_out": nrm(ks[18], (L, D_MODEL, D_MODEL), D_MODEL ** -0.5),
        "g_xattn": gain(ks[19]),
        "g_mem": gain(ks[20]),
        "wq_x": nrm(ks[21], (L, D_MODEL, XATTN_WIDTH), D_MODEL ** -0.5),
        "wk_x": nrm(ks[22], (L, D_MODEL, XATTN_WIDTH), D_MODEL ** -0.5),
        "wv_x": nrm(ks[23], (L, D_MODEL, XATTN_WIDTH), D_MODEL ** -0.5),
        "wo_x": nrm(ks[24], (L, XATTN_WIDTH, D_MODEL), XATTN_WIDTH ** -0.5),
        "g_mlp": gain(ks[25]),
        "w_ff1": nrm(ks[26], (L, D_MODEL, D_FF), D_MODEL ** -0.5),
        "w_ff2": nrm(ks[27], (L, D_FF, D_MODEL), D_FF ** -0.5),
        "g_final": 1.0 + 0.02 * jax.random.normal(ks[28], (D_MODEL,), jnp.float32),
    }


def reference(x, mem, g_mix, w_in, b_f, b_gate, A_re, A_im, log_dt, B_re, B_im, C_re, C_im,
              D_skip, w_glu, b_glu, w_attn_up, w_ssm_up, w_out, g_xattn, g_mem, wq_x, wk_x,
              wv_x, wo_x, g_mlp, w_ff1, w_ff2, g_final):
    bsz, seq, _ = x.shape
    for l in range(DEPTH):
        h = rmsnorm(x, g_mix[l])
        proj = h @ w_in[l]
        q = proj[..., OFF_Q:OFF_K].reshape(bsz, seq, ATTN_HEADS, ATTN_HEAD_DIM)
        k = proj[..., OFF_K:OFF_V].reshape(bsz, seq, ATTN_HEADS, ATTN_HEAD_DIM)
        v = proj[..., OFF_V:OFF_F].reshape(bsz, seq, ATTN_HEADS, ATTN_HEAD_DIM)
        log_f = jax.nn.log_sigmoid((proj[..., OFF_F:OFF_U] + b_f[l]).astype(jnp.float32))
        u = proj[..., OFF_U:OFF_G]
        gates = jax.nn.sigmoid((proj[..., OFF_G:] + b_gate[l]).astype(jnp.float32)).astype(x.dtype)
        g_attn = gates[..., :D_MODEL]
        g_ssm = gates[..., D_MODEL:]

        attn = forgetting_attention(q, k, v, log_f) @ w_attn_up[l]

        y = s5_grouped(u, A_re[l], A_im[l], log_dt[l], B_re[l], B_im[l],
                       C_re[l], C_im[l], D_skip[l])
        y = jax.nn.gelu(y)
        y = y * jax.nn.sigmoid(y @ w_glu[l] + b_glu[l])
        ssm = y @ w_ssm_up[l]

        x = x + (g_attn * attn + g_ssm * ssm) @ w_out[l]

        x = x + memory_cross_attention(rmsnorm(x, g_xattn[l]), rmsnorm(mem, g_mem[l]),
                                       wq_x[l], wk_x[l], wv_x[l], wo_x[l])

        hm = rmsnorm(x, g_mlp[l])
        x = x + jnp.square(jax.nn.relu(hm @ w_ff1[l])) @ w_ff2[l]
    return rmsnorm(x, g_final)
```

```python
import functools
import math

import jax
import jax.numpy as jnp
from jax import lax
from jax.experimental import pallas as pl
from jax.experimental.pallas import tpu as pltpu

F32 = jnp.float32
BF16 = jnp.bfloat16

V7X_VMEM_LIMIT_BYTES = 56 * 1024 * 1024
LANES = 128
SUBLANES = 8

EPS = 1e-6
NEG_INF = -1e30
ATTN_HEAD_DIM = 128
SSM_GROUP = 16
SSM_STATE = 64
S5_CHUNK = 16
XATTN_HEADS = 4
N_BRANCH = 2


def _cparams(*sem):
    return pltpu.CompilerParams(dimension_semantics=sem,
                                vmem_limit_bytes=V7X_VMEM_LIMIT_BYTES)


def _rmsnorm_body(*refs, n_in):
    g_ref, o_ref = refs[n_in], refs[n_in + 1]
    xs = refs[0][...].astype(F32)
    for r in refs[1:n_in]:
        xs = xs + r[...].astype(F32)
    ms = jnp.mean(xs * xs, axis=-1, keepdims=True)
    o_ref[...] = ((xs * lax.rsqrt(ms + EPS)) * g_ref[...]).astype(o_ref.dtype)


def _rmsnorm(xs, g, out_dtype, tm):
    m, d = xs[0].shape
    tm = min(tm, m)
    row = pl.BlockSpec((tm, d), lambda i: (i, 0))
    return pl.pallas_call(
        functools.partial(_rmsnorm_body, n_in=len(xs)),
        grid=(m // tm,),
        in_specs=[row] * len(xs) + [pl.BlockSpec((1, d), lambda i: (0, 0))],
        out_specs=row,
        out_shape=jax.ShapeDtypeStruct((m, d), out_dtype),
        compiler_params=_cparams("parallel"),
        name="rmsnorm",
    )(*xs, g.reshape(1, d).astype(F32))


def _mm_body(*refs, n_pairs, n_aux, has_res, epilogue):
    accs = [jnp.dot(refs[2 * p][...], refs[2 * p + 1][...], preferred_element_type=F32)
            for p in range(n_pairs)]
    pos = 2 * n_pairs
    aux = [refs[pos + i][...] for i in range(n_aux)]
    pos += n_aux
    res = refs[pos][...].astype(F32) if has_res else None
    o_ref = refs[-1]
    o_ref[...] = epilogue(accs, aux, res).astype(o_ref.dtype)


def _mm_ksplit_body(a_ref, b_ref, *rest, n_aux, has_res, epilogue, nk):
    acc_ref = rest[-1]
    o_ref = rest[-2]
    k = pl.program_id(2)

    @pl.when(k == 0)
    def _():
        acc_ref[...] = jnp.zeros_like(acc_ref)

    acc_ref[...] += jnp.dot(a_ref[...], b_ref[...], preferred_element_type=F32)

    @pl.when(k == nk - 1)
    def _():
        aux = [rest[i][...] for i in range(n_aux)]
        res = rest[n_aux][...].astype(F32) if has_res else None
        o_ref[...] = epilogue([acc_ref[...]], aux, res).astype(o_ref.dtype)


def _fused_matmul(pairs, epilogue, out_dtype, *, n, tm, tn, tk=None, aux=(), res=None,
                  name="fused_matmul"):
    m = pairs[0][0].shape[0]
    tm, tn = min(tm, m), min(tn, n)
    n_aux, has_res = len(aux), res is not None
    out_shape = jax.ShapeDtypeStruct((m, n), out_dtype)
    operands, in_specs = [], []

    if tk is None or tk >= pairs[0][0].shape[1]:
        for lhs, rhs, off in pairs:
            kp = lhs.shape[1]
            operands += [lhs, rhs]
            in_specs += [pl.BlockSpec((tm, kp), lambda i, j: (i, 0)),
                         pl.BlockSpec((kp, tn), lambda i, j, off=off: (0, j + off))]
        for vec, off in aux:
            operands.append(vec)
            in_specs.append(pl.BlockSpec((1, tn), lambda i, j, off=off: (0, j + off)))
        if has_res:
            operands.append(res)
            in_specs.append(pl.BlockSpec((tm, tn), lambda i, j: (i, j)))
        return pl.pallas_call(
            functools.partial(_mm_body, n_pairs=len(pairs), n_aux=n_aux, has_res=has_res,
                              epilogue=epilogue),
            grid=(m // tm, n // tn),
            in_specs=in_specs,
            out_specs=pl.BlockSpec((tm, tn), lambda i, j: (i, j)),
            out_shape=out_shape,
            compiler_params=_cparams("parallel", "parallel"),
            name=name,
        )(*operands)

    (lhs, rhs, off), = pairs
    nk = lhs.shape[1] // tk
    operands = [lhs, rhs]
    in_specs = [pl.BlockSpec((tm, tk), lambda i, j, k: (i, k)),
                pl.BlockSpec((tk, tn), lambda i, j, k, off=off: (k, j + off))]
    for vec, voff in aux:
        operands.append(vec)
        in_specs.append(pl.BlockSpec((1, tn), lambda i, j, k, voff=voff: (0, j + voff)))
    if has_res:
        operands.append(res)
        in_specs.append(pl.BlockSpec((tm, tn), lambda i, j, k: (i, j)))
    return pl.pallas_call(
        functools.partial(_mm_ksplit_body, n_aux=n_aux, has_res=has_res, epilogue=epilogue,
                          nk=nk),
        grid=(m // tm, n // tn, nk),
        in_specs=in_specs,
        out_specs=pl.BlockSpec((tm, tn), lambda i, j, k: (i, j)),
        out_shape=out_shape,
        scratch_shapes=[pltpu.VMEM((tm, tn), F32)],
        compiler_params=_cparams("parallel", "parallel", "arbitrary"),
        name=name,
    )(*operands)


def _ep_plain(accs, aux, res):
    return accs[0]


def _ep_add_res(accs, aux, res):
    return res + accs[0]


def _ep_log_sigmoid(accs, aux, res):
    z = accs[0] + aux[0]
    return jnp.minimum(z, 0.0) - jnp.log1p(jnp.exp(-jnp.abs(z)))


def _ep_glu(accs, aux, res):
    return res * jax.nn.sigmoid(accs[0] + aux[0])


def _ep_gated_merge(accs, aux, res):
    g_attn = jax.nn.sigmoid(accs[0] + aux[0])
    g_ssm = jax.nn.sigmoid(accs[2] + aux[1])
    return g_attn * accs[1] + g_ssm * accs[3]


def _ep_relu2(accs, aux, res):
    r = jnp.maximum(accs[0], 0.0)
    return r * r


def _cumsum_body(x_ref, o_ref, *, blk, nblk):
    r = lax.broadcasted_iota(jnp.int32, (blk, blk), 0)
    c = lax.broadcasted_iota(jnp.int32, (blk, blk), 1)
    tri = (c <= r).astype(F32)

    def step(i, carry):
        base = pl.multiple_of(i * blk, blk)
        cs = jnp.dot(tri, x_ref[pl.ds(base, blk), :], preferred_element_type=F32,
                     precision=lax.Precision.HIGHEST) + carry
        o_ref[pl.ds(base, blk), :] = cs
        return cs[blk - 1:blk, :]

    lax.fori_loop(0, nblk, step, jnp.zeros((1, x_ref.shape[1]), F32))


def _cumsum_time(x, bsz, seq):
    blk = min(256, seq)
    spec = pl.BlockSpec((seq, x.shape[1]), lambda b: (b, 0))
    return pl.pallas_call(
        functools.partial(_cumsum_body, blk=blk, nblk=seq // blk),
        grid=(bsz,),
        in_specs=[spec],
        out_specs=spec,
        out_shape=jax.ShapeDtypeStruct(x.shape, F32),
        compiler_params=_cparams("parallel"),
        name="logf_cumsum",
    )(x)


def _fox_body(q_ref, k_ref, v_ref, c_ref, o_ref, *, tq):
    qi = pl.program_id(2)
    q = q_ref[...]

    def scores(j):
        base = pl.multiple_of(j * tq, tq)
        k = k_ref[pl.ds(base, tq), :]
        s = lax.dot_general(q, k, (((1,), (1,)), ((), ())), preferred_element_type=F32)
        return s - c_ref[0, :, pl.ds(base, tq)], v_ref[pl.ds(base, tq), :]

    def update(carry, s, v):
        m, l, acc = carry
        m_new = jnp.maximum(m, jnp.max(s, axis=-1, keepdims=True))
        p = jnp.exp(s - m_new)
        alpha = jnp.exp(m - m_new)
        l = alpha * l + jnp.sum(p, axis=-1, keepdims=True)
        acc = alpha * acc + jnp.dot(p.astype(v.dtype), v, preferred_element_type=F32)
        return m_new, l, acc

    def below_diagonal(j, carry):
        s, v = scores(j)
        return update(carry, s, v)

    init = (jnp.full((tq, 1), NEG_INF, F32), jnp.zeros((tq, 1), F32),
            jnp.zeros((tq, q.shape[1]), F32))
    carry = lax.fori_loop(0, qi, below_diagonal, init)

    s, v = scores(qi)
    row = lax.broadcasted_iota(jnp.int32, (tq, tq), 0)
    col = lax.broadcasted_iota(jnp.int32, (tq, tq), 1)
    _, l, acc = update(carry, jnp.where(col <= row, s, NEG_INF), v)
    o_ref[...] = (acc / l).astype(o_ref.dtype)


def _fox_attention(qkv, c, bsz, seq, heads, tq):
    dh = ATTN_HEAD_DIM
    tq = min(tq, seq)
    nq = seq // tq
    return pl.pallas_call(
        functools.partial(_fox_body, tq=tq),
        grid=(bsz, heads, nq),
        in_specs=[pl.BlockSpec((tq, dh), lambda b, h, i: (b * nq + i, h)),
                  pl.BlockSpec((seq, dh), lambda b, h, i: (b, heads + h)),
                  pl.BlockSpec((seq, dh), lambda b, h, i: (b, 2 * heads + h)),
                  pl.BlockSpec((1, 1, seq), lambda b, h, i: (b * heads + h, 0, 0))],
        out_specs=pl.BlockSpec((tq, dh), lambda b, h, i: (b * nq + i, h)),
        out_shape=jax.ShapeDtypeStruct((bsz * seq, heads * dh), BF16),
        compiler_params=_cparams("parallel", "parallel", "parallel"),
        name="fox_attention",
    )(qkv, qkv, qkv, c)


def _xattn_body(q_ref, k_ref, v_ref, o_ref):
    s = lax.dot_general(q_ref[...], k_ref[...], (((1,), (1,)), ((), ())),
                        preferred_element_type=F32)
    p = jnp.exp(s - jnp.max(s, axis=-1, keepdims=True))
    l = jnp.sum(p, axis=-1, keepdims=True)
    o = jnp.dot(p.astype(v_ref.dtype), v_ref[...], preferred_element_type=F32)
    o_ref[...] = (o / l).astype(o_ref.dtype)


def _xattn(q, kv, bsz, seq, mem_len, tq):
    width = q.shape[1]
    dh = width // XATTN_HEADS
    tq = min(tq, seq)
    nq = seq // tq
    return pl.pallas_call(
        _xattn_body,
        grid=(bsz, XATTN_HEADS, nq),
        in_specs=[pl.BlockSpec((tq, dh), lambda b, h, i: (b * nq + i, h)),
                  pl.BlockSpec((mem_len, dh), lambda b, h, i: (b, h)),
                  pl.BlockSpec((mem_len, dh), lambda b, h, i: (b, XATTN_HEADS + h))],
        out_specs=pl.BlockSpec((tq, dh), lambda b, h, i: (b * nq + i, h)),
        out_shape=jax.ShapeDtypeStruct(q.shape, BF16),
        compiler_params=_cparams("parallel", "parallel", "parallel"),
        name="mem_xattn",
    )(q, kv, kv)


def _s5_operators(A_re, A_im, log_dt, B_re, B_im, C_re, C_im, D_skip):
    hp = lax.Precision.HIGHEST
    n_g, n_p = A_re.shape
    n_h = B_re.shape[-1]
    L = S5_CHUNK
    a_re, a_im = A_re.astype(F32), A_im.astype(F32)
    dt = jnp.exp(log_dt.astype(F32))[:, None]
    tau = jnp.arange(L + 1, dtype=F32)[:, None, None]
    mag = jnp.exp(tau * (dt * a_re))
    ang = tau * (dt * a_im)
    pw_re, pw_im = mag * jnp.cos(ang), mag * jnp.sin(ang)
    lb_re, lb_im = pw_re[1], pw_im[1]
    den = a_re * a_re + a_im * a_im
    nr, ni = lb_re - 1.0, lb_im
    f_re = (nr * a_re + ni * a_im) / den
    f_im = (ni * a_re - nr * a_im) / den
    br, bi = B_re.astype(F32), B_im.astype(F32)
    bb_re = f_re[..., None] * br - f_im[..., None] * bi
    bb_im = f_re[..., None] * bi + f_im[..., None] * br
    cr, ci = C_re.astype(F32), C_im.astype(F32)
    cp_re = cr[None] * pw_re[:, :, None, :] - ci[None] * pw_im[:, :, None, :]
    cp_im = cr[None] * pw_im[:, :, None, :] + ci[None] * pw_re[:, :, None, :]

    kern = (jnp.einsum('tghp,gpk->tghk', cp_re[:L], bb_re, precision=hp)
            - jnp.einsum('tghp,gpk->tghk', cp_im[:L], bb_im, precision=hp))
    kern = kern.at[0].add(D_skip.astype(F32)[:, :, None] * jnp.eye(n_h, dtype=F32))
    lag = jnp.arange(L)[None, :] - jnp.arange(L)[:, None]
    toep = jnp.where((lag >= 0)[:, :, None, None, None], kern[jnp.clip(lag, 0, L - 1)], 0.0)
    t_op = toep.transpose(2, 0, 4, 1, 3).reshape(n_g, L * n_h, L * n_h)

    rp_re, rp_im = pw_re[L - 1::-1][:L], pw_im[L - 1::-1][:L]
    w_re = rp_re[:, :, :, None] * bb_re[None] - rp_im[:, :, :, None] * bb_im[None]
    w_im = rp_re[:, :, :, None] * bb_im[None] + rp_im[:, :, :, None] * bb_re[None]
    w_re = w_re.transpose(1, 0, 3, 2).reshape(n_g, L * n_h, n_p)
    w_im = w_im.transpose(1, 0, 3, 2).reshape(n_g, L * n_h, n_p)
    w_op = jnp.concatenate([w_re, w_im, w_im, w_re], axis=-1)

    v_re = cp_re[1:].transpose(1, 3, 0, 2).reshape(n_g, n_p, L * n_h)
    v_im = cp_im[1:].transpose(1, 3, 0, 2).reshape(n_g, n_p, L * n_h)
    v_op = jnp.concatenate([v_re, -v_im], axis=1)

    al_re, al_im = pw_re[L], pw_im[L]
    a1 = jnp.concatenate([al_re, al_re], axis=-1).reshape(1, n_g * 2 * n_p)
    a2 = jnp.concatenate([-al_im, al_im], axis=-1).reshape(1, n_g * 2 * n_p)
    return t_op.astype(BF16), w_op.astype(BF16), v_op.astype(BF16), a1, a2


def _s5_increment_body(u_ref, w_ref, z_ref, zs_ref):
    s = jnp.dot(u_ref[...], w_ref[...], preferred_element_type=F32)
    half = z_ref.shape[1]
    z_ref[...] = s[:, :half]
    zs_ref[...] = s[:, half:]


def _s5_scan_body(z_ref, zs_ref, a1_ref, a2_ref, o_ref, st_ref, *, tc):
    @pl.when(pl.program_id(2) == 0)
    def _():
        st_ref[...] = jnp.zeros_like(st_ref)

    a1, a2 = a1_ref[...], a2_ref[...]

    def step(i, carry):
        z, zs = carry
        base = pl.multiple_of(i * SUBLANES, SUBLANES)
        sz = z_ref[pl.ds(base, SUBLANES), :]
        szs = zs_ref[pl.ds(base, SUBLANES), :]
        before = []
        for r in range(SUBLANES):
            before.append(z)
            z, zs = (a1 * z + a2 * zs + sz[r:r + 1, :],
                     a1 * zs - a2 * z + szs[r:r + 1, :])
        o_ref[pl.ds(base, SUBLANES), :] = jnp.concatenate(before, axis=0)
        return z, zs

    z, zs = lax.fori_loop(0, tc // SUBLANES, step, (st_ref[0:1, :], st_ref[1:2, :]))
    st_ref[0:1, :] = z
    st_ref[1:2, :] = zs


def _s5_output_body(u_ref, x_ref, t_ref, v_ref, o_ref):
    y = jnp.dot(u_ref[...], t_ref[...], preferred_element_type=F32)
    y = y + jnp.dot(x_ref[...].astype(BF16), v_ref[...], preferred_element_type=F32)
    o_ref[...] = jax.nn.gelu(y).astype(o_ref.dtype)


def _s5_branch(u, ops, bsz, seq):
    t_op, w_op, v_op, a1, a2 = ops
    n_g = t_op.shape[0]
    L, n_h, n_p = S5_CHUNK, SSM_GROUP, SSM_STATE
    cw, sw = L * n_h, 2 * n_p
    nc = bsz * seq // L
    ncb = seq // L
    uu = u.reshape(nc, L, n_g, n_h).transpose(0, 2, 1, 3).reshape(nc, n_g * cw)

    z, zs = pl.pallas_call(
        _s5_increment_body,
        grid=(n_g,),
        in_specs=[pl.BlockSpec((nc, cw), lambda g: (0, g)),
                  pl.BlockSpec((None, cw, 2 * sw), lambda g: (g, 0, 0))],
        out_specs=[pl.BlockSpec((nc, sw), lambda g: (0, g))] * 2,
        out_shape=[jax.ShapeDtypeStruct((nc, n_g * sw), F32)] * 2,
        compiler_params=_cparams("parallel"),
        name="s5_increment",
    )(uu, w_op)

    tc = min(128, ncb)
    lc = 1024
    nt = ncb // tc
    blk = pl.BlockSpec((tc, lc), lambda b, l, t: (b * nt + t, l))
    vec = pl.BlockSpec((1, lc), lambda b, l, t: (0, l))
    xprev = pl.pallas_call(
        functools.partial(_s5_scan_body, tc=tc),
        grid=(bsz, n_g * sw // lc, nt),
        in_specs=[blk, blk, vec, vec],
        out_specs=blk,
        out_shape=jax.ShapeDtypeStruct((nc, n_g * sw), F32),
        scratch_shapes=[pltpu.VMEM((SUBLANES, lc), F32)],
        compiler_params=_cparams("parallel", "parallel", "arbitrary"),
        name="s5_scan",
    )(z, zs, a1, a2)

    yy = pl.pallas_call(
        _s5_output_body,
        grid=(n_g,),
        in_specs=[pl.BlockSpec((nc, cw), lambda g: (0, g)),
                  pl.BlockSpec((nc, sw), lambda g: (0, g)),
                  pl.BlockSpec((None, cw, cw), lambda g: (g, 0, 0)),
                  pl.BlockSpec((None, sw, cw), lambda g: (g, 0, 0))],
        out_specs=pl.BlockSpec((nc, cw), lambda g: (0, g)),
        out_shape=jax.ShapeDtypeStruct((nc, n_g * cw), BF16),
        compiler_params=_cparams("parallel"),
        name="s5_output",
    )(uu, xprev, t_op, v_op)
    return yy.reshape(nc, n_g, L, n_h).transpose(0, 2, 1, 3).reshape(bsz * seq, n_g * n_h)


def _row(v):
    return v.reshape(1, -1).astype(F32)


def _layer(x, mem_n, p, bsz, seq, mem_len):
    d = x.shape[1]
    aw = p["w_attn_up"].shape[0]
    heads = aw // ATTN_HEAD_DIM
    sw = p["w_ssm_up"].shape[0]
    off_f = 3 * aw
    off_u = off_f + heads
    off_g = off_u + sw

    h = _rmsnorm([x], p["g_mix"], BF16, tm=256)
    w_in = p["w_in"]
    q_scale = jnp.concatenate([jnp.full((aw,), ATTN_HEAD_DIM ** -0.5, F32),
                               jnp.ones((2 * aw,), F32)])
    w_qkv = (w_in[:, :off_f] * q_scale).astype(BF16)
    w_f = jnp.pad(w_in[:, off_f:off_u], ((0, 0), (0, LANES - heads))).astype(BF16)
    w_u = w_in[:, off_u:off_g].astype(BF16)
    w_g = w_in[:, off_g:].astype(BF16)

    qkv = _fused_matmul([(h, w_qkv, 0)], _ep_plain, BF16, n=3 * aw, tm=1024, tn=1024,
                        name="qkv_proj")
    b_f = jnp.pad(p["b_f"].astype(F32), (0, LANES - heads)).reshape(1, LANES)
    log_f = _fused_matmul([(h, w_f, 0)], _ep_log_sigmoid, F32, n=LANES, tm=1024, tn=LANES,
                          aux=[(b_f, 0)], name="forget_proj")
    c = _cumsum_time(log_f, bsz, seq)
    c = c[:, :heads].reshape(bsz, seq, heads).transpose(0, 2, 1).reshape(bsz * heads, 1, seq)
    fox = _fox_attention(qkv, c, bsz, seq, heads, tq=512)

    u = _fused_matmul([(h, w_u, 0)], _ep_plain, BF16, n=sw, tm=1024, tn=1024, name="ssm_in_proj")
    ops = _s5_operators(p["A_re"], p["A_im"], p["log_dt"], p["B_re"], p["B_im"],
                        p["C_re"], p["C_im"], p["D_skip"])
    y = _s5_branch(u, ops, bsz, seq)
    y = _fused_matmul([(y, p["w_glu"].astype(BF16), 0)], _ep_glu, BF16, n=sw, tm=1024, tn=1024,
                      aux=[(_row(p["b_glu"]), 0)], res=y, name="ssm_glu")

    tn_mix = 256
    b_gate = _row(p["b_gate"])
    merged = _fused_matmul(
        [(h, w_g, 0), (fox, p["w_attn_up"].astype(BF16), 0),
         (h, w_g, d // tn_mix), (y, p["w_ssm_up"].astype(BF16), 0)],
        _ep_gated_merge, BF16, n=d, tm=512, tn=tn_mix,
        aux=[(b_gate, 0), (b_gate, d // tn_mix)], name="gated_merge")
    x = _fused_matmul([(merged, p["w_out"].astype(BF16), 0)], _ep_add_res, F32, n=d,
                      tm=1024, tn=512, res=x, name="mixer_out_proj")

    xw = p["wq_x"].shape[1]
    hx = _rmsnorm([x], p["g_xattn"], BF16, tm=256)
    wq = (p["wq_x"] * (xw // XATTN_HEADS) ** -0.5).astype(BF16)
    qx = _fused_matmul([(hx, wq, 0)], _ep_plain, BF16, n=xw, tm=1024, tn=1024, name="xattn_q_proj")
    w_kv = jnp.concatenate([p["wk_x"], p["wv_x"]], axis=1).astype(BF16)
    kv = _fused_matmul([(mem_n, w_kv, 0)], _ep_plain, BF16, n=2 * xw, tm=512, tn=1024,
                       name="xattn_kv_proj")
    ox = _xattn(qx, kv, bsz, seq, mem_len, tq=1024)
    x = _fused_matmul([(ox, p["wo_x"].astype(BF16), 0)], _ep_add_res, F32, n=d,
                      tm=1024, tn=512, res=x, name="xattn_out_proj")

    hm = _rmsnorm([x], p["g_mlp"], BF16, tm=256)
    dff = p["w_ff1"].shape[1]
    hid = _fused_matmul([(hm, p["w_ff1"].astype(BF16), 0)], _ep_relu2, BF16, n=dff,
                        tm=1024, tn=1024, name="mlp_up")
    x = _fused_matmul([(hid, p["w_ff2"].astype(BF16), 0)], _ep_add_res, F32, n=d,
                      tm=1024, tn=512, tk=4096, res=x, name="mlp_down")
    return x


_LAYER_PARAMS = ("g_mix", "w_in", "b_f", "b_gate", "A_re", "A_im", "log_dt", "B_re", "B_im",
                 "C_re", "C_im", "D_skip", "w_glu", "b_glu", "w_attn_up", "w_ssm_up", "w_out",
                 "g_xattn", "g_mem", "wq_x", "wk_x", "wv_x", "wo_x", "g_mlp", "w_ff1", "w_ff2")


def kernel(x, mem, g_mix, w_in, b_f, b_gate, A_re, A_im, log_dt, B_re, B_im, C_re, C_im, D_skip, w_glu, b_glu, w_attn_up, w_ssm_up, w_out, g_xattn, g_mem, wq_x, wk_x, wv_x, wo_x, g_mlp, w_ff1, w_ff2, g_final):
    stacked = dict(zip(_LAYER_PARAMS, (g_mix, w_in, b_f, b_gate, A_re, A_im, log_dt, B_re, B_im,
                                       C_re, C_im, D_skip, w_glu, b_glu, w_attn_up, w_ssm_up,
                                       w_out, g_xattn, g_mem, wq_x, wk_x, wv_x, wo_x, g_mlp,
                                       w_ff1, w_ff2)))
    bsz, seq, d = x.shape
    mem_len = mem.shape[1]
    xt = x.reshape(bsz * seq, d)
    mem2 = mem.reshape(bsz * mem_len, d)
    for l in range(g_mix.shape[0]):
        p = {k: v[l] for k, v in stacked.items()}
        mem_n = _rmsnorm([mem2], p["g_mem"], BF16, tm=256)
        xt = _layer(xt, mem_n, p, bsz, seq, mem_len)
    out = _rmsnorm([xt], g_final, x.dtype, tm=256)
    return out.reshape(bsz, seq, d)
```

```python
import functools
import math

import jax
import jax.numpy as jnp
from jax import lax
from jax.experimental import pallas as pl
from jax.experimental.pallas import tpu as pltpu

F32 = jnp.float32
BF16 = jnp.bfloat16

V7X_VMEM_LIMIT_BYTES = 56 * 1024 * 1024
LANES = 128
SUBLANES = 8

EPS = 1e-6
NEG_INF = -1e30
LOG2E = math.log2(math.e)
N_BIAS_PIECES = 3
ATTN_HEAD_DIM = 128
SSM_GROUP = 16
SSM_STATE = 64
S5_CHUNK = 16
XATTN_HEADS = 4
N_BRANCH = 2


def _cparams(*sem):
    return pltpu.CompilerParams(dimension_semantics=sem,
                                vmem_limit_bytes=V7X_VMEM_LIMIT_BYTES)


def _rmsnorm_body(*refs, n_in):
    g_ref, o_ref = refs[n_in], refs[n_in + 1]
    xs = refs[0][...].astype(F32)
    for r in refs[1:n_in]:
        xs = xs + r[...].astype(F32)
    ms = jnp.mean(xs * xs, axis=-1, keepdims=True)
    o_ref[...] = ((xs * lax.rsqrt(ms + EPS)) * g_ref[...]).astype(o_ref.dtype)


def _rmsnorm(xs, g, out_dtype, tm):
    m, d = xs[0].shape
    tm = min(tm, m)
    row = pl.BlockSpec((tm, d), lambda i: (i, 0))
    return pl.pallas_call(
        functools.partial(_rmsnorm_body, n_in=len(xs)),
        grid=(m // tm,),
        in_specs=[row] * len(xs) + [pl.BlockSpec((1, d), lambda i: (0, 0))],
        out_specs=row,
        out_shape=jax.ShapeDtypeStruct((m, d), out_dtype),
        compiler_params=_cparams("parallel"),
        name="rmsnorm",
    )(*xs, g.reshape(1, d).astype(F32))


def _mm_body(*refs, n_pairs, n_aux, has_res, epilogue):
    accs = [jnp.dot(refs[2 * p][...], refs[2 * p + 1][...], preferred_element_type=F32)
            for p in range(n_pairs)]
    pos = 2 * n_pairs
    aux = [refs[pos + i][...] for i in range(n_aux)]
    pos += n_aux
    res = refs[pos][...].astype(F32) if has_res else None
    o_ref = refs[-1]
    o_ref[...] = epilogue(accs, aux, res).astype(o_ref.dtype)


def _mm_ksplit_body(a_ref, b_ref, *rest, n_aux, has_res, epilogue, nk):
    acc_ref = rest[-1]
    o_ref = rest[-2]
    k = pl.program_id(2)

    @pl.when(k == 0)
    def _():
        acc_ref[...] = jnp.zeros_like(acc_ref)

    acc_ref[...] += jnp.dot(a_ref[...], b_ref[...], preferred_element_type=F32)

    @pl.when(k == nk - 1)
    def _():
        aux = [rest[i][...] for i in range(n_aux)]
        res = rest[n_aux][...].astype(F32) if has_res else None
        o_ref[...] = epilogue([acc_ref[...]], aux, res).astype(o_ref.dtype)


def _fused_matmul(pairs, epilogue, out_dtype, *, n, tm, tn, tk=None, aux=(), res=None,
                  name="fused_matmul"):
    m = pairs[0][0].shape[0]
    tm, tn = min(tm, m), min(tn, n)
    n_aux, has_res = len(aux), res is not None
    out_shape = jax.ShapeDtypeStruct((m, n), out_dtype)
    operands, in_specs = [], []

    if tk is None or tk >= pairs[0][0].shape[1]:
        for lhs, rhs, off in pairs:
            kp = lhs.shape[1]
            operands += [lhs, rhs]
            in_specs += [pl.BlockSpec((tm, kp), lambda i, j: (i, 0)),
                         pl.BlockSpec((kp, tn), lambda i, j, off=off: (0, j + off))]
        for vec, off in aux:
            operands.append(vec)
            in_specs.append(pl.BlockSpec((1, tn), lambda i, j, off=off: (0, j + off)))
        if has_res:
            operands.append(res)
            in_specs.append(pl.BlockSpec((tm, tn), lambda i, j: (i, j)))
        return pl.pallas_call(
            functools.partial(_mm_body, n_pairs=len(pairs), n_aux=n_aux, has_res=has_res,
                              epilogue=epilogue),
            grid=(m // tm, n // tn),
            in_specs=in_specs,
            out_specs=pl.BlockSpec((tm, tn), lambda i, j: (i, j)),
            out_shape=out_shape,
            compiler_params=_cparams("parallel", "parallel"),
            name=name,
        )(*operands)

    (lhs, rhs, off), = pairs
    nk = lhs.shape[1] // tk
    operands = [lhs, rhs]
    in_specs = [pl.BlockSpec((tm, tk), lambda i, j, k: (i, k)),
                pl.BlockSpec((tk, tn), lambda i, j, k, off=off: (k, j + off))]
    for vec, voff in aux:
        operands.append(vec)
        in_specs.append(pl.BlockSpec((1, tn), lambda i, j, k, voff=voff: (0, j + voff)))
    if has_res:
        operands.append(res)
        in_specs.append(pl.BlockSpec((tm, tn), lambda i, j, k: (i, j)))
    return pl.pallas_call(
        functools.partial(_mm_ksplit_body, n_aux=n_aux, has_res=has_res, epilogue=epilogue,
                          nk=nk),
        grid=(m // tm, n // tn, nk),
        in_specs=in_specs,
        out_specs=pl.BlockSpec((tm, tn), lambda i, j, k: (i, j)),
        out_shape=out_shape,
        scratch_shapes=[pltpu.VMEM((tm, tn), F32)],
        compiler_params=_cparams("parallel", "parallel", "arbitrary"),
        name=name,
    )(*operands)


def _ep_plain(accs, aux, res):
    return accs[0]


def _ep_add_res(accs, aux, res):
    return res + accs[0]


def _ep_log_sigmoid(accs, aux, res):
    z = accs[0] + aux[0]
    return jnp.minimum(z, 0.0) - jnp.log1p(jnp.exp(-jnp.abs(z)))


def _ep_gated_merge(accs, aux, res):
    g_attn = jax.nn.sigmoid(accs[0] + aux[0])
    g_ssm = jax.nn.sigmoid(accs[2] + aux[1])
    return g_attn * accs[1] + g_ssm * accs[3]


def _ep_relu2(accs, aux, res):
    r = jnp.maximum(accs[0], 0.0)
    return r * r


def _forget_bias_body(x_ref, o_ref, carry_ref, *, blk, nblk, heads):
    @pl.when(pl.program_id(1) == 0)
    def _():
        carry_ref[...] = jnp.zeros_like(carry_ref)

    r = lax.broadcasted_iota(jnp.int32, (blk, blk), 0)
    c = lax.broadcasted_iota(jnp.int32, (blk, blk), 1)
    tri = (c <= r).astype(F32)
    lane = lax.broadcasted_iota(jnp.int32, (blk, LANES), 1)

    def step(i, carry):
        base = pl.multiple_of(i * blk, blk)
        cs = jnp.dot(tri, x_ref[pl.ds(base, blk), :], preferred_element_type=F32,
                     precision=lax.Precision.HIGHEST) + carry
        for h in range(heads):
            bias = jnp.broadcast_to(cs[:, h:h + 1] * (-LOG2E), (blk, LANES))
            hi = bias.astype(BF16).astype(F32)
            mid = (bias - hi).astype(BF16).astype(F32)
            lo = bias - hi - mid
            pieces = jnp.where(lane == 0, hi, jnp.where(lane == 1, mid,
                                                        jnp.where(lane == 2, lo, 0.0)))
            o_ref[pl.ds(base, blk), h * LANES:(h + 1) * LANES] = pieces.astype(o_ref.dtype)
        return cs[blk - 1:blk, :]

    carry_ref[0:1, :] = lax.fori_loop(0, nblk, step, carry_ref[0:1, :])


def _forget_bias(log_f, bsz, seq, heads):
    blk = min(256, seq)
    tt = min(1024, seq)
    nt = seq // tt
    return pl.pallas_call(
        functools.partial(_forget_bias_body, blk=blk, nblk=tt // blk, heads=heads),
        grid=(bsz, nt),
        in_specs=[pl.BlockSpec((tt, LANES), lambda b, t: (b * nt + t, 0))],
        out_specs=pl.BlockSpec((tt, heads * LANES), lambda b, t: (b * nt + t, 0)),
        out_shape=jax.ShapeDtypeStruct((bsz * seq, heads * LANES), BF16),
        scratch_shapes=[pltpu.VMEM((SUBLANES, LANES), F32)],
        compiler_params=_cparams("parallel", "arbitrary"),
        name="forget_bias",
    )(log_f)


def _fox_body(q_ref, k_ref, v_ref, cp_ref, o_ref, kaug_ref, vaug_ref, s_ref, m_ref, acc_ref,
              *, tq, seq):
    qi = pl.program_id(2)
    dh = q_ref.shape[1]

    @pl.when(qi == 0)
    def _():
        def fill(j, _):
            rows = pl.ds(pl.multiple_of(j * tq, tq), tq)
            kaug_ref[rows, :dh] = k_ref[rows, :]
            kaug_ref[rows, dh:] = cp_ref[rows, :]
            vaug_ref[rows, :dh] = v_ref[rows, :]
            vaug_ref[rows, dh:] = jnp.ones((tq, dh), vaug_ref.dtype)
            return 0
        lax.fori_loop(0, seq // tq, fill, 0)

    lane = lax.broadcasted_iota(jnp.int32, (tq, dh), 1)
    q = jnp.concatenate([q_ref[...], (lane < N_BIAS_PIECES).astype(q_ref.dtype)], axis=1)

    def scores(j):
        rows = pl.ds(pl.multiple_of(j * tq, tq), tq)
        return lax.dot_general(q, kaug_ref[rows, :], (((1,), (1,)), ((), ())),
                               preferred_element_type=F32)

    def accumulate(s, j, masked):
        rows = pl.ds(pl.multiple_of(j * tq, tq), tq)
        if masked:
            row = lax.broadcasted_iota(jnp.int32, (tq, tq), 0)
            col = lax.broadcasted_iota(jnp.int32, (tq, tq), 1)
            s = jnp.where(col <= row, s, NEG_INF)
        m = m_ref[...]
        m_new = jnp.maximum(m, jnp.max(s, axis=-1, keepdims=True))
        m_ref[...] = m_new
        p = jnp.exp2(s - m_new)
        acc_ref[...] = jnp.exp2(m - m_new) * acc_ref[...] + jnp.dot(
            p.astype(vaug_ref.dtype), vaug_ref[rows, :], preferred_element_type=F32)

    def step(j, _):
        s = s_ref[...]
        s_ref[...] = scores(j)
        accumulate(s, j - 1, masked=False)
        return 0

    s_ref[...] = scores(0)
    m_ref[...] = jnp.full(m_ref.shape, NEG_INF, F32)
    acc_ref[...] = jnp.zeros(acc_ref.shape, F32)
    lax.fori_loop(1, qi + 1, step, 0)
    accumulate(s_ref[...], qi, masked=True)
    acc = acc_ref[...]
    o_ref[...] = (acc[:, :dh] / acc[:, dh:]).astype(o_ref.dtype)


def _fox_attention(qkv, cp, bsz, seq, heads, tq):
    dh = ATTN_HEAD_DIM
    tq = min(tq, seq)
    nq = seq // tq
    return pl.pallas_call(
        functools.partial(_fox_body, tq=tq, seq=seq),
        grid=(bsz, heads, nq),
        in_specs=[pl.BlockSpec((tq, dh), lambda b, h, i: (b * nq + i, h)),
                  pl.BlockSpec((seq, dh), lambda b, h, i: (b, heads + h)),
                  pl.BlockSpec((seq, dh), lambda b, h, i: (b, 2 * heads + h)),
                  pl.BlockSpec((seq, dh), lambda b, h, i: (b, h))],
        out_specs=pl.BlockSpec((tq, dh), lambda b, h, i: (b * nq + i, h)),
        out_shape=jax.ShapeDtypeStruct((bsz * seq, heads * dh), BF16),
        scratch_shapes=[pltpu.VMEM((seq, 2 * dh), BF16), pltpu.VMEM((seq, 2 * dh), BF16),
                        pltpu.VMEM((tq, tq), F32), pltpu.VMEM((tq, 1), F32),
                        pltpu.VMEM((tq, 2 * dh), F32)],
        compiler_params=_cparams("parallel", "parallel", "arbitrary"),
        name="fox_attention",
    )(qkv, qkv, qkv, cp)


def _xattn_body(q_ref, k_ref, v_ref, o_ref):
    s = lax.dot_general(q_ref[...], k_ref[...], (((1,), (1,)), ((), ())),
                        preferred_element_type=F32)
    p = jnp.exp(s - jnp.max(s, axis=-1, keepdims=True))
    l = jnp.sum(p, axis=-1, keepdims=True)
    o = jnp.dot(p.astype(v_ref.dtype), v_ref[...], preferred_element_type=F32)
    o_ref[...] = (o / l).astype(o_ref.dtype)


def _xattn(q, kv, bsz, seq, mem_len, tq):
    width = q.shape[1]
    dh = width // XATTN_HEADS
    tq = min(tq, seq)
    nq = seq // tq
    return pl.pallas_call(
        _xattn_body,
        grid=(bsz, XATTN_HEADS, nq),
        in_specs=[pl.BlockSpec((tq, dh), lambda b, h, i: (b * nq + i, h)),
                  pl.BlockSpec((mem_len, dh), lambda b, h, i: (b, h)),
                  pl.BlockSpec((mem_len, dh), lambda b, h, i: (b, XATTN_HEADS + h))],
        out_specs=pl.BlockSpec((tq, dh), lambda b, h, i: (b * nq + i, h)),
        out_shape=jax.ShapeDtypeStruct(q.shape, BF16),
        compiler_params=_cparams("parallel", "parallel", "parallel"),
        name="mem_xattn",
    )(q, kv, kv)


def _s5_operators(A_re, A_im, log_dt, B_re, B_im, C_re, C_im, D_skip):
    hp = lax.Precision.HIGHEST
    n_g, n_p = A_re.shape
    n_h = B_re.shape[-1]
    L = S5_CHUNK
    a_re, a_im = A_re.astype(F32), A_im.astype(F32)
    dt = jnp.exp(log_dt.astype(F32))[:, None]
    tau = jnp.arange(L + 1, dtype=F32)[:, None, None]
    mag = jnp.exp(tau * (dt * a_re))
    ang = tau * (dt * a_im)
    pw_re, pw_im = mag * jnp.cos(ang), mag * jnp.sin(ang)
    lb_re, lb_im = pw_re[1], pw_im[1]
    den = a_re * a_re + a_im * a_im
    nr, ni = lb_re - 1.0, lb_im
    f_re = (nr * a_re + ni * a_im) / den
    f_im = (ni * a_re - nr * a_im) / den
    br, bi = B_re.astype(F32), B_im.astype(F32)
    bb_re = f_re[..., None] * br - f_im[..., None] * bi
    bb_im = f_re[..., None] * bi + f_im[..., None] * br
    cr, ci = C_re.astype(F32), C_im.astype(F32)
    cp_re = cr[None] * pw_re[:, :, None, :] - ci[None] * pw_im[:, :, None, :]
    cp_im = cr[None] * pw_im[:, :, None, :] + ci[None] * pw_re[:, :, None, :]

    kern = (jnp.einsum('tghp,gpk->tghk', cp_re[:L], bb_re, precision=hp)
            - jnp.einsum('tghp,gpk->tghk', cp_im[:L], bb_im, precision=hp))
    kern = kern.at[0].add(D_skip.astype(F32)[:, :, None] * jnp.eye(n_h, dtype=F32))
    lag = jnp.arange(L)[None, :] - jnp.arange(L)[:, None]
    toep = jnp.where((lag >= 0)[:, :, None, None, None], kern[jnp.clip(lag, 0, L - 1)], 0.0)
    gs = LANES // n_h
    n_slab = n_g // gs
    eye = jnp.eye(gs, dtype=F32)

    kc = toep.transpose(2, 0, 4, 1, 3).reshape(n_slab, gs, L, n_h, L, n_h)
    t_op = (kc.transpose(0, 2, 1, 3, 4, 5)[:, :, :, :, :, None, :]
            * eye[None, None, :, None, None, :, None]).reshape(n_slab, L * LANES, L * LANES)

    rtau = (L - 1.0) - tau[:L]
    rmag, rang = jnp.exp(rtau * (dt * a_re)), rtau * (dt * a_im)
    rp_re, rp_im = rmag * jnp.cos(rang), rmag * jnp.sin(rang)
    w_re = rp_re[:, :, :, None] * bb_re[None] - rp_im[:, :, :, None] * bb_im[None]
    w_im = rp_re[:, :, :, None] * bb_im[None] + rp_im[:, :, :, None] * bb_re[None]
    w_c = jnp.stack([w_re, w_im], axis=0)
    w_c = w_c.reshape(2, L, n_slab, gs, n_p, n_h).transpose(2, 1, 3, 5, 0, 4)
    w_op = (w_c[:, :, :, :, :, None, :]
            * eye[None, None, :, None, None, :, None]).reshape(n_slab, L * LANES, 2 * gs * n_p)

    v_c = jnp.stack([cp_re[1:], -cp_im[1:]], axis=0)
    v_c = v_c.reshape(2, L, n_slab, gs, n_h, n_p).transpose(2, 0, 3, 5, 1, 4)
    v_op = (v_c[:, :, :, :, :, None, :]
            * eye[None, None, :, None, None, :, None]).reshape(n_slab, 2 * gs * n_p, L * LANES)

    al = jnp.stack([pw_re[L], pw_im[L]], axis=0)
    a_op = al.reshape(2, n_slab, gs * n_p).transpose(1, 0, 2).reshape(1, n_slab * 2 * gs * n_p)
    return t_op.astype(BF16), w_op.astype(BF16), v_op.astype(BF16), a_op


def _chunk_rows(piece_refs):
    return jnp.concatenate([r[...] for r in piece_refs], axis=1)


def _s5_increment_body(*refs):
    w_ref, z_ref = refs[-2], refs[-1]
    z_ref[...] = jnp.dot(_chunk_rows(refs[:-2]), w_ref[...], preferred_element_type=F32)


def _s5_scan_body(z_ref, a_ref, o_ref, st_ref, *, tc):
    @pl.when(pl.program_id(2) == 0)
    def _():
        st_ref[...] = jnp.zeros_like(st_ref)

    half = z_ref.shape[1] // 2
    a_re, a_im = a_ref[:, :half], a_ref[:, half:]

    def step(i, carry):
        re, im = carry
        base = pl.multiple_of(i * SUBLANES, SUBLANES)
        inc = z_ref[pl.ds(base, SUBLANES), :]
        before_re, before_im = [], []
        for r in range(SUBLANES):
            before_re.append(re)
            before_im.append(im)
            re, im = (a_re * re - a_im * im + inc[r:r + 1, :half],
                      a_re * im + a_im * re + inc[r:r + 1, half:])
        o_ref[pl.ds(base, SUBLANES), :half] = jnp.concatenate(before_re, axis=0)
        o_ref[pl.ds(base, SUBLANES), half:] = jnp.concatenate(before_im, axis=0)
        return re, im

    re, im = lax.fori_loop(0, tc // SUBLANES, step, (st_ref[0:1, :half], st_ref[0:1, half:]))
    st_ref[0:1, :half] = re
    st_ref[0:1, half:] = im


def _s5_output_body(*refs):
    x_ref, t_ref, v_ref, o_ref = refs[-4:]
    y = jnp.dot(_chunk_rows(refs[:-4]), t_ref[...], preferred_element_type=F32)
    y = y + jnp.dot(x_ref[...].astype(BF16), v_ref[...], preferred_element_type=F32)
    o_ref[...] = jax.nn.gelu(y).astype(o_ref.dtype)


def _s5_glu_body(*refs):
    w_ref, b_ref, o_ref = refs[-3:]
    y = _chunk_rows(refs[:-3])
    gate = jax.nn.sigmoid(jnp.dot(y, w_ref[...], preferred_element_type=F32) + b_ref[...])
    o_ref[...] = (y.astype(F32) * gate).astype(o_ref.dtype)


def _s5_branch(u, ops, w_glu, b_glu, bsz, seq):
    t_op, w_op, v_op, a_op = ops
    n_slab = t_op.shape[0]
    L = S5_CHUNK
    width = u.shape[1]
    cw, sw = t_op.shape[1], w_op.shape[2]
    nc = bsz * seq // L
    ncb = seq // L
    uu = u.reshape(nc, L * width)
    tm = min(512, nc)
    pieces = [pl.BlockSpec((tm, LANES), lambda j, i, s=s: (i, s * n_slab + j)) for s in range(L)]

    z = pl.pallas_call(
        _s5_increment_body,
        grid=(n_slab, nc // tm),
        in_specs=pieces + [pl.BlockSpec((None, cw, sw), lambda j, i: (j, 0, 0))],
        out_specs=pl.BlockSpec((tm, sw), lambda j, i: (i, j)),
        out_shape=jax.ShapeDtypeStruct((nc, n_slab * sw), F32),
        compiler_params=_cparams("parallel", "parallel"),
        name="s5_increment",
    )(*([uu] * L), w_op)

    tc = min(128, ncb)
    nt = ncb // tc
    blk = pl.BlockSpec((tc, sw), lambda b, j, t: (b * nt + t, j))
    xprev = pl.pallas_call(
        functools.partial(_s5_scan_body, tc=tc),
        grid=(bsz, n_slab, nt),
        in_specs=[blk, pl.BlockSpec((1, sw), lambda b, j, t: (0, j))],
        out_specs=blk,
        out_shape=jax.ShapeDtypeStruct((nc, n_slab * sw), F32),
        scratch_shapes=[pltpu.VMEM((SUBLANES, sw), F32)],
        compiler_params=_cparams("parallel", "parallel", "arbitrary"),
        name="s5_scan",
    )(z, a_op)

    ys = pl.pallas_call(
        _s5_output_body,
        grid=(n_slab, nc // tm),
        in_specs=pieces + [pl.BlockSpec((tm, sw), lambda j, i: (i, j)),
                           pl.BlockSpec((None, cw, cw), lambda j, i: (j, 0, 0)),
                           pl.BlockSpec((None, sw, cw), lambda j, i: (j, 0, 0))],
        out_specs=pl.BlockSpec((tm, cw), lambda j, i: (i, j)),
        out_shape=jax.ShapeDtypeStruct((nc, n_slab * cw), BF16),
        compiler_params=_cparams("parallel", "parallel"),
        name="s5_output",
    )(*([uu] * L), xprev, t_op, v_op)

    tok = [pl.BlockSpec((tm, LANES), lambda i, t, j=j: (i, j * L + t)) for j in range(n_slab)]
    out = pl.pallas_call(
        _s5_glu_body,
        grid=(nc // tm, L),
        in_specs=tok + [pl.BlockSpec((width, width), lambda i, t: (0, 0)),
                        pl.BlockSpec((1, width), lambda i, t: (0, 0))],
        out_specs=pl.BlockSpec((tm, width), lambda i, t: (i, t)),
        out_shape=jax.ShapeDtypeStruct((nc, L * width), BF16),
        compiler_params=_cparams("parallel", "parallel"),
        name="s5_glu",
    )(*([ys] * n_slab), w_glu.astype(BF16), b_glu.reshape(1, width).astype(F32))
    return out.reshape(bsz * seq, width)


def _row(v):
    return v.reshape(1, -1).astype(F32)


def _layer(x, mem_n, p, bsz, seq, mem_len):
    d = x.shape[1]
    aw = p["w_attn_up"].shape[0]
    heads = aw // ATTN_HEAD_DIM
    sw = p["w_ssm_up"].shape[0]
    off_f = 3 * aw
    off_u = off_f + heads
    off_g = off_u + sw

    h = _rmsnorm([x], p["g_mix"], BF16, tm=256)
    w_in = p["w_in"]
    q_scale = jnp.concatenate([jnp.full((aw,), LOG2E * ATTN_HEAD_DIM ** -0.5, F32),
                               jnp.ones((2 * aw,), F32)])
    w_qkv = (w_in[:, :off_f] * q_scale).astype(BF16)
    w_f = jnp.pad(w_in[:, off_f:off_u], ((0, 0), (0, LANES - heads))).astype(BF16)
    w_u = w_in[:, off_u:off_g].astype(BF16)
    w_g = w_in[:, off_g:].astype(BF16)

    qkv = _fused_matmul([(h, w_qkv, 0)], _ep_plain, BF16, n=3 * aw, tm=1024, tn=1024,
                        name="qkv_proj")
    b_f = jnp.pad(p["b_f"].astype(F32), (0, LANES - heads)).reshape(1, LANES)
    log_f = _fused_matmul([(h, w_f, 0)], _ep_log_sigmoid, F32, n=LANES, tm=1024, tn=LANES,
                          aux=[(b_f, 0)], name="forget_proj")
    fox = _fox_attention(qkv, _forget_bias(log_f, bsz, seq, heads), bsz, seq, heads, tq=1024)

    u = _fused_matmul([(h, w_u, 0)], _ep_plain, BF16, n=sw, tm=1024, tn=1024, name="ssm_in_proj")
    ops = _s5_operators(p["A_re"], p["A_im"], p["log_dt"], p["B_re"], p["B_im"],
                        p["C_re"], p["C_im"], p["D_skip"])
    y = _s5_branch(u, ops, p["w_glu"], p["b_glu"], bsz, seq)

    tn_mix = 256
    b_gate = _row(p["b_gate"])
    merged = _fused_matmul(
        [(h, w_g, 0), (fox, p["w_attn_up"].astype(BF16), 0),
         (h, w_g, d // tn_mix), (y, p["w_ssm_up"].astype(BF16), 0)],
        _ep_gated_merge, BF16, n=d, tm=512, tn=tn_mix,
        aux=[(b_gate, 0), (b_gate, d // tn_mix)], name="gated_merge")
    x = _fused_matmul([(merged, p["w_out"].astype(BF16), 0)], _ep_add_res, F32, n=d,
                      tm=1024, tn=512, res=x, name="mixer_out_proj")

    xw = p["wq_x"].shape[1]
    hx = _rmsnorm([x], p["g_xattn"], BF16, tm=256)
    wq = (p["wq_x"] * (xw // XATTN_HEADS) ** -0.5).astype(BF16)
    qx = _fused_matmul([(hx, wq, 0)], _ep_plain, BF16, n=xw, tm=1024, tn=1024, name="xattn_q_proj")
    w_kv = jnp.concatenate([p["wk_x"], p["wv_x"]], axis=1).astype(BF16)
    kv = _fused_matmul([(mem_n, w_kv, 0)], _ep_plain, BF16, n=2 * xw, tm=512, tn=1024,
                       name="xattn_kv_proj")
    ox = _xattn(qx, kv, bsz, seq, mem_len, tq=1024)
    x = _fused_matmul([(ox, p["wo_x"].astype(BF16), 0)], _ep_add_res, F32, n=d,
                      tm=1024, tn=512, res=x, name="xattn_out_proj")

    hm = _rmsnorm([x], p["g_mlp"], BF16, tm=256)
    dff = p["w_ff1"].shape[1]
    hid = _fused_matmul([(hm, p["w_ff1"].astype(BF16), 0)], _ep_relu2, BF16, n=dff,
                        tm=1024, tn=1024, name="mlp_up")
    x = _fused_matmul([(hid, p["w_ff2"].astype(BF16), 0)], _ep_add_res, F32, n=d,
                      tm=1024, tn=512, tk=4096, res=x, name="mlp_down")
    return x


_LAYER_PARAMS = ("g_mix", "w_in", "b_f", "b_gate", "A_re", "A_im", "log_dt", "B_re", "B_im",
                 "C_re", "C_im", "D_skip", "w_glu", "b_glu", "w_attn_up", "w_ssm_up", "w_out",
                 "g_xattn", "g_mem", "wq_x", "wk_x", "wv_x", "wo_x", "g_mlp", "w_ff1", "w_ff2")


def kernel(x, mem, g_mix, w_in, b_f, b_gate, A_re, A_im, log_dt, B_re, B_im, C_re, C_im, D_skip, w_glu, b_glu, w_attn_up, w_ssm_up, w_out, g_xattn, g_mem, wq_x, wk_x, wv_x, wo_x, g_mlp, w_ff1, w_ff2, g_final):
    stacked = dict(zip(_LAYER_PARAMS, (g_mix, w_in, b_f, b_gate, A_re, A_im, log_dt, B_re, B_im,
                                       C_re, C_im, D_skip, w_glu, b_glu, w_attn_up, w_ssm_up,
                                       w_out, g_xattn, g_mem, wq_x, wk_x, wv_x, wo_x, g_mlp,
                                       w_ff1, w_ff2)))
    bsz, seq, d = x.shape
    mem_len = mem.shape[1]
    xt = x.reshape(bsz * seq, d)
    mem2 = mem.reshape(bsz * mem_len, d)
    for l in range(g_mix.shape[0]):
        p = {k: v[l] for k, v in stacked.items()}
        mem_n = _rmsnorm([mem2], p["g_mem"], BF16, tm=256)
        xt = _layer(xt, mem_n, p, bsz, seq, mem_len)
    out = _rmsnorm([xt], g_final, x.dtype, tm=256)
    return out.reshape(bsz, seq, d)
```

```python
import functools
import math

import jax
import jax.numpy as jnp
from jax import lax
from jax.experimental import pallas as pl
from jax.experimental.pallas import tpu as pltpu

F32 = jnp.float32
BF16 = jnp.bfloat16

V7X_VMEM_LIMIT_BYTES = 56 * 1024 * 1024
LANES = 128
SUBLANES = 8

EPS = 1e-6
NEG_INF = -1e30
LOG2E = math.log2(math.e)
N_BIAS_PIECES = 3
ATTN_HEAD_DIM = 128
SSM_GROUP = 16
SSM_STATE = 64
S5_CHUNK = 16
XATTN_HEADS = 4
N_BRANCH = 2


def _cparams(*sem):
    return pltpu.CompilerParams(dimension_semantics=sem,
                                vmem_limit_bytes=V7X_VMEM_LIMIT_BYTES)


def _rmsnorm_body(*refs, n_in):
    g_ref, o_ref = refs[n_in], refs[n_in + 1]
    xs = refs[0][...].astype(F32)
    for r in refs[1:n_in]:
        xs = xs + r[...].astype(F32)
    ms = jnp.mean(xs * xs, axis=-1, keepdims=True)
    o_ref[...] = ((xs * lax.rsqrt(ms + EPS)) * g_ref[...]).astype(o_ref.dtype)


def _rmsnorm(xs, g, out_dtype, tm):
    m, d = xs[0].shape
    tm = min(tm, m)
    row = pl.BlockSpec((tm, d), lambda i: (i, 0))
    return pl.pallas_call(
        functools.partial(_rmsnorm_body, n_in=len(xs)),
        grid=(m // tm,),
        in_specs=[row] * len(xs) + [pl.BlockSpec((1, d), lambda i: (0, 0))],
        out_specs=row,
        out_shape=jax.ShapeDtypeStruct((m, d), out_dtype),
        compiler_params=_cparams("parallel"),
        name="rmsnorm",
    )(*xs, g.reshape(1, d).astype(F32))


def _mm_body(*refs, n_pairs, n_aux, has_res, epilogue):
    accs = [jnp.dot(refs[2 * p][...], refs[2 * p + 1][...], preferred_element_type=F32)
            for p in range(n_pairs)]
    pos = 2 * n_pairs
    aux = [refs[pos + i][...] for i in range(n_aux)]
    pos += n_aux
    res = refs[pos][...].astype(F32) if has_res else None
    o_ref = refs[-1]
    o_ref[...] = epilogue(accs, aux, res).astype(o_ref.dtype)


def _mm_ksplit_body(a_ref, b_ref, *rest, n_aux, has_res, epilogue, nk):
    acc_ref = rest[-1]
    o_ref = rest[-2]
    k = pl.program_id(2)

    @pl.when(k == 0)
    def _():
        acc_ref[...] = jnp.zeros_like(acc_ref)

    acc_ref[...] += jnp.dot(a_ref[...], b_ref[...], preferred_element_type=F32)

    @pl.when(k == nk - 1)
    def _():
        aux = [rest[i][...] for i in range(n_aux)]
        res = rest[n_aux][...].astype(F32) if has_res else None
        o_ref[...] = epilogue([acc_ref[...]], aux, res).astype(o_ref.dtype)


def _fused_matmul(pairs, epilogue, out_dtype, *, n, tm, tn, tk=None, aux=(), res=None,
                  name="fused_matmul"):
    m = pairs[0][0].shape[0]
    tm, tn = min(tm, m), min(tn, n)
    n_aux, has_res = len(aux), res is not None
    out_shape = jax.ShapeDtypeStruct((m, n), out_dtype)
    operands, in_specs = [], []

    if tk is None or tk >= pairs[0][0].shape[1]:
        for lhs, rhs, off in pairs:
            kp = lhs.shape[1]
            operands += [lhs, rhs]
            in_specs += [pl.BlockSpec((tm, kp), lambda i, j: (i, 0)),
                         pl.BlockSpec((kp, tn), lambda i, j, off=off: (0, j + off))]
        for vec, off in aux:
            operands.append(vec)
            in_specs.append(pl.BlockSpec((1, tn), lambda i, j, off=off: (0, j + off)))
        if has_res:
            operands.append(res)
            in_specs.append(pl.BlockSpec((tm, tn), lambda i, j: (i, j)))
        return pl.pallas_call(
            functools.partial(_mm_body, n_pairs=len(pairs), n_aux=n_aux, has_res=has_res,
                              epilogue=epilogue),
            grid=(m // tm, n // tn),
            in_specs=in_specs,
            out_specs=pl.BlockSpec((tm, tn), lambda i, j: (i, j)),
            out_shape=out_shape,
            compiler_params=_cparams("parallel", "parallel"),
            name=name,
        )(*operands)

    (lhs, rhs, off), = pairs
    nk = lhs.shape[1] // tk
    operands = [lhs, rhs]
    in_specs = [pl.BlockSpec((tm, tk), lambda i, j, k: (i, k)),
                pl.BlockSpec((tk, tn), lambda i, j, k, off=off: (k, j + off))]
    for vec, voff in aux:
        operands.append(vec)
        in_specs.append(pl.BlockSpec((1, tn), lambda i, j, k, voff=voff: (0, j + voff)))
    if has_res:
        operands.append(res)
        in_specs.append(pl.BlockSpec((tm, tn), lambda i, j, k: (i, j)))
    return pl.pallas_call(
        functools.partial(_mm_ksplit_body, n_aux=n_aux, has_res=has_res, epilogue=epilogue,
                          nk=nk),
        grid=(m // tm, n // tn, nk),
        in_specs=in_specs,
        out_specs=pl.BlockSpec((tm, tn), lambda i, j, k: (i, j)),
        out_shape=out_shape,
        scratch_shapes=[pltpu.VMEM((tm, tn), F32)],
        compiler_params=_cparams("parallel", "parallel", "arbitrary"),
        name=name,
    )(*operands)


def _ep_plain(accs, aux, res):
    return accs[0]


def _ep_add_res(accs, aux, res):
    return res + accs[0]


def _ep_log_sigmoid(accs, aux, res):
    z = accs[0] + aux[0]
    return jnp.minimum(z, 0.0) - jnp.log1p(jnp.exp(-jnp.abs(z)))


def _ep_gated_merge(accs, aux, res):
    g_attn = jax.nn.sigmoid(accs[0] + aux[0])
    g_ssm = jax.nn.sigmoid(accs[2] + aux[1])
    return g_attn * accs[1] + g_ssm * accs[3]


def _ep_relu2(accs, aux, res):
    r = jnp.maximum(accs[0], 0.0)
    return r * r


def _forget_bias_body(x_ref, o_ref, carry_ref, *, blk, nblk, heads):
    @pl.when(pl.program_id(1) == 0)
    def _():
        carry_ref[...] = jnp.zeros_like(carry_ref)

    r = lax.broadcasted_iota(jnp.int32, (blk, blk), 0)
    c = lax.broadcasted_iota(jnp.int32, (blk, blk), 1)
    tri = (c <= r).astype(F32)
    lane = lax.broadcasted_iota(jnp.int32, (blk, LANES), 1)

    def step(i, carry):
        base = pl.multiple_of(i * blk, blk)
        cs = jnp.dot(tri, x_ref[pl.ds(base, blk), :], preferred_element_type=F32,
                     precision=lax.Precision.HIGHEST) + carry
        for h in range(heads):
            bias = jnp.broadcast_to(cs[:, h:h + 1] * (-LOG2E), (blk, LANES))
            hi = bias.astype(BF16).astype(F32)
            mid = (bias - hi).astype(BF16).astype(F32)
            lo = bias - hi - mid
            pieces = jnp.where(lane == 0, hi, jnp.where(lane == 1, mid,
                                                        jnp.where(lane == 2, lo, 0.0)))
            o_ref[pl.ds(base, blk), h * LANES:(h + 1) * LANES] = pieces.astype(o_ref.dtype)
        return cs[blk - 1:blk, :]

    carry_ref[0:1, :] = lax.fori_loop(0, nblk, step, carry_ref[0:1, :])


def _forget_bias(log_f, bsz, seq, heads):
    blk = min(256, seq)
    tt = min(1024, seq)
    nt = seq // tt
    return pl.pallas_call(
        functools.partial(_forget_bias_body, blk=blk, nblk=tt // blk, heads=heads),
        grid=(bsz, nt),
        in_specs=[pl.BlockSpec((tt, LANES), lambda b, t: (b * nt + t, 0))],
        out_specs=pl.BlockSpec((tt, heads * LANES), lambda b, t: (b * nt + t, 0)),
        out_shape=jax.ShapeDtypeStruct((bsz * seq, heads * LANES), BF16),
        scratch_shapes=[pltpu.VMEM((SUBLANES, LANES), F32)],
        compiler_params=_cparams("parallel", "arbitrary"),
        name="forget_bias",
    )(log_f)


def _fox_body(q_ref, k_ref, v_ref, cp_ref, o_ref, kaug_ref, vaug_ref, s_ref, m_ref, acc_ref,
              *, tq, seq):
    qi = pl.program_id(2)
    dh = q_ref.shape[1]

    @pl.when(qi == 0)
    def _():
        def fill(j, _):
            rows = pl.ds(pl.multiple_of(j * tq, tq), tq)
            kaug_ref[rows, :dh] = k_ref[rows, :]
            kaug_ref[rows, dh:] = cp_ref[rows, :]
            vaug_ref[rows, :dh] = v_ref[rows, :]
            vaug_ref[rows, dh:] = jnp.ones((tq, dh), vaug_ref.dtype)
            return 0
        lax.fori_loop(0, seq // tq, fill, 0)

    lane = lax.broadcasted_iota(jnp.int32, (tq, dh), 1)
    q = jnp.concatenate([q_ref[...], (lane < N_BIAS_PIECES).astype(q_ref.dtype)], axis=1)

    def scores(j):
        rows = pl.ds(pl.multiple_of(j * tq, tq), tq)
        return lax.dot_general(q, kaug_ref[rows, :], (((1,), (1,)), ((), ())),
                               preferred_element_type=F32)

    def accumulate(s, j, masked):
        rows = pl.ds(pl.multiple_of(j * tq, tq), tq)
        if masked:
            row = lax.broadcasted_iota(jnp.int32, (tq, tq), 0)
            col = lax.broadcasted_iota(jnp.int32, (tq, tq), 1)
            s = jnp.where(col <= row, s, NEG_INF)
        m = m_ref[...]
        m_new = jnp.maximum(m, jnp.max(s, axis=-1, keepdims=True))
        m_ref[...] = m_new
        p = jnp.exp2(s - m_new)
        acc_ref[...] = jnp.exp2(m - m_new) * acc_ref[...] + jnp.dot(
            p.astype(vaug_ref.dtype), vaug_ref[rows, :], preferred_element_type=F32)

    def step(j, _):
        s = s_ref[...]
        s_ref[...] = scores(j)
        accumulate(s, j - 1, masked=False)
        return 0

    s_ref[...] = scores(0)
    m_ref[...] = jnp.full(m_ref.shape, NEG_INF, F32)
    acc_ref[...] = jnp.zeros(acc_ref.shape, F32)
    lax.fori_loop(1, qi + 1, step, 0)
    accumulate(s_ref[...], qi, masked=True)
    acc = acc_ref[...]
    o_ref[...] = (acc[:, :dh] / acc[:, dh:]).astype(o_ref.dtype)


def _fox_attention(qkv, cp, bsz, seq, heads, tq):
    dh = ATTN_HEAD_DIM
    tq = min(tq, seq)
    nq = seq // tq
    return pl.pallas_call(
        functools.partial(_fox_body, tq=tq, seq=seq),
        grid=(bsz, heads, nq),
        in_specs=[pl.BlockSpec((tq, dh), lambda b, h, i: (b * nq + i, h)),
                  pl.BlockSpec((seq, dh), lambda b, h, i: (b, heads + h)),
                  pl.BlockSpec((seq, dh), lambda b, h, i: (b, 2 * heads + h)),
                  pl.BlockSpec((seq, dh), lambda b, h, i: (b, h))],
        out_specs=pl.BlockSpec((tq, dh), lambda b, h, i: (b * nq + i, h)),
        out_shape=jax.ShapeDtypeStruct((bsz * seq, heads * dh), BF16),
        scratch_shapes=[pltpu.VMEM((seq, 2 * dh), BF16), pltpu.VMEM((seq, 2 * dh), BF16),
                        pltpu.VMEM((tq, tq), F32), pltpu.VMEM((tq, 1), F32),
                        pltpu.VMEM((tq, 2 * dh), F32)],
        compiler_params=_cparams("parallel", "parallel", "arbitrary"),
        name="fox_attention",
    )(qkv, qkv, qkv, cp)


def _xattn_body(q_ref, k_ref, v_ref, o_ref):
    s = lax.dot_general(q_ref[...], k_ref[...], (((1,), (1,)), ((), ())),
                        preferred_element_type=F32)
    p = jnp.exp(s - jnp.max(s, axis=-1, keepdims=True))
    l = jnp.sum(p, axis=-1, keepdims=True)
    o = jnp.dot(p.astype(v_ref.dtype), v_ref[...], preferred_element_type=F32)
    o_ref[...] = (o / l).astype(o_ref.dtype)


def _xattn(q, kv, bsz, seq, mem_len, tq):
    width = q.shape[1]
    dh = width // XATTN_HEADS
    tq = min(tq, seq)
    nq = seq // tq
    return pl.pallas_call(
        _xattn_body,
        grid=(bsz, XATTN_HEADS, nq),
        in_specs=[pl.BlockSpec((tq, dh), lambda b, h, i: (b * nq + i, h)),
                  pl.BlockSpec((mem_len, dh), lambda b, h, i: (b, h)),
                  pl.BlockSpec((mem_len, dh), lambda b, h, i: (b, XATTN_HEADS + h))],
        out_specs=pl.BlockSpec((tq, dh), lambda b, h, i: (b * nq + i, h)),
        out_shape=jax.ShapeDtypeStruct(q.shape, BF16),
        compiler_params=_cparams("parallel", "parallel", "parallel"),
        name="mem_xattn",
    )(q, kv, kv)


def _s5_operators_body(bre_ref, bim_ref, cre_ref, cim_ref, p_ref, pt_ref, d_ref,
                        t_ref, w_ref, v_ref, *, L):
    hp = lax.Precision.HIGHEST
    bre, bim = bre_ref[...], bim_ref[...]
    cre, cim = cre_ref[...], cim_ref[...]
    k = bre.shape[1]
    row = lax.broadcasted_iota(jnp.int32, (LANES, LANES), 0)
    col = lax.broadcasted_iota(jnp.int32, (LANES, LANES), 1)
    lag_blocks = []
    for tau in range(L):
        pr, pi = p_ref[0, tau:tau + 1, :], p_ref[1, tau:tau + 1, :]
        xr, xi = bre * pr - bim * pi, bre * pi + bim * pr
        rows = slice((L - 1 - tau) * LANES, (L - tau) * LANES)
        w_ref[rows, :k] = xr.astype(w_ref.dtype)
        w_ref[rows, k:] = xi.astype(w_ref.dtype)
        d_tau = (jnp.dot(xr, cre, preferred_element_type=F32, precision=hp)
                 - jnp.dot(xi, cim, preferred_element_type=F32, precision=hp))
        if tau == 0:
            d_tau = d_tau + jnp.where(row == col, d_ref[...], 0.0)
        lag_blocks.append(d_tau.astype(t_ref.dtype))
    zero = jnp.zeros((LANES, LANES), t_ref.dtype)
    for s in range(L):
        for t in range(L):
            t_ref[s * LANES:(s + 1) * LANES, t * LANES:(t + 1) * LANES] = (
                lag_blocks[t - s] if t >= s else zero)
    for t in range(L):
        qr, qi = pt_ref[0, :, t + 1:t + 2], pt_ref[1, :, t + 1:t + 2]
        cols = slice(t * LANES, (t + 1) * LANES)
        v_ref[:k, cols] = (cre * qr - cim * qi).astype(v_ref.dtype)
        v_ref[k:, cols] = (-(cre * qi + cim * qr)).astype(v_ref.dtype)


def _s5_operators(A_re, A_im, log_dt, B_re, B_im, C_re, C_im, D_skip):
    n_g, n_p = A_re.shape
    n_h = B_re.shape[-1]
    L = S5_CHUNK
    gs = LANES // n_h
    n_slab = n_g // gs
    k = gs * n_p
    a_re, a_im = A_re.astype(F32), A_im.astype(F32)
    dt = jnp.exp(log_dt.astype(F32))[:, None]
    tau = jnp.arange(L + 1, dtype=F32)[:, None, None]
    mag = jnp.exp(tau * (dt * a_re))
    ang = tau * (dt * a_im)
    pw = jnp.stack([mag * jnp.cos(ang), mag * jnp.sin(ang)])
    lb_re, lb_im = pw[0, 1], pw[1, 1]
    den = a_re * a_re + a_im * a_im
    nr, ni = lb_re - 1.0, lb_im
    f_re = (nr * a_re + ni * a_im) / den
    f_im = (ni * a_re - nr * a_im) / den
    br, bi = B_re.astype(F32), B_im.astype(F32)
    bb_re = f_re[..., None] * br - f_im[..., None] * bi
    bb_im = f_re[..., None] * bi + f_im[..., None] * br

    eye = jnp.eye(gs, dtype=F32)

    def slab_b(x):
        x = x.reshape(n_slab, gs, n_p, n_h).transpose(0, 1, 3, 2)
        return (x[:, :, :, None, :] * eye[None, :, None, :, None]).reshape(n_slab, LANES, k)

    def slab_c(x):
        x = x.reshape(n_slab, gs, n_h, n_p).transpose(0, 1, 3, 2)
        return (x[:, :, :, None, :] * eye[None, :, None, :, None]).reshape(n_slab, k, LANES)

    p_tab = pw.reshape(2, L + 1, n_slab, k).transpose(2, 0, 1, 3)
    pt_tab = jnp.pad(pw.reshape(2, L + 1, n_slab, k).transpose(2, 0, 3, 1),
                     ((0, 0), (0, 0), (0, 0), (0, LANES - (L + 1))))
    mat_b = pl.BlockSpec((None, LANES, k), lambda j: (j, 0, 0))
    mat_c = pl.BlockSpec((None, k, LANES), lambda j: (j, 0, 0))
    t_op, w_op, v_op = pl.pallas_call(
        functools.partial(_s5_operators_body, L=L),
        grid=(n_slab,),
        in_specs=[mat_b, mat_b, mat_c, mat_c,
                  pl.BlockSpec((None, 2, L + 1, k), lambda j: (j, 0, 0, 0)),
                  pl.BlockSpec((None, 2, k, LANES), lambda j: (j, 0, 0, 0)),
                  pl.BlockSpec((1, LANES), lambda j: (0, j))],
        out_specs=[pl.BlockSpec((None, L * LANES, L * LANES), lambda j: (j, 0, 0)),
                   pl.BlockSpec((None, L * LANES, 2 * k), lambda j: (j, 0, 0)),
                   pl.BlockSpec((None, 2 * k, L * LANES), lambda j: (j, 0, 0))],
        out_shape=[jax.ShapeDtypeStruct((n_slab, L * LANES, L * LANES), BF16),
                   jax.ShapeDtypeStruct((n_slab, L * LANES, 2 * k), BF16),
                   jax.ShapeDtypeStruct((n_slab, 2 * k, L * LANES), BF16)],
        compiler_params=_cparams("parallel"),
        name="s5_operators",
    )(slab_b(bb_re), slab_b(bb_im), slab_c(C_re.astype(F32)), slab_c(C_im.astype(F32)),
      p_tab, pt_tab, D_skip.astype(F32).reshape(1, n_g * n_h))

    a_op = pw[:, L].reshape(2, n_slab, k).transpose(1, 0, 2).reshape(1, n_slab * 2 * k)
    return t_op, w_op, v_op, a_op


def _chunk_rows(piece_refs):
    return jnp.concatenate([r[...] for r in piece_refs], axis=1)


def _s5_increment_body(*refs):
    w_ref, z_ref = refs[-2], refs[-1]
    z_ref[...] = jnp.dot(_chunk_rows(refs[:-2]), w_ref[...], preferred_element_type=F32)


def _s5_scan_body(z_ref, a_ref, o_ref, st_ref, *, tc):
    @pl.when(pl.program_id(2) == 0)
    def _():
        st_ref[...] = jnp.zeros_like(st_ref)

    half = z_ref.shape[1] // 2
    a_re, a_im = a_ref[:, :half], a_ref[:, half:]

    def step(i, carry):
        re, im = carry
        base = pl.multiple_of(i * SUBLANES, SUBLANES)
        inc = z_ref[pl.ds(base, SUBLANES), :]
        before_re, before_im = [], []
        for r in range(SUBLANES):
            before_re.append(re)
            before_im.append(im)
            re, im = (a_re * re - a_im * im + inc[r:r + 1, :half],
                      a_re * im + a_im * re + inc[r:r + 1, half:])
        o_ref[pl.ds(base, SUBLANES), :half] = jnp.concatenate(before_re, axis=0)
        o_ref[pl.ds(base, SUBLANES), half:] = jnp.concatenate(before_im, axis=0)
        return re, im

    re, im = lax.fori_loop(0, tc // SUBLANES, step, (st_ref[0:1, :half], st_ref[0:1, half:]))
    st_ref[0:1, :half] = re
    st_ref[0:1, half:] = im


def _s5_output_body(*refs):
    x_ref, t_ref, v_ref, o_ref = refs[-4:]
    y = jnp.dot(_chunk_rows(refs[:-4]), t_ref[...], preferred_element_type=F32)
    y = y + jnp.dot(x_ref[...].astype(BF16), v_ref[...], preferred_element_type=F32)
    o_ref[...] = jax.nn.gelu(y).astype(o_ref.dtype)


def _s5_glu_body(*refs):
    w_ref, b_ref, o_ref = refs[-3:]
    y = _chunk_rows(refs[:-3])
    gate = jax.nn.sigmoid(jnp.dot(y, w_ref[...], preferred_element_type=F32) + b_ref[...])
    o_ref[...] = (y.astype(F32) * gate).astype(o_ref.dtype)


def _s5_branch(u, ops, w_glu, b_glu, bsz, seq):
    t_op, w_op, v_op, a_op = ops
    n_slab = t_op.shape[0]
    L = S5_CHUNK
    width = u.shape[1]
    cw, sw = t_op.shape[1], w_op.shape[2]
    nc = bsz * seq // L
    ncb = seq // L
    uu = u.reshape(nc, L * width)
    tm = min(512, nc)
    pieces = [pl.BlockSpec((tm, LANES), lambda j, i, s=s: (i, s * n_slab + j)) for s in range(L)]

    z = pl.pallas_call(
        _s5_increment_body,
        grid=(n_slab, nc // tm),
        in_specs=pieces + [pl.BlockSpec((None, cw, sw), lambda j, i: (j, 0, 0))],
        out_specs=pl.BlockSpec((tm, sw), lambda j, i: (i, j)),
        out_shape=jax.ShapeDtypeStruct((nc, n_slab * sw), F32),
        compiler_params=_cparams("parallel", "parallel"),
        name="s5_increment",
    )(*([uu] * L), w_op)

    tc = min(128, ncb)
    nt = ncb // tc
    blk = pl.BlockSpec((tc, sw), lambda b, j, t: (b * nt + t, j))
    xprev = pl.pallas_call(
        functools.partial(_s5_scan_body, tc=tc),
        grid=(bsz, n_slab, nt),
        in_specs=[blk, pl.BlockSpec((1, sw), lambda b, j, t: (0, j))],
        out_specs=blk,
        out_shape=jax.ShapeDtypeStruct((nc, n_slab * sw), F32),
        scratch_shapes=[pltpu.VMEM((SUBLANES, sw), F32)],
        compiler_params=_cparams("parallel", "parallel", "arbitrary"),
        name="s5_scan",
    )(z, a_op)

    ys = pl.pallas_call(
        _s5_output_body,
        grid=(n_slab, nc // tm),
        in_specs=pieces + [pl.BlockSpec((tm, sw), lambda j, i: (i, j)),
                           pl.BlockSpec((None, cw, cw), lambda j, i: (j, 0, 0)),
                           pl.BlockSpec((None, sw, cw), lambda j, i: (j, 0, 0))],
        out_specs=pl.BlockSpec((tm, cw), lambda j, i: (i, j)),
        out_shape=jax.ShapeDtypeStruct((nc, n_slab * cw), BF16),
        compiler_params=_cparams("parallel", "parallel"),
        name="s5_output",
    )(*([uu] * L), xprev, t_op, v_op)

    tok = [pl.BlockSpec((tm, LANES), lambda i, t, j=j: (i, j * L + t)) for j in range(n_slab)]
    out = pl.pallas_call(
        _s5_glu_body,
        grid=(nc // tm, L),
        in_specs=tok + [pl.BlockSpec((width, width), lambda i, t: (0, 0)),
                        pl.BlockSpec((1, width), lambda i, t: (0, 0))],
        out_specs=pl.BlockSpec((tm, width), lambda i, t: (i, t)),
        out_shape=jax.ShapeDtypeStruct((nc, L * width), BF16),
        compiler_params=_cparams("parallel", "parallel"),
        name="s5_glu",
    )(*([ys] * n_slab), w_glu.astype(BF16), b_glu.reshape(1, width).astype(F32))
    return out.reshape(bsz * seq, width)


def _row(v):
    return v.reshape(1, -1).astype(F32)


def _layer(x, mem_n, p, bsz, seq, mem_len):
    d = x.shape[1]
    aw = p["w_attn_up"].shape[0]
    heads = aw // ATTN_HEAD_DIM
    sw = p["w_ssm_up"].shape[0]
    off_f = 3 * aw
    off_u = off_f + heads
    off_g = off_u + sw

    h = _rmsnorm([x], p["g_mix"], BF16, tm=256)
    w_in = p["w_in"]
    q_scale = jnp.concatenate([jnp.full((aw,), LOG2E * ATTN_HEAD_DIM ** -0.5, F32),
                               jnp.ones((2 * aw,), F32)])
    w_qkv = (w_in[:, :off_f] * q_scale).astype(BF16)
    w_f = jnp.pad(w_in[:, off_f:off_u], ((0, 0), (0, LANES - heads))).astype(BF16)
    w_u = w_in[:, off_u:off_g].astype(BF16)
    w_g = w_in[:, off_g:].astype(BF16)

    qkv = _fused_matmul([(h, w_qkv, 0)], _ep_plain, BF16, n=3 * aw, tm=1024, tn=1024,
                        name="qkv_proj")
    b_f = jnp.pad(p["b_f"].astype(F32), (0, LANES - heads)).reshape(1, LANES)
    log_f = _fused_matmul([(h, w_f, 0)], _ep_log_sigmoid, F32, n=LANES, tm=1024, tn=LANES,
                          aux=[(b_f, 0)], name="forget_proj")
    fox = _fox_attention(qkv, _forget_bias(log_f, bsz, seq, heads), bsz, seq, heads, tq=1024)

    u = _fused_matmul([(h, w_u, 0)], _ep_plain, BF16, n=sw, tm=1024, tn=1024, name="ssm_in_proj")
    ops = _s5_operators(p["A_re"], p["A_im"], p["log_dt"], p["B_re"], p["B_im"],
                        p["C_re"], p["C_im"], p["D_skip"])
    y = _s5_branch(u, ops, p["w_glu"], p["b_glu"], bsz, seq)

    tn_mix = 256
    b_gate = _row(p["b_gate"])
    merged = _fused_matmul(
        [(h, w_g, 0), (fox, p["w_attn_up"].astype(BF16), 0),
         (h, w_g, d // tn_mix), (y, p["w_ssm_up"].astype(BF16), 0)],
        _ep_gated_merge, BF16, n=d, tm=512, tn=tn_mix,
        aux=[(b_gate, 0), (b_gate, d // tn_mix)], name="gated_merge")
    x = _fused_matmul([(merged, p["w_out"].astype(BF16), 0)], _ep_add_res, F32, n=d,
                      tm=1024, tn=512, res=x, name="mixer_out_proj")

    xw = p["wq_x"].shape[1]
    hx = _rmsnorm([x], p["g_xattn"], BF16, tm=256)
    wq = (p["wq_x"] * (xw // XATTN_HEADS) ** -0.5).astype(BF16)
    qx = _fused_matmul([(hx, wq, 0)], _ep_plain, BF16, n=xw, tm=1024, tn=1024, name="xattn_q_proj")
    w_kv = jnp.concatenate([p["wk_x"], p["wv_x"]], axis=1).astype(BF16)
    kv = _fused_matmul([(mem_n, w_kv, 0)], _ep_plain, BF16, n=2 * xw, tm=512, tn=1024,
                       name="xattn_kv_proj")
    ox = _xattn(qx, kv, bsz, seq, mem_len, tq=1024)
    x = _fused_matmul([(ox, p["wo_x"].astype(BF16), 0)], _ep_add_res, F32, n=d,
                      tm=1024, tn=512, res=x, name="xattn_out_proj")

    hm = _rmsnorm([x], p["g_mlp"], BF16, tm=256)
    dff = p["w_ff1"].shape[1]
    hid = _fused_matmul([(hm, p["w_ff1"].astype(BF16), 0)], _ep_relu2, BF16, n=dff,
                        tm=1024, tn=1024, name="mlp_up")
    x = _fused_matmul([(hid, p["w_ff2"].astype(BF16), 0)], _ep_add_res, F32, n=d,
                      tm=1024, tn=512, tk=4096, res=x, name="mlp_down")
    return x


_LAYER_PARAMS = ("g_mix", "w_in", "b_f", "b_gate", "A_re", "A_im", "log_dt", "B_re", "B_im",
                 "C_re", "C_im", "D_skip", "w_glu", "b_glu", "w_attn_up", "w_ssm_up", "w_out",
                 "g_xattn", "g_mem", "wq_x", "wk_x", "wv_x", "wo_x", "g_mlp", "w_ff1", "w_ff2")


def kernel(x, mem, g_mix, w_in, b_f, b_gate, A_re, A_im, log_dt, B_re, B_im, C_re, C_im, D_skip, w_glu, b_glu, w_attn_up, w_ssm_up, w_out, g_xattn, g_mem, wq_x, wk_x, wv_x, wo_x, g_mlp, w_ff1, w_ff2, g_final):
    stacked = dict(zip(_LAYER_PARAMS, (g_mix, w_in, b_f, b_gate, A_re, A_im, log_dt, B_re, B_im,
                                       C_re, C_im, D_skip, w_glu, b_glu, w_attn_up, w_ssm_up,
                                       w_out, g_xattn, g_mem, wq_x, wk_x, wv_x, wo_x, g_mlp,
                                       w_ff1, w_ff2)))
    bsz, seq, d = x.shape
    mem_len = mem.shape[1]
    xt = x.reshape(bsz * seq, d)
    mem2 = mem.reshape(bsz * mem_len, d)
    for l in range(g_mix.shape[0]):
        p = {k: v[l] for k, v in stacked.items()}
        mem_n = _rmsnorm([mem2], p["g_mem"], BF16, tm=256)
        xt = _layer(xt, mem_n, p, bsz, seq, mem_len)
    out = _rmsnorm([xt], g_final, x.dtype, tm=256)
    return out.reshape(bsz, seq, d)
```

```python
import functools
import math

import jax
import jax.numpy as jnp
from jax import lax
from jax.experimental import pallas as pl
from jax.experimental.pallas import tpu as pltpu

F32 = jnp.float32
BF16 = jnp.bfloat16

V7X_VMEM_LIMIT_BYTES = 56 * 1024 * 1024
LANES = 128
SUBLANES = 8

EPS = 1e-6
NEG_INF = -1e30
LOG2E = math.log2(math.e)
N_BIAS_PIECES = 3
ATTN_HEAD_DIM = 128
SSM_GROUP = 16
SSM_STATE = 64
S5_CHUNK = 16
XATTN_HEADS = 4
N_BRANCH = 2


def _cparams(*sem):
    return pltpu.CompilerParams(dimension_semantics=sem,
                                vmem_limit_bytes=V7X_VMEM_LIMIT_BYTES)


def _rms_scale(x, g):
    return (x * lax.rsqrt(jnp.mean(x * x, axis=-1, keepdims=True) + EPS)) * g


def _rmsnorm_body(*refs, n_in):
    g_ref, o_ref = refs[n_in], refs[n_in + 1]
    xs = refs[0][...].astype(F32)
    for r in refs[1:n_in]:
        xs = xs + r[...].astype(F32)
    o_ref[...] = _rms_scale(xs, g_ref[...]).astype(o_ref.dtype)


def _rmsnorm(xs, g, out_dtype, tm):
    m, d = xs[0].shape
    tm = min(tm, m)
    row = pl.BlockSpec((tm, d), lambda i: (i, 0))
    return pl.pallas_call(
        functools.partial(_rmsnorm_body, n_in=len(xs)),
        grid=(m // tm,),
        in_specs=[row] * len(xs) + [pl.BlockSpec((1, d), lambda i: (0, 0))],
        out_specs=row,
        out_shape=jax.ShapeDtypeStruct((m, d), out_dtype),
        compiler_params=_cparams("parallel"),
        name="rmsnorm",
    )(*xs, g.reshape(1, d).astype(F32))


def _mm_body(*refs, n_pairs, n_aux, has_res, epilogue):
    accs = [jnp.dot(refs[2 * p][...], refs[2 * p + 1][...], preferred_element_type=F32)
            for p in range(n_pairs)]
    pos = 2 * n_pairs
    aux = [refs[pos + i][...] for i in range(n_aux)]
    pos += n_aux
    res = refs[pos][...].astype(F32) if has_res else None
    o_ref = refs[-1]
    o_ref[...] = epilogue(accs, aux, res).astype(o_ref.dtype)


def _mm_ksplit_body(a_ref, b_ref, *rest, n_aux, has_res, epilogue, nk):
    acc_ref = rest[-1]
    o_ref = rest[-2]
    k = pl.program_id(2)

    @pl.when(k == 0)
    def _():
        acc_ref[...] = jnp.zeros_like(acc_ref)

    acc_ref[...] += jnp.dot(a_ref[...], b_ref[...], preferred_element_type=F32)

    @pl.when(k == nk - 1)
    def _():
        aux = [rest[i][...] for i in range(n_aux)]
        res = rest[n_aux][...].astype(F32) if has_res else None
        o_ref[...] = epilogue([acc_ref[...]], aux, res).astype(o_ref.dtype)


def _fused_matmul(pairs, epilogue, out_dtype, *, n, tm, tn, tk=None, aux=(), res=None,
                  name="fused_matmul"):
    m = pairs[0][0].shape[0]
    tm, tn = min(tm, m), min(tn, n)
    n_aux, has_res = len(aux), res is not None
    out_shape = jax.ShapeDtypeStruct((m, n), out_dtype)
    operands, in_specs = [], []

    if tk is None or tk >= pairs[0][0].shape[1]:
        for lhs, rhs, off, *mult in pairs:
            kp = lhs.shape[1]
            width = tn * (mult[0] if mult else 1)
            operands += [lhs, rhs]
            in_specs += [pl.BlockSpec((tm, kp), lambda i, j: (i, 0)),
                         pl.BlockSpec((kp, width), lambda i, j, off=off: (0, j + off))]
        for vec, off, *mult in aux:
            width = tn * (mult[0] if mult else 1)
            operands.append(vec)
            in_specs.append(pl.BlockSpec((1, width), lambda i, j, off=off: (0, j + off)))
        if has_res:
            operands.append(res)
            in_specs.append(pl.BlockSpec((tm, tn), lambda i, j: (i, j)))
        return pl.pallas_call(
            functools.partial(_mm_body, n_pairs=len(pairs), n_aux=n_aux, has_res=has_res,
                              epilogue=epilogue),
            grid=(m // tm, n // tn),
            in_specs=in_specs,
            out_specs=pl.BlockSpec((tm, tn), lambda i, j: (i, j)),
            out_shape=out_shape,
            compiler_params=_cparams("parallel", "parallel"),
            name=name,
        )(*operands)

    (lhs, rhs, off), = pairs
    nk = lhs.shape[1] // tk
    operands = [lhs, rhs]
    in_specs = [pl.BlockSpec((tm, tk), lambda i, j, k: (i, k)),
                pl.BlockSpec((tk, tn), lambda i, j, k, off=off: (k, j + off))]
    for vec, voff in aux:
        operands.append(vec)
        in_specs.append(pl.BlockSpec((1, tn), lambda i, j, k, voff=voff: (0, j + voff)))
    if has_res:
        operands.append(res)
        in_specs.append(pl.BlockSpec((tm, tn), lambda i, j, k: (i, j)))
    return pl.pallas_call(
        functools.partial(_mm_ksplit_body, n_aux=n_aux, has_res=has_res, epilogue=epilogue,
                          nk=nk),
        grid=(m // tm, n // tn, nk),
        in_specs=in_specs,
        out_specs=pl.BlockSpec((tm, tn), lambda i, j, k: (i, j)),
        out_shape=out_shape,
        scratch_shapes=[pltpu.VMEM((tm, tn), F32)],
        compiler_params=_cparams("parallel", "parallel", "arbitrary"),
        name=name,
    )(*operands)


def _ep_plain(accs, aux, res):
    return accs[0]


def _ep_add_res(accs, aux, res):
    return res + accs[0]


def _ep_log_sigmoid(accs, aux, res):
    z = accs[0] + aux[0]
    return jnp.minimum(z, 0.0) - jnp.log1p(jnp.exp(-jnp.abs(z)))


def _ep_gated_merge(accs, aux, res):
    tn = accs[1].shape[1]
    gates = jax.nn.sigmoid(accs[0] + aux[0])
    return gates[:, :tn] * accs[1] + gates[:, tn:] * accs[2]


def _ep_relu2(accs, aux, res):
    r = jnp.maximum(accs[0], 0.0)
    return r * r


def _forget_bias_body(x_ref, o_ref, carry_ref, *, blk, nblk, heads):
    @pl.when(pl.program_id(1) == 0)
    def _():
        carry_ref[...] = jnp.zeros_like(carry_ref)

    r = lax.broadcasted_iota(jnp.int32, (blk, blk), 0)
    c = lax.broadcasted_iota(jnp.int32, (blk, blk), 1)
    tri = (c <= r).astype(F32)
    lane = lax.broadcasted_iota(jnp.int32, (blk, LANES), 1)

    def step(i, carry):
        base = pl.multiple_of(i * blk, blk)
        cs = jnp.dot(tri, x_ref[pl.ds(base, blk), :], preferred_element_type=F32,
                     precision=lax.Precision.HIGHEST) + carry
        for h in range(heads):
            bias = jnp.broadcast_to(cs[:, h:h + 1] * (-LOG2E), (blk, LANES))
            hi = bias.astype(BF16).astype(F32)
            mid = (bias - hi).astype(BF16).astype(F32)
            lo = bias - hi - mid
            pieces = jnp.where(lane == 0, hi, jnp.where(lane == 1, mid,
                                                        jnp.where(lane == 2, lo, 0.0)))
            o_ref[pl.ds(base, blk), h * LANES:(h + 1) * LANES] = pieces.astype(o_ref.dtype)
        return cs[blk - 1:blk, :]

    carry_ref[0:1, :] = lax.fori_loop(0, nblk, step, carry_ref[0:1, :])


def _forget_bias(log_f, bsz, seq, heads):
    blk = min(256, seq)
    tt = min(1024, seq)
    nt = seq // tt
    return pl.pallas_call(
        functools.partial(_forget_bias_body, blk=blk, nblk=tt // blk, heads=heads),
        grid=(bsz, nt),
        in_specs=[pl.BlockSpec((tt, LANES), lambda b, t: (b * nt + t, 0))],
        out_specs=pl.BlockSpec((tt, heads * LANES), lambda b, t: (b * nt + t, 0)),
        out_shape=jax.ShapeDtypeStruct((bsz * seq, heads * LANES), BF16),
        scratch_shapes=[pltpu.VMEM((SUBLANES, LANES), F32)],
        compiler_params=_cparams("parallel", "arbitrary"),
        name="forget_bias",
    )(log_f)


def _fox_body(q_ref, k_ref, v_ref, cp_ref, o_ref, kaug_ref, vaug_ref, s_ref, m_ref, acc_ref,
              *, tq, seq):
    qi = pl.program_id(2)
    dh = q_ref.shape[1]

    @pl.when(qi == 0)
    def _():
        def fill(j, _):
            rows = pl.ds(pl.multiple_of(j * tq, tq), tq)
            kaug_ref[rows, :dh] = k_ref[rows, :]
            kaug_ref[rows, dh:] = cp_ref[rows, :]
            vaug_ref[rows, :dh] = v_ref[rows, :]
            vaug_ref[rows, dh:] = jnp.ones((tq, dh), vaug_ref.dtype)
            return 0
        lax.fori_loop(0, seq // tq, fill, 0)

    lane = lax.broadcasted_iota(jnp.int32, (tq, dh), 1)
    q = jnp.concatenate([q_ref[...], (lane < N_BIAS_PIECES).astype(q_ref.dtype)], axis=1)

    def scores(j):
        rows = pl.ds(pl.multiple_of(j * tq, tq), tq)
        return lax.dot_general(q, kaug_ref[rows, :], (((1,), (1,)), ((), ())),
                               preferred_element_type=F32)

    def accumulate(s, j, masked):
        rows = pl.ds(pl.multiple_of(j * tq, tq), tq)
        if masked:
            row = lax.broadcasted_iota(jnp.int32, (tq, tq), 0)
            col = lax.broadcasted_iota(jnp.int32, (tq, tq), 1)
            s = jnp.where(col <= row, s, NEG_INF)
        m = m_ref[...]
        m_new = jnp.maximum(m, jnp.max(s, axis=-1, keepdims=True))
        m_ref[...] = m_new
        p = jnp.exp2(s - m_new)
        acc_ref[...] = jnp.exp2(m - m_new) * acc_ref[...] + jnp.dot(
            p.astype(vaug_ref.dtype), vaug_ref[rows, :], preferred_element_type=F32)

    def step(j, _):
        s = s_ref[...]
        s_ref[...] = scores(j)
        accumulate(s, j - 1, masked=False)
        return 0

    s_ref[...] = scores(0)
    m_ref[...] = jnp.full(m_ref.shape, NEG_INF, F32)
    acc_ref[...] = jnp.zeros(acc_ref.shape, F32)
    lax.fori_loop(1, qi + 1, step, 0)
    accumulate(s_ref[...], qi, masked=True)
    acc = acc_ref[...]
    o_ref[...] = (acc[:, :dh] / acc[:, dh:]).astype(o_ref.dtype)


def _fox_attention(qkv, cp, bsz, seq, heads, tq):
    dh = ATTN_HEAD_DIM
    tq = min(tq, seq)
    nq = seq // tq
    return pl.pallas_call(
        functools.partial(_fox_body, tq=tq, seq=seq),
        grid=(bsz, heads, nq),
        in_specs=[pl.BlockSpec((tq, dh), lambda b, h, i: (b * nq + i, h)),
                  pl.BlockSpec((seq, dh), lambda b, h, i: (b, heads + h)),
                  pl.BlockSpec((seq, dh), lambda b, h, i: (b, 2 * heads + h)),
                  pl.BlockSpec((seq, dh), lambda b, h, i: (b, h))],
        out_specs=pl.BlockSpec((tq, dh), lambda b, h, i: (b * nq + i, h)),
        out_shape=jax.ShapeDtypeStruct((bsz * seq, heads * dh), BF16),
        scratch_shapes=[pltpu.VMEM((seq, 2 * dh), BF16), pltpu.VMEM((seq, 2 * dh), BF16),
                        pltpu.VMEM((tq, tq), F32), pltpu.VMEM((tq, 1), F32),
                        pltpu.VMEM((tq, 2 * dh), F32)],
        compiler_params=_cparams("parallel", "parallel", "arbitrary"),
        name="fox_attention",
    )(qkv, qkv, qkv, cp)


def _xattn_block_body(x_ref, gx_ref, wq_ref, kv_ref, wo_ref, gn_ref, xo_ref, hn_ref):
    x = x_ref[...]
    width = wq_ref.shape[1]
    dh = width // XATTN_HEADS
    hx = _rms_scale(x, gx_ref[...]).astype(wq_ref.dtype)
    q = jnp.dot(hx, wq_ref[...], preferred_element_type=F32).astype(kv_ref.dtype)
    heads = []
    for hd in range(XATTN_HEADS):
        k = kv_ref[:, hd * dh:(hd + 1) * dh]
        v = kv_ref[:, width + hd * dh:width + (hd + 1) * dh]
        s = lax.dot_general(q[:, hd * dh:(hd + 1) * dh], k, (((1,), (1,)), ((), ())),
                            preferred_element_type=F32)
        p = jnp.exp(s - jnp.max(s, axis=-1, keepdims=True))
        l = jnp.sum(p, axis=-1, keepdims=True)
        o = jnp.dot(p.astype(v.dtype), v, preferred_element_type=F32) / l
        heads.append(o.astype(wo_ref.dtype))
    x2 = x + jnp.dot(jnp.concatenate(heads, axis=1), wo_ref[...], preferred_element_type=F32)
    xo_ref[...] = x2
    hn_ref[...] = _rms_scale(x2, gn_ref[...]).astype(hn_ref.dtype)


def _xattn_block(x, g_x, wq, kv, wo, g_next, seq, mem_len, tm):
    t, d = x.shape
    width = wq.shape[1]
    tm = min(tm, seq)
    nq = seq // tm
    row = pl.BlockSpec((tm, d), lambda i: (i, 0))
    vec = pl.BlockSpec((1, d), lambda i: (0, 0))
    resident = pl.Buffered(1)
    return pl.pallas_call(
        _xattn_block_body,
        grid=(t // tm,),
        in_specs=[row, vec,
                  pl.BlockSpec((d, width), lambda i: (0, 0), pipeline_mode=resident),
                  pl.BlockSpec((mem_len, 2 * width), lambda i: (i // nq, 0)),
                  pl.BlockSpec((width, d), lambda i: (0, 0), pipeline_mode=resident),
                  vec],
        out_specs=[row, row],
        out_shape=[jax.ShapeDtypeStruct((t, d), F32), jax.ShapeDtypeStruct((t, d), BF16)],
        compiler_params=_cparams("parallel"),
        name="xattn_block",
    )(x, g_x.reshape(1, d).astype(F32), wq, kv, wo, g_next.reshape(1, d).astype(F32))


def _s5_operators_body(bre_ref, bim_ref, cre_ref, cim_ref, p_ref, pt_ref, d_ref,
                        t_ref, w_ref, v_ref, *, L):
    hp = lax.Precision.HIGHEST
    bre, bim = bre_ref[...], bim_ref[...]
    cre, cim = cre_ref[...], cim_ref[...]
    k = bre.shape[1]
    row = lax.broadcasted_iota(jnp.int32, (LANES, LANES), 0)
    col = lax.broadcasted_iota(jnp.int32, (LANES, LANES), 1)
    lag_blocks = []
    for tau in range(L):
        pr, pi = p_ref[0, tau:tau + 1, :], p_ref[1, tau:tau + 1, :]
        xr, xi = bre * pr - bim * pi, bre * pi + bim * pr
        rows = slice((L - 1 - tau) * LANES, (L - tau) * LANES)
        w_ref[rows, :k] = xr.astype(w_ref.dtype)
        w_ref[rows, k:] = xi.astype(w_ref.dtype)
        d_tau = (jnp.dot(xr, cre, preferred_element_type=F32, precision=hp)
                 - jnp.dot(xi, cim, preferred_element_type=F32, precision=hp))
        if tau == 0:
            d_tau = d_tau + jnp.where(row == col, d_ref[...], 0.0)
        lag_blocks.append(d_tau.astype(t_ref.dtype))
    zero = jnp.zeros((LANES, LANES), t_ref.dtype)
    for s in range(L):
        for t in range(L):
            t_ref[s * LANES:(s + 1) * LANES, t * LANES:(t + 1) * LANES] = (
                lag_blocks[t - s] if t >= s else zero)
    for t in range(L):
        qr, qi = pt_ref[0, :, t + 1:t + 2], pt_ref[1, :, t + 1:t + 2]
        cols = slice(t * LANES, (t + 1) * LANES)
        v_ref[:k, cols] = (cre * qr - cim * qi).astype(v_ref.dtype)
        v_ref[k:, cols] = (-(cre * qi + cim * qr)).astype(v_ref.dtype)


def _s5_operators(A_re, A_im, log_dt, B_re, B_im, C_re, C_im, D_skip):
    n_g, n_p = A_re.shape
    n_h = B_re.shape[-1]
    L = S5_CHUNK
    gs = LANES // n_h
    n_slab = n_g // gs
    k = gs * n_p
    a_re, a_im = A_re.astype(F32), A_im.astype(F32)
    dt = jnp.exp(log_dt.astype(F32))[:, None]
    tau = jnp.arange(L + 1, dtype=F32)[:, None, None]
    mag = jnp.exp(tau * (dt * a_re))
    ang = tau * (dt * a_im)
    pw = jnp.stack([mag * jnp.cos(ang), mag * jnp.sin(ang)])
    lb_re, lb_im = pw[0, 1], pw[1, 1]
    den = a_re * a_re + a_im * a_im
    nr, ni = lb_re - 1.0, lb_im
    f_re = (nr * a_re + ni * a_im) / den
    f_im = (ni * a_re - nr * a_im) / den
    br, bi = B_re.astype(F32), B_im.astype(F32)
    bb_re = f_re[..., None] * br - f_im[..., None] * bi
    bb_im = f_re[..., None] * bi + f_im[..., None] * br

    eye = jnp.eye(gs, dtype=F32)

    def slab_b(x):
        x = x.reshape(n_slab, gs, n_p, n_h).transpose(0, 1, 3, 2)
        return (x[:, :, :, None, :] * eye[None, :, None, :, None]).reshape(n_slab, LANES, k)

    def slab_c(x):
        x = x.reshape(n_slab, gs, n_h, n_p).transpose(0, 1, 3, 2)
        return (x[:, :, :, None, :] * eye[None, :, None, :, None]).reshape(n_slab, k, LANES)

    p_tab = pw.reshape(2, L + 1, n_slab, k).transpose(2, 0, 1, 3)
    pt_tab = jnp.pad(pw.reshape(2, L + 1, n_slab, k).transpose(2, 0, 3, 1),
                     ((0, 0), (0, 0), (0, 0), (0, LANES - (L + 1))))
    mat_b = pl.BlockSpec((None, LANES, k), lambda j: (j, 0, 0))
    mat_c = pl.BlockSpec((None, k, LANES), lambda j: (j, 0, 0))
    t_op, w_op, v_op = pl.pallas_call(
        functools.partial(_s5_operators_body, L=L),
        grid=(n_slab,),
        in_specs=[mat_b, mat_b, mat_c, mat_c,
                  pl.BlockSpec((None, 2, L + 1, k), lambda j: (j, 0, 0, 0)),
                  pl.BlockSpec((None, 2, k, LANES), lambda j: (j, 0, 0, 0)),
                  pl.BlockSpec((1, LANES), lambda j: (0, j))],
        out_specs=[pl.BlockSpec((None, L * LANES, L * LANES), lambda j: (j, 0, 0)),
                   pl.BlockSpec((None, L * LANES, 2 * k), lambda j: (j, 0, 0)),
                   pl.BlockSpec((None, 2 * k, L * LANES), lambda j: (j, 0, 0))],
        out_shape=[jax.ShapeDtypeStruct((n_slab, L * LANES, L * LANES), BF16),
                   jax.ShapeDtypeStruct((n_slab, L * LANES, 2 * k), BF16),
                   jax.ShapeDtypeStruct((n_slab, 2 * k, L * LANES), BF16)],
        compiler_params=_cparams("parallel"),
        name="s5_operators",
    )(slab_b(bb_re), slab_b(bb_im), slab_c(C_re.astype(F32)), slab_c(C_im.astype(F32)),
      p_tab, pt_tab, D_skip.astype(F32).reshape(1, n_g * n_h))

    a_op = pw[:, L].reshape(2, n_slab, k).transpose(1, 0, 2).reshape(1, n_slab * 2 * k)
    return t_op, w_op, v_op, a_op


def _chunk_rows(piece_refs):
    return jnp.concatenate([r[...] for r in piece_refs], axis=1)


def _s5_increment_body(*refs):
    w_ref, z_ref = refs[-2], refs[-1]
    z_ref[...] = jnp.dot(_chunk_rows(refs[:-2]), w_ref[...], preferred_element_type=F32)


def _s5_scan_body(z_ref, a_ref, o_ref, st_ref, *, tc):
    @pl.when(pl.program_id(2) == 0)
    def _():
        st_ref[...] = jnp.zeros_like(st_ref)

    half = z_ref.shape[1] // 2
    a_re, a_im = a_ref[:, :half], a_ref[:, half:]

    def step(i, carry):
        re, im = carry
        base = pl.multiple_of(i * SUBLANES, SUBLANES)
        inc = z_ref[pl.ds(base, SUBLANES), :]
        before_re, before_im = [], []
        for r in range(SUBLANES):
            before_re.append(re)
            before_im.append(im)
            re, im = (a_re * re - a_im * im + inc[r:r + 1, :half],
                      a_re * im + a_im * re + inc[r:r + 1, half:])
        o_ref[pl.ds(base, SUBLANES), :half] = jnp.concatenate(before_re, axis=0)
        o_ref[pl.ds(base, SUBLANES), half:] = jnp.concatenate(before_im, axis=0)
        return re, im

    re, im = lax.fori_loop(0, tc // SUBLANES, step, (st_ref[0:1, :half], st_ref[0:1, half:]))
    st_ref[0:1, :half] = re
    st_ref[0:1, half:] = im


def _s5_output_body(*refs):
    x_ref, t_ref, v_ref, o_ref = refs[-4:]
    y = jnp.dot(_chunk_rows(refs[:-4]), t_ref[...], preferred_element_type=F32)
    y = y + jnp.dot(x_ref[...].astype(BF16), v_ref[...], preferred_element_type=F32)
    o_ref[...] = jax.nn.gelu(y).astype(o_ref.dtype)


def _s5_glu_body(*refs):
    w_ref, b_ref, o_ref = refs[-3:]
    y = _chunk_rows(refs[:-3])
    gate = jax.nn.sigmoid(jnp.dot(y, w_ref[...], preferred_element_type=F32) + b_ref[...])
    o_ref[...] = (y.astype(F32) * gate).astype(o_ref.dtype)


def _s5_branch(u, ops, w_glu, b_glu, bsz, seq):
    t_op, w_op, v_op, a_op = ops
    n_slab = t_op.shape[0]
    L = S5_CHUNK
    width = u.shape[1]
    cw, sw = t_op.shape[1], w_op.shape[2]
    nc = bsz * seq // L
    ncb = seq // L
    uu = u.reshape(nc, L * width)
    tm = min(512, nc)
    pieces = [pl.BlockSpec((tm, LANES), lambda j, i, s=s: (i, s * n_slab + j)) for s in range(L)]

    z = pl.pallas_call(
        _s5_increment_body,
        grid=(n_slab, nc // tm),
        in_specs=pieces + [pl.BlockSpec((None, cw, sw), lambda j, i: (j, 0, 0))],
        out_specs=pl.BlockSpec((tm, sw), lambda j, i: (i, j)),
        out_shape=jax.ShapeDtypeStruct((nc, n_slab * sw), F32),
        compiler_params=_cparams("parallel", "parallel"),
        name="s5_increment",
    )(*([uu] * L), w_op)

    tc = min(128, ncb)
    nt = ncb // tc
    blk = pl.BlockSpec((tc, sw), lambda b, j, t: (b * nt + t, j))
    xprev = pl.pallas_call(
        functools.partial(_s5_scan_body, tc=tc),
        grid=(bsz, n_slab, nt),
        in_specs=[blk, pl.BlockSpec((1, sw), lambda b, j, t: (0, j))],
        out_specs=blk,
        out_shape=jax.ShapeDtypeStruct((nc, n_slab * sw), F32),
        scratch_shapes=[pltpu.VMEM((SUBLANES, sw), F32)],
        compiler_params=_cparams("parallel", "parallel", "arbitrary"),
        name="s5_scan",
    )(z, a_op)

    ys = pl.pallas_call(
        _s5_output_body,
        grid=(n_slab, nc // tm),
        in_specs=pieces + [pl.BlockSpec((tm, sw), lambda j, i: (i, j)),
                           pl.BlockSpec((None, cw, cw), lambda j, i: (j, 0, 0)),
                           pl.BlockSpec((None, sw, cw), lambda j, i: (j, 0, 0))],
        out_specs=pl.BlockSpec((tm, cw), lambda j, i: (i, j)),
        out_shape=jax.ShapeDtypeStruct((nc, n_slab * cw), BF16),
        compiler_params=_cparams("parallel", "parallel"),
        name="s5_output",
    )(*([uu] * L), xprev, t_op, v_op)

    tok = [pl.BlockSpec((tm, LANES), lambda i, t, j=j: (i, j * L + t)) for j in range(n_slab)]
    out = pl.pallas_call(
        _s5_glu_body,
        grid=(nc // tm, L),
        in_specs=tok + [pl.BlockSpec((width, width), lambda i, t: (0, 0)),
                        pl.BlockSpec((1, width), lambda i, t: (0, 0))],
        out_specs=pl.BlockSpec((tm, width), lambda i, t: (i, t)),
        out_shape=jax.ShapeDtypeStruct((nc, L * width), BF16),
        compiler_params=_cparams("parallel", "parallel"),
        name="s5_glu",
    )(*([ys] * n_slab), w_glu.astype(BF16), b_glu.reshape(1, width).astype(F32))
    return out.reshape(bsz * seq, width)


def _row(v):
    return v.reshape(1, -1).astype(F32)


def _layer(x, mem_n, p, bsz, seq, mem_len):
    d = x.shape[1]
    aw = p["w_attn_up"].shape[0]
    heads = aw // ATTN_HEAD_DIM
    sw = p["w_ssm_up"].shape[0]
    off_f = 3 * aw
    off_u = off_f + heads
    off_g = off_u + sw

    h = _rmsnorm([x], p["g_mix"], BF16, tm=256)
    w_in = p["w_in"]
    q_scale = jnp.concatenate([jnp.full((aw,), LOG2E * ATTN_HEAD_DIM ** -0.5, F32),
                               jnp.ones((2 * aw,), F32)])
    w_qkv = (w_in[:, :off_f] * q_scale).astype(BF16)
    w_f = jnp.pad(w_in[:, off_f:off_u], ((0, 0), (0, LANES - heads))).astype(BF16)
    w_u = w_in[:, off_u:off_g].astype(BF16)
    w_g = w_in[:, off_g:].astype(BF16)

    qkv = _fused_matmul([(h, w_qkv, 0)], _ep_plain, BF16, n=3 * aw, tm=1024, tn=1024,
                        name="qkv_proj")
    b_f = jnp.pad(p["b_f"].astype(F32), (0, LANES - heads)).reshape(1, LANES)
    log_f = _fused_matmul([(h, w_f, 0)], _ep_log_sigmoid, F32, n=LANES, tm=1024, tn=LANES,
                          aux=[(b_f, 0)], name="forget_proj")
    fox = _fox_attention(qkv, _forget_bias(log_f, bsz, seq, heads), bsz, seq, heads, tq=1024)

    u = _fused_matmul([(h, w_u, 0)], _ep_plain, BF16, n=sw, tm=1024, tn=1024, name="ssm_in_proj")
    ops = _s5_operators(p["A_re"], p["A_im"], p["log_dt"], p["B_re"], p["B_im"],
                        p["C_re"], p["C_im"], p["D_skip"])
    y = _s5_branch(u, ops, p["w_glu"], p["b_glu"], bsz, seq)

    tn_mix = 256

    def pair_tiles(a):
        rows = a.shape[0]
        return (a.reshape(rows, N_BRANCH, d // tn_mix, tn_mix).transpose(0, 2, 1, 3)
                .reshape(rows, N_BRANCH * d))

    merged = _fused_matmul(
        [(h, pair_tiles(w_g), 0, N_BRANCH), (fox, p["w_attn_up"].astype(BF16), 0),
         (y, p["w_ssm_up"].astype(BF16), 0)],
        _ep_gated_merge, BF16, n=d, tm=1024, tn=tn_mix,
        aux=[(pair_tiles(_row(p["b_gate"])), 0, N_BRANCH)], name="gated_merge")
    x = _fused_matmul([(merged, p["w_out"].astype(BF16), 0)], _ep_add_res, F32, n=d,
                      tm=1024, tn=512, res=x, name="mixer_out_proj")

    xw = p["wq_x"].shape[1]
    wq = (p["wq_x"] * (xw // XATTN_HEADS) ** -0.5).astype(BF16)
    w_kv = jnp.concatenate([p["wk_x"], p["wv_x"]], axis=1).astype(BF16)
    kv = _fused_matmul([(mem_n, w_kv, 0)], _ep_plain, BF16, n=2 * xw, tm=512, tn=1024,
                       name="xattn_kv_proj")
    x, hm = _xattn_block(x, p["g_xattn"], wq, kv, p["wo_x"].astype(BF16), p["g_mlp"],
                         seq, mem_len, tm=256)

    dff = p["w_ff1"].shape[1]
    hid = _fused_matmul([(hm, p["w_ff1"].astype(BF16), 0)], _ep_relu2, BF16, n=dff,
                        tm=1024, tn=1024, name="mlp_up")
    x = _fused_matmul([(hid, p["w_ff2"].astype(BF16), 0)], _ep_add_res, F32, n=d,
                      tm=1024, tn=512, tk=4096, res=x, name="mlp_down")
    return x


_LAYER_PARAMS = ("g_mix", "w_in", "b_f", "b_gate", "A_re", "A_im", "log_dt", "B_re", "B_im",
                 "C_re", "C_im", "D_skip", "w_glu", "b_glu", "w_attn_up", "w_ssm_up", "w_out",
                 "g_xattn", "g_mem", "wq_x", "wk_x", "wv_x", "wo_x", "g_mlp", "w_ff1", "w_ff2")


def kernel(x, mem, g_mix, w_in, b_f, b_gate, A_re, A_im, log_dt, B_re, B_im, C_re, C_im, D_skip, w_glu, b_glu, w_attn_up, w_ssm_up, w_out, g_xattn, g_mem, wq_x, wk_x, wv_x, wo_x, g_mlp, w_ff1, w_ff2, g_final):
    stacked = dict(zip(_LAYER_PARAMS, (g_mix, w_in, b_f, b_gate, A_re, A_im, log_dt, B_re, B_im,
                                       C_re, C_im, D_skip, w_glu, b_glu, w_attn_up, w_ssm_up,
                                       w_out, g_xattn, g_mem, wq_x, wk_x, wv_x, wo_x, g_mlp,
                                       w_ff1, w_ff2)))
    bsz, seq, d = x.shape
    mem_len = mem.shape[1]
    xt = x.reshape(bsz * seq, d)
    mem2 = mem.reshape(bsz * mem_len, d)
    for l in range(g_mix.shape[0]):
        p = {k: v[l] for k, v in stacked.items()}
        mem_n = _rmsnorm([mem2], p["g_mem"], BF16, tm=256)
        xt = _layer(xt, mem_n, p, bsz, seq, mem_len)
    out = _rmsnorm([xt], g_final, x.dtype, tm=256)
    return out.reshape(bsz, seq, d)
```

```python
import functools
import math

import jax
import jax.numpy as jnp
from jax import lax
from jax.experimental import pallas as pl
from jax.experimental.pallas import tpu as pltpu

F32 = jnp.float32
BF16 = jnp.bfloat16

V7X_VMEM_LIMIT_BYTES = 56 * 1024 * 1024
LANES = 128
SUBLANES = 8

EPS = 1e-6
NEG_INF = -1e30
LOG2E = math.log2(math.e)
N_BIAS_PIECES = 3
ATTN_HEAD_DIM = 128
SSM_GROUP = 16
SSM_STATE = 64
S5_CHUNK = 16
XATTN_HEADS = 4
N_BRANCH = 2


def _cparams(*sem):
    return pltpu.CompilerParams(dimension_semantics=sem,
                                vmem_limit_bytes=V7X_VMEM_LIMIT_BYTES)


def _rms_scale(x, g):
    return (x * lax.rsqrt(jnp.mean(x * x, axis=-1, keepdims=True) + EPS)) * g


def _rmsnorm_body(*refs, n_in):
    g_ref, o_ref = refs[n_in], refs[n_in + 1]
    xs = refs[0][...].astype(F32)
    for r in refs[1:n_in]:
        xs = xs + r[...].astype(F32)
    o_ref[...] = _rms_scale(xs, g_ref[...]).astype(o_ref.dtype)


def _rmsnorm(xs, g, out_dtype, tm):
    m, d = xs[0].shape
    tm = min(tm, m)
    row = pl.BlockSpec((tm, d), lambda i: (i, 0))
    return pl.pallas_call(
        functools.partial(_rmsnorm_body, n_in=len(xs)),
        grid=(m // tm,),
        in_specs=[row] * len(xs) + [pl.BlockSpec((1, d), lambda i: (0, 0))],
        out_specs=row,
        out_shape=jax.ShapeDtypeStruct((m, d), out_dtype),
        compiler_params=_cparams("parallel"),
        name="rmsnorm",
    )(*xs, g.reshape(1, d).astype(F32))


def _mm_body(*refs, n_pairs, n_aux, has_res, epilogue):
    accs = [jnp.dot(refs[2 * p][...], refs[2 * p + 1][...], preferred_element_type=F32)
            for p in range(n_pairs)]
    pos = 2 * n_pairs
    aux = [refs[pos + i][...] for i in range(n_aux)]
    pos += n_aux
    res = refs[pos][...].astype(F32) if has_res else None
    o_ref = refs[-1]
    o_ref[...] = epilogue(accs, aux, res).astype(o_ref.dtype)


def _mm_ksplit_body(a_ref, b_ref, *rest, n_aux, has_res, epilogue, nk):
    acc_ref = rest[-1]
    o_ref = rest[-2]
    k = pl.program_id(2)

    @pl.when(k == 0)
    def _():
        acc_ref[...] = jnp.zeros_like(acc_ref)

    acc_ref[...] += jnp.dot(a_ref[...], b_ref[...], preferred_element_type=F32)

    @pl.when(k == nk - 1)
    def _():
        aux = [rest[i][...] for i in range(n_aux)]
        res = rest[n_aux][...].astype(F32) if has_res else None
        o_ref[...] = epilogue([acc_ref[...]], aux, res).astype(o_ref.dtype)


def _fused_matmul(pairs, epilogue, out_dtype, *, n, tm, tn, tk=None, aux=(), res=None,
                  name="fused_matmul"):
    m = pairs[0][0].shape[0]
    tm, tn = min(tm, m), min(tn, n)
    n_aux, has_res = len(aux), res is not None
    out_shape = jax.ShapeDtypeStruct((m, n), out_dtype)
    operands, in_specs = [], []

    if tk is None or tk >= pairs[0][0].shape[1]:
        for lhs, rhs, off, *mult in pairs:
            kp = lhs.shape[1]
            width = tn * (mult[0] if mult else 1)
            operands += [lhs, rhs]
            in_specs += [pl.BlockSpec((tm, kp), lambda i, j: (i, 0)),
                         pl.BlockSpec((kp, width), lambda i, j, off=off: (0, j + off))]
        for vec, off, *mult in aux:
            width = tn * (mult[0] if mult else 1)
            operands.append(vec)
            in_specs.append(pl.BlockSpec((1, width), lambda i, j, off=off: (0, j + off)))
        if has_res:
            operands.append(res)
            in_specs.append(pl.BlockSpec((tm, tn), lambda i, j: (i, j)))
        return pl.pallas_call(
            functools.partial(_mm_body, n_pairs=len(pairs), n_aux=n_aux, has_res=has_res,
                              epilogue=epilogue),
            grid=(m // tm, n // tn),
            in_specs=in_specs,
            out_specs=pl.BlockSpec((tm, tn), lambda i, j: (i, j)),
            out_shape=out_shape,
            compiler_params=_cparams("parallel", "parallel"),
            name=name,
        )(*operands)

    (lhs, rhs, off), = pairs
    nk = lhs.shape[1] // tk
    operands = [lhs, rhs]
    in_specs = [pl.BlockSpec((tm, tk), lambda i, j, k: (i, k)),
                pl.BlockSpec((tk, tn), lambda i, j, k, off=off: (k, j + off))]
    for vec, voff in aux:
        operands.append(vec)
        in_specs.append(pl.BlockSpec((1, tn), lambda i, j, k, voff=voff: (0, j + voff)))
    if has_res:
        operands.append(res)
        in_specs.append(pl.BlockSpec((tm, tn), lambda i, j, k: (i, j)))
    return pl.pallas_call(
        functools.partial(_mm_ksplit_body, n_aux=n_aux, has_res=has_res, epilogue=epilogue,
                          nk=nk),
        grid=(m // tm, n // tn, nk),
        in_specs=in_specs,
        out_specs=pl.BlockSpec((tm, tn), lambda i, j, k: (i, j)),
        out_shape=out_shape,
        scratch_shapes=[pltpu.VMEM((tm, tn), F32)],
        compiler_params=_cparams("parallel", "parallel", "arbitrary"),
        name=name,
    )(*operands)


def _realign_cast_body(a_ref, b_ref, o_ref, *, shift):
    width = a_ref.shape[1]
    x = jnp.concatenate([a_ref[...], b_ref[...]], axis=1)
    o_ref[...] = x[:, shift:shift + width].astype(o_ref.dtype)


def _realign_cast(w, start, n_cols, tile, out_block, tr):
    rows = w.shape[0]
    base = start // LANES * LANES
    assert base % tile == 0 and n_cols % tile == 0 and rows % tr == 0
    return pl.pallas_call(
        functools.partial(_realign_cast_body, shift=start - base),
        grid=(rows // tr, n_cols // tile),
        in_specs=[pl.BlockSpec((tr, tile), lambda i, j: (i, base // tile + j)),
                  pl.BlockSpec((tr, LANES),
                               lambda i, j: (i, base // LANES + (tile // LANES) * (j + 1)))],
        out_specs=pl.BlockSpec((tr, tile), lambda i, j: (i, out_block(j))),
        out_shape=jax.ShapeDtypeStruct((rows, n_cols), BF16),
        compiler_params=_cparams("parallel", "parallel"),
        name="realign_cast",
    )(w, w)


def _ep_plain(accs, aux, res):
    return accs[0]


def _ep_add_res(accs, aux, res):
    return res + accs[0]


def _ep_log_sigmoid(accs, aux, res):
    z = accs[0] + aux[0]
    return jnp.minimum(z, 0.0) - jnp.log1p(jnp.exp(-jnp.abs(z)))


def _ep_gated_merge(accs, aux, res):
    tn = accs[1].shape[1]
    gates = jax.nn.sigmoid(accs[0] + aux[0])
    return gates[:, :tn] * accs[1] + gates[:, tn:] * accs[2]


def _ep_relu2(accs, aux, res):
    r = jnp.maximum(accs[0], 0.0)
    return r * r


def _forget_bias_body(x_ref, o_ref, carry_ref, *, blk, nblk, heads):
    @pl.when(pl.program_id(1) == 0)
    def _():
        carry_ref[...] = jnp.zeros_like(carry_ref)

    r = lax.broadcasted_iota(jnp.int32, (blk, blk), 0)
    c = lax.broadcasted_iota(jnp.int32, (blk, blk), 1)
    tri = (c <= r).astype(F32)
    lane = lax.broadcasted_iota(jnp.int32, (blk, LANES), 1)

    def step(i, carry):
        base = pl.multiple_of(i * blk, blk)
        cs = jnp.dot(tri, x_ref[pl.ds(base, blk), :], preferred_element_type=F32,
                     precision=lax.Precision.HIGHEST) + carry
        for h in range(heads):
            bias = jnp.broadcast_to(cs[:, h:h + 1] * (-LOG2E), (blk, LANES))
            hi = bias.astype(BF16).astype(F32)
            mid = (bias - hi).astype(BF16).astype(F32)
            lo = bias - hi - mid
            pieces = jnp.where(lane == 0, hi, jnp.where(lane == 1, mid,
                                                        jnp.where(lane == 2, lo, 0.0)))
            o_ref[pl.ds(base, blk), h * LANES:(h + 1) * LANES] = pieces.astype(o_ref.dtype)
        return cs[blk - 1:blk, :]

    carry_ref[0:1, :] = lax.fori_loop(0, nblk, step, carry_ref[0:1, :])


def _forget_bias(log_f, bsz, seq, heads):
    blk = min(256, seq)
    tt = min(1024, seq)
    nt = seq // tt
    return pl.pallas_call(
        functools.partial(_forget_bias_body, blk=blk, nblk=tt // blk, heads=heads),
        grid=(bsz, nt),
        in_specs=[pl.BlockSpec((tt, LANES), lambda b, t: (b * nt + t, 0))],
        out_specs=pl.BlockSpec((tt, heads * LANES), lambda b, t: (b * nt + t, 0)),
        out_shape=jax.ShapeDtypeStruct((bsz * seq, heads * LANES), BF16),
        scratch_shapes=[pltpu.VMEM((SUBLANES, LANES), F32)],
        compiler_params=_cparams("parallel", "arbitrary"),
        name="forget_bias",
    )(log_f)


def _fox_body(q_ref, k_ref, v_ref, cp_ref, o_ref, kaug_ref, vaug_ref, s_ref, m_ref, acc_ref,
              *, tq, seq):
    qi = pl.program_id(2)
    dh = q_ref.shape[1]

    @pl.when(qi == 0)
    def _():
        def fill(j, _):
            rows = pl.ds(pl.multiple_of(j * tq, tq), tq)
            kaug_ref[rows, :dh] = k_ref[rows, :]
            kaug_ref[rows, dh:] = cp_ref[rows, :]
            vaug_ref[rows, :dh] = v_ref[rows, :]
            vaug_ref[rows, dh:] = jnp.ones((tq, dh), vaug_ref.dtype)
            return 0
        lax.fori_loop(0, seq // tq, fill, 0)

    lane = lax.broadcasted_iota(jnp.int32, (tq, dh), 1)
    q = jnp.concatenate([q_ref[...], (lane < N_BIAS_PIECES).astype(q_ref.dtype)], axis=1)

    def scores(j):
        rows = pl.ds(pl.multiple_of(j * tq, tq), tq)
        return lax.dot_general(q, kaug_ref[rows, :], (((1,), (1,)), ((), ())),
                               preferred_element_type=F32)

    def accumulate(s, j, masked):
        rows = pl.ds(pl.multiple_of(j * tq, tq), tq)
        if masked:
            row = lax.broadcasted_iota(jnp.int32, (tq, tq), 0)
            col = lax.broadcasted_iota(jnp.int32, (tq, tq), 1)
            s = jnp.where(col <= row, s, NEG_INF)
        m = m_ref[...]
        m_new = jnp.maximum(m, jnp.max(s, axis=-1, keepdims=True))
        m_ref[...] = m_new
        p = jnp.exp2(s - m_new)
        acc_ref[...] = jnp.exp2(m - m_new) * acc_ref[...] + jnp.dot(
            p.astype(vaug_ref.dtype), vaug_ref[rows, :], preferred_element_type=F32)

    def step(j, _):
        s = s_ref[...]
        s_ref[...] = scores(j)
        accumulate(s, j - 1, masked=False)
        return 0

    s_ref[...] = scores(0)
    m_ref[...] = jnp.full(m_ref.shape, NEG_INF, F32)
    acc_ref[...] = jnp.zeros(acc_ref.shape, F32)
    lax.fori_loop(1, qi + 1, step, 0)
    accumulate(s_ref[...], qi, masked=True)
    acc = acc_ref[...]
    o_ref[...] = (acc[:, :dh] / acc[:, dh:]).astype(o_ref.dtype)


def _fox_attention(qkv, cp, bsz, seq, heads, tq):
    dh = ATTN_HEAD_DIM
    tq = min(tq, seq)
    nq = seq // tq
    return pl.pallas_call(
        functools.partial(_fox_body, tq=tq, seq=seq),
        grid=(bsz, heads, nq),
        in_specs=[pl.BlockSpec((tq, dh), lambda b, h, i: (b * nq + i, h)),
                  pl.BlockSpec((seq, dh), lambda b, h, i: (b, heads + h)),
                  pl.BlockSpec((seq, dh), lambda b, h, i: (b, 2 * heads + h)),
                  pl.BlockSpec((seq, dh), lambda b, h, i: (b, h))],
        out_specs=pl.BlockSpec((tq, dh), lambda b, h, i: (b * nq + i, h)),
        out_shape=jax.ShapeDtypeStruct((bsz * seq, heads * dh), BF16),
        scratch_shapes=[pltpu.VMEM((seq, 2 * dh), BF16), pltpu.VMEM((seq, 2 * dh), BF16),
                        pltpu.VMEM((tq, tq), F32), pltpu.VMEM((tq, 1), F32),
                        pltpu.VMEM((tq, 2 * dh), F32)],
        compiler_params=_cparams("parallel", "parallel", "arbitrary"),
        name="fox_attention",
    )(qkv, qkv, qkv, cp)


def _xattn_block_body(x_ref, gx_ref, wq_ref, kv_ref, wo_ref, gn_ref, xo_ref, hn_ref):
    x = x_ref[...]
    width = wq_ref.shape[1]
    dh = width // XATTN_HEADS
    hx = _rms_scale(x, gx_ref[...]).astype(wq_ref.dtype)
    q = jnp.dot(hx, wq_ref[...], preferred_element_type=F32).astype(kv_ref.dtype)
    heads = []
    for hd in range(XATTN_HEADS):
        k = kv_ref[:, hd * dh:(hd + 1) * dh]
        v = kv_ref[:, width + hd * dh:width + (hd + 1) * dh]
        s = lax.dot_general(q[:, hd * dh:(hd + 1) * dh], k, (((1,), (1,)), ((), ())),
                            preferred_element_type=F32)
        p = jnp.exp(s - jnp.max(s, axis=-1, keepdims=True))
        l = jnp.sum(p, axis=-1, keepdims=True)
        o = jnp.dot(p.astype(v.dtype), v, preferred_element_type=F32) / l
        heads.append(o.astype(wo_ref.dtype))
    x2 = x + jnp.dot(jnp.concatenate(heads, axis=1), wo_ref[...], preferred_element_type=F32)
    xo_ref[...] = x2
    hn_ref[...] = _rms_scale(x2, gn_ref[...]).astype(hn_ref.dtype)


def _xattn_block(x, g_x, wq, kv, wo, g_next, seq, mem_len, tm):
    t, d = x.shape
    width = wq.shape[1]
    tm = min(tm, seq)
    nq = seq // tm
    row = pl.BlockSpec((tm, d), lambda i: (i, 0))
    vec = pl.BlockSpec((1, d), lambda i: (0, 0))
    resident = pl.Buffered(1)
    return pl.pallas_call(
        _xattn_block_body,
        grid=(t // tm,),
        in_specs=[row, vec,
                  pl.BlockSpec((d, width), lambda i: (0, 0), pipeline_mode=resident),
                  pl.BlockSpec((mem_len, 2 * width), lambda i: (i // nq, 0)),
                  pl.BlockSpec((width, d), lambda i: (0, 0), pipeline_mode=resident),
                  vec],
        out_specs=[row, row],
        out_shape=[jax.ShapeDtypeStruct((t, d), F32), jax.ShapeDtypeStruct((t, d), BF16)],
        compiler_params=_cparams("parallel"),
        name="xattn_block",
    )(x, g_x.reshape(1, d).astype(F32), wq, kv, wo, g_next.reshape(1, d).astype(F32))


def _s5_operators_body(bre_ref, bim_ref, cre_ref, cim_ref, p_ref, pt_ref, d_ref,
                        t_ref, w_ref, v_ref, *, L):
    hp = lax.Precision.HIGHEST
    bre, bim = bre_ref[...], bim_ref[...]
    cre, cim = cre_ref[...], cim_ref[...]
    k = bre.shape[1]
    row = lax.broadcasted_iota(jnp.int32, (LANES, LANES), 0)
    col = lax.broadcasted_iota(jnp.int32, (LANES, LANES), 1)
    lag_blocks = []
    for tau in range(L):
        pr, pi = p_ref[0, tau:tau + 1, :], p_ref[1, tau:tau + 1, :]
        xr, xi = bre * pr - bim * pi, bre * pi + bim * pr
        rows = slice((L - 1 - tau) * LANES, (L - tau) * LANES)
        w_ref[rows, :k] = xr.astype(w_ref.dtype)
        w_ref[rows, k:] = xi.astype(w_ref.dtype)
        d_tau = (jnp.dot(xr, cre, preferred_element_type=F32, precision=hp)
                 - jnp.dot(xi, cim, preferred_element_type=F32, precision=hp))
        if tau == 0:
            d_tau = d_tau + jnp.where(row == col, d_ref[...], 0.0)
        lag_blocks.append(d_tau.astype(t_ref.dtype))
    zero = jnp.zeros((LANES, LANES), t_ref.dtype)
    for s in range(L):
        for t in range(L):
            t_ref[s * LANES:(s + 1) * LANES, t * LANES:(t + 1) * LANES] = (
                lag_blocks[t - s] if t >= s else zero)
    for t in range(L):
        qr, qi = pt_ref[0, :, t + 1:t + 2], pt_ref[1, :, t + 1:t + 2]
        cols = slice(t * LANES, (t + 1) * LANES)
        v_ref[:k, cols] = (cre * qr - cim * qi).astype(v_ref.dtype)
        v_ref[k:, cols] = (-(cre * qi + cim * qr)).astype(v_ref.dtype)


def _s5_operators(A_re, A_im, log_dt, B_re, B_im, C_re, C_im, D_skip):
    n_g, n_p = A_re.shape
    n_h = B_re.shape[-1]
    L = S5_CHUNK
    gs = LANES // n_h
    n_slab = n_g // gs
    k = gs * n_p
    a_re, a_im = A_re.astype(F32), A_im.astype(F32)
    dt = jnp.exp(log_dt.astype(F32))[:, None]
    tau = jnp.arange(L + 1, dtype=F32)[:, None, None]
    mag = jnp.exp(tau * (dt * a_re))
    ang = tau * (dt * a_im)
    pw = jnp.stack([mag * jnp.cos(ang), mag * jnp.sin(ang)])
    lb_re, lb_im = pw[0, 1], pw[1, 1]
    den = a_re * a_re + a_im * a_im
    nr, ni = lb_re - 1.0, lb_im
    f_re = (nr * a_re + ni * a_im) / den
    f_im = (ni * a_re - nr * a_im) / den
    br, bi = B_re.astype(F32), B_im.astype(F32)
    bb_re = f_re[..., None] * br - f_im[..., None] * bi
    bb_im = f_re[..., None] * bi + f_im[..., None] * br

    eye = jnp.eye(gs, dtype=F32)

    def slab_b(x):
        x = x.reshape(n_slab, gs, n_p, n_h).transpose(0, 1, 3, 2)
        return (x[:, :, :, None, :] * eye[None, :, None, :, None]).reshape(n_slab, LANES, k)

    def slab_c(x):
        x = x.reshape(n_slab, gs, n_h, n_p).transpose(0, 1, 3, 2)
        return (x[:, :, :, None, :] * eye[None, :, None, :, None]).reshape(n_slab, k, LANES)

    p_tab = pw.reshape(2, L + 1, n_slab, k).transpose(2, 0, 1, 3)
    pt_tab = jnp.pad(pw.reshape(2, L + 1, n_slab, k).transpose(2, 0, 3, 1),
                     ((0, 0), (0, 0), (0, 0), (0, LANES - (L + 1))))
    mat_b = pl.BlockSpec((None, LANES, k), lambda j: (j, 0, 0))
    mat_c = pl.BlockSpec((None, k, LANES), lambda j: (j, 0, 0))
    t_op, w_op, v_op = pl.pallas_call(
        functools.partial(_s5_operators_body, L=L),
        grid=(n_slab,),
        in_specs=[mat_b, mat_b, mat_c, mat_c,
                  pl.BlockSpec((None, 2, L + 1, k), lambda j: (j, 0, 0, 0)),
                  pl.BlockSpec((None, 2, k, LANES), lambda j: (j, 0, 0, 0)),
                  pl.BlockSpec((1, LANES), lambda j: (0, j))],
        out_specs=[pl.BlockSpec((None, L * LANES, L * LANES), lambda j: (j, 0, 0)),
                   pl.BlockSpec((None, L * LANES, 2 * k), lambda j: (j, 0, 0)),
                   pl.BlockSpec((None, 2 * k, L * LANES), lambda j: (j, 0, 0))],
        out_shape=[jax.ShapeDtypeStruct((n_slab, L * LANES, L * LANES), BF16),
                   jax.ShapeDtypeStruct((n_slab, L * LANES, 2 * k), BF16),
                   jax.ShapeDtypeStruct((n_slab, 2 * k, L * LANES), BF16)],
        compiler_params=_cparams("parallel"),
        name="s5_operators",
    )(slab_b(bb_re), slab_b(bb_im), slab_c(C_re.astype(F32)), slab_c(C_im.astype(F32)),
      p_tab, pt_tab, D_skip.astype(F32).reshape(1, n_g * n_h))

    a_op = pw[:, L].reshape(2, n_slab, k).transpose(1, 0, 2).reshape(1, n_slab * 2 * k)
    return t_op, w_op, v_op, a_op


def _chunk_rows(piece_refs):
    return jnp.concatenate([r[...] for r in piece_refs], axis=1)


def _s5_increment_body(*refs):
    w_ref, z_ref = refs[-2], refs[-1]
    z_ref[...] = jnp.dot(_chunk_rows(refs[:-2]), w_ref[...], preferred_element_type=F32)


def _s5_scan_body(z_ref, a_ref, o_ref, st_ref, *, tc):
    @pl.when(pl.program_id(2) == 0)
    def _():
        st_ref[...] = jnp.zeros_like(st_ref)

    half = z_ref.shape[1] // 2
    a_re, a_im = a_ref[:, :half], a_ref[:, half:]

    def step(i, carry):
        re, im = carry
        base = pl.multiple_of(i * SUBLANES, SUBLANES)
        inc = z_ref[pl.ds(base, SUBLANES), :]
        before_re, before_im = [], []
        for r in range(SUBLANES):
            before_re.append(re)
            before_im.append(im)
            re, im = (a_re * re - a_im * im + inc[r:r + 1, :half],
                      a_re * im + a_im * re + inc[r:r + 1, half:])
        o_ref[pl.ds(base, SUBLANES), :half] = jnp.concatenate(before_re, axis=0)
        o_ref[pl.ds(base, SUBLANES), half:] = jnp.concatenate(before_im, axis=0)
        return re, im

    re, im = lax.fori_loop(0, tc // SUBLANES, step, (st_ref[0:1, :half], st_ref[0:1, half:]))
    st_ref[0:1, :half] = re
    st_ref[0:1, half:] = im


def _s5_output_body(*refs):
    x_ref, t_ref, v_ref, o_ref = refs[-4:]
    y = jnp.dot(_chunk_rows(refs[:-4]), t_ref[...], preferred_element_type=F32)
    y = y + jnp.dot(x_ref[...].astype(BF16), v_ref[...], preferred_element_type=F32)
    o_ref[...] = jax.nn.gelu(y).astype(o_ref.dtype)


def _s5_glu_body(*refs):
    w_ref, b_ref, o_ref = refs[-3:]
    y = _chunk_rows(refs[:-3])
    gate = jax.nn.sigmoid(jnp.dot(y, w_ref[...], preferred_element_type=F32) + b_ref[...])
    o_ref[...] = (y.astype(F32) * gate).astype(o_ref.dtype)


def _s5_branch(u, ops, w_glu, b_glu, bsz, seq):
    t_op, w_op, v_op, a_op = ops
    n_slab = t_op.shape[0]
    L = S5_CHUNK
    width = u.shape[1]
    cw, sw = t_op.shape[1], w_op.shape[2]
    nc = bsz * seq // L
    ncb = seq // L
    uu = u.reshape(nc, L * width)
    tm = min(512, nc)
    pieces = [pl.BlockSpec((tm, LANES), lambda j, i, s=s: (i, s * n_slab + j)) for s in range(L)]

    z = pl.pallas_call(
        _s5_increment_body,
        grid=(n_slab, nc // tm),
        in_specs=pieces + [pl.BlockSpec((None, cw, sw), lambda j, i: (j, 0, 0))],
        out_specs=pl.BlockSpec((tm, sw), lambda j, i: (i, j)),
        out_shape=jax.ShapeDtypeStruct((nc, n_slab * sw), F32),
        compiler_params=_cparams("parallel", "parallel"),
        name="s5_increment",
    )(*([uu] * L), w_op)

    tc = min(128, ncb)
    nt = ncb // tc
    blk = pl.BlockSpec((tc, sw), lambda b, j, t: (b * nt + t, j))
    xprev = pl.pallas_call(
        functools.partial(_s5_scan_body, tc=tc),
        grid=(bsz, n_slab, nt),
        in_specs=[blk, pl.BlockSpec((1, sw), lambda b, j, t: (0, j))],
        out_specs=blk,
        out_shape=jax.ShapeDtypeStruct((nc, n_slab * sw), F32),
        scratch_shapes=[pltpu.VMEM((SUBLANES, sw), F32)],
        compiler_params=_cparams("parallel", "parallel", "arbitrary"),
        name="s5_scan",
    )(z, a_op)

    ys = pl.pallas_call(
        _s5_output_body,
        grid=(n_slab, nc // tm),
        in_specs=pieces + [pl.BlockSpec((tm, sw), lambda j, i: (i, j)),
                           pl.BlockSpec((None, cw, cw), lambda j, i: (j, 0, 0)),
                           pl.BlockSpec((None, sw, cw), lambda j, i: (j, 0, 0))],
        out_specs=pl.BlockSpec((tm, cw), lambda j, i: (i, j)),
        out_shape=jax.ShapeDtypeStruct((nc, n_slab * cw), BF16),
        compiler_params=_cparams("parallel", "parallel"),
        name="s5_output",
    )(*([uu] * L), xprev, t_op, v_op)

    tok = [pl.BlockSpec((tm, LANES), lambda i, t, j=j: (i, j * L + t)) for j in range(n_slab)]
    out = pl.pallas_call(
        _s5_glu_body,
        grid=(nc // tm, L),
        in_specs=tok + [pl.BlockSpec((width, width), lambda i, t: (0, 0)),
                        pl.BlockSpec((1, width), lambda i, t: (0, 0))],
        out_specs=pl.BlockSpec((tm, width), lambda i, t: (i, t)),
        out_shape=jax.ShapeDtypeStruct((nc, L * width), BF16),
        compiler_params=_cparams("parallel", "parallel"),
        name="s5_glu",
    )(*([ys] * n_slab), w_glu.astype(BF16), b_glu.reshape(1, width).astype(F32))
    return out.reshape(bsz * seq, width)


def _row(v):
    return v.reshape(1, -1).astype(F32)


def _layer(x, mem_n, p, bsz, seq, mem_len):
    d = x.shape[1]
    aw = p["w_attn_up"].shape[0]
    heads = aw // ATTN_HEAD_DIM
    sw = p["w_ssm_up"].shape[0]
    off_f = 3 * aw
    off_u = off_f + heads
    off_g = off_u + sw

    h = _rmsnorm([x], p["g_mix"], BF16, tm=256)
    w_in = p["w_in"]
    q_scale = jnp.concatenate([jnp.full((aw,), LOG2E * ATTN_HEAD_DIM ** -0.5, F32),
                               jnp.ones((2 * aw,), F32)])
    w_qkv = (w_in[:, :off_f] * q_scale).astype(BF16)
    w_f = jnp.pad(w_in[:, off_f:off_u], ((0, 0), (0, LANES - heads))).astype(BF16)
    w_u = _realign_cast(w_in, off_u, sw, min(sw, 1024), lambda j: j, tr=512)
    tn_mix = 256
    n_mix = d // tn_mix
    w_g = _realign_cast(w_in, off_g, N_BRANCH * d, tn_mix,
                        lambda j: (j % n_mix) * N_BRANCH + j // n_mix, tr=2048)

    qkv = _fused_matmul([(h, w_qkv, 0)], _ep_plain, BF16, n=3 * aw, tm=1024, tn=1024,
                        name="qkv_proj")
    b_f = jnp.pad(p["b_f"].astype(F32), (0, LANES - heads)).reshape(1, LANES)
    log_f = _fused_matmul([(h, w_f, 0)], _ep_log_sigmoid, F32, n=LANES, tm=1024, tn=LANES,
                          aux=[(b_f, 0)], name="forget_proj")
    fox = _fox_attention(qkv, _forget_bias(log_f, bsz, seq, heads), bsz, seq, heads, tq=1024)

    u = _fused_matmul([(h, w_u, 0)], _ep_plain, BF16, n=sw, tm=1024, tn=1024, name="ssm_in_proj")
    ops = _s5_operators(p["A_re"], p["A_im"], p["log_dt"], p["B_re"], p["B_im"],
                        p["C_re"], p["C_im"], p["D_skip"])
    y = _s5_branch(u, ops, p["w_glu"], p["b_glu"], bsz, seq)

    b_gate = (p["b_gate"].astype(F32).reshape(N_BRANCH, n_mix, tn_mix).transpose(1, 0, 2)
              .reshape(1, N_BRANCH * d))
    merged = _fused_matmul(
        [(h, w_g, 0, N_BRANCH), (fox, p["w_attn_up"].astype(BF16), 0),
         (y, p["w_ssm_up"].astype(BF16), 0)],
        _ep_gated_merge, BF16, n=d, tm=1024, tn=tn_mix,
        aux=[(b_gate, 0, N_BRANCH)], name="gated_merge")
    x = _fused_matmul([(merged, p["w_out"].astype(BF16), 0)], _ep_add_res, F32, n=d,
                      tm=1024, tn=512, res=x, name="mixer_out_proj")

    xw = p["wq_x"].shape[1]
    wq = (p["wq_x"] * (xw // XATTN_HEADS) ** -0.5).astype(BF16)
    w_kv = jnp.concatenate([p["wk_x"], p["wv_x"]], axis=1).astype(BF16)
    kv = _fused_matmul([(mem_n, w_kv, 0)], _ep_plain, BF16, n=2 * xw, tm=512, tn=1024,
                       name="xattn_kv_proj")
    x, hm = _xattn_block(x, p["g_xattn"], wq, kv, p["wo_x"].astype(BF16), p["g_mlp"],
                         seq, mem_len, tm=256)

    dff = p["w_ff1"].shape[1]
    hid = _fused_matmul([(hm, p["w_ff1"].astype(BF16), 0)], _ep_relu2, BF16, n=dff,
                        tm=1024, tn=1024, name="mlp_up")
    x = _fused_matmul([(hid, p["w_ff2"].astype(BF16), 0)], _ep_add_res, F32, n=d,
                      tm=1024, tn=512, tk=4096, res=x, name="mlp_down")
    return x


_LAYER_PARAMS = ("g_mix", "w_in", "b_f", "b_gate", "A_re", "A_im", "log_dt", "B_re", "B_im",
                 "C_re", "C_im", "D_skip", "w_glu", "b_glu", "w_attn_up", "w_ssm_up", "w_out",
                 "g_xattn", "g_mem", "wq_x", "wk_x", "wv_x", "wo_x", "g_mlp", "w_ff1", "w_ff2")


def kernel(x, mem, g_mix, w_in, b_f, b_gate, A_re, A_im, log_dt, B_re, B_im, C_re, C_im, D_skip, w_glu, b_glu, w_attn_up, w_ssm_up, w_out, g_xattn, g_mem, wq_x, wk_x, wv_x, wo_x, g_mlp, w_ff1, w_ff2, g_final):
    stacked = dict(zip(_LAYER_PARAMS, (g_mix, w_in, b_f, b_gate, A_re, A_im, log_dt, B_re, B_im,
                                       C_re, C_im, D_skip, w_glu, b_glu, w_attn_up, w_ssm_up,
                                       w_out, g_xattn, g_mem, wq_x, wk_x, wv_x, wo_x, g_mlp,
                                       w_ff1, w_ff2)))
    bsz, seq, d = x.shape
    mem_len = mem.shape[1]
    xt = x.reshape(bsz * seq, d)
    mem2 = mem.reshape(bsz * mem_len, d)
    for l in range(g_mix.shape[0]):
        p = {k: v[l] for k, v in stacked.items()}
        mem_n = _rmsnorm([mem2], p["g_mem"], BF16, tm=256)
        xt = _layer(xt, mem_n, p, bsz, seq, mem_len)
    out = _rmsnorm([xt], g_final, x.dtype, tm=256)
    return out.reshape(bsz, seq, d)
```

```python
import functools
import math

import jax
import jax.numpy as jnp
from jax import lax
from jax.experimental import pallas as pl
from jax.experimental.pallas import tpu as pltpu

F32 = jnp.float32
BF16 = jnp.bfloat16

V7X_VMEM_LIMIT_BYTES = 56 * 1024 * 1024
LANES = 128
SUBLANES = 8

EPS = 1e-6
NEG_INF = -1e30
LOG2E = math.log2(math.e)
N_BIAS_PIECES = 3
ATTN_HEAD_DIM = 128
SSM_GROUP = 16
SSM_STATE = 64
S5_CHUNK = 16
XATTN_HEADS = 4
N_BRANCH = 2


def _cparams(*sem):
    return pltpu.CompilerParams(dimension_semantics=sem,
                                vmem_limit_bytes=V7X_VMEM_LIMIT_BYTES)


def _rms_scale(x, g):
    return (x * lax.rsqrt(jnp.mean(x * x, axis=-1, keepdims=True) + EPS)) * g


def _rmsnorm_body(*refs, n_in):
    g_ref, o_ref = refs[n_in], refs[n_in + 1]
    xs = refs[0][...].astype(F32)
    for r in refs[1:n_in]:
        xs = xs + r[...].astype(F32)
    o_ref[...] = _rms_scale(xs, g_ref[...]).astype(o_ref.dtype)


def _rmsnorm(xs, g, out_dtype, tm):
    m, d = xs[0].shape
    tm = min(tm, m)
    row = pl.BlockSpec((tm, d), lambda i: (i, 0))
    return pl.pallas_call(
        functools.partial(_rmsnorm_body, n_in=len(xs)),
        grid=(m // tm,),
        in_specs=[row] * len(xs) + [pl.BlockSpec((1, d), lambda i: (0, 0))],
        out_specs=row,
        out_shape=jax.ShapeDtypeStruct((m, d), out_dtype),
        compiler_params=_cparams("parallel"),
        name="rmsnorm",
    )(*xs, g.reshape(1, d).astype(F32))


def _mm_body(*refs, n_pairs, n_aux, has_res, epilogue):
    accs = [jnp.dot(refs[2 * p][...], refs[2 * p + 1][...], preferred_element_type=F32)
            for p in range(n_pairs)]
    pos = 2 * n_pairs
    aux = [refs[pos + i][...] for i in range(n_aux)]
    pos += n_aux
    res = refs[pos][...].astype(F32) if has_res else None
    o_ref = refs[-1]
    o_ref[...] = epilogue(accs, aux, res).astype(o_ref.dtype)


def _mm_ksplit_body(a_ref, b_ref, *rest, n_aux, has_res, epilogue, nk):
    acc_ref = rest[-1]
    o_ref = rest[-2]
    k = pl.program_id(2)

    @pl.when(k == 0)
    def _():
        acc_ref[...] = jnp.zeros_like(acc_ref)

    acc_ref[...] += jnp.dot(a_ref[...], b_ref[...], preferred_element_type=F32)

    @pl.when(k == nk - 1)
    def _():
        aux = [rest[i][...] for i in range(n_aux)]
        res = rest[n_aux][...].astype(F32) if has_res else None
        o_ref[...] = epilogue([acc_ref[...]], aux, res).astype(o_ref.dtype)


def _fused_matmul(pairs, epilogue, out_dtype, *, n, tm, tn, tk=None, aux=(), res=None,
                  name="fused_matmul"):
    m = pairs[0][0].shape[0]
    tm, tn = min(tm, m), min(tn, n)
    n_aux, has_res = len(aux), res is not None
    out_shape = jax.ShapeDtypeStruct((m, n), out_dtype)
    operands, in_specs = [], []

    if tk is None or tk >= pairs[0][0].shape[1]:
        for lhs, rhs, off, *mult in pairs:
            kp = lhs.shape[1]
            width = tn * (mult[0] if mult else 1)
            operands += [lhs, rhs]
            in_specs += [pl.BlockSpec((tm, kp), lambda i, j: (i, 0)),
                         pl.BlockSpec((kp, width), lambda i, j, off=off: (0, j + off))]
        for vec, off, *mult in aux:
            width = tn * (mult[0] if mult else 1)
            operands.append(vec)
            in_specs.append(pl.BlockSpec((1, width), lambda i, j, off=off: (0, j + off)))
        if has_res:
            operands.append(res)
            in_specs.append(pl.BlockSpec((tm, tn), lambda i, j: (i, j)))
        return pl.pallas_call(
            functools.partial(_mm_body, n_pairs=len(pairs), n_aux=n_aux, has_res=has_res,
                              epilogue=epilogue),
            grid=(m // tm, n // tn),
            in_specs=in_specs,
            out_specs=pl.BlockSpec((tm, tn), lambda i, j: (i, j)),
            out_shape=out_shape,
            compiler_params=_cparams("parallel", "parallel"),
            name=name,
        )(*operands)

    (lhs, rhs, off), = pairs
    nk = lhs.shape[1] // tk
    operands = [lhs, rhs]
    in_specs = [pl.BlockSpec((tm, tk), lambda i, j, k: (i, k)),
                pl.BlockSpec((tk, tn), lambda i, j, k, off=off: (k, j + off))]
    for vec, voff in aux:
        operands.append(vec)
        in_specs.append(pl.BlockSpec((1, tn), lambda i, j, k, voff=voff: (0, j + voff)))
    if has_res:
        operands.append(res)
        in_specs.append(pl.BlockSpec((tm, tn), lambda i, j, k: (i, j)))
    return pl.pallas_call(
        functools.partial(_mm_ksplit_body, n_aux=n_aux, has_res=has_res, epilogue=epilogue,
                          nk=nk),
        grid=(m // tm, n // tn, nk),
        in_specs=in_specs,
        out_specs=pl.BlockSpec((tm, tn), lambda i, j, k: (i, j)),
        out_shape=out_shape,
        scratch_shapes=[pltpu.VMEM((tm, tn), F32)],
        compiler_params=_cparams("parallel", "parallel", "arbitrary"),
        name=name,
    )(*operands)


W_IN_EDGE_ROWS = 16


def _transpose_cast_body(a_ref, b_ref, o_ref, *, shift, scale, n_scaled):
    rows = a_ref.shape[0]
    x = jnp.concatenate([a_ref[...], b_ref[...]], axis=0)[shift:shift + rows, :]
    if n_scaled:
        x = x * jnp.where(pl.program_id(0) < n_scaled, scale, 1.0)
    o_ref[...] = x.T.astype(o_ref.dtype)


def _transpose_cast(wt, start, n_cols, tile, out_block, tk, scale=1.0, n_scaled=0):
    k = wt.shape[1]
    base = start // tile * tile
    shift = start - base
    assert shift % SUBLANES == 0 and shift <= W_IN_EDGE_ROWS and n_cols % tile == 0
    assert start + n_cols + (W_IN_EDGE_ROWS - shift) <= wt.shape[0] or shift == 0
    edge_per_tile = tile // W_IN_EDGE_ROWS
    last_edge = wt.shape[0] // W_IN_EDGE_ROWS - 1
    return pl.pallas_call(
        functools.partial(_transpose_cast_body, shift=shift, scale=scale, n_scaled=n_scaled),
        grid=(n_cols // tile, k // tk),
        in_specs=[pl.BlockSpec((tile, tk), lambda j, c: (base // tile + j, c)),
                  pl.BlockSpec((W_IN_EDGE_ROWS, tk),
                               lambda j, c: (jnp.minimum((base // tile + j + 1) * edge_per_tile,
                                                         last_edge), c))],
        out_specs=pl.BlockSpec((tk, tile), lambda j, c: (c, out_block(j))),
        out_shape=jax.ShapeDtypeStruct((k, n_cols), BF16),
        compiler_params=_cparams("parallel", "parallel"),
        name="transpose_cast",
    )(wt, wt)


def _matmul_kpieces_body(*refs, n_pieces):
    res_ref, o_ref = refs[2 * n_pieces], refs[2 * n_pieces + 1]
    acc = res_ref[...].astype(F32)
    for p in range(n_pieces):
        acc = acc + jnp.dot(refs[p][...], refs[n_pieces + p][...], preferred_element_type=F32)
    o_ref[...] = acc.astype(o_ref.dtype)


def _matmul_kpieces(lhs, rhs, res, *, n_pieces, tm, tn, name):
    m, k = lhs.shape
    n = rhs.shape[1]
    kp = k // n_pieces
    tm, tn = min(tm, m), min(tn, n)
    in_specs = ([pl.BlockSpec((tm, kp), lambda i, j, p=p: (i, p)) for p in range(n_pieces)]
                + [pl.BlockSpec((kp, tn), lambda i, j, p=p: (p, j)) for p in range(n_pieces)]
                + [pl.BlockSpec((tm, tn), lambda i, j: (i, j))])
    return pl.pallas_call(
        functools.partial(_matmul_kpieces_body, n_pieces=n_pieces),
        grid=(m // tm, n // tn),
        in_specs=in_specs,
        out_specs=pl.BlockSpec((tm, tn), lambda i, j: (i, j)),
        out_shape=jax.ShapeDtypeStruct((m, n), res.dtype),
        compiler_params=_cparams("parallel", "parallel"),
        name=name,
    )(*([lhs] * n_pieces), *([rhs] * n_pieces), res)


def _ep_plain(accs, aux, res):
    return accs[0]


def _ep_add_res(accs, aux, res):
    return res + accs[0]


def _ep_log_sigmoid(accs, aux, res):
    z = accs[0] + aux[0]
    return jnp.minimum(z, 0.0) - jnp.log1p(jnp.exp(-jnp.abs(z)))


def _ep_gated_merge(accs, aux, res):
    tn = accs[1].shape[1]
    gates = jax.nn.sigmoid(accs[0] + aux[0])
    return gates[:, :tn] * accs[1] + gates[:, tn:] * accs[2]


def _ep_relu2(accs, aux, res):
    r = jnp.maximum(accs[0], 0.0)
    return r * r


def _forget_bias_body(x_ref, o_ref, carry_ref, *, blk, nblk, heads):
    @pl.when(pl.program_id(1) == 0)
    def _():
        carry_ref[...] = jnp.zeros_like(carry_ref)

    r = lax.broadcasted_iota(jnp.int32, (blk, blk), 0)
    c = lax.broadcasted_iota(jnp.int32, (blk, blk), 1)
    tri = (c <= r).astype(F32)
    lane = lax.broadcasted_iota(jnp.int32, (blk, LANES), 1)

    def step(i, carry):
        base = pl.multiple_of(i * blk, blk)
        cs = jnp.dot(tri, x_ref[pl.ds(base, blk), :], preferred_element_type=F32,
                     precision=lax.Precision.HIGHEST) + carry
        for h in range(heads):
            bias = jnp.broadcast_to(cs[:, h:h + 1] * (-LOG2E), (blk, LANES))
            hi = bias.astype(BF16).astype(F32)
            mid = (bias - hi).astype(BF16).astype(F32)
            lo = bias - hi - mid
            pieces = jnp.where(lane == 0, hi, jnp.where(lane == 1, mid,
                                                        jnp.where(lane == 2, lo, 0.0)))
            o_ref[pl.ds(base, blk), h * LANES:(h + 1) * LANES] = pieces.astype(o_ref.dtype)
        return cs[blk - 1:blk, :]

    carry_ref[0:1, :] = lax.fori_loop(0, nblk, step, carry_ref[0:1, :])


def _forget_bias(log_f, bsz, seq, heads):
    blk = min(256, seq)
    tt = min(1024, seq)
    nt = seq // tt
    return pl.pallas_call(
        functools.partial(_forget_bias_body, blk=blk, nblk=tt // blk, heads=heads),
        grid=(bsz, nt),
        in_specs=[pl.BlockSpec((tt, LANES), lambda b, t: (b * nt + t, 0))],
        out_specs=pl.BlockSpec((tt, heads * LANES), lambda b, t: (b * nt + t, 0)),
        out_shape=jax.ShapeDtypeStruct((bsz * seq, heads * LANES), BF16),
        scratch_shapes=[pltpu.VMEM((SUBLANES, LANES), F32)],
        compiler_params=_cparams("parallel", "arbitrary"),
        name="forget_bias",
    )(log_f)


def _fox_body(q_ref, k_ref, v_ref, cp_ref, o_ref, kaug_ref, vaug_ref, s_ref, m_ref, acc_ref,
              *, tq, seq):
    qi = pl.program_id(2)
    dh = q_ref.shape[1]

    @pl.when(qi == 0)
    def _():
        def fill(j, _):
            rows = pl.ds(pl.multiple_of(j * tq, tq), tq)
            kaug_ref[rows, :dh] = k_ref[rows, :]
            kaug_ref[rows, dh:] = cp_ref[rows, :]
            vaug_ref[rows, :dh] = v_ref[rows, :]
            vaug_ref[rows, dh:] = jnp.ones((tq, dh), vaug_ref.dtype)
            return 0
        lax.fori_loop(0, seq // tq, fill, 0)

    lane = lax.broadcasted_iota(jnp.int32, (tq, dh), 1)
    q = jnp.concatenate([q_ref[...], (lane < N_BIAS_PIECES).astype(q_ref.dtype)], axis=1)

    def scores(j):
        rows = pl.ds(pl.multiple_of(j * tq, tq), tq)
        return lax.dot_general(q, kaug_ref[rows, :], (((1,), (1,)), ((), ())),
                               preferred_element_type=F32)

    def accumulate(s, j, masked):
        rows = pl.ds(pl.multiple_of(j * tq, tq), tq)
        if masked:
            row = lax.broadcasted_iota(jnp.int32, (tq, tq), 0)
            col = lax.broadcasted_iota(jnp.int32, (tq, tq), 1)
            s = jnp.where(col <= row, s, NEG_INF)
        m = m_ref[...]
        m_new = jnp.maximum(m, jnp.max(s, axis=-1, keepdims=True))
        m_ref[...] = m_new
        p = jnp.exp2(s - m_new)
        acc_ref[...] = jnp.exp2(m - m_new) * acc_ref[...] + jnp.dot(
            p.astype(vaug_ref.dtype), vaug_ref[rows, :], preferred_element_type=F32)

    def step(j, _):
        s = s_ref[...]
        s_ref[...] = scores(j)
        accumulate(s, j - 1, masked=False)
        return 0

    s_ref[...] = scores(0)
    m_ref[...] = jnp.full(m_ref.shape, NEG_INF, F32)
    acc_ref[...] = jnp.zeros(acc_ref.shape, F32)
    lax.fori_loop(1, qi + 1, step, 0)
    accumulate(s_ref[...], qi, masked=True)
    acc = acc_ref[...]
    o_ref[...] = (acc[:, :dh] / acc[:, dh:]).astype(o_ref.dtype)


def _fox_attention(qkv, cp, bsz, seq, heads, tq):
    dh = ATTN_HEAD_DIM
    tq = min(tq, seq)
    nq = seq // tq
    return pl.pallas_call(
        functools.partial(_fox_body, tq=tq, seq=seq),
        grid=(bsz, heads, nq),
        in_specs=[pl.BlockSpec((tq, dh), lambda b, h, i: (b * nq + i, h)),
                  pl.BlockSpec((seq, dh), lambda b, h, i: (b, heads + h)),
                  pl.BlockSpec((seq, dh), lambda b, h, i: (b, 2 * heads + h)),
                  pl.BlockSpec((seq, dh), lambda b, h, i: (b, h))],
        out_specs=pl.BlockSpec((tq, dh), lambda b, h, i: (b * nq + i, h)),
        out_shape=jax.ShapeDtypeStruct((bsz * seq, heads * dh), BF16),
        scratch_shapes=[pltpu.VMEM((seq, 2 * dh), BF16), pltpu.VMEM((seq, 2 * dh), BF16),
                        pltpu.VMEM((tq, tq), F32), pltpu.VMEM((tq, 1), F32),
                        pltpu.VMEM((tq, 2 * dh), F32)],
        compiler_params=_cparams("parallel", "parallel", "arbitrary"),
        name="fox_attention",
    )(qkv, qkv, qkv, cp)


def _xattn_block_body(x_ref, gx_ref, wq_ref, kv_ref, wo_ref, gn_ref, xo_ref, hn_ref):
    x = x_ref[...]
    width = wq_ref.shape[1]
    dh = width // XATTN_HEADS
    hx = _rms_scale(x, gx_ref[...]).astype(wq_ref.dtype)
    q = jnp.dot(hx, wq_ref[...], preferred_element_type=F32).astype(kv_ref.dtype)
    heads = []
    for hd in range(XATTN_HEADS):
        k = kv_ref[:, hd * dh:(hd + 1) * dh]
        v = kv_ref[:, width + hd * dh:width + (hd + 1) * dh]
        s = lax.dot_general(q[:, hd * dh:(hd + 1) * dh], k, (((1,), (1,)), ((), ())),
                            preferred_element_type=F32)
        p = jnp.exp(s - jnp.max(s, axis=-1, keepdims=True))
        l = jnp.sum(p, axis=-1, keepdims=True)
        o = jnp.dot(p.astype(v.dtype), v, preferred_element_type=F32) / l
        heads.append(o.astype(wo_ref.dtype))
    x2 = x + jnp.dot(jnp.concatenate(heads, axis=1), wo_ref[...], preferred_element_type=F32)
    xo_ref[...] = x2
    hn_ref[...] = _rms_scale(x2, gn_ref[...]).astype(hn_ref.dtype)


def _xattn_block(x, g_x, wq, kv, wo, g_next, seq, mem_len, tm):
    t, d = x.shape
    width = wq.shape[1]
    tm = min(tm, seq)
    nq = seq // tm
    row = pl.BlockSpec((tm, d), lambda i: (i, 0))
    vec = pl.BlockSpec((1, d), lambda i: (0, 0))
    resident = pl.Buffered(1)
    return pl.pallas_call(
        _xattn_block_body,
        grid=(t // tm,),
        in_specs=[row, vec,
                  pl.BlockSpec((d, width), lambda i: (0, 0), pipeline_mode=resident),
                  pl.BlockSpec((mem_len, 2 * width), lambda i: (i // nq, 0)),
                  pl.BlockSpec((width, d), lambda i: (0, 0), pipeline_mode=resident),
                  vec],
        out_specs=[row, row],
        out_shape=[jax.ShapeDtypeStruct((t, d), F32), jax.ShapeDtypeStruct((t, d), BF16)],
        compiler_params=_cparams("parallel"),
        name="xattn_block",
    )(x, g_x.reshape(1, d).astype(F32), wq, kv, wo, g_next.reshape(1, d).astype(F32))


def _s5_operators_body(bre_ref, bim_ref, cre_ref, cim_ref, p_ref, pt_ref, d_ref,
                        t_ref, w_ref, v_ref, *, L):
    hp = lax.Precision.HIGHEST
    bre, bim = bre_ref[...], bim_ref[...]
    cre, cim = cre_ref[...], cim_ref[...]
    k = bre.shape[1]
    row = lax.broadcasted_iota(jnp.int32, (LANES, LANES), 0)
    col = lax.broadcasted_iota(jnp.int32, (LANES, LANES), 1)
    lag_blocks = []
    for tau in range(L):
        pr, pi = p_ref[0, tau:tau + 1, :], p_ref[1, tau:tau + 1, :]
        xr, xi = bre * pr - bim * pi, bre * pi + bim * pr
        rows = slice((L - 1 - tau) * LANES, (L - tau) * LANES)
        w_ref[rows, :k] = xr.astype(w_ref.dtype)
        w_ref[rows, k:] = xi.astype(w_ref.dtype)
        d_tau = (jnp.dot(xr, cre, preferred_element_type=F32, precision=hp)
                 - jnp.dot(xi, cim, preferred_element_type=F32, precision=hp))
        if tau == 0:
            d_tau = d_tau + jnp.where(row == col, d_ref[...], 0.0)
        lag_blocks.append(d_tau.astype(t_ref.dtype))
    zero = jnp.zeros((LANES, LANES), t_ref.dtype)
    for s in range(L):
        for t in range(L):
            t_ref[s * LANES:(s + 1) * LANES, t * LANES:(t + 1) * LANES] = (
                lag_blocks[t - s] if t >= s else zero)
    for t in range(L):
        qr, qi = pt_ref[0, :, t + 1:t + 2], pt_ref[1, :, t + 1:t + 2]
        cols = slice(t * LANES, (t + 1) * LANES)
        v_ref[:k, cols] = (cre * qr - cim * qi).astype(v_ref.dtype)
        v_ref[k:, cols] = (-(cre * qi + cim * qr)).astype(v_ref.dtype)


def _s5_operators(A_re, A_im, log_dt, B_re, B_im, C_re, C_im, D_skip):
    n_g, n_p = A_re.shape
    n_h = B_re.shape[-1]
    L = S5_CHUNK
    gs = LANES // n_h
    n_slab = n_g // gs
    k = gs * n_p
    a_re, a_im = A_re.astype(F32), A_im.astype(F32)
    dt = jnp.exp(log_dt.astype(F32))[:, None]
    tau = jnp.arange(L + 1, dtype=F32)[:, None, None]
    mag = jnp.exp(tau * (dt * a_re))
    ang = tau * (dt * a_im)
    pw = jnp.stack([mag * jnp.cos(ang), mag * jnp.sin(ang)])
    lb_re, lb_im = pw[0, 1], pw[1, 1]
    den = a_re * a_re + a_im * a_im
    nr, ni = lb_re - 1.0, lb_im
    f_re = (nr * a_re + ni * a_im) / den
    f_im = (ni * a_re - nr * a_im) / den
    br, bi = B_re.astype(F32), B_im.astype(F32)
    bb_re = f_re[..., None] * br - f_im[..., None] * bi
    bb_im = f_re[..., None] * bi + f_im[..., None] * br

    eye = jnp.eye(gs, dtype=F32)

    def slab_b(x):
        x = x.reshape(n_slab, gs, n_p, n_h).transpose(0, 1, 3, 2)
        return (x[:, :, :, None, :] * eye[None, :, None, :, None]).reshape(n_slab, LANES, k)

    def slab_c(x):
        x = x.reshape(n_slab, gs, n_h, n_p).transpose(0, 1, 3, 2)
        return (x[:, :, :, None, :] * eye[None, :, None, :, None]).reshape(n_slab, k, LANES)

    p_tab = pw.reshape(2, L + 1, n_slab, k).transpose(2, 0, 1, 3)
    pt_tab = jnp.pad(pw.reshape(2, L + 1, n_slab, k).transpose(2, 0, 3, 1),
                     ((0, 0), (0, 0), (0, 0), (0, LANES - (L + 1))))
    mat_b = pl.BlockSpec((None, LANES, k), lambda j: (j, 0, 0))
    mat_c = pl.BlockSpec((None, k, LANES), lambda j: (j, 0, 0))
    t_op, w_op, v_op = pl.pallas_call(
        functools.partial(_s5_operators_body, L=L),
        grid=(n_slab,),
        in_specs=[mat_b, mat_b, mat_c, mat_c,
                  pl.BlockSpec((None, 2, L + 1, k), lambda j: (j, 0, 0, 0)),
                  pl.BlockSpec((None, 2, k, LANES), lambda j: (j, 0, 0, 0)),
                  pl.BlockSpec((1, LANES), lambda j: (0, j))],
        out_specs=[pl.BlockSpec((None, L * LANES, L * LANES), lambda j: (j, 0, 0)),
                   pl.BlockSpec((None, L * LANES, 2 * k), lambda j: (j, 0, 0)),
                   pl.BlockSpec((None, 2 * k, L * LANES), lambda j: (j, 0, 0))],
        out_shape=[jax.ShapeDtypeStruct((n_slab, L * LANES, L * LANES), BF16),
                   jax.ShapeDtypeStruct((n_slab, L * LANES, 2 * k), BF16),
                   jax.ShapeDtypeStruct((n_slab, 2 * k, L * LANES), BF16)],
        compiler_params=_cparams("parallel"),
        name="s5_operators",
    )(slab_b(bb_re), slab_b(bb_im), slab_c(C_re.astype(F32)), slab_c(C_im.astype(F32)),
      p_tab, pt_tab, D_skip.astype(F32).reshape(1, n_g * n_h))

    a_op = pw[:, L].reshape(2, n_slab, k).transpose(1, 0, 2).reshape(1, n_slab * 2 * k)
    return t_op, w_op, v_op, a_op


def _chunk_rows(piece_refs):
    return jnp.concatenate([r[...] for r in piece_refs], axis=1)


def _s5_increment_body(*refs):
    w_ref, z_ref = refs[-2], refs[-1]
    z_ref[...] = jnp.dot(_chunk_rows(refs[:-2]), w_ref[...], preferred_element_type=F32)


def _s5_scan_body(z_ref, a_ref, o_ref, st_ref, *, tc):
    @pl.when(pl.program_id(2) == 0)
    def _():
        st_ref[...] = jnp.zeros_like(st_ref)

    half = z_ref.shape[1] // 2
    a_re, a_im = a_ref[:, :half], a_ref[:, half:]

    def step(i, carry):
        re, im = carry
        base = pl.multiple_of(i * SUBLANES, SUBLANES)
        inc = z_ref[pl.ds(base, SUBLANES), :]
        before_re, before_im = [], []
        for r in range(SUBLANES):
            before_re.append(re)
            before_im.append(im)
            re, im = (a_re * re - a_im * im + inc[r:r + 1, :half],
                      a_re * im + a_im * re + inc[r:r + 1, half:])
        o_ref[pl.ds(base, SUBLANES), :half] = jnp.concatenate(before_re, axis=0)
        o_ref[pl.ds(base, SUBLANES), half:] = jnp.concatenate(before_im, axis=0)
        return re, im

    re, im = lax.fori_loop(0, tc // SUBLANES, step, (st_ref[0:1, :half], st_ref[0:1, half:]))
    st_ref[0:1, :half] = re
    st_ref[0:1, half:] = im


def _s5_output_body(*refs):
    x_ref, t_ref, v_ref, o_ref = refs[-4:]
    y = jnp.dot(_chunk_rows(refs[:-4]), t_ref[...], preferred_element_type=F32)
    y = y + jnp.dot(x_ref[...].astype(BF16), v_ref[...], preferred_element_type=F32)
    o_ref[...] = jax.nn.gelu(y).astype(o_ref.dtype)


def _s5_glu_body(*refs):
    w_ref, b_ref, o_ref = refs[-3:]
    y = _chunk_rows(refs[:-3])
    gate = jax.nn.sigmoid(jnp.dot(y, w_ref[...], preferred_element_type=F32) + b_ref[...])
    o_ref[...] = (y.astype(F32) * gate).astype(o_ref.dtype)


def _s5_branch(u, ops, w_glu, b_glu, bsz, seq):
    t_op, w_op, v_op, a_op = ops
    n_slab = t_op.shape[0]
    L = S5_CHUNK
    width = u.shape[1]
    cw, sw = t_op.shape[1], w_op.shape[2]
    nc = bsz * seq // L
    ncb = seq // L
    uu = u.reshape(nc, L * width)
    tm = min(512, nc)
    pieces = [pl.BlockSpec((tm, LANES), lambda j, i, s=s: (i, s * n_slab + j)) for s in range(L)]

    z = pl.pallas_call(
        _s5_increment_body,
        grid=(n_slab, nc // tm),
        in_specs=pieces + [pl.BlockSpec((None, cw, sw), lambda j, i: (j, 0, 0))],
        out_specs=pl.BlockSpec((tm, sw), lambda j, i: (i, j)),
        out_shape=jax.ShapeDtypeStruct((nc, n_slab * sw), F32),
        compiler_params=_cparams("parallel", "parallel"),
        name="s5_increment",
    )(*([uu] * L), w_op)

    tc = min(128, ncb)
    nt = ncb // tc
    blk = pl.BlockSpec((tc, sw), lambda b, j, t: (b * nt + t, j))
    xprev = pl.pallas_call(
        functools.partial(_s5_scan_body, tc=tc),
        grid=(bsz, n_slab, nt),
        in_specs=[blk, pl.BlockSpec((1, sw), lambda b, j, t: (0, j))],
        out_specs=blk,
        out_shape=jax.ShapeDtypeStruct((nc, n_slab * sw), F32),
        scratch_shapes=[pltpu.VMEM((SUBLANES, sw), F32)],
        compiler_params=_cparams("parallel", "parallel", "arbitrary"),
        name="s5_scan",
    )(z, a_op)

    ys = pl.pallas_call(
        _s5_output_body,
        grid=(n_slab, nc // tm),
        in_specs=pieces + [pl.BlockSpec((tm, sw), lambda j, i: (i, j)),
                           pl.BlockSpec((None, cw, cw), lambda j, i: (j, 0, 0)),
                           pl.BlockSpec((None, sw, cw), lambda j, i: (j, 0, 0))],
        out_specs=pl.BlockSpec((tm, cw), lambda j, i: (i, j)),
        out_shape=jax.ShapeDtypeStruct((nc, n_slab * cw), BF16),
        compiler_params=_cparams("parallel", "parallel"),
        name="s5_output",
    )(*([uu] * L), xprev, t_op, v_op)

    tok = [pl.BlockSpec((tm, LANES), lambda i, t, j=j: (i, j * L + t)) for j in range(n_slab)]
    out = pl.pallas_call(
        _s5_glu_body,
        grid=(nc // tm, L),
        in_specs=tok + [pl.BlockSpec((width, width), lambda i, t: (0, 0)),
                        pl.BlockSpec((1, width), lambda i, t: (0, 0))],
        out_specs=pl.BlockSpec((tm, width), lambda i, t: (i, t)),
        out_shape=jax.ShapeDtypeStruct((nc, L * width), BF16),
        compiler_params=_cparams("parallel", "parallel"),
        name="s5_glu",
    )(*([ys] * n_slab), w_glu.astype(BF16), b_glu.reshape(1, width).astype(F32))
    return out.reshape(bsz * seq, width)


def _row(v):
    return v.reshape(1, -1).astype(F32)


def _layer(x, mem_n, p, bsz, seq, mem_len):
    d = x.shape[1]
    aw = p["w_attn_up"].shape[0]
    heads = aw // ATTN_HEAD_DIM
    sw = p["w_ssm_up"].shape[0]
    off_f = 3 * aw
    off_u = off_f + heads
    off_g = off_u + sw

    h = _rmsnorm([x], p["g_mix"], BF16, tm=256)
    w_in_t = p["w_in"].T
    tile = 256
    w_qkv = _transpose_cast(w_in_t, 0, off_f, tile, lambda j: j, tk=2048,
                            scale=LOG2E * ATTN_HEAD_DIM ** -0.5, n_scaled=aw // tile)
    w_f = jnp.pad(w_in_t[off_f:off_u].T, ((0, 0), (0, LANES - heads))).astype(BF16)
    w_u = _transpose_cast(w_in_t, off_u, sw, tile, lambda j: j, tk=2048)
    tn_mix = tile
    n_mix = d // tn_mix
    w_g = _transpose_cast(w_in_t, off_g, N_BRANCH * d, tn_mix,
                          lambda j: (j % n_mix) * N_BRANCH + j // n_mix, tk=2048)

    qkv = _fused_matmul([(h, w_qkv, 0)], _ep_plain, BF16, n=3 * aw, tm=1024, tn=1024,
                        name="qkv_proj")
    b_f = jnp.pad(p["b_f"].astype(F32), (0, LANES - heads)).reshape(1, LANES)
    log_f = _fused_matmul([(h, w_f, 0)], _ep_log_sigmoid, F32, n=LANES, tm=1024, tn=LANES,
                          aux=[(b_f, 0)], name="forget_proj")
    fox = _fox_attention(qkv, _forget_bias(log_f, bsz, seq, heads), bsz, seq, heads, tq=1024)

    u = _fused_matmul([(h, w_u, 0)], _ep_plain, BF16, n=sw, tm=1024, tn=1024, name="ssm_in_proj")
    ops = _s5_operators(p["A_re"], p["A_im"], p["log_dt"], p["B_re"], p["B_im"],
                        p["C_re"], p["C_im"], p["D_skip"])
    y = _s5_branch(u, ops, p["w_glu"], p["b_glu"], bsz, seq)

    b_gate = (p["b_gate"].astype(F32).reshape(N_BRANCH, n_mix, tn_mix).transpose(1, 0, 2)
              .reshape(1, N_BRANCH * d))
    merged = _fused_matmul(
        [(h, w_g, 0, N_BRANCH), (fox, p["w_attn_up"].astype(BF16), 0),
         (y, p["w_ssm_up"].astype(BF16), 0)],
        _ep_gated_merge, BF16, n=d, tm=1024, tn=tn_mix,
        aux=[(b_gate, 0, N_BRANCH)], name="gated_merge")
    x = _fused_matmul([(merged, p["w_out"].astype(BF16), 0)], _ep_add_res, F32, n=d,
                      tm=1024, tn=512, res=x, name="mixer_out_proj")

    xw = p["wq_x"].shape[1]
    wq = (p["wq_x"] * (xw // XATTN_HEADS) ** -0.5).astype(BF16)
    w_kv = jnp.concatenate([p["wk_x"], p["wv_x"]], axis=1).astype(BF16)
    kv = _fused_matmul([(mem_n, w_kv, 0)], _ep_plain, BF16, n=2 * xw, tm=512, tn=1024,
                       name="xattn_kv_proj")
    x, hm = _xattn_block(x, p["g_xattn"], wq, kv, p["wo_x"].astype(BF16), p["g_mlp"],
                         seq, mem_len, tm=256)

    dff = p["w_ff1"].shape[1]
    hid = _fused_matmul([(hm, p["w_ff1"].astype(BF16), 0)], _ep_relu2, BF16, n=dff,
                        tm=1024, tn=1024, name="mlp_up")
    x = _matmul_kpieces(hid, p["w_ff2"].astype(BF16), x, n_pieces=4, tm=512, tn=256,
                        name="mlp_down")
    return x


_LAYER_PARAMS = ("g_mix", "w_in", "b_f", "b_gate", "A_re", "A_im", "log_dt", "B_re", "B_im",
                 "C_re", "C_im", "D_skip", "w_glu", "b_glu", "w_attn_up", "w_ssm_up", "w_out",
                 "g_xattn", "g_mem", "wq_x", "wk_x", "wv_x", "wo_x", "g_mlp", "w_ff1", "w_ff2")


def kernel(x, mem, g_mix, w_in, b_f, b_gate, A_re, A_im, log_dt, B_re, B_im, C_re, C_im, D_skip, w_glu, b_glu, w_attn_up, w_ssm_up, w_out, g_xattn, g_mem, wq_x, wk_x, wv_x, wo_x, g_mlp, w_ff1, w_ff2, g_final):
    stacked = dict(zip(_LAYER_PARAMS, (g_mix, w_in, b_f, b_gate, A_re, A_im, log_dt, B_re, B_im,
                                       C_re, C_im, D_skip, w_glu, b_glu, w_attn_up, w_ssm_up,
                                       w_out, g_xattn, g_mem, wq_x, wk_x, wv_x, wo_x, g_mlp,
                                       w_ff1, w_ff2)))
    bsz, seq, d = x.shape
    mem_len = mem.shape[1]
    xt = x.reshape(bsz * seq, d)
    mem2 = mem.reshape(bsz * mem_len, d)
    for l in range(g_mix.shape[0]):
        p = {k: v[l] for k, v in stacked.items()}
        mem_n = _rmsnorm([mem2], p["g_mem"], BF16, tm=256)
        xt = _layer(xt, mem_n, p, bsz, seq, mem_len)
    out = _rmsnorm([xt], g_final, x.dtype, tm=256)
    return out.reshape(bsz, seq, d)
```

```python
import functools
import math

import jax
import jax.numpy as jnp
from jax import lax
from jax.experimental import pallas as pl
from jax.experimental.pallas import tpu as pltpu

F32 = jnp.float32
BF16 = jnp.bfloat16

V7X_VMEM_LIMIT_BYTES = 56 * 1024 * 1024
LANES = 128
SUBLANES = 8

EPS = 1e-6
NEG_INF = -1e30
LOG2E = math.log2(math.e)
N_BIAS_PIECES = 3
ATTN_HEAD_DIM = 128
SSM_GROUP = 16
SSM_STATE = 64
S5_CHUNK = 16
XATTN_HEADS = 4
N_BRANCH = 2


def _cparams(*sem):
    return pltpu.CompilerParams(dimension_semantics=sem,
                                vmem_limit_bytes=V7X_VMEM_LIMIT_BYTES)


def _rms_scale(x, g):
    return (x * lax.rsqrt(jnp.mean(x * x, axis=-1, keepdims=True) + EPS)) * g


def _rmsnorm_body(*refs, n_in):
    g_ref, o_ref = refs[n_in], refs[n_in + 1]
    xs = refs[0][...].astype(F32)
    for r in refs[1:n_in]:
        xs = xs + r[...].astype(F32)
    o_ref[...] = _rms_scale(xs, g_ref[...]).astype(o_ref.dtype)


def _rmsnorm(xs, g, out_dtype, tm):
    m, d = xs[0].shape
    tm = min(tm, m)
    row = pl.BlockSpec((tm, d), lambda i: (i, 0))
    return pl.pallas_call(
        functools.partial(_rmsnorm_body, n_in=len(xs)),
        grid=(m // tm,),
        in_specs=[row] * len(xs) + [pl.BlockSpec((1, d), lambda i: (0, 0))],
        out_specs=row,
        out_shape=jax.ShapeDtypeStruct((m, d), out_dtype),
        compiler_params=_cparams("parallel"),
        name="rmsnorm",
    )(*xs, g.reshape(1, d).astype(F32))


def _mm_body(*refs, n_pairs, n_aux, has_res, epilogue):
    accs = [jnp.dot(refs[2 * p][...], refs[2 * p + 1][...], preferred_element_type=F32)
            for p in range(n_pairs)]
    pos = 2 * n_pairs
    aux = [refs[pos + i][...] for i in range(n_aux)]
    pos += n_aux
    res = refs[pos][...].astype(F32) if has_res else None
    o_ref = refs[-1]
    o_ref[...] = epilogue(accs, aux, res).astype(o_ref.dtype)


def _mm_ksplit_body(a_ref, b_ref, *rest, n_aux, has_res, epilogue, nk):
    acc_ref = rest[-1]
    o_ref = rest[-2]
    k = pl.program_id(2)

    @pl.when(k == 0)
    def _():
        acc_ref[...] = jnp.zeros_like(acc_ref)

    acc_ref[...] += jnp.dot(a_ref[...], b_ref[...], preferred_element_type=F32)

    @pl.when(k == nk - 1)
    def _():
        aux = [rest[i][...] for i in range(n_aux)]
        res = rest[n_aux][...].astype(F32) if has_res else None
        o_ref[...] = epilogue([acc_ref[...]], aux, res).astype(o_ref.dtype)


def _fused_matmul(pairs, epilogue, out_dtype, *, n, tm, tn, tk=None, aux=(), res=None,
                  name="fused_matmul"):
    m = pairs[0][0].shape[0]
    tm, tn = min(tm, m), min(tn, n)
    n_aux, has_res = len(aux), res is not None
    out_shape = jax.ShapeDtypeStruct((m, n), out_dtype)
    operands, in_specs = [], []

    if tk is None or tk >= pairs[0][0].shape[1]:
        for lhs, rhs, off, *opt in pairs:
            kp = lhs.shape[1]
            width = tn * (opt[0] if opt else 1)
            row_block = opt[1] if len(opt) > 1 else (lambda i: i)
            operands += [lhs, rhs]
            in_specs += [pl.BlockSpec((tm, kp), lambda i, j, rb=row_block: (rb(i), 0)),
                         pl.BlockSpec((kp, width), lambda i, j, off=off: (0, j + off))]
        for vec, off, *mult in aux:
            width = tn * (mult[0] if mult else 1)
            operands.append(vec)
            in_specs.append(pl.BlockSpec((1, width), lambda i, j, off=off: (0, j + off)))
        if has_res:
            operands.append(res)
            in_specs.append(pl.BlockSpec((tm, tn), lambda i, j: (i, j)))
        return pl.pallas_call(
            functools.partial(_mm_body, n_pairs=len(pairs), n_aux=n_aux, has_res=has_res,
                              epilogue=epilogue),
            grid=(m // tm, n // tn),
            in_specs=in_specs,
            out_specs=pl.BlockSpec((tm, tn), lambda i, j: (i, j)),
            out_shape=out_shape,
            compiler_params=_cparams("parallel", "parallel"),
            name=name,
        )(*operands)

    (lhs, rhs, off), = pairs
    nk = lhs.shape[1] // tk
    operands = [lhs, rhs]
    in_specs = [pl.BlockSpec((tm, tk), lambda i, j, k: (i, k)),
                pl.BlockSpec((tk, tn), lambda i, j, k, off=off: (k, j + off))]
    for vec, voff in aux:
        operands.append(vec)
        in_specs.append(pl.BlockSpec((1, tn), lambda i, j, k, voff=voff: (0, j + voff)))
    if has_res:
        operands.append(res)
        in_specs.append(pl.BlockSpec((tm, tn), lambda i, j, k: (i, j)))
    return pl.pallas_call(
        functools.partial(_mm_ksplit_body, n_aux=n_aux, has_res=has_res, epilogue=epilogue,
                          nk=nk),
        grid=(m // tm, n // tn, nk),
        in_specs=in_specs,
        out_specs=pl.BlockSpec((tm, tn), lambda i, j, k: (i, j)),
        out_shape=out_shape,
        scratch_shapes=[pltpu.VMEM((tm, tn), F32)],
        compiler_params=_cparams("parallel", "parallel", "arbitrary"),
        name=name,
    )(*operands)


W_IN_EDGE_ROWS = 16


def _transpose_cast_body(a_ref, b_ref, o_ref, *, shift, scale, n_scaled):
    rows = a_ref.shape[0]
    x = jnp.concatenate([a_ref[...], b_ref[...]], axis=0)[shift:shift + rows, :]
    if n_scaled:
        x = x * jnp.where(pl.program_id(0) < n_scaled, scale, 1.0)
    o_ref[...] = x.T.astype(o_ref.dtype)


def _transpose_cast(wt, start, n_cols, tile, out_block, tk, scale=1.0, n_scaled=0):
    k = wt.shape[1]
    base = start // tile * tile
    shift = start - base
    assert shift % SUBLANES == 0 and shift <= W_IN_EDGE_ROWS and n_cols % tile == 0
    assert start + n_cols + (W_IN_EDGE_ROWS - shift) <= wt.shape[0] or shift == 0
    edge_per_tile = tile // W_IN_EDGE_ROWS
    last_edge = wt.shape[0] // W_IN_EDGE_ROWS - 1
    return pl.pallas_call(
        functools.partial(_transpose_cast_body, shift=shift, scale=scale, n_scaled=n_scaled),
        grid=(n_cols // tile, k // tk),
        in_specs=[pl.BlockSpec((tile, tk), lambda j, c: (base // tile + j, c)),
                  pl.BlockSpec((W_IN_EDGE_ROWS, tk),
                               lambda j, c: (jnp.minimum((base // tile + j + 1) * edge_per_tile,
                                                         last_edge), c))],
        out_specs=pl.BlockSpec((tk, tile), lambda j, c: (c, out_block(j))),
        out_shape=jax.ShapeDtypeStruct((k, n_cols), BF16),
        compiler_params=_cparams("parallel", "parallel"),
        name="transpose_cast",
    )(wt, wt)


def _matmul_kpieces_body(*refs, n_pieces):
    res_ref, o_ref = refs[2 * n_pieces], refs[2 * n_pieces + 1]
    acc = res_ref[...].astype(F32)
    for p in range(n_pieces):
        acc = acc + jnp.dot(refs[p][...], refs[n_pieces + p][...], preferred_element_type=F32)
    o_ref[...] = acc.astype(o_ref.dtype)


def _matmul_kpieces(lhs, rhs, res, *, n_pieces, tm, tn, name):
    m, k = lhs.shape
    n = rhs.shape[1]
    kp = k // n_pieces
    tm, tn = min(tm, m), min(tn, n)
    in_specs = ([pl.BlockSpec((tm, kp), lambda i, j, p=p: (i, p)) for p in range(n_pieces)]
                + [pl.BlockSpec((kp, tn), lambda i, j, p=p: (p, j)) for p in range(n_pieces)]
                + [pl.BlockSpec((tm, tn), lambda i, j: (i, j))])
    return pl.pallas_call(
        functools.partial(_matmul_kpieces_body, n_pieces=n_pieces),
        grid=(m // tm, n // tn),
        in_specs=in_specs,
        out_specs=pl.BlockSpec((tm, tn), lambda i, j: (i, j)),
        out_shape=jax.ShapeDtypeStruct((m, n), res.dtype),
        compiler_params=_cparams("parallel", "parallel"),
        name=name,
    )(*([lhs] * n_pieces), *([rhs] * n_pieces), res)


def _ep_plain(accs, aux, res):
    return accs[0]


def _ep_add_res(accs, aux, res):
    return res + accs[0]


def _ep_log_sigmoid(accs, aux, res):
    z = accs[0] + aux[0]
    return jnp.minimum(z, 0.0) - jnp.log1p(jnp.exp(-jnp.abs(z)))


def _ep_gated_merge(accs, aux, res):
    tn = accs[1].shape[1]
    gates = jax.nn.sigmoid(accs[0] + aux[0])
    return gates[:, :tn] * accs[1] + gates[:, tn:] * accs[2]


def _ep_relu2(accs, aux, res):
    r = jnp.maximum(accs[0], 0.0)
    return r * r


def _forget_bias_body(x_ref, o_ref, carry_ref, *, blk, nblk, heads):
    @pl.when(pl.program_id(1) == 0)
    def _():
        carry_ref[...] = jnp.zeros_like(carry_ref)

    r = lax.broadcasted_iota(jnp.int32, (blk, blk), 0)
    c = lax.broadcasted_iota(jnp.int32, (blk, blk), 1)
    tri = (c <= r).astype(F32)
    lane = lax.broadcasted_iota(jnp.int32, (blk, LANES), 1)

    def step(i, carry):
        base = pl.multiple_of(i * blk, blk)
        cs = jnp.dot(tri, x_ref[pl.ds(base, blk), :], preferred_element_type=F32,
                     precision=lax.Precision.HIGHEST) + carry
        for h in range(heads):
            bias = jnp.broadcast_to(cs[:, h:h + 1] * (-LOG2E), (blk, LANES))
            hi = bias.astype(BF16).astype(F32)
            mid = (bias - hi).astype(BF16).astype(F32)
            lo = bias - hi - mid
            pieces = jnp.where(lane == 0, hi, jnp.where(lane == 1, mid,
                                                        jnp.where(lane == 2, lo, 0.0)))
            o_ref[pl.ds(base, blk), h * LANES:(h + 1) * LANES] = pieces.astype(o_ref.dtype)
        return cs[blk - 1:blk, :]

    carry_ref[0:1, :] = lax.fori_loop(0, nblk, step, carry_ref[0:1, :])


def _forget_bias(log_f, bsz, seq, heads):
    blk = min(256, seq)
    tt = min(1024, seq)
    nt = seq // tt
    return pl.pallas_call(
        functools.partial(_forget_bias_body, blk=blk, nblk=tt // blk, heads=heads),
        grid=(bsz, nt),
        in_specs=[pl.BlockSpec((tt, LANES), lambda b, t: (b * nt + t, 0))],
        out_specs=pl.BlockSpec((tt, heads * LANES), lambda b, t: (b * nt + t, 0)),
        out_shape=jax.ShapeDtypeStruct((bsz * seq, heads * LANES), BF16),
        scratch_shapes=[pltpu.VMEM((SUBLANES, LANES), F32)],
        compiler_params=_cparams("parallel", "arbitrary"),
        name="forget_bias",
    )(log_f)


def _fox_body(qa_ref, qb_ref, k_ref, v_ref, cp_ref, o_ref, kaug_ref, vaug_ref, s_ref,
              *, tq, seq):
    p = pl.program_id(2)
    nq = seq // tq
    dh = qa_ref.shape[1]

    @pl.when(p == 0)
    def _():
        def fill(j, _):
            rows = pl.ds(pl.multiple_of(j * tq, tq), tq)
            kaug_ref[rows, :dh] = k_ref[rows, :]
            kaug_ref[rows, dh:] = cp_ref[rows, :]
            vaug_ref[rows, :dh] = v_ref[rows, :]
            vaug_ref[rows, dh:] = jnp.ones((tq, dh), vaug_ref.dtype)
            return 0
        lax.fori_loop(0, nq, fill, 0)

    lane = lax.broadcasted_iota(jnp.int32, (tq, dh), 1)
    unit = (lane < N_BIAS_PIECES).astype(qa_ref.dtype)
    q_a = jnp.concatenate([qa_ref[...], unit], axis=1)
    q_b = jnp.concatenate([qb_ref[...], unit], axis=1)
    above_diag = (lax.broadcasted_iota(jnp.int32, (tq, tq), 1)
                  > lax.broadcasted_iota(jnp.int32, (tq, tq), 0))

    def key_rows(t):
        j = jnp.where(t <= p, t, t - (p + 1))
        return pl.ds(pl.multiple_of(j * tq, tq), tq)

    def scores(t):
        q = jnp.where(t <= p, q_a, q_b)
        s = lax.dot_general(q, kaug_ref[key_rows(t), :], (((1,), (1,)), ((), ())),
                            preferred_element_type=F32)
        if t == nq:
            s = jnp.where(above_diag, NEG_INF, s)
        elif t < nq // 2:
            s = jnp.where(jnp.logical_and(above_diag, t == p), NEG_INF, s)
        return s

    m = jnp.full((tq, 1), NEG_INF, F32)
    acc = jnp.zeros((tq, 2 * dh), F32)
    s_ref[0] = scores(0)
    for t in range(nq + 1):
        if t < nq:
            s_ref[(t + 1) % 2] = scores(t + 1)
        s = s_ref[t % 2]
        if 1 <= t <= nq // 2:
            first_b = t == p + 1
            m = jnp.where(first_b, NEG_INF, m)
            acc = jnp.where(first_b, 0.0, acc)
        m_new = jnp.maximum(m, jnp.max(s, axis=-1, keepdims=True))
        acc = jnp.exp2(m - m_new) * acc + jnp.dot(
            jnp.exp2(s - m_new).astype(vaug_ref.dtype), vaug_ref[key_rows(t), :],
            preferred_element_type=F32)
        m = m_new
        if t == 0:
            o_ref[:tq, :] = (acc[:, :dh] / acc[:, dh:]).astype(o_ref.dtype)
        elif t < nq // 2:
            out = (acc[:, :dh] / acc[:, dh:]).astype(o_ref.dtype)
            o_ref[:tq, :] = jnp.where(t <= p, out, o_ref[:tq, :])
    o_ref[tq:, :] = (acc[:, :dh] / acc[:, dh:]).astype(o_ref.dtype)


def _fox_attention(qkv, cp, bsz, seq, heads, tq):
    dh = ATTN_HEAD_DIM
    tq = min(tq, seq // 2)
    nq = seq // tq
    half = nq // 2
    assert nq % 2 == 0

    def position(i):
        b, qi = i // nq, i % nq
        return jnp.where(qi < half, (b * half + qi) * 2, (b * half + nq - 1 - qi) * 2 + 1)

    out = pl.pallas_call(
        functools.partial(_fox_body, tq=tq, seq=seq),
        grid=(bsz, heads, half),
        in_specs=[pl.BlockSpec((tq, dh), lambda b, h, p: (b * nq + p, h)),
                  pl.BlockSpec((tq, dh), lambda b, h, p: (b * nq + nq - 1 - p, h)),
                  pl.BlockSpec((seq, dh), lambda b, h, p: (b, heads + h)),
                  pl.BlockSpec((seq, dh), lambda b, h, p: (b, 2 * heads + h)),
                  pl.BlockSpec((seq, dh), lambda b, h, p: (b, h))],
        out_specs=pl.BlockSpec((2 * tq, dh), lambda b, h, p: (b * half + p, h)),
        out_shape=jax.ShapeDtypeStruct((bsz * seq, heads * dh), BF16),
        scratch_shapes=[pltpu.VMEM((seq, 2 * dh), BF16), pltpu.VMEM((seq, 2 * dh), BF16),
                        pltpu.VMEM((2, tq, tq), F32)],
        compiler_params=_cparams("parallel", "parallel", "arbitrary"),
        name="fox_attention",
    )(qkv, qkv, qkv, qkv, cp)
    return out, tq, position


def _xattn_block_body(x_ref, gx_ref, wq_ref, kv_ref, wo_ref, gn_ref, xo_ref, hn_ref):
    x = x_ref[...]
    width = wq_ref.shape[1]
    dh = width // XATTN_HEADS
    hx = _rms_scale(x, gx_ref[...]).astype(wq_ref.dtype)
    q = jnp.dot(hx, wq_ref[...], preferred_element_type=F32).astype(kv_ref.dtype)
    heads = []
    for hd in range(XATTN_HEADS):
        k = kv_ref[:, hd * dh:(hd + 1) * dh]
        v = kv_ref[:, width + hd * dh:width + (hd + 1) * dh]
        s = lax.dot_general(q[:, hd * dh:(hd + 1) * dh], k, (((1,), (1,)), ((), ())),
                            preferred_element_type=F32)
        p = jnp.exp(s - jnp.max(s, axis=-1, keepdims=True))
        l = jnp.sum(p, axis=-1, keepdims=True)
        o = jnp.dot(p.astype(v.dtype), v, preferred_element_type=F32) / l
        heads.append(o.astype(wo_ref.dtype))
    x2 = x + jnp.dot(jnp.concatenate(heads, axis=1), wo_ref[...], preferred_element_type=F32)
    xo_ref[...] = x2
    hn_ref[...] = _rms_scale(x2, gn_ref[...]).astype(hn_ref.dtype)


def _xattn_block(x, g_x, wq, kv, wo, g_next, seq, mem_len, tm):
    t, d = x.shape
    width = wq.shape[1]
    tm = min(tm, seq)
    nq = seq // tm
    row = pl.BlockSpec((tm, d), lambda i: (i, 0))
    vec = pl.BlockSpec((1, d), lambda i: (0, 0))
    resident = pl.Buffered(1)
    return pl.pallas_call(
        _xattn_block_body,
        grid=(t // tm,),
        in_specs=[row, vec,
                  pl.BlockSpec((d, width), lambda i: (0, 0), pipeline_mode=resident),
                  pl.BlockSpec((mem_len, 2 * width), lambda i: (i // nq, 0)),
                  pl.BlockSpec((width, d), lambda i: (0, 0), pipeline_mode=resident),
                  vec],
        out_specs=[row, row],
        out_shape=[jax.ShapeDtypeStruct((t, d), F32), jax.ShapeDtypeStruct((t, d), BF16)],
        compiler_params=_cparams("parallel"),
        name="xattn_block",
    )(x, g_x.reshape(1, d).astype(F32), wq, kv, wo, g_next.reshape(1, d).astype(F32))


def _s5_operators_body(bre_ref, bim_ref, cre_ref, cim_ref, p_ref, pt_ref, d_ref,
                        t_ref, w_ref, v_ref, *, L):
    hp = lax.Precision.HIGHEST
    bre, bim = bre_ref[...], bim_ref[...]
    cre, cim = cre_ref[...], cim_ref[...]
    k = bre.shape[1]
    row = lax.broadcasted_iota(jnp.int32, (LANES, LANES), 0)
    col = lax.broadcasted_iota(jnp.int32, (LANES, LANES), 1)
    lag_blocks = []
    for tau in range(L):
        pr, pi = p_ref[0, tau:tau + 1, :], p_ref[1, tau:tau + 1, :]
        xr, xi = bre * pr - bim * pi, bre * pi + bim * pr
        rows = slice((L - 1 - tau) * LANES, (L - tau) * LANES)
        w_ref[rows, :k] = xr.astype(w_ref.dtype)
        w_ref[rows, k:] = xi.astype(w_ref.dtype)
        d_tau = (jnp.dot(xr, cre, preferred_element_type=F32, precision=hp)
                 - jnp.dot(xi, cim, preferred_element_type=F32, precision=hp))
        if tau == 0:
            d_tau = d_tau + jnp.where(row == col, d_ref[...], 0.0)
        lag_blocks.append(d_tau.astype(t_ref.dtype))
    zero = jnp.zeros((LANES, LANES), t_ref.dtype)
    for s in range(L):
        for t in range(L):
            t_ref[s * LANES:(s + 1) * LANES, t * LANES:(t + 1) * LANES] = (
                lag_blocks[t - s] if t >= s else zero)
    for t in range(L):
        qr, qi = pt_ref[0, :, t + 1:t + 2], pt_ref[1, :, t + 1:t + 2]
        cols = slice(t * LANES, (t + 1) * LANES)
        v_ref[:k, cols] = (cre * qr - cim * qi).astype(v_ref.dtype)
        v_ref[k:, cols] = (-(cre * qi + cim * qr)).astype(v_ref.dtype)


def _s5_operators(A_re, A_im, log_dt, B_re, B_im, C_re, C_im, D_skip):
    n_g, n_p = A_re.shape
    n_h = B_re.shape[-1]
    L = S5_CHUNK
    gs = LANES // n_h
    n_slab = n_g // gs
    k = gs * n_p
    a_re, a_im = A_re.astype(F32), A_im.astype(F32)
    dt = jnp.exp(log_dt.astype(F32))[:, None]
    tau = jnp.arange(L + 1, dtype=F32)[:, None, None]
    mag = jnp.exp(tau * (dt * a_re))
    ang = tau * (dt * a_im)
    pw = jnp.stack([mag * jnp.cos(ang), mag * jnp.sin(ang)])
    lb_re, lb_im = pw[0, 1], pw[1, 1]
    den = a_re * a_re + a_im * a_im
    nr, ni = lb_re - 1.0, lb_im
    f_re = (nr * a_re + ni * a_im) / den
    f_im = (ni * a_re - nr * a_im) / den
    br, bi = B_re.astype(F32), B_im.astype(F32)
    bb_re = f_re[..., None] * br - f_im[..., None] * bi
    bb_im = f_re[..., None] * bi + f_im[..., None] * br

    eye = jnp.eye(gs, dtype=F32)

    def slab_b(x):
        x = x.reshape(n_slab, gs, n_p, n_h).transpose(0, 1, 3, 2)
        return (x[:, :, :, None, :] * eye[None, :, None, :, None]).reshape(n_slab, LANES, k)

    def slab_c(x):
        x = x.reshape(n_slab, gs, n_h, n_p).transpose(0, 1, 3, 2)
        return (x[:, :, :, None, :] * eye[None, :, None, :, None]).reshape(n_slab, k, LANES)

    p_tab = pw.reshape(2, L + 1, n_slab, k).transpose(2, 0, 1, 3)
    pt_tab = jnp.pad(pw.reshape(2, L + 1, n_slab, k).transpose(2, 0, 3, 1),
                     ((0, 0), (0, 0), (0, 0), (0, LANES - (L + 1))))
    mat_b = pl.BlockSpec((None, LANES, k), lambda j: (j, 0, 0))
    mat_c = pl.BlockSpec((None, k, LANES), lambda j: (j, 0, 0))
    t_op, w_op, v_op = pl.pallas_call(
        functools.partial(_s5_operators_body, L=L),
        grid=(n_slab,),
        in_specs=[mat_b, mat_b, mat_c, mat_c,
                  pl.BlockSpec((None, 2, L + 1, k), lambda j: (j, 0, 0, 0)),
                  pl.BlockSpec((None, 2, k, LANES), lambda j: (j, 0, 0, 0)),
                  pl.BlockSpec((1, LANES), lambda j: (0, j))],
        out_specs=[pl.BlockSpec((None, L * LANES, L * LANES), lambda j: (j, 0, 0)),
                   pl.BlockSpec((None, L * LANES, 2 * k), lambda j: (j, 0, 0)),
                   pl.BlockSpec((None, 2 * k, L * LANES), lambda j: (j, 0, 0))],
        out_shape=[jax.ShapeDtypeStruct((n_slab, L * LANES, L * LANES), BF16),
                   jax.ShapeDtypeStruct((n_slab, L * LANES, 2 * k), BF16),
                   jax.ShapeDtypeStruct((n_slab, 2 * k, L * LANES), BF16)],
        compiler_params=_cparams("parallel"),
        name="s5_operators",
    )(slab_b(bb_re), slab_b(bb_im), slab_c(C_re.astype(F32)), slab_c(C_im.astype(F32)),
      p_tab, pt_tab, D_skip.astype(F32).reshape(1, n_g * n_h))

    a_op = pw[:, L].reshape(2, n_slab, k).transpose(1, 0, 2).reshape(1, n_slab * 2 * k)
    return t_op, w_op, v_op, a_op


def _chunk_rows(piece_refs):
    return jnp.concatenate([r[...] for r in piece_refs], axis=1)


def _s5_increment_body(*refs):
    w_ref, z_ref = refs[-2], refs[-1]
    z_ref[...] = jnp.dot(_chunk_rows(refs[:-2]), w_ref[...], preferred_element_type=F32)


def _s5_scan_body(z_ref, a_ref, o_ref, st_ref, *, tc):
    @pl.when(pl.program_id(2) == 0)
    def _():
        st_ref[...] = jnp.zeros_like(st_ref)

    half = z_ref.shape[1] // 2
    a_re, a_im = a_ref[:, :half], a_ref[:, half:]

    def step(i, carry):
        re, im = carry
        base = pl.multiple_of(i * SUBLANES, SUBLANES)
        inc = z_ref[pl.ds(base, SUBLANES), :]
        before_re, before_im = [], []
        for r in range(SUBLANES):
            before_re.append(re)
            before_im.append(im)
            re, im = (a_re * re - a_im * im + inc[r:r + 1, :half],
                      a_re * im + a_im * re + inc[r:r + 1, half:])
        o_ref[pl.ds(base, SUBLANES), :half] = jnp.concatenate(before_re, axis=0)
        o_ref[pl.ds(base, SUBLANES), half:] = jnp.concatenate(before_im, axis=0)
        return re, im

    re, im = lax.fori_loop(0, tc // SUBLANES, step, (st_ref[0:1, :half], st_ref[0:1, half:]))
    st_ref[0:1, :half] = re
    st_ref[0:1, half:] = im


def _s5_output_body(*refs):
    x_ref, t_ref, v_ref, o_ref = refs[-4:]
    y = jnp.dot(_chunk_rows(refs[:-4]), t_ref[...], preferred_element_type=F32)
    y = y + jnp.dot(x_ref[...].astype(BF16), v_ref[...], preferred_element_type=F32)
    o_ref[...] = jax.nn.gelu(y).astype(o_ref.dtype)


def _s5_glu_body(*refs):
    w_ref, b_ref, o_ref = refs[-3:]
    y = _chunk_rows(refs[:-3])
    gate = jax.nn.sigmoid(jnp.dot(y, w_ref[...], preferred_element_type=F32) + b_ref[...])
    o_ref[...] = (y.astype(F32) * gate).astype(o_ref.dtype)


def _s5_branch(u, ops, w_glu, b_glu, bsz, seq):
    t_op, w_op, v_op, a_op = ops
    n_slab = t_op.shape[0]
    L = S5_CHUNK
    width = u.shape[1]
    cw, sw = t_op.shape[1], w_op.shape[2]
    nc = bsz * seq // L
    ncb = seq // L
    uu = u.reshape(nc, L * width)
    tm = min(512, nc)
    pieces = [pl.BlockSpec((tm, LANES), lambda j, i, s=s: (i, s * n_slab + j)) for s in range(L)]

    z = pl.pallas_call(
        _s5_increment_body,
        grid=(n_slab, nc // tm),
        in_specs=pieces + [pl.BlockSpec((None, cw, sw), lambda j, i: (j, 0, 0))],
        out_specs=pl.BlockSpec((tm, sw), lambda j, i: (i, j)),
        out_shape=jax.ShapeDtypeStruct((nc, n_slab * sw), F32),
        compiler_params=_cparams("parallel", "parallel"),
        name="s5_increment",
    )(*([uu] * L), w_op)

    tc = min(128, ncb)
    nt = ncb // tc
    blk = pl.BlockSpec((tc, sw), lambda b, j, t: (b * nt + t, j))
    xprev = pl.pallas_call(
        functools.partial(_s5_scan_body, tc=tc),
        grid=(bsz, n_slab, nt),
        in_specs=[blk, pl.BlockSpec((1, sw), lambda b, j, t: (0, j))],
        out_specs=blk,
        out_shape=jax.ShapeDtypeStruct((nc, n_slab * sw), F32),
        scratch_shapes=[pltpu.VMEM((SUBLANES, sw), F32)],
        compiler_params=_cparams("parallel", "parallel", "arbitrary"),
        name="s5_scan",
    )(z, a_op)

    ys = pl.pallas_call(
        _s5_output_body,
        grid=(n_slab, nc // tm),
        in_specs=pieces + [pl.BlockSpec((tm, sw), lambda j, i: (i, j)),
                           pl.BlockSpec((None, cw, cw), lambda j, i: (j, 0, 0)),
                           pl.BlockSpec((None, sw, cw), lambda j, i: (j, 0, 0))],
        out_specs=pl.BlockSpec((tm, cw), lambda j, i: (i, j)),
        out_shape=jax.ShapeDtypeStruct((nc, n_slab * cw), BF16),
        compiler_params=_cparams("parallel", "parallel"),
        name="s5_output",
    )(*([uu] * L), xprev, t_op, v_op)

    tok = [pl.BlockSpec((tm, LANES), lambda i, t, j=j: (i, j * L + t)) for j in range(n_slab)]
    out = pl.pallas_call(
        _s5_glu_body,
        grid=(nc // tm, L),
        in_specs=tok + [pl.BlockSpec((width, width), lambda i, t: (0, 0)),
                        pl.BlockSpec((1, width), lambda i, t: (0, 0))],
        out_specs=pl.BlockSpec((tm, width), lambda i, t: (i, t)),
        out_shape=jax.ShapeDtypeStruct((nc, L * width), BF16),
        compiler_params=_cparams("parallel", "parallel"),
        name="s5_glu",
    )(*([ys] * n_slab), w_glu.astype(BF16), b_glu.reshape(1, width).astype(F32))
    return out.reshape(bsz * seq, width)


def _row(v):
    return v.reshape(1, -1).astype(F32)


def _layer(x, mem_n, p, bsz, seq, mem_len):
    d = x.shape[1]
    aw = p["w_attn_up"].shape[0]
    heads = aw // ATTN_HEAD_DIM
    sw = p["w_ssm_up"].shape[0]
    off_f = 3 * aw
    off_u = off_f + heads
    off_g = off_u + sw

    h = _rmsnorm([x], p["g_mix"], BF16, tm=256)
    w_in_t = p["w_in"].T
    tile = 256
    w_qkv = _transpose_cast(w_in_t, 0, off_f, tile, lambda j: j, tk=2048,
                            scale=LOG2E * ATTN_HEAD_DIM ** -0.5, n_scaled=aw // tile)
    w_f = jnp.pad(w_in_t[off_f:off_u].T, ((0, 0), (0, LANES - heads))).astype(BF16)
    w_u = _transpose_cast(w_in_t, off_u, sw, tile, lambda j: j, tk=2048)
    tn_mix = tile
    n_mix = d // tn_mix
    w_g = _transpose_cast(w_in_t, off_g, N_BRANCH * d, tn_mix,
                          lambda j: (j % n_mix) * N_BRANCH + j // n_mix, tk=2048)

    qkv = _fused_matmul([(h, w_qkv, 0)], _ep_plain, BF16, n=3 * aw, tm=1024, tn=1024,
                        name="qkv_proj")
    b_f = jnp.pad(p["b_f"].astype(F32), (0, LANES - heads)).reshape(1, LANES)
    log_f = _fused_matmul([(h, w_f, 0)], _ep_log_sigmoid, F32, n=LANES, tm=1024, tn=LANES,
                          aux=[(b_f, 0)], name="forget_proj")
    fox, fox_rows, fox_position = _fox_attention(qkv, _forget_bias(log_f, bsz, seq, heads),
                                                 bsz, seq, heads, tq=1024)

    u = _fused_matmul([(h, w_u, 0)], _ep_plain, BF16, n=sw, tm=1024, tn=1024, name="ssm_in_proj")
    ops = _s5_operators(p["A_re"], p["A_im"], p["log_dt"], p["B_re"], p["B_im"],
                        p["C_re"], p["C_im"], p["D_skip"])
    y = _s5_branch(u, ops, p["w_glu"], p["b_glu"], bsz, seq)

    b_gate = (p["b_gate"].astype(F32).reshape(N_BRANCH, n_mix, tn_mix).transpose(1, 0, 2)
              .reshape(1, N_BRANCH * d))
    merged = _fused_matmul(
        [(h, w_g, 0, N_BRANCH), (fox, p["w_attn_up"].astype(BF16), 0, 1, fox_position),
         (y, p["w_ssm_up"].astype(BF16), 0)],
        _ep_gated_merge, BF16, n=d, tm=fox_rows, tn=tn_mix,
        aux=[(b_gate, 0, N_BRANCH)], name="gated_merge")
    x = _fused_matmul([(merged, p["w_out"].astype(BF16), 0)], _ep_add_res, F32, n=d,
                      tm=1024, tn=512, res=x, name="mixer_out_proj")

    xw = p["wq_x"].shape[1]
    wq = (p["wq_x"] * (xw // XATTN_HEADS) ** -0.5).astype(BF16)
    w_kv = jnp.concatenate([p["wk_x"], p["wv_x"]], axis=1).astype(BF16)
    kv = _fused_matmul([(mem_n, w_kv, 0)], _ep_plain, BF16, n=2 * xw, tm=512, tn=1024,
                       name="xattn_kv_proj")
    x, hm = _xattn_block(x, p["g_xattn"], wq, kv, p["wo_x"].astype(BF16), p["g_mlp"],
                         seq, mem_len, tm=256)

    dff = p["w_ff1"].shape[1]
    hid = _fused_matmul([(hm, p["w_ff1"].astype(BF16), 0)], _ep_relu2, BF16, n=dff,
                        tm=1024, tn=1024, name="mlp_up")
    x = _matmul_kpieces(hid, p["w_ff2"].astype(BF16), x, n_pieces=4, tm=512, tn=256,
                        name="mlp_down")
    return x


_LAYER_PARAMS = ("g_mix", "w_in", "b_f", "b_gate", "A_re", "A_im", "log_dt", "B_re", "B_im",
                 "C_re", "C_im", "D_skip", "w_glu", "b_glu", "w_attn_up", "w_ssm_up", "w_out",
                 "g_xattn", "g_mem", "wq_x", "wk_x", "wv_x", "wo_x", "g_mlp", "w_ff1", "w_ff2")


def kernel(x, mem, g_mix, w_in, b_f, b_gate, A_re, A_im, log_dt, B_re, B_im, C_re, C_im, D_skip, w_glu, b_glu, w_attn_up, w_ssm_up, w_out, g_xattn, g_mem, wq_x, wk_x, wv_x, wo_x, g_mlp, w_ff1, w_ff2, g_final):
    stacked = dict(zip(_LAYER_PARAMS, (g_mix, w_in, b_f, b_gate, A_re, A_im, log_dt, B_re, B_im,
                                       C_re, C_im, D_skip, w_glu, b_glu, w_attn_up, w_ssm_up,
                                       w_out, g_xattn, g_mem, wq_x, wk_x, wv_x, wo_x, g_mlp,
                                       w_ff1, w_ff2)))
    bsz, seq, d = x.shape
    mem_len = mem.shape[1]
    xt = x.reshape(bsz * seq, d)
    mem2 = mem.reshape(bsz * mem_len, d)
    for l in range(g_mix.shape[0]):
        p = {k: v[l] for k, v in stacked.items()}
        mem_n = _rmsnorm([mem2], p["g_mem"], BF16, tm=256)
        xt = _layer(xt, mem_n, p, bsz, seq, mem_len)
    out = _rmsnorm([xt], g_final, x.dtype, tm=256)
    return out.reshape(bsz, seq, d)
```

```python
import functools
import math

import jax
import jax.numpy as jnp
from jax import lax
from jax.experimental import pallas as pl
from jax.experimental.pallas import tpu as pltpu

F32 = jnp.float32
BF16 = jnp.bfloat16

V7X_VMEM_LIMIT_BYTES = 56 * 1024 * 1024
LANES = 128
SUBLANES = 8
BF16_SUBLANES = 16

EPS = 1e-6
NEG_INF = -1e30
LOG2E = math.log2(math.e)
N_BIAS_PIECES = 3
ATTN_HEAD_DIM = 128
SSM_GROUP = 16
SSM_STATE = 64
S5_CHUNK = 16
XATTN_HEADS = 4
N_BRANCH = 2


def _cparams(*sem):
    return pltpu.CompilerParams(dimension_semantics=sem,
                                vmem_limit_bytes=V7X_VMEM_LIMIT_BYTES)


def _rms_scale(x, g):
    return (x * lax.rsqrt(jnp.mean(x * x, axis=-1, keepdims=True) + EPS)) * g


def _rmsnorm_body(*refs, n_in):
    g_ref, o_ref = refs[n_in], refs[n_in + 1]
    xs = refs[0][...].astype(F32)
    for r in refs[1:n_in]:
        xs = xs + r[...].astype(F32)
    o_ref[...] = _rms_scale(xs, g_ref[...]).astype(o_ref.dtype)


def _rmsnorm(xs, g, out_dtype, tm):
    m, d = xs[0].shape
    tm = min(tm, m)
    row = pl.BlockSpec((tm, d), lambda i: (i, 0))
    return pl.pallas_call(
        functools.partial(_rmsnorm_body, n_in=len(xs)),
        grid=(m // tm,),
        in_specs=[row] * len(xs) + [pl.BlockSpec((1, d), lambda i: (0, 0))],
        out_specs=row,
        out_shape=jax.ShapeDtypeStruct((m, d), out_dtype),
        compiler_params=_cparams("parallel"),
        name="rmsnorm",
    )(*xs, g.reshape(1, d).astype(F32))


def _mm_body(*refs, n_pairs, n_aux, has_res, has_side, epilogue):
    accs = [jnp.dot(refs[2 * p][...], refs[2 * p + 1][...], preferred_element_type=F32)
            for p in range(n_pairs)]
    pos = 2 * n_pairs
    aux = [refs[pos + i][...] for i in range(n_aux)]
    pos += n_aux
    res = refs[pos][...].astype(F32) if has_res else None
    pos += has_res
    if has_side:
        refs[-1][...] = refs[pos][...].astype(refs[-1].dtype)
        pos += 1
    o_ref = refs[pos]
    o_ref[...] = epilogue(accs, aux, res).astype(o_ref.dtype)


def _fused_matmul(pairs, epilogue, out_dtype, *, n, tm, tn, aux=(), res=None, side_cast=None,
                  name="fused_matmul"):
    m = pairs[0][0].shape[0]
    tm, tn = min(tm, m), min(tn, n)
    n_aux, has_res, has_side = len(aux), res is not None, side_cast is not None
    nj = n // tn
    operands, in_specs = [], []
    for lhs, rhs, off, *opt in pairs:
        kp = lhs.shape[1]
        width = tn * (opt[0] if opt else 1)
        row_block = opt[1] if len(opt) > 1 else (lambda i: i)
        operands += [lhs, rhs]
        in_specs += [pl.BlockSpec((tm, kp), lambda i, j, rb=row_block: (rb(i), 0)),
                     pl.BlockSpec((kp, width), lambda i, j, off=off: (0, j + off))]
    for vec, off, *mult in aux:
        width = tn * (mult[0] if mult else 1)
        operands.append(vec)
        in_specs.append(pl.BlockSpec((1, width), lambda i, j, off=off: (0, j + off)))
    if has_res:
        operands.append(res)
        in_specs.append(pl.BlockSpec((tm, tn), lambda i, j: (i, j)))
    out_specs = [pl.BlockSpec((tm, tn), lambda i, j: (i, j))]
    out_shape = [jax.ShapeDtypeStruct((m, n), out_dtype)]
    if has_side:
        slab = side_cast.shape[0] // ((m // tm) * nj)
        assert slab * (m // tm) * nj == side_cast.shape[0] and slab % BF16_SUBLANES == 0
        side_spec = pl.BlockSpec((slab, side_cast.shape[1]), lambda i, j: (i * nj + j, 0))
        operands.append(side_cast)
        in_specs.append(side_spec)
        out_specs.append(side_spec)
        out_shape.append(jax.ShapeDtypeStruct(side_cast.shape, BF16))
    outs = pl.pallas_call(
        functools.partial(_mm_body, n_pairs=len(pairs), n_aux=n_aux, has_res=has_res,
                          has_side=has_side, epilogue=epilogue),
        grid=(m // tm, nj),
        in_specs=in_specs,
        out_specs=out_specs,
        out_shape=out_shape,
        compiler_params=_cparams("parallel", "parallel"),
        name=name,
    )(*operands)
    return outs if has_side else outs[0]


W_IN_EDGE_ROWS = 16


def _transpose_cast_body(a_ref, b_ref, o_ref, *, shift, scale, n_scaled):
    rows = a_ref.shape[0]
    x = jnp.concatenate([a_ref[...], b_ref[...]], axis=0)[shift:shift + rows, :]
    if n_scaled:
        x = x * jnp.where(pl.program_id(0) < n_scaled, scale, 1.0)
    o_ref[...] = x.T.astype(o_ref.dtype)


def _transpose_cast(wt, start, n_cols, tile, out_block, tk, scale=1.0, n_scaled=0):
    k = wt.shape[1]
    base = start // tile * tile
    shift = start - base
    assert shift % SUBLANES == 0 and shift <= W_IN_EDGE_ROWS and n_cols % tile == 0
    assert start + n_cols + (W_IN_EDGE_ROWS - shift) <= wt.shape[0] or shift == 0
    edge_per_tile = tile // W_IN_EDGE_ROWS
    last_edge = wt.shape[0] // W_IN_EDGE_ROWS - 1
    return pl.pallas_call(
        functools.partial(_transpose_cast_body, shift=shift, scale=scale, n_scaled=n_scaled),
        grid=(n_cols // tile, k // tk),
        in_specs=[pl.BlockSpec((tile, tk), lambda j, c: (base // tile + j, c)),
                  pl.BlockSpec((W_IN_EDGE_ROWS, tk),
                               lambda j, c: (jnp.minimum((base // tile + j + 1) * edge_per_tile,
                                                         last_edge), c))],
        out_specs=pl.BlockSpec((tk, tile), lambda j, c: (c, out_block(j))),
        out_shape=jax.ShapeDtypeStruct((k, n_cols), BF16),
        compiler_params=_cparams("parallel", "parallel"),
        name="transpose_cast",
    )(wt, wt)


def _matmul_kpieces_body(*refs, n_pieces):
    res_ref, o_ref = refs[2 * n_pieces], refs[2 * n_pieces + 1]
    acc = res_ref[...].astype(F32)
    for p in range(n_pieces):
        acc = acc + jnp.dot(refs[p][...], refs[n_pieces + p][...], preferred_element_type=F32)
    o_ref[...] = acc.astype(o_ref.dtype)


def _matmul_kpieces(lhs, rhs, res, *, n_pieces, tm, tn, name):
    m, k = lhs.shape
    n = rhs.shape[1]
    kp = k // n_pieces
    tm, tn = min(tm, m), min(tn, n)
    in_specs = ([pl.BlockSpec((tm, kp), lambda i, j, p=p: (i, p)) for p in range(n_pieces)]
                + [pl.BlockSpec((kp, tn), lambda i, j, p=p: (p, j)) for p in range(n_pieces)]
                + [pl.BlockSpec((tm, tn), lambda i, j: (i, j))])
    return pl.pallas_call(
        functools.partial(_matmul_kpieces_body, n_pieces=n_pieces),
        grid=(m // tm, n // tn),
        in_specs=in_specs,
        out_specs=pl.BlockSpec((tm, tn), lambda i, j: (i, j)),
        out_shape=jax.ShapeDtypeStruct((m, n), res.dtype),
        compiler_params=_cparams("parallel", "parallel"),
        name=name,
    )(*([lhs] * n_pieces), *([rhs] * n_pieces), res)


def _ep_plain(accs, aux, res):
    return accs[0]


def _ep_add_res(accs, aux, res):
    return res + accs[0]


def _ep_log_sigmoid(accs, aux, res):
    z = accs[0] + aux[0]
    return jnp.minimum(z, 0.0) - jnp.log1p(jnp.exp(-jnp.abs(z)))


def _ep_gated_merge(accs, aux, res):
    tn = accs[1].shape[1]
    gates = jax.nn.sigmoid(accs[0] + aux[0])
    return gates[:, :tn] * accs[1] + gates[:, tn:] * accs[2]


def _ep_relu2(accs, aux, res):
    r = jnp.maximum(accs[0], 0.0)
    return r * r


def _forget_bias_body(x_ref, o_ref, carry_ref, *, blk, nblk, heads):
    @pl.when(pl.program_id(1) == 0)
    def _():
        carry_ref[...] = jnp.zeros_like(carry_ref)

    r = lax.broadcasted_iota(jnp.int32, (blk, blk), 0)
    c = lax.broadcasted_iota(jnp.int32, (blk, blk), 1)
    tri = (c <= r).astype(F32)
    lane = lax.broadcasted_iota(jnp.int32, (blk, LANES), 1)

    def step(i, carry):
        base = pl.multiple_of(i * blk, blk)
        cs = jnp.dot(tri, x_ref[pl.ds(base, blk), :], preferred_element_type=F32,
                     precision=lax.Precision.HIGHEST) + carry
        for h in range(heads):
            bias = jnp.broadcast_to(cs[:, h:h + 1] * (-LOG2E), (blk, LANES))
            hi = bias.astype(BF16).astype(F32)
            mid = (bias - hi).astype(BF16).astype(F32)
            lo = bias - hi - mid
            pieces = jnp.where(lane == 0, hi, jnp.where(lane == 1, mid,
                                                        jnp.where(lane == 2, lo, 0.0)))
            o_ref[pl.ds(base, blk), h * LANES:(h + 1) * LANES] = pieces.astype(o_ref.dtype)
        return cs[blk - 1:blk, :]

    carry_ref[0:1, :] = lax.fori_loop(0, nblk, step, carry_ref[0:1, :])


def _forget_bias(log_f, bsz, seq, heads):
    blk = min(256, seq)
    tt = min(1024, seq)
    nt = seq // tt
    return pl.pallas_call(
        functools.partial(_forget_bias_body, blk=blk, nblk=tt // blk, heads=heads),
        grid=(bsz, nt),
        in_specs=[pl.BlockSpec((tt, LANES), lambda b, t: (b * nt + t, 0))],
        out_specs=pl.BlockSpec((tt, heads * LANES), lambda b, t: (b * nt + t, 0)),
        out_shape=jax.ShapeDtypeStruct((bsz * seq, heads * LANES), BF16),
        scratch_shapes=[pltpu.VMEM((SUBLANES, LANES), F32)],
        compiler_params=_cparams("parallel", "arbitrary"),
        name="forget_bias",
    )(log_f)


def _fox_body(qa_ref, qb_ref, k_ref, v_ref, cp_ref, o_ref, kaug_ref, vaug_ref, s_ref,
              *, tq, seq):
    p = pl.program_id(2)
    nq = seq // tq
    dh = qa_ref.shape[1]

    @pl.when(p == 0)
    def _():
        def fill(j, _):
            rows = pl.ds(pl.multiple_of(j * tq, tq), tq)
            kaug_ref[rows, :dh] = k_ref[rows, :]
            kaug_ref[rows, dh:] = cp_ref[rows, :]
            vaug_ref[rows, :dh] = v_ref[rows, :]
            vaug_ref[rows, dh:] = jnp.ones((tq, dh), vaug_ref.dtype)
            return 0
        lax.fori_loop(0, nq, fill, 0)

    lane = lax.broadcasted_iota(jnp.int32, (tq, dh), 1)
    unit = (lane < N_BIAS_PIECES).astype(qa_ref.dtype)
    q_a = jnp.concatenate([qa_ref[...], unit], axis=1)
    q_b = jnp.concatenate([qb_ref[...], unit], axis=1)
    above_diag = (lax.broadcasted_iota(jnp.int32, (tq, tq), 1)
                  > lax.broadcasted_iota(jnp.int32, (tq, tq), 0))

    def key_rows(t):
        j = jnp.where(t <= p, t, t - (p + 1))
        return pl.ds(pl.multiple_of(j * tq, tq), tq)

    def scores(t):
        q = jnp.where(t <= p, q_a, q_b)
        s = lax.dot_general(q, kaug_ref[key_rows(t), :], (((1,), (1,)), ((), ())),
                            preferred_element_type=F32)
        if t == nq:
            s = jnp.where(above_diag, NEG_INF, s)
        elif t < nq // 2:
            s = jnp.where(jnp.logical_and(above_diag, t == p), NEG_INF, s)
        return s

    m = jnp.full((tq, 1), NEG_INF, F32)
    acc = jnp.zeros((tq, 2 * dh), F32)
    s_ref[0] = scores(0)
    for t in range(nq + 1):
        if t < nq:
            s_ref[(t + 1) % 2] = scores(t + 1)
        s = s_ref[t % 2]
        if 1 <= t <= nq // 2:
            first_b = t == p + 1
            m = jnp.where(first_b, NEG_INF, m)
            acc = jnp.where(first_b, 0.0, acc)
        m_new = jnp.maximum(m, jnp.max(s, axis=-1, keepdims=True))
        acc = jnp.exp2(m - m_new) * acc + jnp.dot(
            jnp.exp2(s - m_new).astype(vaug_ref.dtype), vaug_ref[key_rows(t), :],
            preferred_element_type=F32)
        m = m_new
        if t == 0:
            o_ref[:tq, :] = (acc[:, :dh] / acc[:, dh:]).astype(o_ref.dtype)
        elif t < nq // 2:
            out = (acc[:, :dh] / acc[:, dh:]).astype(o_ref.dtype)
            o_ref[:tq, :] = jnp.where(t <= p, out, o_ref[:tq, :])
    o_ref[tq:, :] = (acc[:, :dh] / acc[:, dh:]).astype(o_ref.dtype)


def _fox_attention(qkv, cp, bsz, seq, heads, tq):
    dh = ATTN_HEAD_DIM
    tq = min(tq, seq // 2)
    nq = seq // tq
    half = nq // 2
    assert nq % 2 == 0

    def position(i):
        b, qi = i // nq, i % nq
        return jnp.where(qi < half, (b * half + qi) * 2, (b * half + nq - 1 - qi) * 2 + 1)

    out = pl.pallas_call(
        functools.partial(_fox_body, tq=tq, seq=seq),
        grid=(bsz, heads, half),
        in_specs=[pl.BlockSpec((tq, dh), lambda b, h, p: (b * nq + p, h)),
                  pl.BlockSpec((tq, dh), lambda b, h, p: (b * nq + nq - 1 - p, h)),
                  pl.BlockSpec((seq, dh), lambda b, h, p: (b, heads + h)),
                  pl.BlockSpec((seq, dh), lambda b, h, p: (b, 2 * heads + h)),
                  pl.BlockSpec((seq, dh), lambda b, h, p: (b, h))],
        out_specs=pl.BlockSpec((2 * tq, dh), lambda b, h, p: (b * half + p, h)),
        out_shape=jax.ShapeDtypeStruct((bsz * seq, heads * dh), BF16),
        scratch_shapes=[pltpu.VMEM((seq, 2 * dh), BF16), pltpu.VMEM((seq, 2 * dh), BF16),
                        pltpu.VMEM((2, tq, tq), F32)],
        compiler_params=_cparams("parallel", "parallel", "arbitrary"),
        name="fox_attention",
    )(qkv, qkv, qkv, qkv, cp)
    return out, tq, position


def _xattn_block_body(x_ref, gx_ref, wq_ref, kv_ref, wo_ref, gn_ref, xo_ref, hn_ref):
    x = x_ref[...]
    width = wq_ref.shape[1]
    dh = width // XATTN_HEADS
    hx = _rms_scale(x, gx_ref[...]).astype(wq_ref.dtype)
    q = jnp.dot(hx, wq_ref[...], preferred_element_type=F32).astype(kv_ref.dtype)
    heads = []
    for hd in range(XATTN_HEADS):
        k = kv_ref[:, hd * dh:(hd + 1) * dh]
        v = kv_ref[:, width + hd * dh:width + (hd + 1) * dh]
        s = lax.dot_general(q[:, hd * dh:(hd + 1) * dh], k, (((1,), (1,)), ((), ())),
                            preferred_element_type=F32)
        p = jnp.exp(s - jnp.max(s, axis=-1, keepdims=True))
        l = jnp.sum(p, axis=-1, keepdims=True)
        o = jnp.dot(p.astype(v.dtype), v, preferred_element_type=F32) / l
        heads.append(o.astype(wo_ref.dtype))
    x2 = x + jnp.dot(jnp.concatenate(heads, axis=1), wo_ref[...], preferred_element_type=F32)
    xo_ref[...] = x2
    hn_ref[...] = _rms_scale(x2, gn_ref[...]).astype(hn_ref.dtype)


def _xattn_block(x, g_x, wq, kv, wo, g_next, seq, mem_len, tm):
    t, d = x.shape
    width = wq.shape[1]
    tm = min(tm, seq)
    nq = seq // tm
    row = pl.BlockSpec((tm, d), lambda i: (i, 0))
    vec = pl.BlockSpec((1, d), lambda i: (0, 0))
    resident = pl.Buffered(1)
    return pl.pallas_call(
        _xattn_block_body,
        grid=(t // tm,),
        in_specs=[row, vec,
                  pl.BlockSpec((d, width), lambda i: (0, 0), pipeline_mode=resident),
                  pl.BlockSpec((mem_len, 2 * width), lambda i: (i // nq, 0)),
                  pl.BlockSpec((width, d), lambda i: (0, 0), pipeline_mode=resident),
                  vec],
        out_specs=[row, row],
        out_shape=[jax.ShapeDtypeStruct((t, d), F32), jax.ShapeDtypeStruct((t, d), BF16)],
        compiler_params=_cparams("parallel"),
        name="xattn_block",
    )(x, g_x.reshape(1, d).astype(F32), wq, kv, wo, g_next.reshape(1, d).astype(F32))


def _s5_operators_body(bre_ref, bim_ref, cre_ref, cim_ref, p_ref, pt_ref, d_ref,
                        t_ref, w_ref, v_ref, *, L):
    hp = lax.Precision.HIGHEST
    bre, bim = bre_ref[...], bim_ref[...]
    cre, cim = cre_ref[...], cim_ref[...]
    k = bre.shape[1]
    row = lax.broadcasted_iota(jnp.int32, (LANES, LANES), 0)
    col = lax.broadcasted_iota(jnp.int32, (LANES, LANES), 1)
    lag_blocks = []
    for tau in range(L):
        pr, pi = p_ref[0, tau:tau + 1, :], p_ref[1, tau:tau + 1, :]
        xr, xi = bre * pr - bim * pi, bre * pi + bim * pr
        rows = slice((L - 1 - tau) * LANES, (L - tau) * LANES)
        w_ref[rows, :k] = xr.astype(w_ref.dtype)
        w_ref[rows, k:] = xi.astype(w_ref.dtype)
        d_tau = (jnp.dot(xr, cre, preferred_element_type=F32, precision=hp)
                 - jnp.dot(xi, cim, preferred_element_type=F32, precision=hp))
        if tau == 0:
            d_tau = d_tau + jnp.where(row == col, d_ref[...], 0.0)
        lag_blocks.append(d_tau.astype(t_ref.dtype))
    zero = jnp.zeros((LANES, LANES), t_ref.dtype)
    for s in range(L):
        for t in range(L):
            t_ref[s * LANES:(s + 1) * LANES, t * LANES:(t + 1) * LANES] = (
                lag_blocks[t - s] if t >= s else zero)
    for t in range(L):
        qr, qi = pt_ref[0, :, t + 1:t + 2], pt_ref[1, :, t + 1:t + 2]
        cols = slice(t * LANES, (t + 1) * LANES)
        v_ref[:k, cols] = (cre * qr - cim * qi).astype(v_ref.dtype)
        v_ref[k:, cols] = (-(cre * qi + cim * qr)).astype(v_ref.dtype)


def _s5_operators(A_re, A_im, log_dt, B_re, B_im, C_re, C_im, D_skip):
    n_g, n_p = A_re.shape
    n_h = B_re.shape[-1]
    L = S5_CHUNK
    gs = LANES // n_h
    n_slab = n_g // gs
    k = gs * n_p
    a_re, a_im = A_re.astype(F32), A_im.astype(F32)
    dt = jnp.exp(log_dt.astype(F32))[:, None]
    tau = jnp.arange(L + 1, dtype=F32)[:, None, None]
    mag = jnp.exp(tau * (dt * a_re))
    ang = tau * (dt * a_im)
    pw = jnp.stack([mag * jnp.cos(ang), mag * jnp.sin(ang)])
    lb_re, lb_im = pw[0, 1], pw[1, 1]
    den = a_re * a_re + a_im * a_im
    nr, ni = lb_re - 1.0, lb_im
    f_re = (nr * a_re + ni * a_im) / den
    f_im = (ni * a_re - nr * a_im) / den
    br, bi = B_re.astype(F32), B_im.astype(F32)
    bb_re = f_re[..., None] * br - f_im[..., None] * bi
    bb_im = f_re[..., None] * bi + f_im[..., None] * br

    eye = jnp.eye(gs, dtype=F32)

    def slab_b(x):
        x = x.reshape(n_slab, gs, n_p, n_h).transpose(0, 1, 3, 2)
        return (x[:, :, :, None, :] * eye[None, :, None, :, None]).reshape(n_slab, LANES, k)

    def slab_c(x):
        x = x.reshape(n_slab, gs, n_h, n_p).transpose(0, 1, 3, 2)
        return (x[:, :, :, None, :] * eye[None, :, None, :, None]).reshape(n_slab, k, LANES)

    p_tab = pw.reshape(2, L + 1, n_slab, k).transpose(2, 0, 1, 3)
    pt_tab = jnp.pad(pw.reshape(2, L + 1, n_slab, k).transpose(2, 0, 3, 1),
                     ((0, 0), (0, 0), (0, 0), (0, LANES - (L + 1))))
    mat_b = pl.BlockSpec((None, LANES, k), lambda j: (j, 0, 0))
    mat_c = pl.BlockSpec((None, k, LANES), lambda j: (j, 0, 0))
    t_op, w_op, v_op = pl.pallas_call(
        functools.partial(_s5_operators_body, L=L),
        grid=(n_slab,),
        in_specs=[mat_b, mat_b, mat_c, mat_c,
                  pl.BlockSpec((None, 2, L + 1, k), lambda j: (j, 0, 0, 0)),
                  pl.BlockSpec((None, 2, k, LANES), lambda j: (j, 0, 0, 0)),
                  pl.BlockSpec((1, LANES), lambda j: (0, j))],
        out_specs=[pl.BlockSpec((None, L * LANES, L * LANES), lambda j: (j, 0, 0)),
                   pl.BlockSpec((None, L * LANES, 2 * k), lambda j: (j, 0, 0)),
                   pl.BlockSpec((None, 2 * k, L * LANES), lambda j: (j, 0, 0))],
        out_shape=[jax.ShapeDtypeStruct((n_slab, L * LANES, L * LANES), BF16),
                   jax.ShapeDtypeStruct((n_slab, L * LANES, 2 * k), BF16),
                   jax.ShapeDtypeStruct((n_slab, 2 * k, L * LANES), BF16)],
        compiler_params=_cparams("parallel"),
        name="s5_operators",
    )(slab_b(bb_re), slab_b(bb_im), slab_c(C_re.astype(F32)), slab_c(C_im.astype(F32)),
      p_tab, pt_tab, D_skip.astype(F32).reshape(1, n_g * n_h))

    a_op = pw[:, L].reshape(2, n_slab, k).transpose(1, 0, 2).reshape(1, n_slab * 2 * k)
    return t_op, w_op, v_op, a_op


def _chunk_rows(piece_refs):
    return jnp.concatenate([r[...] for r in piece_refs], axis=1)


def _s5_increment_body(*refs):
    w_ref, z_ref = refs[-2], refs[-1]
    z_ref[...] = jnp.dot(_chunk_rows(refs[:-2]), w_ref[...], preferred_element_type=F32)


def _s5_scan_body(z_ref, a_ref, o_ref, st_ref, *, tc):
    @pl.when(pl.program_id(2) == 0)
    def _():
        st_ref[...] = jnp.zeros_like(st_ref)

    half = z_ref.shape[1] // 2
    a_re, a_im = a_ref[:, :half], a_ref[:, half:]

    def step(i, carry):
        re, im = carry
        base = pl.multiple_of(i * SUBLANES, SUBLANES)
        inc = z_ref[pl.ds(base, SUBLANES), :]
        before_re, before_im = [], []
        for r in range(SUBLANES):
            before_re.append(re)
            before_im.append(im)
            re, im = (a_re * re - a_im * im + inc[r:r + 1, :half],
                      a_re * im + a_im * re + inc[r:r + 1, half:])
        o_ref[pl.ds(base, SUBLANES), :half] = jnp.concatenate(before_re, axis=0)
        o_ref[pl.ds(base, SUBLANES), half:] = jnp.concatenate(before_im, axis=0)
        return re, im

    re, im = lax.fori_loop(0, tc // SUBLANES, step, (st_ref[0:1, :half], st_ref[0:1, half:]))
    st_ref[0:1, :half] = re
    st_ref[0:1, half:] = im


def _s5_output_body(*refs):
    x_ref, t_ref, v_ref, o_ref = refs[-4:]
    y = jnp.dot(_chunk_rows(refs[:-4]), t_ref[...], preferred_element_type=F32)
    y = y + jnp.dot(x_ref[...].astype(BF16), v_ref[...], preferred_element_type=F32)
    o_ref[...] = jax.nn.gelu(y).astype(o_ref.dtype)


def _s5_glu_body(*refs):
    w_ref, b_ref, o_ref = refs[-3:]
    y = _chunk_rows(refs[:-3])
    gate = jax.nn.sigmoid(jnp.dot(y, w_ref[...], preferred_element_type=F32) + b_ref[...])
    o_ref[...] = (y.astype(F32) * gate).astype(o_ref.dtype)


def _s5_branch(u, ops, w_glu, b_glu, bsz, seq):
    t_op, w_op, v_op, a_op = ops
    n_slab = t_op.shape[0]
    L = S5_CHUNK
    width = u.shape[1]
    cw, sw = t_op.shape[1], w_op.shape[2]
    nc = bsz * seq // L
    ncb = seq // L
    uu = u.reshape(nc, L * width)
    tm = min(512, nc)
    pieces = [pl.BlockSpec((tm, LANES), lambda j, i, s=s: (i, s * n_slab + j)) for s in range(L)]

    z = pl.pallas_call(
        _s5_increment_body,
        grid=(n_slab, nc // tm),
        in_specs=pieces + [pl.BlockSpec((None, cw, sw), lambda j, i: (j, 0, 0))],
        out_specs=pl.BlockSpec((tm, sw), lambda j, i: (i, j)),
        out_shape=jax.ShapeDtypeStruct((nc, n_slab * sw), F32),
        compiler_params=_cparams("parallel", "parallel"),
        name="s5_increment",
    )(*([uu] * L), w_op)

    tc = min(128, ncb)
    nt = ncb // tc
    blk = pl.BlockSpec((tc, sw), lambda b, j, t: (b * nt + t, j))
    xprev = pl.pallas_call(
        functools.partial(_s5_scan_body, tc=tc),
        grid=(bsz, n_slab, nt),
        in_specs=[blk, pl.BlockSpec((1, sw), lambda b, j, t: (0, j))],
        out_specs=blk,
        out_shape=jax.ShapeDtypeStruct((nc, n_slab * sw), F32),
        scratch_shapes=[pltpu.VMEM((SUBLANES, sw), F32)],
        compiler_params=_cparams("parallel", "parallel", "arbitrary"),
        name="s5_scan",
    )(z, a_op)

    ys = pl.pallas_call(
        _s5_output_body,
        grid=(n_slab, nc // tm),
        in_specs=pieces + [pl.BlockSpec((tm, sw), lambda j, i: (i, j)),
                           pl.BlockSpec((None, cw, cw), lambda j, i: (j, 0, 0)),
                           pl.BlockSpec((None, sw, cw), lambda j, i: (j, 0, 0))],
        out_specs=pl.BlockSpec((tm, cw), lambda j, i: (i, j)),
        out_shape=jax.ShapeDtypeStruct((nc, n_slab * cw), BF16),
        compiler_params=_cparams("parallel", "parallel"),
        name="s5_output",
    )(*([uu] * L), xprev, t_op, v_op)

    tok = [pl.BlockSpec((tm, LANES), lambda i, t, j=j: (i, j * L + t)) for j in range(n_slab)]
    out = pl.pallas_call(
        _s5_glu_body,
        grid=(nc // tm, L),
        in_specs=tok + [pl.BlockSpec((width, width), lambda i, t: (0, 0)),
                        pl.BlockSpec((1, width), lambda i, t: (0, 0))],
        out_specs=pl.BlockSpec((tm, width), lambda i, t: (i, t)),
        out_shape=jax.ShapeDtypeStruct((nc, L * width), BF16),
        compiler_params=_cparams("parallel", "parallel"),
        name="s5_glu",
    )(*([ys] * n_slab), w_glu.astype(BF16), b_glu.reshape(1, width).astype(F32))
    return out.reshape(bsz * seq, width)


def _row(v):
    return v.reshape(1, -1).astype(F32)


def _layer(x, mem_n, p, bsz, seq, mem_len):
    d = x.shape[1]
    aw = p["w_attn_up"].shape[0]
    heads = aw // ATTN_HEAD_DIM
    sw = p["w_ssm_up"].shape[0]
    off_f = 3 * aw
    off_u = off_f + heads
    off_g = off_u + sw

    h = _rmsnorm([x], p["g_mix"], BF16, tm=256)
    w_in_t = p["w_in"].T
    tile = 256
    w_qkv = _transpose_cast(w_in_t, 0, off_f, tile, lambda j: j, tk=2048,
                            scale=LOG2E * ATTN_HEAD_DIM ** -0.5, n_scaled=aw // tile)
    w_f = jnp.pad(w_in_t[off_f:off_u].T, ((0, 0), (0, LANES - heads))).astype(BF16)
    w_u = _transpose_cast(w_in_t, off_u, sw, tile, lambda j: j, tk=2048)
    tn_mix = tile
    n_mix = d // tn_mix
    w_g = _transpose_cast(w_in_t, off_g, N_BRANCH * d, tn_mix,
                          lambda j: (j % n_mix) * N_BRANCH + j // n_mix, tk=2048)

    qkv = _fused_matmul([(h, w_qkv, 0)], _ep_plain, BF16, n=3 * aw, tm=1024, tn=1024,
                        name="qkv_proj")
    b_f = jnp.pad(p["b_f"].astype(F32), (0, LANES - heads)).reshape(1, LANES)
    log_f = _fused_matmul([(h, w_f, 0)], _ep_log_sigmoid, F32, n=LANES, tm=1024, tn=LANES,
                          aux=[(b_f, 0)], name="forget_proj")
    fox, fox_rows, fox_position = _fox_attention(qkv, _forget_bias(log_f, bsz, seq, heads),
                                                 bsz, seq, heads, tq=1024)

    u, w_out = _fused_matmul([(h, w_u, 0)], _ep_plain, BF16, n=sw, tm=1024, tn=1024,
                             side_cast=p["w_out"], name="ssm_in_proj")
    ops = _s5_operators(p["A_re"], p["A_im"], p["log_dt"], p["B_re"], p["B_im"],
                        p["C_re"], p["C_im"], p["D_skip"])
    y = _s5_branch(u, ops, p["w_glu"], p["b_glu"], bsz, seq)

    b_gate = (p["b_gate"].astype(F32).reshape(N_BRANCH, n_mix, tn_mix).transpose(1, 0, 2)
              .reshape(1, N_BRANCH * d))
    merged, w_ff1 = _fused_matmul(
        [(h, w_g, 0, N_BRANCH), (fox, p["w_attn_up"].astype(BF16), 0, 1, fox_position),
         (y, p["w_ssm_up"].astype(BF16), 0)],
        _ep_gated_merge, BF16, n=d, tm=fox_rows, tn=tn_mix,
        aux=[(b_gate, 0, N_BRANCH)], side_cast=p["w_ff1"], name="gated_merge")
    x = _fused_matmul([(merged, w_out, 0)], _ep_add_res, F32, n=d,
                      tm=1024, tn=512, res=x, name="mixer_out_proj")

    xw = p["wq_x"].shape[1]
    wq = (p["wq_x"] * (xw // XATTN_HEADS) ** -0.5).astype(BF16)
    w_kv = jnp.concatenate([p["wk_x"], p["wv_x"]], axis=1).astype(BF16)
    kv = _fused_matmul([(mem_n, w_kv, 0)], _ep_plain, BF16, n=2 * xw, tm=512, tn=1024,
                       name="xattn_kv_proj")
    x, hm = _xattn_block(x, p["g_xattn"], wq, kv, p["wo_x"].astype(BF16), p["g_mlp"],
                         seq, mem_len, tm=256)

    dff = p["w_ff1"].shape[1]
    hid, w_ff2 = _fused_matmul([(hm, w_ff1, 0)], _ep_relu2, BF16, n=dff, tm=1024, tn=1024,
                               side_cast=p["w_ff2"], name="mlp_up")
    x = _matmul_kpieces(hid, w_ff2, x, n_pieces=4, tm=512, tn=256, name="mlp_down")
    return x


_LAYER_PARAMS = ("g_mix", "w_in", "b_f", "b_gate", "A_re", "A_im", "log_dt", "B_re", "B_im",
                 "C_re", "C_im", "D_skip", "w_glu", "b_glu", "w_attn_up", "w_ssm_up", "w_out",
                 "g_xattn", "g_mem", "wq_x", "wk_x", "wv_x", "wo_x", "g_mlp", "w_ff1", "w_ff2")


def kernel(x, mem, g_mix, w_in, b_f, b_gate, A_re, A_im, log_dt, B_re, B_im, C_re, C_im, D_skip, w_glu, b_glu, w_attn_up, w_ssm_up, w_out, g_xattn, g_mem, wq_x, wk_x, wv_x, wo_x, g_mlp, w_ff1, w_ff2, g_final):
    stacked = dict(zip(_LAYER_PARAMS, (g_mix, w_in, b_f, b_gate, A_re, A_im, log_dt, B_re, B_im,
                                       C_re, C_im, D_skip, w_glu, b_glu, w_attn_up, w_ssm_up,
                                       w_out, g_xattn, g_mem, wq_x, wk_x, wv_x, wo_x, g_mlp,
                                       w_ff1, w_ff2)))
    bsz, seq, d = x.shape
    mem_len = mem.shape[1]
    xt = x.reshape(bsz * seq, d)
    mem2 = mem.reshape(bsz * mem_len, d)
    for l in range(g_mix.shape[0]):
        p = {k: v[l] for k, v in stacked.items()}
        mem_n = _rmsnorm([mem2], p["g_mem"], BF16, tm=256)
        xt = _layer(xt, mem_n, p, bsz, seq, mem_len)
    out = _rmsnorm([xt], g_final, x.dtype, tm=256)
    return out.reshape(bsz, seq, d)
```

```python
import functools
import math

import jax
import jax.numpy as jnp
from jax import lax
from jax.experimental import pallas as pl
from jax.experimental.pallas import tpu as pltpu

F32 = jnp.float32
BF16 = jnp.bfloat16

V7X_VMEM_LIMIT_BYTES = 56 * 1024 * 1024
LANES = 128
SUBLANES = 8
BF16_SUBLANES = 16

EPS = 1e-6
NEG_INF = -1e30
LOG2E = math.log2(math.e)
N_BIAS_PIECES = 3
ATTN_HEAD_DIM = 128
S5_CHUNK = 16
XATTN_HEADS = 4
N_BRANCH = 2


def _cparams(*sem):
    return pltpu.CompilerParams(dimension_semantics=sem,
                                vmem_limit_bytes=V7X_VMEM_LIMIT_BYTES)


def _rms_scale(x, g):
    return (x * lax.rsqrt(jnp.mean(x * x, axis=-1, keepdims=True) + EPS)) * g


def _log_sigmoid(z):
    return jnp.minimum(z, 0.0) - jnp.log1p(jnp.exp(-jnp.abs(z)))


def _sigmoid(z):
    return 0.5 * jnp.tanh(0.5 * z) + 0.5


def _rmsnorm_body(x_ref, g_ref, *rest, has_proj):
    h = _rms_scale(x_ref[...].astype(F32), g_ref[...]).astype(rest[-1 - has_proj].dtype)
    if has_proj:
        w_ref, b_ref, o_ref, f_ref = rest
        f_ref[...] = _log_sigmoid(jnp.dot(h, w_ref[...], preferred_element_type=F32) + b_ref[...])
    else:
        o_ref, = rest
    o_ref[...] = h


def _rmsnorm(x, g, out_dtype, tm, forget=None):
    m, d = x.shape
    tm = min(tm, m)
    row = pl.BlockSpec((tm, d), lambda i: (i, 0))
    operands = [x, g.reshape(1, d).astype(F32)]
    in_specs = [row, pl.BlockSpec((1, d), lambda i: (0, 0))]
    out_specs, out_shape = [row], [jax.ShapeDtypeStruct((m, d), out_dtype)]
    if forget is not None:
        n = forget[0].shape[1]
        operands += list(forget)
        in_specs += [pl.BlockSpec((d, n), lambda i: (0, 0)), pl.BlockSpec((1, n), lambda i: (0, 0))]
        out_specs.append(pl.BlockSpec((tm, n), lambda i: (i, 0)))
        out_shape.append(jax.ShapeDtypeStruct((m, n), F32))
    outs = pl.pallas_call(
        functools.partial(_rmsnorm_body, has_proj=forget is not None),
        grid=(m // tm,),
        in_specs=in_specs,
        out_specs=out_specs,
        out_shape=out_shape,
        compiler_params=_cparams("parallel"),
        name="rmsnorm",
    )(*operands)
    return outs if forget is not None else outs[0]


def _mm_body(*refs, n_pairs, n_aux, has_res, has_side, epilogue):
    accs = [jnp.dot(refs[2 * p][...], refs[2 * p + 1][...], preferred_element_type=F32)
            for p in range(n_pairs)]
    pos = 2 * n_pairs
    aux = [refs[pos + i][...] for i in range(n_aux)]
    pos += n_aux
    res = refs[pos][...].astype(F32) if has_res else None
    pos += has_res
    if has_side:
        refs[-1][...] = refs[pos][...].astype(refs[-1].dtype)
        pos += 1
    o_ref = refs[pos]
    o_ref[...] = epilogue(accs, aux, res).astype(o_ref.dtype)


def _fused_matmul(pairs, epilogue, out_dtype, *, n, tm, tn, aux=(), res=None, side_cast=None,
                  name="fused_matmul"):
    m = pairs[0][0].shape[0]
    tm, tn = min(tm, m), min(tn, n)
    n_aux, has_res, has_side = len(aux), res is not None, side_cast is not None
    nj = n // tn
    operands, in_specs = [], []
    for lhs, rhs, off, *opt in pairs:
        kp = lhs.shape[1]
        width = tn * (opt[0] if opt else 1)
        row_block = opt[1] if len(opt) > 1 else (lambda i: i)
        operands += [lhs, rhs]
        in_specs += [pl.BlockSpec((tm, kp), lambda i, j, rb=row_block: (rb(i), 0)),
                     pl.BlockSpec((kp, width), lambda i, j, off=off: (0, j + off))]
    for vec, off, *mult in aux:
        width = tn * (mult[0] if mult else 1)
        operands.append(vec)
        in_specs.append(pl.BlockSpec((1, width), lambda i, j, off=off: (0, j + off)))
    if has_res:
        operands.append(res)
        in_specs.append(pl.BlockSpec((tm, tn), lambda i, j: (i, j)))
    out_specs = [pl.BlockSpec((tm, tn), lambda i, j: (i, j))]
    out_shape = [jax.ShapeDtypeStruct((m, n), out_dtype)]
    if has_side:
        slab = side_cast.shape[0] // ((m // tm) * nj)
        assert slab * (m // tm) * nj == side_cast.shape[0] and slab % BF16_SUBLANES == 0
        side_spec = pl.BlockSpec((slab, side_cast.shape[1]), lambda i, j: (i * nj + j, 0))
        operands.append(side_cast)
        in_specs.append(side_spec)
        out_specs.append(side_spec)
        out_shape.append(jax.ShapeDtypeStruct(side_cast.shape, BF16))
    outs = pl.pallas_call(
        functools.partial(_mm_body, n_pairs=len(pairs), n_aux=n_aux, has_res=has_res,
                          has_side=has_side, epilogue=epilogue),
        grid=(m // tm, nj),
        in_specs=in_specs,
        out_specs=out_specs,
        out_shape=out_shape,
        compiler_params=_cparams("parallel", "parallel"),
        name=name,
    )(*operands)
    return outs if has_side else outs[0]


W_IN_EDGE_ROWS = 16


def _transpose_cast_body(a_ref, b_ref, o_ref, *, shift, scale, n_scaled):
    rows = a_ref.shape[0]
    x = jnp.concatenate([a_ref[...], b_ref[...]], axis=0)[shift:shift + rows, :]
    if n_scaled:
        x = x * jnp.where(pl.program_id(0) < n_scaled, scale, 1.0)
    o_ref[...] = x.T.astype(o_ref.dtype)


def _transpose_cast(wt, start, n_cols, tile, out_block, tk, scale=1.0, n_scaled=0):
    k = wt.shape[1]
    base = start // tile * tile
    shift = start - base
    assert shift % SUBLANES == 0 and shift <= W_IN_EDGE_ROWS and n_cols % tile == 0
    assert start + n_cols + (W_IN_EDGE_ROWS - shift) <= wt.shape[0] or shift == 0
    edge_per_tile = tile // W_IN_EDGE_ROWS
    last_edge = wt.shape[0] // W_IN_EDGE_ROWS - 1
    return pl.pallas_call(
        functools.partial(_transpose_cast_body, shift=shift, scale=scale, n_scaled=n_scaled),
        grid=(n_cols // tile, k // tk),
        in_specs=[pl.BlockSpec((tile, tk), lambda j, c: (base // tile + j, c)),
                  pl.BlockSpec((W_IN_EDGE_ROWS, tk),
                               lambda j, c: (jnp.minimum((base // tile + j + 1) * edge_per_tile,
                                                         last_edge), c))],
        out_specs=pl.BlockSpec((tk, tile), lambda j, c: (c, out_block(j))),
        out_shape=jax.ShapeDtypeStruct((k, n_cols), BF16),
        compiler_params=_cparams("parallel", "parallel"),
        name="transpose_cast",
    )(wt, wt)


def _matmul_kpieces_body(*refs, n_pieces):
    res_ref, o_ref = refs[2 * n_pieces], refs[2 * n_pieces + 1]
    acc = res_ref[...].astype(F32)
    for p in range(n_pieces):
        acc = acc + jnp.dot(refs[p][...], refs[n_pieces + p][...], preferred_element_type=F32)
    o_ref[...] = acc.astype(o_ref.dtype)


def _matmul_kpieces(lhs, rhs, res, *, n_pieces, tm, tn, name):
    m, k = lhs.shape
    n = rhs.shape[1]
    kp = k // n_pieces
    tm, tn = min(tm, m), min(tn, n)
    in_specs = ([pl.BlockSpec((tm, kp), lambda i, j, p=p: (i, p)) for p in range(n_pieces)]
                + [pl.BlockSpec((kp, tn), lambda i, j, p=p: (p, j)) for p in range(n_pieces)]
                + [pl.BlockSpec((tm, tn), lambda i, j: (i, j))])
    return pl.pallas_call(
        functools.partial(_matmul_kpieces_body, n_pieces=n_pieces),
        grid=(m // tm, n // tn),
        in_specs=in_specs,
        out_specs=pl.BlockSpec((tm, tn), lambda i, j: (i, j)),
        out_shape=jax.ShapeDtypeStruct((m, n), res.dtype),
        compiler_params=_cparams("parallel", "parallel"),
        name=name,
    )(*([lhs] * n_pieces), *([rhs] * n_pieces), res)


def _ep_plain(accs, aux, res):
    return accs[0]


def _ep_add_res(accs, aux, res):
    return res + accs[0]


def _ep_gated_merge(accs, aux, res):
    tn = accs[1].shape[1]
    gates = _sigmoid(accs[0] + aux[0])
    return gates[:, :tn] * accs[1] + gates[:, tn:] * accs[2]


def _ep_relu2(accs, aux, res):
    r = jnp.maximum(accs[0], 0.0)
    return r * r


def _forget_bias_body(x_ref, o_ref, carry_ref, *, blk, nblk, heads):
    @pl.when(pl.program_id(1) == 0)
    def _():
        carry_ref[...] = jnp.zeros_like(carry_ref)

    r = lax.broadcasted_iota(jnp.int32, (blk, blk), 0)
    c = lax.broadcasted_iota(jnp.int32, (blk, blk), 1)
    tri = (c <= r).astype(F32)
    lane = lax.broadcasted_iota(jnp.int32, (blk, LANES), 1)

    def step(i, carry):
        base = pl.multiple_of(i * blk, blk)
        cs = jnp.dot(tri, x_ref[pl.ds(base, blk), :], preferred_element_type=F32,
                     precision=lax.Precision.HIGHEST) + carry
        for h in range(heads):
            bias = jnp.broadcast_to(cs[:, h:h + 1] * (-LOG2E), (blk, LANES))
            hi = bias.astype(BF16).astype(F32)
            mid = (bias - hi).astype(BF16).astype(F32)
            lo = bias - hi - mid
            pieces = jnp.where(lane == 0, hi, jnp.where(lane == 1, mid,
                                                        jnp.where(lane == 2, lo, 0.0)))
            o_ref[pl.ds(base, blk), h * LANES:(h + 1) * LANES] = pieces.astype(o_ref.dtype)
        return cs[blk - 1:blk, :]

    carry_ref[0:1, :] = lax.fori_loop(0, nblk, step, carry_ref[0:1, :])


def _forget_bias(log_f, bsz, seq, heads):
    blk = min(256, seq)
    tt = min(1024, seq)
    nt = seq // tt
    return pl.pallas_call(
        functools.partial(_forget_bias_body, blk=blk, nblk=tt // blk, heads=heads),
        grid=(bsz, nt),
        in_specs=[pl.BlockSpec((tt, LANES), lambda b, t: (b * nt + t, 0))],
        out_specs=pl.BlockSpec((tt, heads * LANES), lambda b, t: (b * nt + t, 0)),
        out_shape=jax.ShapeDtypeStruct((bsz * seq, heads * LANES), BF16),
        scratch_shapes=[pltpu.VMEM((SUBLANES, LANES), F32)],
        compiler_params=_cparams("parallel", "arbitrary"),
        name="forget_bias",
    )(log_f)


def _fox_body(qa_ref, qb_ref, k_ref, v_ref, cp_ref, o_ref, kaug_ref, vaug_ref, s_ref,
              *, tq, seq):
    p = pl.program_id(2)
    nq = seq // tq
    dh = qa_ref.shape[1]

    @pl.when(p == 0)
    def _():
        def fill(j, _):
            rows = pl.ds(pl.multiple_of(j * tq, tq), tq)
            kaug_ref[rows, :dh] = k_ref[rows, :]
            kaug_ref[rows, dh:] = cp_ref[rows, :]
            vaug_ref[rows, :dh] = v_ref[rows, :]
            vaug_ref[rows, dh:] = jnp.ones((tq, dh), vaug_ref.dtype)
            return 0
        lax.fori_loop(0, nq, fill, 0)

    lane = lax.broadcasted_iota(jnp.int32, (tq, dh), 1)
    unit = (lane < N_BIAS_PIECES).astype(qa_ref.dtype)
    q_a = jnp.concatenate([qa_ref[...], unit], axis=1)
    q_b = jnp.concatenate([qb_ref[...], unit], axis=1)
    above_diag = (lax.broadcasted_iota(jnp.int32, (tq, tq), 1)
                  > lax.broadcasted_iota(jnp.int32, (tq, tq), 0))

    def key_rows(t):
        j = jnp.where(t <= p, t, t - (p + 1))
        return pl.ds(pl.multiple_of(j * tq, tq), tq)

    def scores(t):
        q = jnp.where(t <= p, q_a, q_b)
        s = lax.dot_general(q, kaug_ref[key_rows(t), :], (((1,), (1,)), ((), ())),
                            preferred_element_type=F32)
        if t == nq:
            s = jnp.where(above_diag, NEG_INF, s)
        elif t < nq // 2:
            s = jnp.where(jnp.logical_and(above_diag, t == p), NEG_INF, s)
        return s

    m = jnp.full((tq, 1), NEG_INF, F32)
    acc = jnp.zeros((tq, 2 * dh), F32)
    s_ref[0] = scores(0)
    for t in range(nq + 1):
        if t < nq:
            s_ref[(t + 1) % 2] = scores(t + 1)
        s = s_ref[t % 2]
        if 1 <= t <= nq // 2:
            first_b = t == p + 1
            m = jnp.where(first_b, NEG_INF, m)
            acc = jnp.where(first_b, 0.0, acc)
        m_new = jnp.maximum(m, jnp.max(s, axis=-1, keepdims=True))
        acc = jnp.exp2(m - m_new) * acc + jnp.dot(
            jnp.exp2(s - m_new).astype(vaug_ref.dtype), vaug_ref[key_rows(t), :],
            preferred_element_type=F32)
        m = m_new
        if t == 0:
            o_ref[:tq, :] = (acc[:, :dh] / acc[:, dh:]).astype(o_ref.dtype)
        elif t < nq // 2:
            out = (acc[:, :dh] / acc[:, dh:]).astype(o_ref.dtype)
            o_ref[:tq, :] = jnp.where(t <= p, out, o_ref[:tq, :])
    o_ref[tq:, :] = (acc[:, :dh] / acc[:, dh:]).astype(o_ref.dtype)


def _fox_attention(qkv, cp, bsz, seq, heads, tq):
    dh = ATTN_HEAD_DIM
    tq = min(tq, seq // 2)
    nq = seq // tq
    half = nq // 2
    assert nq % 2 == 0

    def position(i):
        b, qi = i // nq, i % nq
        return jnp.where(qi < half, (b * half + qi) * 2, (b * half + nq - 1 - qi) * 2 + 1)

    out = pl.pallas_call(
        functools.partial(_fox_body, tq=tq, seq=seq),
        grid=(bsz, heads, half),
        in_specs=[pl.BlockSpec((tq, dh), lambda b, h, p: (b * nq + p, h)),
                  pl.BlockSpec((tq, dh), lambda b, h, p: (b * nq + nq - 1 - p, h)),
                  pl.BlockSpec((seq, dh), lambda b, h, p: (b, heads + h)),
                  pl.BlockSpec((seq, dh), lambda b, h, p: (b, 2 * heads + h)),
                  pl.BlockSpec((seq, dh), lambda b, h, p: (b, h))],
        out_specs=pl.BlockSpec((2 * tq, dh), lambda b, h, p: (b * half + p, h)),
        out_shape=jax.ShapeDtypeStruct((bsz * seq, heads * dh), BF16),
        scratch_shapes=[pltpu.VMEM((seq, 2 * dh), BF16), pltpu.VMEM((seq, 2 * dh), BF16),
                        pltpu.VMEM((2, tq, tq), F32)],
        compiler_params=_cparams("parallel", "parallel", "arbitrary"),
        name="fox_attention",
    )(qkv, qkv, qkv, qkv, cp)
    return out, tq, position


def _xattn_block_body(x_ref, gx_ref, wq_ref, kv_ref, wo_ref, gn_ref, xo_ref, hn_ref):
    x = x_ref[...]
    width = wq_ref.shape[1]
    dh = width // XATTN_HEADS
    hx = _rms_scale(x, gx_ref[...]).astype(wq_ref.dtype)
    q = jnp.dot(hx, wq_ref[...], preferred_element_type=F32).astype(kv_ref.dtype)
    heads = []
    for hd in range(XATTN_HEADS):
        k = kv_ref[:, hd * dh:(hd + 1) * dh]
        v = kv_ref[:, width + hd * dh:width + (hd + 1) * dh]
        s = lax.dot_general(q[:, hd * dh:(hd + 1) * dh], k, (((1,), (1,)), ((), ())),
                            preferred_element_type=F32)
        p = jnp.exp(s - jnp.max(s, axis=-1, keepdims=True))
        l = jnp.sum(p, axis=-1, keepdims=True)
        o = jnp.dot(p.astype(v.dtype), v, preferred_element_type=F32) / l
        heads.append(o.astype(wo_ref.dtype))
    x2 = x + jnp.dot(jnp.concatenate(heads, axis=1), wo_ref[...], preferred_element_type=F32)
    xo_ref[...] = x2
    hn_ref[...] = _rms_scale(x2, gn_ref[...]).astype(hn_ref.dtype)


def _xattn_block(x, g_x, wq, kv, wo, g_next, seq, mem_len, tm):
    t, d = x.shape
    width = wq.shape[1]
    tm = min(tm, seq)
    nq = seq // tm
    row = pl.BlockSpec((tm, d), lambda i: (i, 0))
    vec = pl.BlockSpec((1, d), lambda i: (0, 0))
    resident = pl.Buffered(1)
    return pl.pallas_call(
        _xattn_block_body,
        grid=(t // tm,),
        in_specs=[row, vec,
                  pl.BlockSpec((d, width), lambda i: (0, 0), pipeline_mode=resident),
                  pl.BlockSpec((mem_len, 2 * width), lambda i: (i // nq, 0)),
                  pl.BlockSpec((width, d), lambda i: (0, 0), pipeline_mode=resident),
                  vec],
        out_specs=[row, row],
        out_shape=[jax.ShapeDtypeStruct((t, d), F32), jax.ShapeDtypeStruct((t, d), BF16)],
        compiler_params=_cparams("parallel"),
        name="xattn_block",
    )(x, g_x.reshape(1, d).astype(F32), wq, kv, wo, g_next.reshape(1, d).astype(F32))


def _s5_operators_body(bre_ref, bim_ref, cre_ref, cim_ref, p_ref, pt_ref, d_ref,
                        t_ref, w_ref, v_ref, *, L):
    hp = lax.Precision.HIGHEST
    bre, bim = bre_ref[...], bim_ref[...]
    cre, cim = cre_ref[...], cim_ref[...]
    k = bre.shape[1]
    row = lax.broadcasted_iota(jnp.int32, (LANES, LANES), 0)
    col = lax.broadcasted_iota(jnp.int32, (LANES, LANES), 1)
    lag_blocks = []
    for tau in range(L):
        pr, pi = p_ref[0, tau:tau + 1, :], p_ref[1, tau:tau + 1, :]
        xr, xi = bre * pr - bim * pi, bre * pi + bim * pr
        rows = slice((L - 1 - tau) * LANES, (L - tau) * LANES)
        w_ref[rows, :k] = xr.astype(w_ref.dtype)
        w_ref[rows, k:] = xi.astype(w_ref.dtype)
        d_tau = (jnp.dot(xr, cre, preferred_element_type=F32, precision=hp)
                 - jnp.dot(xi, cim, preferred_element_type=F32, precision=hp))
        if tau == 0:
            d_tau = d_tau + jnp.where(row == col, d_ref[...], 0.0)
        lag_blocks.append(d_tau.astype(t_ref.dtype))
    zero = jnp.zeros((LANES, LANES), t_ref.dtype)
    for s in range(L):
        for t in range(L):
            t_ref[s * LANES:(s + 1) * LANES, t * LANES:(t + 1) * LANES] = (
                lag_blocks[t - s] if t >= s else zero)
    for t in range(L):
        qr, qi = pt_ref[0, :, t + 1:t + 2], pt_ref[1, :, t + 1:t + 2]
        cols = slice(t * LANES, (t + 1) * LANES)
        v_ref[:k, cols] = (cre * qr - cim * qi).astype(v_ref.dtype)
        v_ref[k:, cols] = (-(cre * qi + cim * qr)).astype(v_ref.dtype)


def _s5_operators(A_re, A_im, log_dt, B_re, B_im, C_re, C_im, D_skip):
    n_g, n_p = A_re.shape
    n_h = B_re.shape[-1]
    L = S5_CHUNK
    gs = LANES // n_h
    n_slab = n_g // gs
    k = gs * n_p
    a_re, a_im = A_re.astype(F32), A_im.astype(F32)
    dt = jnp.exp(log_dt.astype(F32))[:, None]
    tau = jnp.arange(L + 1, dtype=F32)[:, None, None]
    mag = jnp.exp(tau * (dt * a_re))
    ang = tau * (dt * a_im)
    pw = jnp.stack([mag * jnp.cos(ang), mag * jnp.sin(ang)])
    lb_re, lb_im = pw[0, 1], pw[1, 1]
    den = a_re * a_re + a_im * a_im
    nr, ni = lb_re - 1.0, lb_im
    f_re = (nr * a_re + ni * a_im) / den
    f_im = (ni * a_re - nr * a_im) / den
    br, bi = B_re.astype(F32), B_im.astype(F32)
    bb_re = f_re[..., None] * br - f_im[..., None] * bi
    bb_im = f_re[..., None] * bi + f_im[..., None] * br

    eye = jnp.eye(gs, dtype=F32)

    def slab_b(x):
        x = x.reshape(n_slab, gs, n_p, n_h).transpose(0, 1, 3, 2)
        return (x[:, :, :, None, :] * eye[None, :, None, :, None]).reshape(n_slab, LANES, k)

    def slab_c(x):
        x = x.reshape(n_slab, gs, n_h, n_p).transpose(0, 1, 3, 2)
        return (x[:, :, :, None, :] * eye[None, :, None, :, None]).reshape(n_slab, k, LANES)

    p_tab = pw.reshape(2, L + 1, n_slab, k).transpose(2, 0, 1, 3)
    pt_tab = jnp.pad(pw.reshape(2, L + 1, n_slab, k).transpose(2, 0, 3, 1),
                     ((0, 0), (0, 0), (0, 0), (0, LANES - (L + 1))))
    mat_b = pl.BlockSpec((None, LANES, k), lambda j: (j, 0, 0))
    mat_c = pl.BlockSpec((None, k, LANES), lambda j: (j, 0, 0))
    t_op, w_op, v_op = pl.pallas_call(
        functools.partial(_s5_operators_body, L=L),
        grid=(n_slab,),
        in_specs=[mat_b, mat_b, mat_c, mat_c,
                  pl.BlockSpec((None, 2, L + 1, k), lambda j: (j, 0, 0, 0)),
                  pl.BlockSpec((None, 2, k, LANES), lambda j: (j, 0, 0, 0)),
                  pl.BlockSpec((1, LANES), lambda j: (0, j))],
        out_specs=[pl.BlockSpec((None, L * LANES, L * LANES), lambda j: (j, 0, 0)),
                   pl.BlockSpec((None, L * LANES, 2 * k), lambda j: (j, 0, 0)),
                   pl.BlockSpec((None, 2 * k, L * LANES), lambda j: (j, 0, 0))],
        out_shape=[jax.ShapeDtypeStruct((n_slab, L * LANES, L * LANES), BF16),
                   jax.ShapeDtypeStruct((n_slab, L * LANES, 2 * k), BF16),
                   jax.ShapeDtypeStruct((n_slab, 2 * k, L * LANES), BF16)],
        compiler_params=_cparams("parallel"),
        name="s5_operators",
    )(slab_b(bb_re), slab_b(bb_im), slab_c(C_re.astype(F32)), slab_c(C_im.astype(F32)),
      p_tab, pt_tab, D_skip.astype(F32).reshape(1, n_g * n_h))

    a_op = pw[:, L].reshape(2, n_slab, k).transpose(1, 0, 2).reshape(1, n_slab * 2 * k)
    return t_op, w_op, v_op, a_op


def _chunk_rows(piece_refs):
    return jnp.concatenate([r[...] for r in piece_refs], axis=1)


def _s5_increment_body(*refs):
    w_ref, z_ref = refs[-2], refs[-1]
    z_ref[...] = jnp.dot(_chunk_rows(refs[:-2]), w_ref[...], preferred_element_type=F32)


def _s5_scan_body(z_ref, a_ref, o_ref, st_ref, *, tc):
    @pl.when(pl.program_id(2) == 0)
    def _():
        st_ref[...] = jnp.zeros_like(st_ref)

    half = z_ref.shape[1] // 2
    a_re, a_im = a_ref[:, :half], a_ref[:, half:]

    def step(i, carry):
        re, im = carry
        base = pl.multiple_of(i * SUBLANES, SUBLANES)
        inc = z_ref[pl.ds(base, SUBLANES), :]
        before_re, before_im = [], []
        for r in range(SUBLANES):
            before_re.append(re)
            before_im.append(im)
            re, im = (a_re * re - a_im * im + inc[r:r + 1, :half],
                      a_re * im + a_im * re + inc[r:r + 1, half:])
        o_ref[pl.ds(base, SUBLANES), :half] = jnp.concatenate(before_re, axis=0)
        o_ref[pl.ds(base, SUBLANES), half:] = jnp.concatenate(before_im, axis=0)
        return re, im

    re, im = lax.fori_loop(0, tc // SUBLANES, step, (st_ref[0:1, :half], st_ref[0:1, half:]))
    st_ref[0:1, :half] = re
    st_ref[0:1, half:] = im


def _s5_output_body(*refs):
    x_ref, t_ref, v_ref, o_ref = refs[-4:]
    y = jnp.dot(_chunk_rows(refs[:-4]), t_ref[...], preferred_element_type=F32)
    y = y + jnp.dot(x_ref[...].astype(BF16), v_ref[...], preferred_element_type=F32)
    o_ref[...] = jax.nn.gelu(y).astype(o_ref.dtype)


def _s5_glu_body(*refs):
    w_ref, b_ref, o_ref = refs[-3:]
    y = _chunk_rows(refs[:-3])
    gate = _sigmoid(jnp.dot(y, w_ref[...], preferred_element_type=F32) + b_ref[...])
    o_ref[...] = (y.astype(F32) * gate).astype(o_ref.dtype)


def _s5_branch(u, ops, w_glu, b_glu, bsz, seq):
    t_op, w_op, v_op, a_op = ops
    n_slab = t_op.shape[0]
    L = S5_CHUNK
    width = u.shape[1]
    cw, sw = t_op.shape[1], w_op.shape[2]
    nc = bsz * seq // L
    ncb = seq // L
    uu = u.reshape(nc, L * width)
    tm = min(512, nc)
    pieces = [pl.BlockSpec((tm, LANES), lambda j, i, s=s: (i, s * n_slab + j)) for s in range(L)]

    z = pl.pallas_call(
        _s5_increment_body,
        grid=(n_slab, nc // tm),
        in_specs=pieces + [pl.BlockSpec((None, cw, sw), lambda j, i: (j, 0, 0))],
        out_specs=pl.BlockSpec((tm, sw), lambda j, i: (i, j)),
        out_shape=jax.ShapeDtypeStruct((nc, n_slab * sw), F32),
        compiler_params=_cparams("parallel", "parallel"),
        name="s5_increment",
    )(*([uu] * L), w_op)

    tc = min(128, ncb)
    nt = ncb // tc
    blk = pl.BlockSpec((tc, sw), lambda b, j, t: (b * nt + t, j))
    xprev = pl.pallas_call(
        functools.partial(_s5_scan_body, tc=tc),
        grid=(bsz, n_slab, nt),
        in_specs=[blk, pl.BlockSpec((1, sw), lambda b, j, t: (0, j))],
        out_specs=blk,
        out_shape=jax.ShapeDtypeStruct((nc, n_slab * sw), F32),
        scratch_shapes=[pltpu.VMEM((SUBLANES, sw), F32)],
        compiler_params=_cparams("parallel", "parallel", "arbitrary"),
        name="s5_scan",
    )(z, a_op)

    ys = pl.pallas_call(
        _s5_output_body,
        grid=(n_slab, nc // tm),
        in_specs=pieces + [pl.BlockSpec((tm, sw), lambda j, i: (i, j)),
                           pl.BlockSpec((None, cw, cw), lambda j, i: (j, 0, 0)),
                           pl.BlockSpec((None, sw, cw), lambda j, i: (j, 0, 0))],
        out_specs=pl.BlockSpec((tm, cw), lambda j, i: (i, j)),
        out_shape=jax.ShapeDtypeStruct((nc, n_slab * cw), BF16),
        compiler_params=_cparams("parallel", "parallel"),
        name="s5_output",
    )(*([uu] * L), xprev, t_op, v_op)

    tok = [pl.BlockSpec((tm, LANES), lambda i, t, j=j: (i, j * L + t)) for j in range(n_slab)]
    out = pl.pallas_call(
        _s5_glu_body,
        grid=(nc // tm, L),
        in_specs=tok + [pl.BlockSpec((width, width), lambda i, t: (0, 0)),
                        pl.BlockSpec((1, width), lambda i, t: (0, 0))],
        out_specs=pl.BlockSpec((tm, width), lambda i, t: (i, t)),
        out_shape=jax.ShapeDtypeStruct((nc, L * width), BF16),
        compiler_params=_cparams("parallel", "parallel"),
        name="s5_glu",
    )(*([ys] * n_slab), w_glu.astype(BF16), b_glu.reshape(1, width).astype(F32))
    return out.reshape(bsz * seq, width)


def _layer(x, mem_n, p, bsz, seq, mem_len):
    d = x.shape[1]
    aw = p["w_attn_up"].shape[0]
    heads = aw // ATTN_HEAD_DIM
    sw = p["w_ssm_up"].shape[0]
    off_f = 3 * aw
    off_u = off_f + heads
    off_g = off_u + sw

    w_in_t = p["w_in"].T
    tile = 256
    w_qkv = _transpose_cast(w_in_t, 0, off_f, tile, lambda j: j, tk=d,
                            scale=LOG2E * ATTN_HEAD_DIM ** -0.5, n_scaled=aw // tile)
    w_f = jnp.pad(w_in_t[off_f:off_u].T, ((0, 0), (0, LANES - heads))).astype(BF16)
    b_f = jnp.pad(p["b_f"].astype(F32), (0, LANES - heads)).reshape(1, LANES)
    h, log_f = _rmsnorm(x, p["g_mix"], BF16, tm=256, forget=(w_f, b_f))
    w_u = _transpose_cast(w_in_t, off_u, sw, tile, lambda j: j, tk=d)
    tn_mix = tile
    n_mix = d // tn_mix
    w_g = _transpose_cast(w_in_t, off_g, N_BRANCH * d, tn_mix,
                          lambda j: (j % n_mix) * N_BRANCH + j // n_mix, tk=d)

    qkv = _fused_matmul([(h, w_qkv, 0)], _ep_plain, BF16, n=3 * aw, tm=1024, tn=1024,
                        name="qkv_proj")
    fox, fox_rows, fox_position = _fox_attention(qkv, _forget_bias(log_f, bsz, seq, heads),
                                                 bsz, seq, heads, tq=1024)

    u, w_out = _fused_matmul([(h, w_u, 0)], _ep_plain, BF16, n=sw, tm=1024, tn=1024,
                             side_cast=p["w_out"], name="ssm_in_proj")
    ops = _s5_operators(p["A_re"], p["A_im"], p["log_dt"], p["B_re"], p["B_im"],
                        p["C_re"], p["C_im"], p["D_skip"])
    y = _s5_branch(u, ops, p["w_glu"], p["b_glu"], bsz, seq)

    b_gate = (p["b_gate"].astype(F32).reshape(N_BRANCH, n_mix, tn_mix).transpose(1, 0, 2)
              .reshape(1, N_BRANCH * d))
    merged, w_ff1 = _fused_matmul(
        [(h, w_g, 0, N_BRANCH), (fox, p["w_attn_up"].astype(BF16), 0, 1, fox_position),
         (y, p["w_ssm_up"].astype(BF16), 0)],
        _ep_gated_merge, BF16, n=d, tm=fox_rows, tn=tn_mix,
        aux=[(b_gate, 0, N_BRANCH)], side_cast=p["w_ff1"], name="gated_merge")
    x = _fused_matmul([(merged, w_out, 0)], _ep_add_res, F32, n=d,
                      tm=1024, tn=512, res=x, name="mixer_out_proj")

    xw = p["wq_x"].shape[1]
    wq = (p["wq_x"] * (xw // XATTN_HEADS) ** -0.5).astype(BF16)
    w_kv = jnp.concatenate([p["wk_x"], p["wv_x"]], axis=1).astype(BF16)
    kv = _fused_matmul([(mem_n, w_kv, 0)], _ep_plain, BF16, n=2 * xw, tm=512, tn=1024,
                       name="xattn_kv_proj")
    x, hm = _xattn_block(x, p["g_xattn"], wq, kv, p["wo_x"].astype(BF16), p["g_mlp"],
                         seq, mem_len, tm=256)

    dff = p["w_ff1"].shape[1]
    hid, w_ff2 = _fused_matmul([(hm, w_ff1, 0)], _ep_relu2, BF16, n=dff, tm=1024, tn=1024,
                               side_cast=p["w_ff2"], name="mlp_up")
    x = _matmul_kpieces(hid, w_ff2, x, n_pieces=4, tm=512, tn=256, name="mlp_down")
    return x


_LAYER_PARAMS = ("g_mix", "w_in", "b_f", "b_gate", "A_re", "A_im", "log_dt", "B_re", "B_im",
                 "C_re", "C_im", "D_skip", "w_glu", "b_glu", "w_attn_up", "w_ssm_up", "w_out",
                 "g_xattn", "g_mem", "wq_x", "wk_x", "wv_x", "wo_x", "g_mlp", "w_ff1", "w_ff2")


def kernel(x, mem, g_mix, w_in, b_f, b_gate, A_re, A_im, log_dt, B_re, B_im, C_re, C_im, D_skip, w_glu, b_glu, w_attn_up, w_ssm_up, w_out, g_xattn, g_mem, wq_x, wk_x, wv_x, wo_x, g_mlp, w_ff1, w_ff2, g_final):
    stacked = dict(zip(_LAYER_PARAMS, (g_mix, w_in, b_f, b_gate, A_re, A_im, log_dt, B_re, B_im,
                                       C_re, C_im, D_skip, w_glu, b_glu, w_attn_up, w_ssm_up,
                                       w_out, g_xattn, g_mem, wq_x, wk_x, wv_x, wo_x, g_mlp,
                                       w_ff1, w_ff2)))
    bsz, seq, d = x.shape
    mem_len = mem.shape[1]
    xt = x.reshape(bsz * seq, d)
    mem2 = mem.reshape(bsz * mem_len, d)
    for l in range(g_mix.shape[0]):
        p = {k: v[l] for k, v in stacked.items()}
        mem_n = _rmsnorm(mem2, p["g_mem"], BF16, tm=256)
        xt = _layer(xt, mem_n, p, bsz, seq, mem_len)
    out = _rmsnorm(xt, g_final, x.dtype, tm=256)
    return out.reshape(bsz, seq, d)
```

```python
import functools
import math

import jax
import jax.numpy as jnp
from jax import lax
from jax.experimental import pallas as pl
from jax.experimental.pallas import tpu as pltpu

F32 = jnp.float32
BF16 = jnp.bfloat16

V7X_VMEM_LIMIT_BYTES = 56 * 1024 * 1024
LANES = 128
SUBLANES = 8
BF16_SUBLANES = 16

EPS = 1e-6
NEG_INF = -1e30
LOG2E = math.log2(math.e)
N_BIAS_PIECES = 3
ATTN_HEAD_DIM = 128
S5_CHUNK = 16
XATTN_HEADS = 4
N_BRANCH = 2


def _cparams(*sem):
    return pltpu.CompilerParams(dimension_semantics=sem,
                                vmem_limit_bytes=V7X_VMEM_LIMIT_BYTES)


def _rms_scale(x, g):
    return (x * lax.rsqrt(jnp.mean(x * x, axis=-1, keepdims=True) + EPS)) * g


def _log_sigmoid(z):
    return jnp.minimum(z, 0.0) - jnp.log1p(jnp.exp(-jnp.abs(z)))


def _sigmoid(z):
    return 0.5 * jnp.tanh(0.5 * z) + 0.5


def _rmsnorm_body(x_ref, g_ref, *rest, has_proj):
    h = _rms_scale(x_ref[...].astype(F32), g_ref[...]).astype(rest[-1 - has_proj].dtype)
    if has_proj:
        w_ref, b_ref, o_ref, f_ref = rest
        f_ref[...] = _log_sigmoid(jnp.dot(h, w_ref[...], preferred_element_type=F32) + b_ref[...])
    else:
        o_ref, = rest
    o_ref[...] = h


def _rmsnorm(x, g, out_dtype, tm, forget=None):
    m, d = x.shape
    tm = min(tm, m)
    row = pl.BlockSpec((tm, d), lambda i: (i, 0))
    operands = [x, g.reshape(1, d).astype(F32)]
    in_specs = [row, pl.BlockSpec((1, d), lambda i: (0, 0))]
    out_specs, out_shape = [row], [jax.ShapeDtypeStruct((m, d), out_dtype)]
    if forget is not None:
        n = forget[0].shape[1]
        operands += list(forget)
        in_specs += [pl.BlockSpec((d, n), lambda i: (0, 0)), pl.BlockSpec((1, n), lambda i: (0, 0))]
        out_specs.append(pl.BlockSpec((tm, n), lambda i: (i, 0)))
        out_shape.append(jax.ShapeDtypeStruct((m, n), F32))
    outs = pl.pallas_call(
        functools.partial(_rmsnorm_body, has_proj=forget is not None),
        grid=(m // tm,),
        in_specs=in_specs,
        out_specs=out_specs,
        out_shape=out_shape,
        compiler_params=_cparams("parallel"),
        name="rmsnorm",
    )(*operands)
    return outs if forget is not None else outs[0]


def _mm_body(*refs, n_pairs, n_aux, has_res, has_side, epilogue):
    accs = [jnp.dot(refs[2 * p][...], refs[2 * p + 1][...], preferred_element_type=F32)
            for p in range(n_pairs)]
    pos = 2 * n_pairs
    aux = [refs[pos + i][...] for i in range(n_aux)]
    pos += n_aux
    res = refs[pos][...].astype(F32) if has_res else None
    pos += has_res
    if has_side:
        side_in, side_out = refs[pos], refs[-1]
        if len(side_out.shape) == 3:
            width = side_out.shape[2]
            for c in range(side_out.shape[0]):
                side_out[c] = side_in[:, c * width:(c + 1) * width].astype(side_out.dtype)
        else:
            side_out[...] = side_in[...].astype(side_out.dtype)
        pos += 1
    o_ref = refs[pos]
    o_ref[...] = epilogue(accs, aux, res).astype(o_ref.dtype)


def _fused_matmul(pairs, epilogue, out_dtype, *, n, tm, tn, aux=(), res=None, side_cast=None,
                  side_tile=None, name="fused_matmul"):
    m = pairs[0][0].shape[0]
    tm, tn = min(tm, m), min(tn, n)
    n_aux, has_res, has_side = len(aux), res is not None, side_cast is not None
    nj = n // tn
    operands, in_specs = [], []
    for lhs, rhs, off, *opt in pairs:
        kp = lhs.shape[1]
        width = tn * (opt[0] if opt else 1)
        row_block = opt[1] if len(opt) > 1 else (lambda i: i)
        operands += [lhs, rhs]
        in_specs.append(pl.BlockSpec((tm, kp), lambda i, j, rb=row_block: (rb(i), 0)))
        if rhs.ndim == 3:
            in_specs.append(pl.BlockSpec((None, kp, width), lambda i, j, off=off: (j + off, 0, 0)))
        else:
            in_specs.append(pl.BlockSpec((kp, width), lambda i, j, off=off: (0, j + off)))
    for vec, off, *mult in aux:
        width = tn * (mult[0] if mult else 1)
        operands.append(vec)
        in_specs.append(pl.BlockSpec((1, width), lambda i, j, off=off: (0, j + off)))
    if has_res:
        operands.append(res)
        in_specs.append(pl.BlockSpec((tm, tn), lambda i, j: (i, j)))
    out_specs = [pl.BlockSpec((tm, tn), lambda i, j: (i, j))]
    out_shape = [jax.ShapeDtypeStruct((m, n), out_dtype)]
    if has_side:
        slab = side_cast.shape[0] // ((m // tm) * nj)
        assert slab * (m // tm) * nj == side_cast.shape[0] and slab % BF16_SUBLANES == 0
        side_spec = pl.BlockSpec((slab, side_cast.shape[1]), lambda i, j: (i * nj + j, 0))
        operands.append(side_cast)
        in_specs.append(side_spec)
        if side_tile is None:
            out_specs.append(side_spec)
            out_shape.append(jax.ShapeDtypeStruct(side_cast.shape, BF16))
        else:
            n_tiles = side_cast.shape[1] // side_tile
            out_specs.append(pl.BlockSpec((n_tiles, slab, side_tile),
                                          lambda i, j: (0, i * nj + j, 0)))
            out_shape.append(jax.ShapeDtypeStruct((n_tiles, side_cast.shape[0], side_tile), BF16))
    outs = pl.pallas_call(
        functools.partial(_mm_body, n_pairs=len(pairs), n_aux=n_aux, has_res=has_res,
                          has_side=has_side, epilogue=epilogue),
        grid=(m // tm, nj),
        in_specs=in_specs,
        out_specs=out_specs,
        out_shape=out_shape,
        compiler_params=_cparams("parallel", "parallel"),
        name=name,
    )(*operands)
    return outs if has_side else outs[0]


W_IN_EDGE_ROWS = 16


def _transpose_cast_body(a_ref, b_ref, o_ref, *, shift, scale, n_scaled):
    rows = a_ref.shape[0]
    x = jnp.concatenate([a_ref[...], b_ref[...]], axis=0)[shift:shift + rows, :]
    if n_scaled:
        x = x * jnp.where(pl.program_id(0) < n_scaled, scale, 1.0)
    o_ref[...] = x.T.astype(o_ref.dtype)


def _transpose_cast(wt, start, n_cols, tile, out_block, tk, scale=1.0, n_scaled=0):
    k = wt.shape[1]
    base = start // tile * tile
    shift = start - base
    assert shift % SUBLANES == 0 and shift <= W_IN_EDGE_ROWS and n_cols % tile == 0
    assert start + n_cols + (W_IN_EDGE_ROWS - shift) <= wt.shape[0] or shift == 0
    edge_per_tile = tile // W_IN_EDGE_ROWS
    last_edge = wt.shape[0] // W_IN_EDGE_ROWS - 1
    return pl.pallas_call(
        functools.partial(_transpose_cast_body, shift=shift, scale=scale, n_scaled=n_scaled),
        grid=(n_cols // tile, k // tk),
        in_specs=[pl.BlockSpec((tile, tk), lambda j, c: (base // tile + j, c)),
                  pl.BlockSpec((W_IN_EDGE_ROWS, tk),
                               lambda j, c: (jnp.minimum((base // tile + j + 1) * edge_per_tile,
                                                         last_edge), c))],
        out_specs=pl.BlockSpec((tk, tile), lambda j, c: (c, out_block(j))),
        out_shape=jax.ShapeDtypeStruct((k, n_cols), BF16),
        compiler_params=_cparams("parallel", "parallel"),
        name="transpose_cast",
    )(wt, wt)


def _matmul_kpieces_body(*refs, n_pieces):
    res_ref, o_ref = refs[2 * n_pieces], refs[2 * n_pieces + 1]
    acc = res_ref[...].astype(F32)
    for p in range(n_pieces):
        acc = acc + jnp.dot(refs[p][...], refs[n_pieces + p][...], preferred_element_type=F32)
    o_ref[...] = acc.astype(o_ref.dtype)


def _matmul_kpieces(lhs, rhs, res, *, n_pieces, tm, name):
    m, k = lhs.shape
    n_tiles, _, tn = rhs.shape
    n = n_tiles * tn
    kp = k // n_pieces
    tm = min(tm, m)
    in_specs = ([pl.BlockSpec((tm, kp), lambda i, j, p=p: (i, p)) for p in range(n_pieces)]
                + [pl.BlockSpec((None, kp, tn), lambda i, j, p=p: (j, p, 0))
                   for p in range(n_pieces)]
                + [pl.BlockSpec((tm, tn), lambda i, j: (i, j))])
    return pl.pallas_call(
        functools.partial(_matmul_kpieces_body, n_pieces=n_pieces),
        grid=(m // tm, n // tn),
        in_specs=in_specs,
        out_specs=pl.BlockSpec((tm, tn), lambda i, j: (i, j)),
        out_shape=jax.ShapeDtypeStruct((m, n), res.dtype),
        compiler_params=_cparams("parallel", "parallel"),
        name=name,
    )(*([lhs] * n_pieces), *([rhs] * n_pieces), res)


def _ep_plain(accs, aux, res):
    return accs[0]


def _ep_add_res(accs, aux, res):
    return res + accs[0]


def _ep_gated_merge(accs, aux, res):
    tn = accs[1].shape[1]
    gates = _sigmoid(accs[0] + aux[0])
    return gates[:, :tn] * accs[1] + gates[:, tn:] * accs[2]


def _ep_relu2(accs, aux, res):
    r = jnp.maximum(accs[0], 0.0)
    return r * r


def _forget_bias_body(x_ref, o_ref, carry_ref, *, blk, nblk, heads):
    @pl.when(pl.program_id(1) == 0)
    def _():
        carry_ref[...] = jnp.zeros_like(carry_ref)

    r = lax.broadcasted_iota(jnp.int32, (blk, blk), 0)
    c = lax.broadcasted_iota(jnp.int32, (blk, blk), 1)
    tri = (c <= r).astype(F32)
    lane = lax.broadcasted_iota(jnp.int32, (blk, LANES), 1)

    def step(i, carry):
        base = pl.multiple_of(i * blk, blk)
        cs = jnp.dot(tri, x_ref[pl.ds(base, blk), :], preferred_element_type=F32,
                     precision=lax.Precision.HIGHEST) + carry
        for h in range(heads):
            bias = jnp.broadcast_to(cs[:, h:h + 1] * (-LOG2E), (blk, LANES))
            hi = bias.astype(BF16).astype(F32)
            mid = (bias - hi).astype(BF16).astype(F32)
            lo = bias - hi - mid
            pieces = jnp.where(lane == 0, hi, jnp.where(lane == 1, mid,
                                                        jnp.where(lane == 2, lo, 0.0)))
            o_ref[pl.ds(base, blk), h * LANES:(h + 1) * LANES] = pieces.astype(o_ref.dtype)
        return cs[blk - 1:blk, :]

    carry_ref[0:1, :] = lax.fori_loop(0, nblk, step, carry_ref[0:1, :])


def _forget_bias(log_f, bsz, seq, heads):
    blk = min(256, seq)
    tt = min(1024, seq)
    nt = seq // tt
    return pl.pallas_call(
        functools.partial(_forget_bias_body, blk=blk, nblk=tt // blk, heads=heads),
        grid=(bsz, nt),
        in_specs=[pl.BlockSpec((tt, LANES), lambda b, t: (b * nt + t, 0))],
        out_specs=pl.BlockSpec((tt, heads * LANES), lambda b, t: (b * nt + t, 0)),
        out_shape=jax.ShapeDtypeStruct((bsz * seq, heads * LANES), BF16),
        scratch_shapes=[pltpu.VMEM((SUBLANES, LANES), F32)],
        compiler_params=_cparams("parallel", "arbitrary"),
        name="forget_bias",
    )(log_f)


def _fox_body(qa_ref, qb_ref, k_ref, v_ref, cp_ref, o_ref, kaug_ref, vaug_ref, s_ref,
              *, tq, seq):
    p = pl.program_id(2)
    nq = seq // tq
    dh = qa_ref.shape[1]

    @pl.when(p == 0)
    def _():
        def fill(j, _):
            rows = pl.ds(pl.multiple_of(j * tq, tq), tq)
            kaug_ref[rows, :dh] = k_ref[rows, :]
            kaug_ref[rows, dh:] = cp_ref[rows, :]
            vaug_ref[rows, :dh] = v_ref[rows, :]
            vaug_ref[rows, dh:] = jnp.ones((tq, dh), vaug_ref.dtype)
            return 0
        lax.fori_loop(0, nq, fill, 0)

    lane = lax.broadcasted_iota(jnp.int32, (tq, dh), 1)
    unit = (lane < N_BIAS_PIECES).astype(qa_ref.dtype)
    q_a = jnp.concatenate([qa_ref[...], unit], axis=1)
    q_b = jnp.concatenate([qb_ref[...], unit], axis=1)
    above_diag = (lax.broadcasted_iota(jnp.int32, (tq, tq), 1)
                  > lax.broadcasted_iota(jnp.int32, (tq, tq), 0))

    def key_rows(t):
        j = jnp.where(t <= p, t, t - (p + 1))
        return pl.ds(pl.multiple_of(j * tq, tq), tq)

    def scores(t):
        q = jnp.where(t <= p, q_a, q_b)
        s = lax.dot_general(q, kaug_ref[key_rows(t), :], (((1,), (1,)), ((), ())),
                            preferred_element_type=F32)
        if t == nq:
            s = jnp.where(above_diag, NEG_INF, s)
        elif t < nq // 2:
            s = jnp.where(jnp.logical_and(above_diag, t == p), NEG_INF, s)
        return s

    m = jnp.full((tq, 1), NEG_INF, F32)
    acc = jnp.zeros((tq, 2 * dh), F32)
    s_ref[0] = scores(0)
    for t in range(nq + 1):
        if t < nq:
            s_ref[(t + 1) % 2] = scores(t + 1)
        s = s_ref[t % 2]
        if 1 <= t <= nq // 2:
            first_b = t == p + 1
            m = jnp.where(first_b, NEG_INF, m)
            acc = jnp.where(first_b, 0.0, acc)
        m_new = jnp.maximum(m, jnp.max(s, axis=-1, keepdims=True))
        acc = jnp.exp2(m - m_new) * acc + jnp.dot(
            jnp.exp2(s - m_new).astype(vaug_ref.dtype), vaug_ref[key_rows(t), :],
            preferred_element_type=F32)
        m = m_new
        if t == 0:
            o_ref[:tq, :] = (acc[:, :dh] / acc[:, dh:]).astype(o_ref.dtype)
        elif t < nq // 2:
            out = (acc[:, :dh] / acc[:, dh:]).astype(o_ref.dtype)
            o_ref[:tq, :] = jnp.where(t <= p, out, o_ref[:tq, :])
    o_ref[tq:, :] = (acc[:, :dh] / acc[:, dh:]).astype(o_ref.dtype)


def _fox_attention(qkv, cp, bsz, seq, heads, tq):
    dh = ATTN_HEAD_DIM
    tq = min(tq, seq // 2)
    nq = seq // tq
    half = nq // 2
    assert nq % 2 == 0

    def position(i):
        b, qi = i // nq, i % nq
        return jnp.where(qi < half, (b * half + qi) * 2, (b * half + nq - 1 - qi) * 2 + 1)

    out = pl.pallas_call(
        functools.partial(_fox_body, tq=tq, seq=seq),
        grid=(bsz, heads, half),
        in_specs=[pl.BlockSpec((tq, dh), lambda b, h, p: (b * nq + p, h)),
                  pl.BlockSpec((tq, dh), lambda b, h, p: (b * nq + nq - 1 - p, h)),
                  pl.BlockSpec((seq, dh), lambda b, h, p: (b, heads + h)),
                  pl.BlockSpec((seq, dh), lambda b, h, p: (b, 2 * heads + h)),
                  pl.BlockSpec((seq, dh), lambda b, h, p: (b, h))],
        out_specs=pl.BlockSpec((2 * tq, dh), lambda b, h, p: (b * half + p, h)),
        out_shape=jax.ShapeDtypeStruct((bsz * seq, heads * dh), BF16),
        scratch_shapes=[pltpu.VMEM((seq, 2 * dh), BF16), pltpu.VMEM((seq, 2 * dh), BF16),
                        pltpu.VMEM((2, tq, tq), F32)],
        compiler_params=_cparams("parallel", "parallel", "arbitrary"),
        name="fox_attention",
    )(qkv, qkv, qkv, qkv, cp)
    return out, tq, position


def _xattn_block_body(x_ref, gx_ref, wq_ref, kv_ref, wo_ref, gn_ref, xo_ref, hn_ref):
    x = x_ref[...]
    width = wq_ref.shape[1]
    dh = width // XATTN_HEADS
    hx = _rms_scale(x, gx_ref[...]).astype(wq_ref.dtype)
    q = jnp.dot(hx, wq_ref[...], preferred_element_type=F32).astype(kv_ref.dtype)
    heads = []
    for hd in range(XATTN_HEADS):
        k = kv_ref[:, hd * dh:(hd + 1) * dh]
        v = kv_ref[:, width + hd * dh:width + (hd + 1) * dh]
        s = lax.dot_general(q[:, hd * dh:(hd + 1) * dh], k, (((1,), (1,)), ((), ())),
                            preferred_element_type=F32)
        p = jnp.exp(s - jnp.max(s, axis=-1, keepdims=True))
        l = jnp.sum(p, axis=-1, keepdims=True)
        o = jnp.dot(p.astype(v.dtype), v, preferred_element_type=F32) / l
        heads.append(o.astype(wo_ref.dtype))
    x2 = x + jnp.dot(jnp.concatenate(heads, axis=1), wo_ref[...], preferred_element_type=F32)
    xo_ref[...] = x2
    hn_ref[...] = _rms_scale(x2, gn_ref[...]).astype(hn_ref.dtype)


def _xattn_block(x, g_x, wq, kv, wo, g_next, seq, mem_len, tm):
    t, d = x.shape
    width = wq.shape[1]
    tm = min(tm, seq)
    nq = seq // tm
    row = pl.BlockSpec((tm, d), lambda i: (i, 0))
    vec = pl.BlockSpec((1, d), lambda i: (0, 0))
    resident = pl.Buffered(1)
    return pl.pallas_call(
        _xattn_block_body,
        grid=(t // tm,),
        in_specs=[row, vec,
                  pl.BlockSpec((d, width), lambda i: (0, 0), pipeline_mode=resident),
                  pl.BlockSpec((mem_len, 2 * width), lambda i: (i // nq, 0)),
                  pl.BlockSpec((width, d), lambda i: (0, 0), pipeline_mode=resident),
                  vec],
        out_specs=[row, row],
        out_shape=[jax.ShapeDtypeStruct((t, d), F32), jax.ShapeDtypeStruct((t, d), BF16)],
        compiler_params=_cparams("parallel"),
        name="xattn_block",
    )(x, g_x.reshape(1, d).astype(F32), wq, kv, wo, g_next.reshape(1, d).astype(F32))


def _s5_operators_body(bre_ref, bim_ref, cre_ref, cim_ref, p_ref, pt_ref, d_ref,
                        t_ref, w_ref, v_ref, *, L):
    hp = lax.Precision.HIGHEST
    bre, bim = bre_ref[...], bim_ref[...]
    cre, cim = cre_ref[...], cim_ref[...]
    k = bre.shape[1]
    row = lax.broadcasted_iota(jnp.int32, (LANES, LANES), 0)
    col = lax.broadcasted_iota(jnp.int32, (LANES, LANES), 1)
    lag_blocks = []
    for tau in range(L):
        pr, pi = p_ref[0, tau:tau + 1, :], p_ref[1, tau:tau + 1, :]
        xr, xi = bre * pr - bim * pi, bre * pi + bim * pr
        rows = slice((L - 1 - tau) * LANES, (L - tau) * LANES)
        w_ref[rows, :k] = xr.astype(w_ref.dtype)
        w_ref[rows, k:] = xi.astype(w_ref.dtype)
        d_tau = (jnp.dot(xr, cre, preferred_element_type=F32, precision=hp)
                 - jnp.dot(xi, cim, preferred_element_type=F32, precision=hp))
        if tau == 0:
            d_tau = d_tau + jnp.where(row == col, d_ref[...], 0.0)
        lag_blocks.append(d_tau.astype(t_ref.dtype))
    zero = jnp.zeros((LANES, LANES), t_ref.dtype)
    for s in range(L):
        for t in range(L):
            t_ref[s * LANES:(s + 1) * LANES, t * LANES:(t + 1) * LANES] = (
                lag_blocks[t - s] if t >= s else zero)
    for t in range(L):
        qr, qi = pt_ref[0, :, t + 1:t + 2], pt_ref[1, :, t + 1:t + 2]
        cols = slice(t * LANES, (t + 1) * LANES)
        v_ref[:k, cols] = (cre * qr - cim * qi).astype(v_ref.dtype)
        v_ref[k:, cols] = (-(cre * qi + cim * qr)).astype(v_ref.dtype)


def _s5_operators(A_re, A_im, log_dt, B_re, B_im, C_re, C_im, D_skip):
    n_g, n_p = A_re.shape
    n_h = B_re.shape[-1]
    L = S5_CHUNK
    gs = LANES // n_h
    n_slab = n_g // gs
    k = gs * n_p
    a_re, a_im = A_re.astype(F32), A_im.astype(F32)
    dt = jnp.exp(log_dt.astype(F32))[:, None]
    tau = jnp.arange(L + 1, dtype=F32)[:, None, None]
    mag = jnp.exp(tau * (dt * a_re))
    ang = tau * (dt * a_im)
    pw = jnp.stack([mag * jnp.cos(ang), mag * jnp.sin(ang)])
    lb_re, lb_im = pw[0, 1], pw[1, 1]
    den = a_re * a_re + a_im * a_im
    nr, ni = lb_re - 1.0, lb_im
    f_re = (nr * a_re + ni * a_im) / den
    f_im = (ni * a_re - nr * a_im) / den
    br, bi = B_re.astype(F32), B_im.astype(F32)
    bb_re = f_re[..., None] * br - f_im[..., None] * bi
    bb_im = f_re[..., None] * bi + f_im[..., None] * br

    eye = jnp.eye(gs, dtype=F32)

    def slab_b(x):
        x = x.reshape(n_slab, gs, n_p, n_h).transpose(0, 1, 3, 2)
        return (x[:, :, :, None, :] * eye[None, :, None, :, None]).reshape(n_slab, LANES, k)

    def slab_c(x):
        x = x.reshape(n_slab, gs, n_h, n_p).transpose(0, 1, 3, 2)
        return (x[:, :, :, None, :] * eye[None, :, None, :, None]).reshape(n_slab, k, LANES)

    p_tab = pw.reshape(2, L + 1, n_slab, k).transpose(2, 0, 1, 3)
    pt_tab = jnp.pad(pw.reshape(2, L + 1, n_slab, k).transpose(2, 0, 3, 1),
                     ((0, 0), (0, 0), (0, 0), (0, LANES - (L + 1))))
    mat_b = pl.BlockSpec((None, LANES, k), lambda j: (j, 0, 0))
    mat_c = pl.BlockSpec((None, k, LANES), lambda j: (j, 0, 0))
    t_op, w_op, v_op = pl.pallas_call(
        functools.partial(_s5_operators_body, L=L),
        grid=(n_slab,),
        in_specs=[mat_b, mat_b, mat_c, mat_c,
                  pl.BlockSpec((None, 2, L + 1, k), lambda j: (j, 0, 0, 0)),
                  pl.BlockSpec((None, 2, k, LANES), lambda j: (j, 0, 0, 0)),
                  pl.BlockSpec((1, LANES), lambda j: (0, j))],
        out_specs=[pl.BlockSpec((None, L * LANES, L * LANES), lambda j: (j, 0, 0)),
                   pl.BlockSpec((None, L * LANES, 2 * k), lambda j: (j, 0, 0)),
                   pl.BlockSpec((None, 2 * k, L * LANES), lambda j: (j, 0, 0))],
        out_shape=[jax.ShapeDtypeStruct((n_slab, L * LANES, L * LANES), BF16),
                   jax.ShapeDtypeStruct((n_slab, L * LANES, 2 * k), BF16),
                   jax.ShapeDtypeStruct((n_slab, 2 * k, L * LANES), BF16)],
        compiler_params=_cparams("parallel"),
        name="s5_operators",
    )(slab_b(bb_re), slab_b(bb_im), slab_c(C_re.astype(F32)), slab_c(C_im.astype(F32)),
      p_tab, pt_tab, D_skip.astype(F32).reshape(1, n_g * n_h))

    a_op = pw[:, L].reshape(2, n_slab, k).transpose(1, 0, 2).reshape(1, n_slab * 2 * k)
    return t_op, w_op, v_op, a_op


def _chunk_rows(piece_refs):
    return jnp.concatenate([r[...] for r in piece_refs], axis=1)


def _s5_increment_body(*refs):
    w_ref, z_ref = refs[-2], refs[-1]
    z_ref[...] = jnp.dot(_chunk_rows(refs[:-2]), w_ref[...], preferred_element_type=F32)


def _s5_scan_body(z_ref, a_ref, o_ref, st_ref, *, tc):
    @pl.when(pl.program_id(2) == 0)
    def _():
        st_ref[...] = jnp.zeros_like(st_ref)

    half = z_ref.shape[1] // 2
    a_re, a_im = a_ref[:, :half], a_ref[:, half:]

    def step(i, carry):
        re, im = carry
        base = pl.multiple_of(i * SUBLANES, SUBLANES)
        inc = z_ref[pl.ds(base, SUBLANES), :]
        before_re, before_im = [], []
        for r in range(SUBLANES):
            before_re.append(re)
            before_im.append(im)
            re, im = (a_re * re - a_im * im + inc[r:r + 1, :half],
                      a_re * im + a_im * re + inc[r:r + 1, half:])
        o_ref[pl.ds(base, SUBLANES), :half] = jnp.concatenate(before_re, axis=0)
        o_ref[pl.ds(base, SUBLANES), half:] = jnp.concatenate(before_im, axis=0)
        return re, im

    re, im = lax.fori_loop(0, tc // SUBLANES, step, (st_ref[0:1, :half], st_ref[0:1, half:]))
    st_ref[0:1, :half] = re
    st_ref[0:1, half:] = im


def _s5_output_body(*refs):
    x_ref, t_ref, v_ref, o_ref = refs[-4:]
    y = jnp.dot(_chunk_rows(refs[:-4]), t_ref[...], preferred_element_type=F32)
    y = y + jnp.dot(x_ref[...].astype(BF16), v_ref[...], preferred_element_type=F32)
    o_ref[...] = jax.nn.gelu(y).astype(o_ref.dtype)


def _s5_glu_body(*refs):
    w_ref, b_ref, o_ref = refs[-3:]
    y = _chunk_rows(refs[:-3])
    gate = _sigmoid(jnp.dot(y, w_ref[...], preferred_element_type=F32) + b_ref[...])
    o_ref[...] = (y.astype(F32) * gate).astype(o_ref.dtype)


def _s5_branch(u, ops, w_glu, b_glu, bsz, seq):
    t_op, w_op, v_op, a_op = ops
    n_slab = t_op.shape[0]
    L = S5_CHUNK
    width = u.shape[1]
    cw, sw = t_op.shape[1], w_op.shape[2]
    nc = bsz * seq // L
    ncb = seq // L
    uu = u.reshape(nc, L * width)
    tm = min(512, nc)
    pieces = [pl.BlockSpec((tm, LANES), lambda j, i, s=s: (i, s * n_slab + j)) for s in range(L)]

    z = pl.pallas_call(
        _s5_increment_body,
        grid=(n_slab, nc // tm),
        in_specs=pieces + [pl.BlockSpec((None, cw, sw), lambda j, i: (j, 0, 0))],
        out_specs=pl.BlockSpec((tm, sw), lambda j, i: (i, j)),
        out_shape=jax.ShapeDtypeStruct((nc, n_slab * sw), F32),
        compiler_params=_cparams("parallel", "parallel"),
        name="s5_increment",
    )(*([uu] * L), w_op)

    tc = min(128, ncb)
    nt = ncb // tc
    blk = pl.BlockSpec((tc, sw), lambda b, j, t: (b * nt + t, j))
    xprev = pl.pallas_call(
        functools.partial(_s5_scan_body, tc=tc),
        grid=(bsz, n_slab, nt),
        in_specs=[blk, pl.BlockSpec((1, sw), lambda b, j, t: (0, j))],
        out_specs=blk,
        out_shape=jax.ShapeDtypeStruct((nc, n_slab * sw), F32),
        scratch_shapes=[pltpu.VMEM((SUBLANES, sw), F32)],
        compiler_params=_cparams("parallel", "parallel", "arbitrary"),
        name="s5_scan",
    )(z, a_op)

    ys = pl.pallas_call(
        _s5_output_body,
        grid=(n_slab, nc // tm),
        in_specs=pieces + [pl.BlockSpec((tm, sw), lambda j, i: (i, j)),
                           pl.BlockSpec((None, cw, cw), lambda j, i: (j, 0, 0)),
                           pl.BlockSpec((None, sw, cw), lambda j, i: (j, 0, 0))],
        out_specs=pl.BlockSpec((tm, cw), lambda j, i: (i, j)),
        out_shape=jax.ShapeDtypeStruct((nc, n_slab * cw), BF16),
        compiler_params=_cparams("parallel", "parallel"),
        name="s5_output",
    )(*([uu] * L), xprev, t_op, v_op)

    tok = [pl.BlockSpec((tm, LANES), lambda i, t, j=j: (i, j * L + t)) for j in range(n_slab)]
    out = pl.pallas_call(
        _s5_glu_body,
        grid=(nc // tm, L),
        in_specs=tok + [pl.BlockSpec((width, width), lambda i, t: (0, 0)),
                        pl.BlockSpec((1, width), lambda i, t: (0, 0))],
        out_specs=pl.BlockSpec((tm, width), lambda i, t: (i, t)),
        out_shape=jax.ShapeDtypeStruct((nc, L * width), BF16),
        compiler_params=_cparams("parallel", "parallel"),
        name="s5_glu",
    )(*([ys] * n_slab), w_glu.astype(BF16), b_glu.reshape(1, width).astype(F32))
    return out.reshape(bsz * seq, width)


def _layer(x, mem_n, p, bsz, seq, mem_len):
    d = x.shape[1]
    aw = p["w_attn_up"].shape[0]
    heads = aw // ATTN_HEAD_DIM
    sw = p["w_ssm_up"].shape[0]
    off_f = 3 * aw
    off_u = off_f + heads
    off_g = off_u + sw

    w_in_t = p["w_in"].T
    tile = 256
    w_qkv = _transpose_cast(w_in_t, 0, off_f, tile, lambda j: j, tk=d,
                            scale=LOG2E * ATTN_HEAD_DIM ** -0.5, n_scaled=aw // tile)
    w_f = jnp.pad(w_in_t[off_f:off_u].T, ((0, 0), (0, LANES - heads))).astype(BF16)
    b_f = jnp.pad(p["b_f"].astype(F32), (0, LANES - heads)).reshape(1, LANES)
    h, log_f = _rmsnorm(x, p["g_mix"], BF16, tm=256, forget=(w_f, b_f))
    w_u = _transpose_cast(w_in_t, off_u, sw, tile, lambda j: j, tk=d)
    tn_mix = tile
    n_mix = d // tn_mix
    w_g = _transpose_cast(w_in_t, off_g, N_BRANCH * d, tn_mix,
                          lambda j: (j % n_mix) * N_BRANCH + j // n_mix, tk=d)

    qkv = _fused_matmul([(h, w_qkv, 0)], _ep_plain, BF16, n=3 * aw, tm=1024, tn=1024,
                        name="qkv_proj")
    fox, fox_rows, fox_position = _fox_attention(qkv, _forget_bias(log_f, bsz, seq, heads),
                                                 bsz, seq, heads, tq=1024)

    tn_out = 512
    u, w_out = _fused_matmul([(h, w_u, 0)], _ep_plain, BF16, n=sw, tm=1024, tn=1024,
                             side_cast=p["w_out"], side_tile=tn_out, name="ssm_in_proj")
    ops = _s5_operators(p["A_re"], p["A_im"], p["log_dt"], p["B_re"], p["B_im"],
                        p["C_re"], p["C_im"], p["D_skip"])
    y = _s5_branch(u, ops, p["w_glu"], p["b_glu"], bsz, seq)

    b_gate = (p["b_gate"].astype(F32).reshape(N_BRANCH, n_mix, tn_mix).transpose(1, 0, 2)
              .reshape(1, N_BRANCH * d))
    merged, w_ff1 = _fused_matmul(
        [(h, w_g, 0, N_BRANCH), (fox, p["w_attn_up"].astype(BF16), 0, 1, fox_position),
         (y, p["w_ssm_up"].astype(BF16), 0)],
        _ep_gated_merge, BF16, n=d, tm=fox_rows, tn=tn_mix,
        aux=[(b_gate, 0, N_BRANCH)], side_cast=p["w_ff1"], name="gated_merge")
    x = _fused_matmul([(merged, w_out, 0)], _ep_add_res, F32, n=d,
                      tm=1024, tn=tn_out, res=x, name="mixer_out_proj")

    xw = p["wq_x"].shape[1]
    wq = (p["wq_x"] * (xw // XATTN_HEADS) ** -0.5).astype(BF16)
    w_kv = jnp.concatenate([p["wk_x"], p["wv_x"]], axis=1).astype(BF16)
    kv = _fused_matmul([(mem_n, w_kv, 0)], _ep_plain, BF16, n=2 * xw, tm=512, tn=1024,
                       name="xattn_kv_proj")
    x, hm = _xattn_block(x, p["g_xattn"], wq, kv, p["wo_x"].astype(BF16), p["g_mlp"],
                         seq, mem_len, tm=256)

    dff = p["w_ff1"].shape[1]
    hid, w_ff2 = _fused_matmul([(hm, w_ff1, 0)], _ep_relu2, BF16, n=dff, tm=1024, tn=1024,
                               side_cast=p["w_ff2"], side_tile=256, name="mlp_up")
    x = _matmul_kpieces(hid, w_ff2, x, n_pieces=4, tm=512, name="mlp_down")
    return x


_LAYER_PARAMS = ("g_mix", "w_in", "b_f", "b_gate", "A_re", "A_im", "log_dt", "B_re", "B_im",
                 "C_re", "C_im", "D_skip", "w_glu", "b_glu", "w_attn_up", "w_ssm_up", "w_out",
                 "g_xattn", "g_mem", "wq_x", "wk_x", "wv_x", "wo_x", "g_mlp", "w_ff1", "w_ff2")


def kernel(x, mem, g_mix, w_in, b_f, b_gate, A_re, A_im, log_dt, B_re, B_im, C_re, C_im, D_skip, w_glu, b_glu, w_attn_up, w_ssm_up, w_out, g_xattn, g_mem, wq_x, wk_x, wv_x, wo_x, g_mlp, w_ff1, w_ff2, g_final):
    stacked = dict(zip(_LAYER_PARAMS, (g_mix, w_in, b_f, b_gate, A_re, A_im, log_dt, B_re, B_im,
                                       C_re, C_im, D_skip, w_glu, b_glu, w_attn_up, w_ssm_up,
                                       w_out, g_xattn, g_mem, wq_x, wk_x, wv_x, wo_x, g_mlp,
                                       w_ff1, w_ff2)))
    bsz, seq, d = x.shape
    mem_len = mem.shape[1]
    xt = x.reshape(bsz * seq, d)
    mem2 = mem.reshape(bsz * mem_len, d)
    for l in range(g_mix.shape[0]):
        p = {k: v[l] for k, v in stacked.items()}
        mem_n = _rmsnorm(mem2, p["g_mem"], BF16, tm=256)
        xt = _layer(xt, mem_n, p, bsz, seq, mem_len)
    out = _rmsnorm(xt, g_final, x.dtype, tm=256)
    return out.reshape(bsz, seq, d)
```

```python
import functools
import math

import jax
import jax.numpy as jnp
from jax import lax
from jax.experimental import pallas as pl
from jax.experimental.pallas import tpu as pltpu

F32 = jnp.float32
BF16 = jnp.bfloat16

V7X_VMEM_LIMIT_BYTES = 56 * 1024 * 1024
LANES = 128
SUBLANES = 8
BF16_SUBLANES = 16

EPS = 1e-6
NEG_INF = -1e30
LOG2E = math.log2(math.e)
N_BIAS_PIECES = 3
ATTN_HEAD_DIM = 128
S5_CHUNK = 16
XATTN_HEADS = 4
N_BRANCH = 2


def _cparams(*sem):
    return pltpu.CompilerParams(dimension_semantics=sem,
                                vmem_limit_bytes=V7X_VMEM_LIMIT_BYTES)


def _rms_scale(x, g):
    return (x * lax.rsqrt(jnp.mean(x * x, axis=-1, keepdims=True) + EPS)) * g


def _log_sigmoid(z):
    return jnp.minimum(z, 0.0) - jnp.log1p(jnp.exp(-jnp.abs(z)))


def _sigmoid(z):
    return 0.5 * jnp.tanh(0.5 * z) + 0.5


def _rmsnorm_body(x_ref, g_ref, *rest, has_proj):
    h = _rms_scale(x_ref[...].astype(F32), g_ref[...]).astype(rest[-1 - has_proj].dtype)
    if has_proj:
        w_ref, b_ref, o_ref, f_ref = rest
        f_ref[...] = _log_sigmoid(jnp.dot(h, w_ref[...], preferred_element_type=F32) + b_ref[...])
    else:
        o_ref, = rest
    o_ref[...] = h


def _rmsnorm(x, g, out_dtype, tm, forget=None):
    m, d = x.shape
    tm = min(tm, m)
    row = pl.BlockSpec((tm, d), lambda i: (i, 0))
    operands = [x, g.reshape(1, d).astype(F32)]
    in_specs = [row, pl.BlockSpec((1, d), lambda i: (0, 0))]
    out_specs, out_shape = [row], [jax.ShapeDtypeStruct((m, d), out_dtype)]
    if forget is not None:
        n = forget[0].shape[1]
        operands += list(forget)
        in_specs += [pl.BlockSpec((d, n), lambda i: (0, 0)), pl.BlockSpec((1, n), lambda i: (0, 0))]
        out_specs.append(pl.BlockSpec((tm, n), lambda i: (i, 0)))
        out_shape.append(jax.ShapeDtypeStruct((m, n), F32))
    outs = pl.pallas_call(
        functools.partial(_rmsnorm_body, has_proj=forget is not None),
        grid=(m // tm,),
        in_specs=in_specs,
        out_specs=out_specs,
        out_shape=out_shape,
        compiler_params=_cparams("parallel"),
        name="rmsnorm",
    )(*operands)
    return outs if forget is not None else outs[0]


def _mm_body(*refs, n_pairs, n_aux, has_res, has_side, epilogue):
    accs = [jnp.dot(refs[2 * p][...], refs[2 * p + 1][...], preferred_element_type=F32)
            for p in range(n_pairs)]
    pos = 2 * n_pairs
    aux = [refs[pos + i][...] for i in range(n_aux)]
    pos += n_aux
    res = refs[pos][...].astype(F32) if has_res else None
    pos += has_res
    if has_side:
        refs[-1][...] = refs[pos][...].astype(refs[-1].dtype)
        pos += 1
    o_ref = refs[pos]
    o_ref[...] = epilogue(accs, aux, res).astype(o_ref.dtype)


def _fused_matmul(pairs, epilogue, out_dtype, *, n, tm, tn, aux=(), res=None, side_cast=None,
                  name="fused_matmul"):
    m = pairs[0][0].shape[0]
    tm, tn = min(tm, m), min(tn, n)
    n_aux, has_res, has_side = len(aux), res is not None, side_cast is not None
    nj = n // tn
    operands, in_specs = [], []
    for lhs, rhs, off, *opt in pairs:
        kp = lhs.shape[1]
        width = tn * (opt[0] if opt else 1)
        row_block = opt[1] if len(opt) > 1 else (lambda i: i)
        operands += [lhs, rhs]
        in_specs += [pl.BlockSpec((tm, kp), lambda i, j, rb=row_block: (rb(i), 0)),
                     pl.BlockSpec((kp, width), lambda i, j, off=off: (0, j + off))]
    for vec, off, *mult in aux:
        width = tn * (mult[0] if mult else 1)
        operands.append(vec)
        in_specs.append(pl.BlockSpec((1, width), lambda i, j, off=off: (0, j + off)))
    if has_res:
        operands.append(res)
        in_specs.append(pl.BlockSpec((tm, tn), lambda i, j: (i, j)))
    out_specs = [pl.BlockSpec((tm, tn), lambda i, j: (i, j))]
    out_shape = [jax.ShapeDtypeStruct((m, n), out_dtype)]
    if has_side:
        slab = side_cast.shape[0] // ((m // tm) * nj)
        assert slab * (m // tm) * nj == side_cast.shape[0] and slab % BF16_SUBLANES == 0
        side_spec = pl.BlockSpec((slab, side_cast.shape[1]), lambda i, j: (i * nj + j, 0))
        operands.append(side_cast)
        in_specs.append(side_spec)
        out_specs.append(side_spec)
        out_shape.append(jax.ShapeDtypeStruct(side_cast.shape, BF16))
    outs = pl.pallas_call(
        functools.partial(_mm_body, n_pairs=len(pairs), n_aux=n_aux, has_res=has_res,
                          has_side=has_side, epilogue=epilogue),
        grid=(m // tm, nj),
        in_specs=in_specs,
        out_specs=out_specs,
        out_shape=out_shape,
        compiler_params=_cparams("parallel", "parallel"),
        name=name,
    )(*operands)
    return outs if has_side else outs[0]


W_IN_EDGE_ROWS = 16


def _transpose_cast_body(a_ref, b_ref, o_ref, *, shift, scale, n_scaled):
    rows = a_ref.shape[0]
    x = jnp.concatenate([a_ref[...], b_ref[...]], axis=0)[shift:shift + rows, :]
    if n_scaled:
        x = x * jnp.where(pl.program_id(0) < n_scaled, scale, 1.0)
    o_ref[...] = x.T.astype(o_ref.dtype)


def _transpose_cast(wt, start, n_cols, tile, out_block, tk, scale=1.0, n_scaled=0):
    k = wt.shape[1]
    base = start // tile * tile
    shift = start - base
    assert shift % SUBLANES == 0 and shift <= W_IN_EDGE_ROWS and n_cols % tile == 0
    assert start + n_cols + (W_IN_EDGE_ROWS - shift) <= wt.shape[0] or shift == 0
    edge_per_tile = tile // W_IN_EDGE_ROWS
    last_edge = wt.shape[0] // W_IN_EDGE_ROWS - 1
    return pl.pallas_call(
        functools.partial(_transpose_cast_body, shift=shift, scale=scale, n_scaled=n_scaled),
        grid=(n_cols // tile, k // tk),
        in_specs=[pl.BlockSpec((tile, tk), lambda j, c: (base // tile + j, c)),
                  pl.BlockSpec((W_IN_EDGE_ROWS, tk),
                               lambda j, c: (jnp.minimum((base // tile + j + 1) * edge_per_tile,
                                                         last_edge), c))],
        out_specs=pl.BlockSpec((tk, tile), lambda j, c: (c, out_block(j))),
        out_shape=jax.ShapeDtypeStruct((k, n_cols), BF16),
        compiler_params=_cparams("parallel", "parallel"),
        name="transpose_cast",
    )(wt, wt)


def _matmul_kpieces_body(*refs, n_pieces):
    res_ref, o_ref = refs[2 * n_pieces], refs[2 * n_pieces + 1]
    acc = res_ref[...].astype(F32)
    for p in range(n_pieces):
        acc = acc + jnp.dot(refs[p][...], refs[n_pieces + p][...], preferred_element_type=F32)
    o_ref[...] = acc.astype(o_ref.dtype)


def _matmul_kpieces(lhs, rhs, res, *, n_pieces, tm, tn, name):
    m, k = lhs.shape
    n = rhs.shape[1]
    kp = k // n_pieces
    tm, tn = min(tm, m), min(tn, n)
    in_specs = ([pl.BlockSpec((tm, kp), lambda i, j, p=p: (i, p)) for p in range(n_pieces)]
                + [pl.BlockSpec((kp, tn), lambda i, j, p=p: (p, j)) for p in range(n_pieces)]
                + [pl.BlockSpec((tm, tn), lambda i, j: (i, j))])
    return pl.pallas_call(
        functools.partial(_matmul_kpieces_body, n_pieces=n_pieces),
        grid=(m // tm, n // tn),
        in_specs=in_specs,
        out_specs=pl.BlockSpec((tm, tn), lambda i, j: (i, j)),
        out_shape=jax.ShapeDtypeStruct((m, n), res.dtype),
        compiler_params=_cparams("parallel", "parallel"),
        name=name,
    )(*([lhs] * n_pieces), *([rhs] * n_pieces), res)


def _ep_plain(accs, aux, res):
    return accs[0]


def _ep_add_res(accs, aux, res):
    return res + accs[0]


def _ep_gated_merge(accs, aux, res):
    tn = accs[1].shape[1]
    gates = _sigmoid(accs[0] + aux[0])
    return gates[:, :tn] * accs[1] + gates[:, tn:] * accs[2]


def _ep_relu2(accs, aux, res):
    r = jnp.maximum(accs[0], 0.0)
    return r * r


def _forget_bias_body(x_ref, place_ref, o_ref, carry_ref, *, blk, nblk):
    @pl.when(pl.program_id(1) == 0)
    def _():
        carry_ref[...] = jnp.zeros_like(carry_ref)

    r = lax.broadcasted_iota(jnp.int32, (blk, blk), 0)
    c = lax.broadcasted_iota(jnp.int32, (blk, blk), 1)
    tri = (c <= r).astype(F32)

    def step(i, carry):
        base = pl.multiple_of(i * blk, blk)
        cs = jnp.dot(tri, x_ref[pl.ds(base, blk), :], preferred_element_type=F32,
                     precision=lax.Precision.HIGHEST) + carry
        bias = cs * (-LOG2E)
        hi = bias.astype(BF16)
        mid = (bias - hi.astype(F32)).astype(BF16)
        lo = (bias - hi.astype(F32) - mid.astype(F32)).astype(BF16)
        o_ref[pl.ds(base, blk), :] = jnp.dot(
            jnp.concatenate([hi, mid, lo], axis=1), place_ref[...],
            preferred_element_type=F32).astype(o_ref.dtype)
        return cs[blk - 1:blk, :]

    carry_ref[0:1, :] = lax.fori_loop(0, nblk, step, carry_ref[0:1, :])


def _forget_bias(log_f, bsz, seq, heads):
    blk = min(256, seq)
    tt = min(1024, seq)
    nt = seq // tt
    row = jnp.arange(N_BIAS_PIECES * LANES)[:, None]
    col = jnp.arange(heads * LANES)[None, :]
    place = ((row % LANES == col // LANES) & (row // LANES == col % LANES)).astype(BF16)
    return pl.pallas_call(
        functools.partial(_forget_bias_body, blk=blk, nblk=tt // blk),
        grid=(bsz, nt),
        in_specs=[pl.BlockSpec((tt, LANES), lambda b, t: (b * nt + t, 0)),
                  pl.BlockSpec(place.shape, lambda b, t: (0, 0))],
        out_specs=pl.BlockSpec((tt, heads * LANES), lambda b, t: (b * nt + t, 0)),
        out_shape=jax.ShapeDtypeStruct((bsz * seq, heads * LANES), BF16),
        scratch_shapes=[pltpu.VMEM((SUBLANES, LANES), F32)],
        compiler_params=_cparams("parallel", "arbitrary"),
        name="forget_bias",
    )(log_f, place)


def _fox_body(qa_ref, qb_ref, k_ref, v_ref, cp_ref, o_ref, kaug_ref, vaug_ref, s_ref,
              *, tq, seq):
    p = pl.program_id(2)
    nq = seq // tq
    dh = qa_ref.shape[1]

    @pl.when(p == 0)
    def _():
        def fill(j, _):
            rows = pl.ds(pl.multiple_of(j * tq, tq), tq)
            kaug_ref[rows, :dh] = k_ref[rows, :]
            kaug_ref[rows, dh:] = cp_ref[rows, :]
            vaug_ref[rows, :dh] = v_ref[rows, :]
            vaug_ref[rows, dh:] = jnp.ones((tq, dh), vaug_ref.dtype)
            return 0
        lax.fori_loop(0, nq, fill, 0)

    lane = lax.broadcasted_iota(jnp.int32, (tq, dh), 1)
    unit = (lane < N_BIAS_PIECES).astype(qa_ref.dtype)
    q_a = jnp.concatenate([qa_ref[...], unit], axis=1)
    q_b = jnp.concatenate([qb_ref[...], unit], axis=1)
    above_diag = (lax.broadcasted_iota(jnp.int32, (tq, tq), 1)
                  > lax.broadcasted_iota(jnp.int32, (tq, tq), 0))

    def key_rows(t):
        j = jnp.where(t <= p, t, t - (p + 1))
        return pl.ds(pl.multiple_of(j * tq, tq), tq)

    def scores(t):
        q = jnp.where(t <= p, q_a, q_b)
        s = lax.dot_general(q, kaug_ref[key_rows(t), :], (((1,), (1,)), ((), ())),
                            preferred_element_type=F32)
        if t == nq:
            s = jnp.where(above_diag, NEG_INF, s)
        elif t < nq // 2:
            s = jnp.where(jnp.logical_and(above_diag, t == p), NEG_INF, s)
        return s

    m = jnp.full((tq, 1), NEG_INF, F32)
    acc = jnp.zeros((tq, 2 * dh), F32)
    s_ref[0] = scores(0)
    for t in range(nq + 1):
        if t < nq:
            s_ref[(t + 1) % 2] = scores(t + 1)
        s = s_ref[t % 2]
        if 1 <= t <= nq // 2:
            first_b = t == p + 1
            m = jnp.where(first_b, NEG_INF, m)
            acc = jnp.where(first_b, 0.0, acc)
        m_new = jnp.maximum(m, jnp.max(s, axis=-1, keepdims=True))
        acc = jnp.exp2(m - m_new) * acc + jnp.dot(
            jnp.exp2(s - m_new).astype(vaug_ref.dtype), vaug_ref[key_rows(t), :],
            preferred_element_type=F32)
        m = m_new
        if t == 0:
            o_ref[:tq, :] = (acc[:, :dh] / acc[:, dh:]).astype(o_ref.dtype)
        elif t < nq // 2:
            out = (acc[:, :dh] / acc[:, dh:]).astype(o_ref.dtype)
            o_ref[:tq, :] = jnp.where(t <= p, out, o_ref[:tq, :])
    o_ref[tq:, :] = (acc[:, :dh] / acc[:, dh:]).astype(o_ref.dtype)


def _fox_attention(qkv, cp, bsz, seq, heads, tq):
    dh = ATTN_HEAD_DIM
    tq = min(tq, seq // 2)
    nq = seq // tq
    half = nq // 2
    assert nq % 2 == 0

    def position(i):
        b, qi = i // nq, i % nq
        return jnp.where(qi < half, (b * half + qi) * 2, (b * half + nq - 1 - qi) * 2 + 1)

    out = pl.pallas_call(
        functools.partial(_fox_body, tq=tq, seq=seq),
        grid=(bsz, heads, half),
        in_specs=[pl.BlockSpec((tq, dh), lambda b, h, p: (b * nq + p, h)),
                  pl.BlockSpec((tq, dh), lambda b, h, p: (b * nq + nq - 1 - p, h)),
                  pl.BlockSpec((seq, dh), lambda b, h, p: (b, heads + h)),
                  pl.BlockSpec((seq, dh), lambda b, h, p: (b, 2 * heads + h)),
                  pl.BlockSpec((seq, dh), lambda b, h, p: (b, h))],
        out_specs=pl.BlockSpec((2 * tq, dh), lambda b, h, p: (b * half + p, h)),
        out_shape=jax.ShapeDtypeStruct((bsz * seq, heads * dh), BF16),
        scratch_shapes=[pltpu.VMEM((seq, 2 * dh), BF16), pltpu.VMEM((seq, 2 * dh), BF16),
                        pltpu.VMEM((2, tq, tq), F32)],
        compiler_params=_cparams("parallel", "parallel", "arbitrary"),
        name="fox_attention",
    )(qkv, qkv, qkv, qkv, cp)
    return out, tq, position


def _xattn_block_body(x_ref, gx_ref, wq_ref, kv_ref, wo_ref, gn_ref, xo_ref, hn_ref):
    x = x_ref[...]
    width = wq_ref.shape[1]
    dh = width // XATTN_HEADS
    hx = _rms_scale(x, gx_ref[...]).astype(wq_ref.dtype)
    q = jnp.dot(hx, wq_ref[...], preferred_element_type=F32).astype(kv_ref.dtype)
    heads = []
    for hd in range(XATTN_HEADS):
        k = kv_ref[:, hd * dh:(hd + 1) * dh]
        v = kv_ref[:, width + hd * dh:width + (hd + 1) * dh]
        s = lax.dot_general(q[:, hd * dh:(hd + 1) * dh], k, (((1,), (1,)), ((), ())),
                            preferred_element_type=F32)
        p = jnp.exp(s - jnp.max(s, axis=-1, keepdims=True))
        l = jnp.sum(p, axis=-1, keepdims=True)
        o = jnp.dot(p.astype(v.dtype), v, preferred_element_type=F32) / l
        heads.append(o.astype(wo_ref.dtype))
    x2 = x + jnp.dot(jnp.concatenate(heads, axis=1), wo_ref[...], preferred_element_type=F32)
    xo_ref[...] = x2
    hn_ref[...] = _rms_scale(x2, gn_ref[...]).astype(hn_ref.dtype)


def _xattn_block(x, g_x, wq, kv, wo, g_next, seq, mem_len, tm):
    t, d = x.shape
    width = wq.shape[1]
    tm = min(tm, seq)
    nq = seq // tm
    row = pl.BlockSpec((tm, d), lambda i: (i, 0))
    vec = pl.BlockSpec((1, d), lambda i: (0, 0))
    resident = pl.Buffered(1)
    return pl.pallas_call(
        _xattn_block_body,
        grid=(t // tm,),
        in_specs=[row, vec,
                  pl.BlockSpec((d, width), lambda i: (0, 0), pipeline_mode=resident),
                  pl.BlockSpec((mem_len, 2 * width), lambda i: (i // nq, 0)),
                  pl.BlockSpec((width, d), lambda i: (0, 0), pipeline_mode=resident),
                  vec],
        out_specs=[row, row],
        out_shape=[jax.ShapeDtypeStruct((t, d), F32), jax.ShapeDtypeStruct((t, d), BF16)],
        compiler_params=_cparams("parallel"),
        name="xattn_block",
    )(x, g_x.reshape(1, d).astype(F32), wq, kv, wo, g_next.reshape(1, d).astype(F32))


def _s5_operators_body(bre_ref, bim_ref, cre_ref, cim_ref, p_ref, pt_ref, d_ref,
                        t_ref, w_ref, v_ref, *, L):
    hp = lax.Precision.HIGHEST
    bre, bim = bre_ref[...], bim_ref[...]
    cre, cim = cre_ref[...], cim_ref[...]
    k = bre.shape[1]
    row = lax.broadcasted_iota(jnp.int32, (LANES, LANES), 0)
    col = lax.broadcasted_iota(jnp.int32, (LANES, LANES), 1)
    lag_blocks = []
    for tau in range(L):
        pr, pi = p_ref[0, tau:tau + 1, :], p_ref[1, tau:tau + 1, :]
        xr, xi = bre * pr - bim * pi, bre * pi + bim * pr
        rows = slice((L - 1 - tau) * LANES, (L - tau) * LANES)
        w_ref[rows, :k] = xr.astype(w_ref.dtype)
        w_ref[rows, k:] = xi.astype(w_ref.dtype)
        d_tau = (jnp.dot(xr, cre, preferred_element_type=F32, precision=hp)
                 - jnp.dot(xi, cim, preferred_element_type=F32, precision=hp))
        if tau == 0:
            d_tau = d_tau + jnp.where(row == col, d_ref[...], 0.0)
        lag_blocks.append(d_tau.astype(t_ref.dtype))
    zero = jnp.zeros((LANES, LANES), t_ref.dtype)
    for s in range(L):
        for t in range(L):
            t_ref[s * LANES:(s + 1) * LANES, t * LANES:(t + 1) * LANES] = (
                lag_blocks[t - s] if t >= s else zero)
    for t in range(L):
        qr, qi = pt_ref[0, :, t + 1:t + 2], pt_ref[1, :, t + 1:t + 2]
        cols = slice(t * LANES, (t + 1) * LANES)
        v_ref[:k, cols] = (cre * qr - cim * qi).astype(v_ref.dtype)
        v_ref[k:, cols] = (-(cre * qi + cim * qr)).astype(v_ref.dtype)


def _s5_operators(A_re, A_im, log_dt, B_re, B_im, C_re, C_im, D_skip):
    n_g, n_p = A_re.shape
    n_h = B_re.shape[-1]
    L = S5_CHUNK
    gs = LANES // n_h
    n_slab = n_g // gs
    k = gs * n_p
    a_re, a_im = A_re.astype(F32), A_im.astype(F32)
    dt = jnp.exp(log_dt.astype(F32))[:, None]
    tau = jnp.arange(L + 1, dtype=F32)[:, None, None]
    mag = jnp.exp(tau * (dt * a_re))
    ang = tau * (dt * a_im)
    pw = jnp.stack([mag * jnp.cos(ang), mag * jnp.sin(ang)])
    lb_re, lb_im = pw[0, 1], pw[1, 1]
    den = a_re * a_re + a_im * a_im
    nr, ni = lb_re - 1.0, lb_im
    f_re = (nr * a_re + ni * a_im) / den
    f_im = (ni * a_re - nr * a_im) / den
    br, bi = B_re.astype(F32), B_im.astype(F32)
    bb_re = f_re[..., None] * br - f_im[..., None] * bi
    bb_im = f_re[..., None] * bi + f_im[..., None] * br

    eye = jnp.eye(gs, dtype=F32)

    def slab_b(x):
        x = x.reshape(n_slab, gs, n_p, n_h).transpose(0, 1, 3, 2)
        return (x[:, :, :, None, :] * eye[None, :, None, :, None]).reshape(n_slab, LANES, k)

    def slab_c(x):
        x = x.reshape(n_slab, gs, n_h, n_p).transpose(0, 1, 3, 2)
        return (x[:, :, :, None, :] * eye[None, :, None, :, None]).reshape(n_slab, k, LANES)

    p_tab = pw.reshape(2, L + 1, n_slab, k).transpose(2, 0, 1, 3)
    pt_tab = jnp.pad(pw.reshape(2, L + 1, n_slab, k).transpose(2, 0, 3, 1),
                     ((0, 0), (0, 0), (0, 0), (0, LANES - (L + 1))))
    mat_b = pl.BlockSpec((None, LANES, k), lambda j: (j, 0, 0))
    mat_c = pl.BlockSpec((None, k, LANES), lambda j: (j, 0, 0))
    t_op, w_op, v_op = pl.pallas_call(
        functools.partial(_s5_operators_body, L=L),
        grid=(n_slab,),
        in_specs=[mat_b, mat_b, mat_c, mat_c,
                  pl.BlockSpec((None, 2, L + 1, k), lambda j: (j, 0, 0, 0)),
                  pl.BlockSpec((None, 2, k, LANES), lambda j: (j, 0, 0, 0)),
                  pl.BlockSpec((1, LANES), lambda j: (0, j))],
        out_specs=[pl.BlockSpec((None, L * LANES, L * LANES), lambda j: (j, 0, 0)),
                   pl.BlockSpec((None, L * LANES, 2 * k), lambda j: (j, 0, 0)),
                   pl.BlockSpec((None, 2 * k, L * LANES), lambda j: (j, 0, 0))],
        out_shape=[jax.ShapeDtypeStruct((n_slab, L * LANES, L * LANES), BF16),
                   jax.ShapeDtypeStruct((n_slab, L * LANES, 2 * k), BF16),
                   jax.ShapeDtypeStruct((n_slab, 2 * k, L * LANES), BF16)],
        compiler_params=_cparams("parallel"),
        name="s5_operators",
    )(slab_b(bb_re), slab_b(bb_im), slab_c(C_re.astype(F32)), slab_c(C_im.astype(F32)),
      p_tab, pt_tab, D_skip.astype(F32).reshape(1, n_g * n_h))

    a_op = pw[:, L].reshape(2, n_slab, k).transpose(1, 0, 2).reshape(1, n_slab * 2 * k)
    return t_op, w_op, v_op, a_op


def _chunk_rows(piece_refs):
    return jnp.concatenate([r[...] for r in piece_refs], axis=1)


def _s5_increment_body(*refs):
    w_ref, z_ref = refs[-2], refs[-1]
    z_ref[...] = jnp.dot(_chunk_rows(refs[:-2]), w_ref[...], preferred_element_type=F32)


def _s5_scan_body(z_ref, a_ref, o_ref, st_ref, *, tc):
    @pl.when(pl.program_id(2) == 0)
    def _():
        st_ref[...] = jnp.zeros_like(st_ref)

    half = z_ref.shape[1] // 2
    a_re, a_im = a_ref[:, :half], a_ref[:, half:]

    def step(i, carry):
        re, im = carry
        base = pl.multiple_of(i * SUBLANES, SUBLANES)
        inc = z_ref[pl.ds(base, SUBLANES), :]
        before_re, before_im = [], []
        for r in range(SUBLANES):
            before_re.append(re)
            before_im.append(im)
            re, im = (a_re * re - a_im * im + inc[r:r + 1, :half],
                      a_re * im + a_im * re + inc[r:r + 1, half:])
        o_ref[pl.ds(base, SUBLANES), :half] = jnp.concatenate(before_re, axis=0)
        o_ref[pl.ds(base, SUBLANES), half:] = jnp.concatenate(before_im, axis=0)
        return re, im

    re, im = lax.fori_loop(0, tc // SUBLANES, step, (st_ref[0:1, :half], st_ref[0:1, half:]))
    st_ref[0:1, :half] = re
    st_ref[0:1, half:] = im


def _s5_output_body(*refs):
    x_ref, t_ref, v_ref, o_ref = refs[-4:]
    y = jnp.dot(_chunk_rows(refs[:-4]), t_ref[...], preferred_element_type=F32)
    y = y + jnp.dot(x_ref[...].astype(BF16), v_ref[...], preferred_element_type=F32)
    o_ref[...] = jax.nn.gelu(y).astype(o_ref.dtype)


def _s5_glu_body(*refs):
    w_ref, b_ref, o_ref = refs[-3:]
    y = _chunk_rows(refs[:-3])
    gate = _sigmoid(jnp.dot(y, w_ref[...], preferred_element_type=F32) + b_ref[...])
    o_ref[...] = (y.astype(F32) * gate).astype(o_ref.dtype)


def _chunked_proj_body(h_ref, w_ref, side_in_ref, u_ref, side_out_ref, acc_ref, *, L):
    acc = jnp.dot(h_ref[...], w_ref[...], preferred_element_type=F32)
    n_chunks, width = u_ref.shape[0], w_ref.shape[1]
    for c in range(width // LANES):
        acc_ref[c] = acc[:, c * LANES:(c + 1) * LANES]
    for tau in range(L):
        for c in range(width // LANES):
            u_ref[:, tau * width + c * LANES:tau * width + (c + 1) * LANES] = (
                acc_ref[c, pl.ds(tau, n_chunks, stride=L), :].astype(u_ref.dtype))
    side_out_ref[...] = side_in_ref[...].astype(side_out_ref.dtype)


def _chunked_proj(h, w, side_cast, tm):
    m, k = h.shape
    width = w.shape[1]
    L = S5_CHUNK
    steps = m // tm
    slab = side_cast.shape[0] // steps
    assert slab * steps == side_cast.shape[0] and slab % BF16_SUBLANES == 0 and tm % L == 0
    side_spec = pl.BlockSpec((slab, side_cast.shape[1]), lambda i: (i, 0))
    chunk_spec = pl.BlockSpec((tm // L, L * width), lambda i: (i, 0))
    return pl.pallas_call(
        functools.partial(_chunked_proj_body, L=L),
        grid=(steps,),
        in_specs=[pl.BlockSpec((tm, k), lambda i: (i, 0)), pl.BlockSpec((k, width), lambda i: (0, 0)),
                  side_spec],
        out_specs=[chunk_spec, side_spec],
        out_shape=[jax.ShapeDtypeStruct((m // L, L * width), BF16),
                   jax.ShapeDtypeStruct(side_cast.shape, BF16)],
        scratch_shapes=[pltpu.VMEM((width // LANES, tm, LANES), F32)],
        compiler_params=_cparams("parallel"),
        name="ssm_in_proj",
    )(h, w, side_cast)


def _s5_branch(uu, ops, w_glu, b_glu, bsz, seq):
    t_op, w_op, v_op, a_op = ops
    n_slab = t_op.shape[0]
    L = S5_CHUNK
    width = uu.shape[1] // L
    cw, sw = t_op.shape[1], w_op.shape[2]
    nc = bsz * seq // L
    ncb = seq // L
    tm = min(512, nc)
    pieces = [pl.BlockSpec((tm, LANES), lambda j, i, s=s: (i, s * n_slab + j)) for s in range(L)]

    z = pl.pallas_call(
        _s5_increment_body,
        grid=(n_slab, nc // tm),
        in_specs=pieces + [pl.BlockSpec((None, cw, sw), lambda j, i: (j, 0, 0))],
        out_specs=pl.BlockSpec((tm, sw), lambda j, i: (i, j)),
        out_shape=jax.ShapeDtypeStruct((nc, n_slab * sw), F32),
        compiler_params=_cparams("parallel", "parallel"),
        name="s5_increment",
    )(*([uu] * L), w_op)

    tc = min(128, ncb)
    nt = ncb // tc
    blk = pl.BlockSpec((tc, sw), lambda b, j, t: (b * nt + t, j))
    xprev = pl.pallas_call(
        functools.partial(_s5_scan_body, tc=tc),
        grid=(bsz, n_slab, nt),
        in_specs=[blk, pl.BlockSpec((1, sw), lambda b, j, t: (0, j))],
        out_specs=blk,
        out_shape=jax.ShapeDtypeStruct((nc, n_slab * sw), F32),
        scratch_shapes=[pltpu.VMEM((SUBLANES, sw), F32)],
        compiler_params=_cparams("parallel", "parallel", "arbitrary"),
        name="s5_scan",
    )(z, a_op)

    ys = pl.pallas_call(
        _s5_output_body,
        grid=(n_slab, nc // tm),
        in_specs=pieces + [pl.BlockSpec((tm, sw), lambda j, i: (i, j)),
                           pl.BlockSpec((None, cw, cw), lambda j, i: (j, 0, 0)),
                           pl.BlockSpec((None, sw, cw), lambda j, i: (j, 0, 0))],
        out_specs=pl.BlockSpec((tm, cw), lambda j, i: (i, j)),
        out_shape=jax.ShapeDtypeStruct((nc, n_slab * cw), BF16),
        compiler_params=_cparams("parallel", "parallel"),
        name="s5_output",
    )(*([uu] * L), xprev, t_op, v_op)

    tok = [pl.BlockSpec((tm, LANES), lambda i, t, j=j: (i, j * L + t)) for j in range(n_slab)]
    out = pl.pallas_call(
        _s5_glu_body,
        grid=(nc // tm, L),
        in_specs=tok + [pl.BlockSpec((width, width), lambda i, t: (0, 0)),
                        pl.BlockSpec((1, width), lambda i, t: (0, 0))],
        out_specs=pl.BlockSpec((tm, width), lambda i, t: (i, t)),
        out_shape=jax.ShapeDtypeStruct((nc, L * width), BF16),
        compiler_params=_cparams("parallel", "parallel"),
        name="s5_glu",
    )(*([ys] * n_slab), w_glu.astype(BF16), b_glu.reshape(1, width).astype(F32))
    return out.reshape(bsz * seq, width)


def _layer(x, mem_n, p, bsz, seq, mem_len):
    d = x.shape[1]
    aw = p["w_attn_up"].shape[0]
    heads = aw // ATTN_HEAD_DIM
    sw = p["w_ssm_up"].shape[0]
    off_f = 3 * aw
    off_u = off_f + heads
    off_g = off_u + sw

    w_in_t = p["w_in"].T
    tile = 256
    w_qkv = _transpose_cast(w_in_t, 0, off_f, tile, lambda j: j, tk=d,
                            scale=LOG2E * ATTN_HEAD_DIM ** -0.5, n_scaled=aw // tile)
    w_f = jnp.pad(w_in_t[off_f:off_u].T, ((0, 0), (0, LANES - heads))).astype(BF16)
    b_f = jnp.pad(p["b_f"].astype(F32), (0, LANES - heads)).reshape(1, LANES)
    h, log_f = _rmsnorm(x, p["g_mix"], BF16, tm=256, forget=(w_f, b_f))
    w_u = _transpose_cast(w_in_t, off_u, sw, tile, lambda j: j, tk=d)
    tn_mix = tile
    n_mix = d // tn_mix
    w_g = _transpose_cast(w_in_t, off_g, N_BRANCH * d, tn_mix,
                          lambda j: (j % n_mix) * N_BRANCH + j // n_mix, tk=d)

    qkv = _fused_matmul([(h, w_qkv, 0)], _ep_plain, BF16, n=3 * aw, tm=1024, tn=1024,
                        name="qkv_proj")
    fox, fox_rows, fox_position = _fox_attention(qkv, _forget_bias(log_f, bsz, seq, heads),
                                                 bsz, seq, heads, tq=1024)

    u, w_out = _chunked_proj(h, w_u, p["w_out"], tm=1024)
    ops = _s5_operators(p["A_re"], p["A_im"], p["log_dt"], p["B_re"], p["B_im"],
                        p["C_re"], p["C_im"], p["D_skip"])
    y = _s5_branch(u, ops, p["w_glu"], p["b_glu"], bsz, seq)

    b_gate = (p["b_gate"].astype(F32).reshape(N_BRANCH, n_mix, tn_mix).transpose(1, 0, 2)
              .reshape(1, N_BRANCH * d))
    merged, w_ff1 = _fused_matmul(
        [(h, w_g, 0, N_BRANCH), (fox, p["w_attn_up"].astype(BF16), 0, 1, fox_position),
         (y, p["w_ssm_up"].astype(BF16), 0)],
        _ep_gated_merge, BF16, n=d, tm=fox_rows, tn=tn_mix,
        aux=[(b_gate, 0, N_BRANCH)], side_cast=p["w_ff1"], name="gated_merge")
    x = _fused_matmul([(merged, w_out, 0)], _ep_add_res, F32, n=d,
                      tm=1024, tn=512, res=x, name="mixer_out_proj")

    xw = p["wq_x"].shape[1]
    wq = (p["wq_x"] * (xw // XATTN_HEADS) ** -0.5).astype(BF16)
    w_kv = jnp.concatenate([p["wk_x"], p["wv_x"]], axis=1).astype(BF16)
    kv = _fused_matmul([(mem_n, w_kv, 0)], _ep_plain, BF16, n=2 * xw, tm=512, tn=1024,
                       name="xattn_kv_proj")
    x, hm = _xattn_block(x, p["g_xattn"], wq, kv, p["wo_x"].astype(BF16), p["g_mlp"],
                         seq, mem_len, tm=256)

    dff = p["w_ff1"].shape[1]
    hid, w_ff2 = _fused_matmul([(hm, w_ff1, 0)], _ep_relu2, BF16, n=dff, tm=1024, tn=1024,
                               side_cast=p["w_ff2"], name="mlp_up")
    x = _matmul_kpieces(hid, w_ff2, x, n_pieces=4, tm=512, tn=256, name="mlp_down")
    return x


_LAYER_PARAMS = ("g_mix", "w_in", "b_f", "b_gate", "A_re", "A_im", "log_dt", "B_re", "B_im",
                 "C_re", "C_im", "D_skip", "w_glu", "b_glu", "w_attn_up", "w_ssm_up", "w_out",
                 "g_xattn", "g_mem", "wq_x", "wk_x", "wv_x", "wo_x", "g_mlp", "w_ff1", "w_ff2")


def kernel(x, mem, g_mix, w_in, b_f, b_gate, A_re, A_im, log_dt, B_re, B_im, C_re, C_im, D_skip, w_glu, b_glu, w_attn_up, w_ssm_up, w_out, g_xattn, g_mem, wq_x, wk_x, wv_x, wo_x, g_mlp, w_ff1, w_ff2, g_final):
    stacked = dict(zip(_LAYER_PARAMS, (g_mix, w_in, b_f, b_gate, A_re, A_im, log_dt, B_re, B_im,
                                       C_re, C_im, D_skip, w_glu, b_glu, w_attn_up, w_ssm_up,
                                       w_out, g_xattn, g_mem, wq_x, wk_x, wv_x, wo_x, g_mlp,
                                       w_ff1, w_ff2)))
    bsz, seq, d = x.shape
    mem_len = mem.shape[1]
    xt = x.reshape(bsz * seq, d)
    mem2 = mem.reshape(bsz * mem_len, d)
    for l in range(g_mix.shape[0]):
        p = {k: v[l] for k, v in stacked.items()}
        mem_n = _rmsnorm(mem2, p["g_mem"], BF16, tm=256)
        xt = _layer(xt, mem_n, p, bsz, seq, mem_len)
    out = _rmsnorm(xt, g_final, x.dtype, tm=256)
    return out.reshape(bsz, seq, d)
```

```python
import functools
import math

import jax
import jax.numpy as jnp
from jax import lax
from jax.experimental import pallas as pl
from jax.experimental.pallas import tpu as pltpu

F32 = jnp.float32
BF16 = jnp.bfloat16

V7X_VMEM_LIMIT_BYTES = 56 * 1024 * 1024
LANES = 128
SUBLANES = 8
BF16_SUBLANES = 16

EPS = 1e-6
NEG_INF = -1e30
LOG2E = math.log2(math.e)
N_BIAS_PIECES = 3
ATTN_HEAD_DIM = 128
S5_CHUNK = 16
XATTN_HEADS = 4
N_BRANCH = 2


def _cparams(*sem):
    return pltpu.CompilerParams(dimension_semantics=sem,
                                vmem_limit_bytes=V7X_VMEM_LIMIT_BYTES)


def _rms_scale(x, g):
    return (x * lax.rsqrt(jnp.mean(x * x, axis=-1, keepdims=True) + EPS)) * g


def _log_sigmoid(z):
    return jnp.minimum(z, 0.0) - jnp.log1p(jnp.exp(-jnp.abs(z)))


def _sigmoid(z):
    return 0.5 * jnp.tanh(0.5 * z) + 0.5


def _rmsnorm_body(x_ref, g_ref, *rest, has_proj):
    h = _rms_scale(x_ref[...].astype(F32), g_ref[...]).astype(rest[-1 - has_proj].dtype)
    if has_proj:
        w_ref, b_ref, o_ref, f_ref = rest
        f_ref[...] = _log_sigmoid(jnp.dot(h, w_ref[...], preferred_element_type=F32) + b_ref[...])
    else:
        o_ref, = rest
    o_ref[...] = h


def _rmsnorm(x, g, out_dtype, tm, forget=None):
    m, d = x.shape
    tm = min(tm, m)
    row = pl.BlockSpec((tm, d), lambda i: (i, 0))
    operands = [x, g.reshape(1, d).astype(F32)]
    in_specs = [row, pl.BlockSpec((1, d), lambda i: (0, 0))]
    out_specs, out_shape = [row], [jax.ShapeDtypeStruct((m, d), out_dtype)]
    if forget is not None:
        n = forget[0].shape[1]
        operands += list(forget)
        in_specs += [pl.BlockSpec((d, n), lambda i: (0, 0)), pl.BlockSpec((1, n), lambda i: (0, 0))]
        out_specs.append(pl.BlockSpec((tm, n), lambda i: (i, 0)))
        out_shape.append(jax.ShapeDtypeStruct((m, n), F32))
    outs = pl.pallas_call(
        functools.partial(_rmsnorm_body, has_proj=forget is not None),
        grid=(m // tm,),
        in_specs=in_specs,
        out_specs=out_specs,
        out_shape=out_shape,
        compiler_params=_cparams("parallel"),
        name="rmsnorm",
    )(*operands)
    return outs if forget is not None else outs[0]


def _mm_body(*refs, n_pairs, n_aux, has_res, has_side, epilogue):
    accs = [jnp.dot(refs[2 * p][...], refs[2 * p + 1][...], preferred_element_type=F32)
            for p in range(n_pairs)]
    pos = 2 * n_pairs
    aux = [refs[pos + i][...] for i in range(n_aux)]
    pos += n_aux
    res = refs[pos][...].astype(F32) if has_res else None
    pos += has_res
    if has_side:
        refs[-1][...] = refs[pos][...].astype(refs[-1].dtype)
        pos += 1
    o_ref = refs[pos]
    o_ref[...] = epilogue(accs, aux, res).astype(o_ref.dtype)


def _fused_matmul(pairs, epilogue, out_dtype, *, n, tm, tn, aux=(), res=None, side_cast=None,
                  name="fused_matmul"):
    m = pairs[0][0].shape[0]
    tm, tn = min(tm, m), min(tn, n)
    n_aux, has_res, has_side = len(aux), res is not None, side_cast is not None
    nj = n // tn
    operands, in_specs = [], []
    for lhs, rhs, off, *opt in pairs:
        kp = lhs.shape[1]
        width = tn * (opt[0] if opt else 1)
        row_block = opt[1] if len(opt) > 1 else (lambda i: i)
        operands += [lhs, rhs]
        in_specs += [pl.BlockSpec((tm, kp), lambda i, j, rb=row_block: (rb(i), 0)),
                     pl.BlockSpec((kp, width), lambda i, j, off=off: (0, j + off))]
    for vec, off, *mult in aux:
        width = tn * (mult[0] if mult else 1)
        operands.append(vec)
        in_specs.append(pl.BlockSpec((1, width), lambda i, j, off=off: (0, j + off)))
    if has_res:
        operands.append(res)
        in_specs.append(pl.BlockSpec((tm, tn), lambda i, j: (i, j)))
    out_specs = [pl.BlockSpec((tm, tn), lambda i, j: (i, j))]
    out_shape = [jax.ShapeDtypeStruct((m, n), out_dtype)]
    if has_side:
        slab = side_cast.shape[0] // ((m // tm) * nj)
        assert slab * (m // tm) * nj == side_cast.shape[0] and slab % BF16_SUBLANES == 0
        side_spec = pl.BlockSpec((slab, side_cast.shape[1]), lambda i, j: (i * nj + j, 0))
        operands.append(side_cast)
        in_specs.append(side_spec)
        out_specs.append(side_spec)
        out_shape.append(jax.ShapeDtypeStruct(side_cast.shape, BF16))
    outs = pl.pallas_call(
        functools.partial(_mm_body, n_pairs=len(pairs), n_aux=n_aux, has_res=has_res,
                          has_side=has_side, epilogue=epilogue),
        grid=(m // tm, nj),
        in_specs=in_specs,
        out_specs=out_specs,
        out_shape=out_shape,
        compiler_params=_cparams("parallel", "parallel"),
        name=name,
    )(*operands)
    return outs if has_side else outs[0]


W_IN_EDGE_ROWS = 16


def _transpose_cast_body(a_ref, b_ref, o_ref, *, shift, scale, n_scaled):
    rows = a_ref.shape[0]
    x = jnp.concatenate([a_ref[...], b_ref[...]], axis=0)[shift:shift + rows, :]
    if n_scaled:
        x = x * jnp.where(pl.program_id(0) < n_scaled, scale, 1.0)
    o_ref[...] = x.T.astype(o_ref.dtype)


def _transpose_cast(wt, start, n_cols, tile, out_block, tk, scale=1.0, n_scaled=0):
    k = wt.shape[1]
    base = start // tile * tile
    shift = start - base
    assert shift % SUBLANES == 0 and shift <= W_IN_EDGE_ROWS and n_cols % tile == 0
    assert start + n_cols + (W_IN_EDGE_ROWS - shift) <= wt.shape[0] or shift == 0
    edge_per_tile = tile // W_IN_EDGE_ROWS
    last_edge = wt.shape[0] // W_IN_EDGE_ROWS - 1
    return pl.pallas_call(
        functools.partial(_transpose_cast_body, shift=shift, scale=scale, n_scaled=n_scaled),
        grid=(n_cols // tile, k // tk),
        in_specs=[pl.BlockSpec((tile, tk), lambda j, c: (base // tile + j, c)),
                  pl.BlockSpec((W_IN_EDGE_ROWS, tk),
                               lambda j, c: (jnp.minimum((base // tile + j + 1) * edge_per_tile,
                                                         last_edge), c))],
        out_specs=pl.BlockSpec((tk, tile), lambda j, c: (c, out_block(j))),
        out_shape=jax.ShapeDtypeStruct((k, n_cols), BF16),
        compiler_params=_cparams("parallel", "parallel"),
        name="transpose_cast",
    )(wt, wt)


def _matmul_kpieces_body(*refs, n_pieces):
    res_ref, o_ref = refs[2 * n_pieces], refs[2 * n_pieces + 1]
    acc = res_ref[...].astype(F32)
    for p in range(n_pieces):
        acc = acc + jnp.dot(refs[p][...], refs[n_pieces + p][...], preferred_element_type=F32)
    o_ref[...] = acc.astype(o_ref.dtype)


def _matmul_kpieces(lhs, rhs, res, *, n_pieces, tm, tn, name):
    m, k = lhs.shape
    n = rhs.shape[1]
    kp = k // n_pieces
    tm, tn = min(tm, m), min(tn, n)
    in_specs = ([pl.BlockSpec((tm, kp), lambda i, j, p=p: (i, p)) for p in range(n_pieces)]
                + [pl.BlockSpec((kp, tn), lambda i, j, p=p: (p, j)) for p in range(n_pieces)]
                + [pl.BlockSpec((tm, tn), lambda i, j: (i, j))])
    return pl.pallas_call(
        functools.partial(_matmul_kpieces_body, n_pieces=n_pieces),
        grid=(m // tm, n // tn),
        in_specs=in_specs,
        out_specs=pl.BlockSpec((tm, tn), lambda i, j: (i, j)),
        out_shape=jax.ShapeDtypeStruct((m, n), res.dtype),
        compiler_params=_cparams("parallel", "parallel"),
        name=name,
    )(*([lhs] * n_pieces), *([rhs] * n_pieces), res)


def _ep_plain(accs, aux, res):
    return accs[0]


def _ep_add_res(accs, aux, res):
    return res + accs[0]


def _ep_gated_merge(accs, aux, res):
    tn = accs[1].shape[1]
    gates = _sigmoid(accs[0] + aux[0])
    return gates[:, :tn] * accs[1] + gates[:, tn:] * accs[2]


def _ep_relu2(accs, aux, res):
    r = jnp.maximum(accs[0], 0.0)
    return r * r


def _forget_bias_body(x_ref, o_ref, carry_ref, *, blk, nblk, heads):
    @pl.when(pl.program_id(1) == 0)
    def _():
        carry_ref[...] = jnp.zeros_like(carry_ref)

    r = lax.broadcasted_iota(jnp.int32, (blk, blk), 0)
    c = lax.broadcasted_iota(jnp.int32, (blk, blk), 1)
    tri = (c <= r).astype(F32)
    lane = lax.broadcasted_iota(jnp.int32, (blk, LANES), 1)

    def step(i, carry):
        base = pl.multiple_of(i * blk, blk)
        cs = jnp.dot(tri, x_ref[pl.ds(base, blk), :], preferred_element_type=F32,
                     precision=lax.Precision.HIGHEST) + carry
        for h in range(heads):
            bias = jnp.broadcast_to(cs[:, h:h + 1] * (-LOG2E), (blk, LANES))
            hi = bias.astype(BF16).astype(F32)
            mid = (bias - hi).astype(BF16).astype(F32)
            lo = bias - hi - mid
            pieces = jnp.where(lane == 0, hi, jnp.where(lane == 1, mid,
                                                        jnp.where(lane == 2, lo, 0.0)))
            o_ref[pl.ds(base, blk), h * LANES:(h + 1) * LANES] = pieces.astype(o_ref.dtype)
        return cs[blk - 1:blk, :]

    carry_ref[0:1, :] = lax.fori_loop(0, nblk, step, carry_ref[0:1, :])


def _forget_bias(log_f, bsz, seq, heads):
    blk = min(256, seq)
    tt = min(1024, seq)
    nt = seq // tt
    return pl.pallas_call(
        functools.partial(_forget_bias_body, blk=blk, nblk=tt // blk, heads=heads),
        grid=(bsz, nt),
        in_specs=[pl.BlockSpec((tt, LANES), lambda b, t: (b * nt + t, 0))],
        out_specs=pl.BlockSpec((tt, heads * LANES), lambda b, t: (b * nt + t, 0)),
        out_shape=jax.ShapeDtypeStruct((bsz * seq, heads * LANES), BF16),
        scratch_shapes=[pltpu.VMEM((SUBLANES, LANES), F32)],
        compiler_params=_cparams("parallel", "arbitrary"),
        name="forget_bias",
    )(log_f)


def _fox_body(qa_ref, qb_ref, k_ref, v_ref, cp_ref, o_ref, kaug_ref, vaug_ref, s_ref,
              *, tq, seq):
    p = pl.program_id(2)
    nq = seq // tq
    dh = qa_ref.shape[1]

    @pl.when(p == 0)
    def _():
        def fill(j, _):
            rows = pl.ds(pl.multiple_of(j * tq, tq), tq)
            kaug_ref[rows, :dh] = k_ref[rows, :]
            kaug_ref[rows, dh:] = cp_ref[rows, :]
            vaug_ref[rows, :dh] = v_ref[rows, :]
            vaug_ref[rows, dh:] = jnp.ones((tq, dh), vaug_ref.dtype)
            return 0
        lax.fori_loop(0, nq, fill, 0)

    lane = lax.broadcasted_iota(jnp.int32, (tq, dh), 1)
    unit = (lane < N_BIAS_PIECES).astype(qa_ref.dtype)
    q_a = jnp.concatenate([qa_ref[...], unit], axis=1)
    q_b = jnp.concatenate([qb_ref[...], unit], axis=1)
    above_diag = (lax.broadcasted_iota(jnp.int32, (tq, tq), 1)
                  > lax.broadcasted_iota(jnp.int32, (tq, tq), 0))

    def key_rows(t):
        j = jnp.where(t <= p, t, t - (p + 1))
        return pl.ds(pl.multiple_of(j * tq, tq), tq)

    def scores(t):
        q = jnp.where(t <= p, q_a, q_b)
        s = lax.dot_general(q, kaug_ref[key_rows(t), :], (((1,), (1,)), ((), ())),
                            preferred_element_type=F32)
        if t == nq:
            s = jnp.where(above_diag, NEG_INF, s)
        elif t < nq // 2:
            s = jnp.where(jnp.logical_and(above_diag, t == p), NEG_INF, s)
        return s

    m = jnp.full((tq, 1), NEG_INF, F32)
    acc = jnp.zeros((tq, 2 * dh), F32)
    s_ref[0] = scores(0)
    for t in range(nq + 1):
        if t < nq:
            s_ref[(t + 1) % 2] = scores(t + 1)
        s = s_ref[t % 2]
        if 1 <= t <= nq // 2:
            first_b = t == p + 1
            m = jnp.where(first_b, NEG_INF, m)
            acc = jnp.where(first_b, 0.0, acc)
        m_new = jnp.maximum(m, jnp.max(s, axis=-1, keepdims=True))
        acc = jnp.exp2(m - m_new) * acc + jnp.dot(
            jnp.exp2(s - m_new).astype(vaug_ref.dtype), vaug_ref[key_rows(t), :],
            preferred_element_type=F32)
        m = m_new
        if t == 0:
            o_ref[:tq, :] = (acc[:, :dh] / acc[:, dh:]).astype(o_ref.dtype)
        elif t < nq // 2:
            out = (acc[:, :dh] / acc[:, dh:]).astype(o_ref.dtype)
            o_ref[:tq, :] = jnp.where(t <= p, out, o_ref[:tq, :])
    o_ref[tq:, :] = (acc[:, :dh] / acc[:, dh:]).astype(o_ref.dtype)


def _fox_attention(qkv, cp, bsz, seq, heads, tq):
    dh = ATTN_HEAD_DIM
    tq = min(tq, seq // 2)
    nq = seq // tq
    half = nq // 2
    assert nq % 2 == 0

    def position(i):
        b, qi = i // nq, i % nq
        return jnp.where(qi < half, (b * half + qi) * 2, (b * half + nq - 1 - qi) * 2 + 1)

    out = pl.pallas_call(
        functools.partial(_fox_body, tq=tq, seq=seq),
        grid=(bsz, heads, half),
        in_specs=[pl.BlockSpec((tq, dh), lambda b, h, p: (b * nq + p, h)),
                  pl.BlockSpec((tq, dh), lambda b, h, p: (b * nq + nq - 1 - p, h)),
                  pl.BlockSpec((seq, dh), lambda b, h, p: (b, heads + h)),
                  pl.BlockSpec((seq, dh), lambda b, h, p: (b, 2 * heads + h)),
                  pl.BlockSpec((seq, dh), lambda b, h, p: (b, h))],
        out_specs=pl.BlockSpec((2 * tq, dh), lambda b, h, p: (b * half + p, h)),
        out_shape=jax.ShapeDtypeStruct((bsz * seq, heads * dh), BF16),
        scratch_shapes=[pltpu.VMEM((seq, 2 * dh), BF16), pltpu.VMEM((seq, 2 * dh), BF16),
                        pltpu.VMEM((2, tq, tq), F32)],
        compiler_params=_cparams("parallel", "parallel", "arbitrary"),
        name="fox_attention",
    )(qkv, qkv, qkv, qkv, cp)
    return out, tq, position


def _xattn_block_body(x_ref, gx_ref, wq_ref, kv_ref, wo_ref, gn_ref, xo_ref, hn_ref):
    x = x_ref[...]
    width = wq_ref.shape[1]
    dh = width // XATTN_HEADS
    hx = _rms_scale(x, gx_ref[...]).astype(wq_ref.dtype)
    q = jnp.dot(hx, wq_ref[...], preferred_element_type=F32).astype(kv_ref.dtype)
    heads = []
    for hd in range(XATTN_HEADS):
        k = kv_ref[:, hd * dh:(hd + 1) * dh]
        v = kv_ref[:, width + hd * dh:width + (hd + 1) * dh]
        s = lax.dot_general(q[:, hd * dh:(hd + 1) * dh], k, (((1,), (1,)), ((), ())),
                            preferred_element_type=F32)
        p = jnp.exp(s - jnp.max(s, axis=-1, keepdims=True))
        l = jnp.sum(p, axis=-1, keepdims=True)
        o = jnp.dot(p.astype(v.dtype), v, preferred_element_type=F32) / l
        heads.append(o.astype(wo_ref.dtype))
    x2 = x + jnp.dot(jnp.concatenate(heads, axis=1), wo_ref[...], preferred_element_type=F32)
    xo_ref[...] = x2
    hn_ref[...] = _rms_scale(x2, gn_ref[...]).astype(hn_ref.dtype)


def _xattn_block(x, g_x, wq, kv, wo, g_next, seq, mem_len, tm):
    t, d = x.shape
    width = wq.shape[1]
    tm = min(tm, seq)
    nq = seq // tm
    row = pl.BlockSpec((tm, d), lambda i: (i, 0))
    vec = pl.BlockSpec((1, d), lambda i: (0, 0))
    resident = pl.Buffered(1)
    return pl.pallas_call(
        _xattn_block_body,
        grid=(t // tm,),
        in_specs=[row, vec,
                  pl.BlockSpec((d, width), lambda i: (0, 0), pipeline_mode=resident),
                  pl.BlockSpec((mem_len, 2 * width), lambda i: (i // nq, 0)),
                  pl.BlockSpec((width, d), lambda i: (0, 0), pipeline_mode=resident),
                  vec],
        out_specs=[row, row],
        out_shape=[jax.ShapeDtypeStruct((t, d), F32), jax.ShapeDtypeStruct((t, d), BF16)],
        compiler_params=_cparams("parallel"),
        name="xattn_block",
    )(x, g_x.reshape(1, d).astype(F32), wq, kv, wo, g_next.reshape(1, d).astype(F32))


def _s5_operators_body(bre_ref, bim_ref, cre_ref, cim_ref, p_ref, pt_ref, d_ref,
                        t_ref, w_ref, v_ref, *, L):
    hp = lax.Precision.HIGHEST
    bre, bim = bre_ref[...], bim_ref[...]
    cre, cim = cre_ref[...], cim_ref[...]
    k = bre.shape[1]
    row = lax.broadcasted_iota(jnp.int32, (LANES, LANES), 0)
    col = lax.broadcasted_iota(jnp.int32, (LANES, LANES), 1)
    lag_blocks = []
    for tau in range(L):
        pr, pi = p_ref[0, tau:tau + 1, :], p_ref[1, tau:tau + 1, :]
        xr, xi = bre * pr - bim * pi, bre * pi + bim * pr
        rows = slice((L - 1 - tau) * LANES, (L - tau) * LANES)
        w_ref[rows, :k] = xr.astype(w_ref.dtype)
        w_ref[rows, k:] = xi.astype(w_ref.dtype)
        d_tau = (jnp.dot(xr, cre, preferred_element_type=F32, precision=hp)
                 - jnp.dot(xi, cim, preferred_element_type=F32, precision=hp))
        if tau == 0:
            d_tau = d_tau + jnp.where(row == col, d_ref[...], 0.0)
        lag_blocks.append(d_tau.astype(t_ref.dtype))
    zero = jnp.zeros((LANES, LANES), t_ref.dtype)
    for s in range(L):
        for t in range(L):
            t_ref[s * LANES:(s + 1) * LANES, t * LANES:(t + 1) * LANES] = (
                lag_blocks[t - s] if t >= s else zero)
    for t in range(L):
        qr, qi = pt_ref[0, :, t + 1:t + 2], pt_ref[1, :, t + 1:t + 2]
        cols = slice(t * LANES, (t + 1) * LANES)
        v_ref[:k, cols] = (cre * qr - cim * qi).astype(v_ref.dtype)
        v_ref[k:, cols] = (-(cre * qi + cim * qr)).astype(v_ref.dtype)


def _s5_operators(A_re, A_im, log_dt, B_re, B_im, C_re, C_im, D_skip):
    n_g, n_p = A_re.shape
    n_h = B_re.shape[-1]
    L = S5_CHUNK
    gs = LANES // n_h
    n_slab = n_g // gs
    k = gs * n_p
    a_re, a_im = A_re.astype(F32), A_im.astype(F32)
    dt = jnp.exp(log_dt.astype(F32))[:, None]
    tau = jnp.arange(L + 1, dtype=F32)[:, None, None]
    mag = jnp.exp(tau * (dt * a_re))
    ang = tau * (dt * a_im)
    pw = jnp.stack([mag * jnp.cos(ang), mag * jnp.sin(ang)])
    lb_re, lb_im = pw[0, 1], pw[1, 1]
    den = a_re * a_re + a_im * a_im
    nr, ni = lb_re - 1.0, lb_im
    f_re = (nr * a_re + ni * a_im) / den
    f_im = (ni * a_re - nr * a_im) / den
    br, bi = B_re.astype(F32), B_im.astype(F32)
    bb_re = f_re[..., None] * br - f_im[..., None] * bi
    bb_im = f_re[..., None] * bi + f_im[..., None] * br

    eye = jnp.eye(gs, dtype=F32)

    def slab_b(x):
        x = x.reshape(n_slab, gs, n_p, n_h).transpose(0, 1, 3, 2)
        return (x[:, :, :, None, :] * eye[None, :, None, :, None]).reshape(n_slab, LANES, k)

    def slab_c(x):
        x = x.reshape(n_slab, gs, n_h, n_p).transpose(0, 1, 3, 2)
        return (x[:, :, :, None, :] * eye[None, :, None, :, None]).reshape(n_slab, k, LANES)

    p_tab = pw.reshape(2, L + 1, n_slab, k).transpose(2, 0, 1, 3)
    pt_tab = jnp.pad(pw.reshape(2, L + 1, n_slab, k).transpose(2, 0, 3, 1),
                     ((0, 0), (0, 0), (0, 0), (0, LANES - (L + 1))))
    mat_b = pl.BlockSpec((None, LANES, k), lambda j: (j, 0, 0))
    mat_c = pl.BlockSpec((None, k, LANES), lambda j: (j, 0, 0))
    t_op, w_op, v_op = pl.pallas_call(
        functools.partial(_s5_operators_body, L=L),
        grid=(n_slab,),
        in_specs=[mat_b, mat_b, mat_c, mat_c,
                  pl.BlockSpec((None, 2, L + 1, k), lambda j: (j, 0, 0, 0)),
                  pl.BlockSpec((None, 2, k, LANES), lambda j: (j, 0, 0, 0)),
                  pl.BlockSpec((1, LANES), lambda j: (0, j))],
        out_specs=[pl.BlockSpec((None, L * LANES, L * LANES), lambda j: (j, 0, 0)),
                   pl.BlockSpec((None, L * LANES, 2 * k), lambda j: (j, 0, 0)),
                   pl.BlockSpec((None, 2 * k, L * LANES), lambda j: (j, 0, 0))],
        out_shape=[jax.ShapeDtypeStruct((n_slab, L * LANES, L * LANES), BF16),
                   jax.ShapeDtypeStruct((n_slab, L * LANES, 2 * k), BF16),
                   jax.ShapeDtypeStruct((n_slab, 2 * k, L * LANES), BF16)],
        compiler_params=_cparams("parallel"),
        name="s5_operators",
    )(slab_b(bb_re), slab_b(bb_im), slab_c(C_re.astype(F32)), slab_c(C_im.astype(F32)),
      p_tab, pt_tab, D_skip.astype(F32).reshape(1, n_g * n_h))

    a_op = pw[:, L].reshape(2, n_slab, k).transpose(1, 0, 2).reshape(1, n_slab * 2 * k)
    return t_op, w_op, v_op, a_op


def _chunk_rows(piece_refs):
    return jnp.concatenate([r[...] for r in piece_refs], axis=1)


def _s5_increment_body(*refs):
    w_ref, z_ref = refs[-2], refs[-1]
    z_ref[...] = jnp.dot(_chunk_rows(refs[:-2]), w_ref[...], preferred_element_type=F32)


def _s5_scan_body(z_ref, a_ref, o_ref, st_ref, *, tc):
    @pl.when(pl.program_id(2) == 0)
    def _():
        st_ref[...] = jnp.zeros_like(st_ref)

    half = z_ref.shape[1] // 2
    a_re, a_im = a_ref[:, :half], a_ref[:, half:]

    def step(i, carry):
        re, im = carry
        base = pl.multiple_of(i * SUBLANES, SUBLANES)
        inc = z_ref[pl.ds(base, SUBLANES), :]
        before_re, before_im = [], []
        for r in range(SUBLANES):
            before_re.append(re)
            before_im.append(im)
            re, im = (a_re * re - a_im * im + inc[r:r + 1, :half],
                      a_re * im + a_im * re + inc[r:r + 1, half:])
        o_ref[pl.ds(base, SUBLANES), :half] = jnp.concatenate(before_re, axis=0)
        o_ref[pl.ds(base, SUBLANES), half:] = jnp.concatenate(before_im, axis=0)
        return re, im

    re, im = lax.fori_loop(0, tc // SUBLANES, step, (st_ref[0:1, :half], st_ref[0:1, half:]))
    st_ref[0:1, :half] = re
    st_ref[0:1, half:] = im


def _s5_output_body(*refs):
    x_ref, t_ref, v_ref, o_ref = refs[-4:]
    y = jnp.dot(_chunk_rows(refs[:-4]), t_ref[...], preferred_element_type=F32)
    y = y + jnp.dot(x_ref[...].astype(BF16), v_ref[...], preferred_element_type=F32)
    o_ref[...] = jax.nn.gelu(y).astype(o_ref.dtype)


def _s5_glu_body(*refs):
    w_ref, b_ref, o_ref = refs[-3:]
    y = _chunk_rows(refs[:-3])
    gate = _sigmoid(jnp.dot(y, w_ref[...], preferred_element_type=F32) + b_ref[...])
    o_ref[...] = (y.astype(F32) * gate).astype(o_ref.dtype)


def _s5_branch(u, ops, w_glu, b_glu, bsz, seq):
    t_op, w_op, v_op, a_op = ops
    n_slab = t_op.shape[0]
    L = S5_CHUNK
    width = u.shape[1]
    cw, sw = t_op.shape[1], w_op.shape[2]
    nc = bsz * seq // L
    ncb = seq // L
    uu = u.reshape(nc, L * width)
    tm = min(512, nc)
    pieces = [pl.BlockSpec((tm, LANES), lambda j, i, s=s: (i, s * n_slab + j)) for s in range(L)]

    z = pl.pallas_call(
        _s5_increment_body,
        grid=(n_slab, nc // tm),
        in_specs=pieces + [pl.BlockSpec((None, cw, sw), lambda j, i: (j, 0, 0))],
        out_specs=pl.BlockSpec((tm, sw), lambda j, i: (i, j)),
        out_shape=jax.ShapeDtypeStruct((nc, n_slab * sw), F32),
        compiler_params=_cparams("parallel", "parallel"),
        name="s5_increment",
    )(*([uu] * L), w_op)

    tc = min(128, ncb)
    nt = ncb // tc
    blk = pl.BlockSpec((tc, sw), lambda b, j, t: (b * nt + t, j))
    xprev = pl.pallas_call(
        functools.partial(_s5_scan_body, tc=tc),
        grid=(bsz, n_slab, nt),
        in_specs=[blk, pl.BlockSpec((1, sw), lambda b, j, t: (0, j))],
        out_specs=blk,
        out_shape=jax.ShapeDtypeStruct((nc, n_slab * sw), F32),
        scratch_shapes=[pltpu.VMEM((SUBLANES, sw), F32)],
        compiler_params=_cparams("parallel", "parallel", "arbitrary"),
        name="s5_scan",
    )(z, a_op)

    ys = pl.pallas_call(
        _s5_output_body,
        grid=(n_slab, nc // tm),
        in_specs=pieces + [pl.BlockSpec((tm, sw), lambda j, i: (i, j)),
                           pl.BlockSpec((None, cw, cw), lambda j, i: (j, 0, 0)),
                           pl.BlockSpec((None, sw, cw), lambda j, i: (j, 0, 0))],
        out_specs=pl.BlockSpec((tm, cw), lambda j, i: (i, j)),
        out_shape=jax.ShapeDtypeStruct((nc, n_slab * cw), BF16),
        compiler_params=_cparams("parallel", "parallel"),
        name="s5_output",
    )(*([uu] * L), xprev, t_op, v_op)

    tok = [pl.BlockSpec((tm, LANES), lambda i, t, j=j: (i, j * L + t)) for j in range(n_slab)]
    out = pl.pallas_call(
        _s5_glu_body,
        grid=(nc // tm, L),
        in_specs=tok + [pl.BlockSpec((width, width), lambda i, t: (0, 0)),
                        pl.BlockSpec((1, width), lambda i, t: (0, 0))],
        out_specs=pl.BlockSpec((tm, width), lambda i, t: (i, t)),
        out_shape=jax.ShapeDtypeStruct((nc, L * width), BF16),
        compiler_params=_cparams("parallel", "parallel"),
        name="s5_glu",
    )(*([ys] * n_slab), w_glu.astype(BF16), b_glu.reshape(1, width).astype(F32))
    return out.reshape(bsz * seq, width)


def _layer(x, mem_n, p, bsz, seq, mem_len):
    d = x.shape[1]
    aw = p["w_attn_up"].shape[0]
    heads = aw // ATTN_HEAD_DIM
    sw = p["w_ssm_up"].shape[0]
    off_f = 3 * aw
    off_u = off_f + heads
    off_g = off_u + sw

    w_in_t = p["w_in"].T
    tile = 256
    w_qkv = _transpose_cast(w_in_t, 0, off_f, 2 * tile, lambda j: j, tk=d,
                            scale=LOG2E * ATTN_HEAD_DIM ** -0.5, n_scaled=aw // (2 * tile))
    w_f = jnp.pad(w_in_t[off_f:off_u].T, ((0, 0), (0, LANES - heads))).astype(BF16)
    b_f = jnp.pad(p["b_f"].astype(F32), (0, LANES - heads)).reshape(1, LANES)
    h, log_f = _rmsnorm(x, p["g_mix"], BF16, tm=512, forget=(w_f, b_f))
    w_u = _transpose_cast(w_in_t, off_u, sw, tile, lambda j: j, tk=d)
    tn_mix = tile
    n_mix = d // tn_mix
    w_g = _transpose_cast(w_in_t, off_g, N_BRANCH * d, tn_mix,
                          lambda j: (j % n_mix) * N_BRANCH + j // n_mix, tk=d)

    qkv = _fused_matmul([(h, w_qkv, 0)], _ep_plain, BF16, n=3 * aw, tm=1024, tn=1024,
                        name="qkv_proj")
    fox, fox_rows, fox_position = _fox_attention(qkv, _forget_bias(log_f, bsz, seq, heads),
                                                 bsz, seq, heads, tq=1024)

    u, w_out = _fused_matmul([(h, w_u, 0)], _ep_plain, BF16, n=sw, tm=1024, tn=1024,
                             side_cast=p["w_out"], name="ssm_in_proj")
    ops = _s5_operators(p["A_re"], p["A_im"], p["log_dt"], p["B_re"], p["B_im"],
                        p["C_re"], p["C_im"], p["D_skip"])
    y = _s5_branch(u, ops, p["w_glu"], p["b_glu"], bsz, seq)

    b_gate = (p["b_gate"].astype(F32).reshape(N_BRANCH, n_mix, tn_mix).transpose(1, 0, 2)
              .reshape(1, N_BRANCH * d))
    merged, w_ff1 = _fused_matmul(
        [(h, w_g, 0, N_BRANCH), (fox, p["w_attn_up"].astype(BF16), 0, 1, fox_position),
         (y, p["w_ssm_up"].astype(BF16), 0)],
        _ep_gated_merge, BF16, n=d, tm=fox_rows, tn=tn_mix,
        aux=[(b_gate, 0, N_BRANCH)], side_cast=p["w_ff1"], name="gated_merge")
    x = _fused_matmul([(merged, w_out, 0)], _ep_add_res, F32, n=d,
                      tm=1024, tn=1024, res=x, name="mixer_out_proj")

    xw = p["wq_x"].shape[1]
    wq = (p["wq_x"] * (xw // XATTN_HEADS) ** -0.5).astype(BF16)
    w_kv = jnp.concatenate([p["wk_x"], p["wv_x"]], axis=1).astype(BF16)
    kv = _fused_matmul([(mem_n, w_kv, 0)], _ep_plain, BF16, n=2 * xw, tm=512, tn=1024,
                       name="xattn_kv_proj")
    x, hm = _xattn_block(x, p["g_xattn"], wq, kv, p["wo_x"].astype(BF16), p["g_mlp"],
                         seq, mem_len, tm=256)

    dff = p["w_ff1"].shape[1]
    hid, w_ff2 = _fused_matmul([(hm, w_ff1, 0)], _ep_relu2, BF16, n=dff, tm=1024, tn=1024,
                               side_cast=p["w_ff2"], name="mlp_up")
    x = _matmul_kpieces(hid, w_ff2, x, n_pieces=4, tm=512, tn=256, name="mlp_down")
    return x


_LAYER_PARAMS = ("g_mix", "w_in", "b_f", "b_gate", "A_re", "A_im", "log_dt", "B_re", "B_im",
                 "C_re", "C_im", "D_skip", "w_glu", "b_glu", "w_attn_up", "w_ssm_up", "w_out",
                 "g_xattn", "g_mem", "wq_x", "wk_x", "wv_x", "wo_x", "g_mlp", "w_ff1", "w_ff2")


def kernel(x, mem, g_mix, w_in, b_f, b_gate, A_re, A_im, log_dt, B_re, B_im, C_re, C_im, D_skip, w_glu, b_glu, w_attn_up, w_ssm_up, w_out, g_xattn, g_mem, wq_x, wk_x, wv_x, wo_x, g_mlp, w_ff1, w_ff2, g_final):
    stacked = dict(zip(_LAYER_PARAMS, (g_mix, w_in, b_f, b_gate, A_re, A_im, log_dt, B_re, B_im,
                                       C_re, C_im, D_skip, w_glu, b_glu, w_attn_up, w_ssm_up,
                                       w_out, g_xattn, g_mem, wq_x, wk_x, wv_x, wo_x, g_mlp,
                                       w_ff1, w_ff2)))
    bsz, seq, d = x.shape
    mem_len = mem.shape[1]
    xt = x.reshape(bsz * seq, d)
    mem2 = mem.reshape(bsz * mem_len, d)
    for l in range(g_mix.shape[0]):
        p = {k: v[l] for k, v in stacked.items()}
        mem_n = _rmsnorm(mem2, p["g_mem"], BF16, tm=256)
        xt = _layer(xt, mem_n, p, bsz, seq, mem_len)
    out = _rmsnorm(xt, g_final, x.dtype, tm=512)
    return out.reshape(bsz, seq, d)
```

```python
import functools
import math

import jax
import jax.numpy as jnp
from jax import lax
from jax.experimental import pallas as pl
from jax.experimental.pallas import tpu as pltpu

F32 = jnp.float32
BF16 = jnp.bfloat16

V7X_VMEM_LIMIT_BYTES = 56 * 1024 * 1024
LANES = 128
SUBLANES = 8
BF16_SUBLANES = 16

EPS = 1e-6
NEG_INF = -1e30
LOG2E = math.log2(math.e)
N_BIAS_PIECES = 3
ATTN_HEAD_DIM = 128
S5_CHUNK = 16
XATTN_HEADS = 4
N_BRANCH = 2

TILE_MM = 1024
TILE_GATE_COLS = 256
TILE_ROWS = 512
TILE_XATTN_ROWS = 256
MLP_DOWN_COLS = 256
MLP_DOWN_K_PIECES = 4
FORGET_BLOCK = 256
S5_SCAN_ROWS = 128


def _cparams(*sem):
    return pltpu.CompilerParams(dimension_semantics=sem,
                                vmem_limit_bytes=V7X_VMEM_LIMIT_BYTES)


def _rms_scale(x, g):
    return (x * lax.rsqrt(jnp.mean(x * x, axis=-1, keepdims=True) + EPS)) * g


def _log_sigmoid(z):
    return jnp.minimum(z, 0.0) - jnp.log1p(jnp.exp(-jnp.abs(z)))


def _sigmoid(z):
    return 0.5 * jnp.tanh(0.5 * z) + 0.5


def _rmsnorm_body(x_ref, g_ref, *rest, has_proj):
    h = _rms_scale(x_ref[...].astype(F32), g_ref[...]).astype(rest[-1 - has_proj].dtype)
    if has_proj:
        w_ref, b_ref, o_ref, f_ref = rest
        f_ref[...] = _log_sigmoid(jnp.dot(h, w_ref[...], preferred_element_type=F32) + b_ref[...])
    else:
        o_ref, = rest
    o_ref[...] = h


def _rmsnorm(x, g, out_dtype, tm, forget=None):
    m, d = x.shape
    tm = min(tm, m)
    row = pl.BlockSpec((tm, d), lambda i: (i, 0))
    operands = [x, g.reshape(1, d).astype(F32)]
    in_specs = [row, pl.BlockSpec((1, d), lambda i: (0, 0))]
    out_specs, out_shape = [row], [jax.ShapeDtypeStruct((m, d), out_dtype)]
    if forget is not None:
        n = forget[0].shape[1]
        operands += list(forget)
        in_specs += [pl.BlockSpec((d, n), lambda i: (0, 0)), pl.BlockSpec((1, n), lambda i: (0, 0))]
        out_specs.append(pl.BlockSpec((tm, n), lambda i: (i, 0)))
        out_shape.append(jax.ShapeDtypeStruct((m, n), F32))
    outs = pl.pallas_call(
        functools.partial(_rmsnorm_body, has_proj=forget is not None),
        grid=(m // tm,),
        in_specs=in_specs,
        out_specs=out_specs,
        out_shape=out_shape,
        compiler_params=_cparams("parallel"),
        name="rmsnorm",
    )(*operands)
    return outs if forget is not None else outs[0]


def _mm_body(*refs, n_pairs, n_aux, has_res, has_side, epilogue):
    accs = [jnp.dot(refs[2 * p][...], refs[2 * p + 1][...], preferred_element_type=F32)
            for p in range(n_pairs)]
    pos = 2 * n_pairs
    aux = [refs[pos + i][...] for i in range(n_aux)]
    pos += n_aux
    res = refs[pos][...].astype(F32) if has_res else None
    pos += has_res
    if has_side:
        refs[-1][...] = refs[pos][...].astype(refs[-1].dtype)
        pos += 1
    o_ref = refs[pos]
    o_ref[...] = epilogue(accs, aux, res).astype(o_ref.dtype)


def _fused_matmul(pairs, epilogue, out_dtype, *, n, tm, tn, aux=(), res=None, side_cast=None,
                  name="fused_matmul"):
    m = pairs[0][0].shape[0]
    tm, tn = min(tm, m), min(tn, n)
    n_aux, has_res, has_side = len(aux), res is not None, side_cast is not None
    nj = n // tn
    operands, in_specs = [], []
    for lhs, rhs, off, *opt in pairs:
        kp = lhs.shape[1]
        width = tn * (opt[0] if opt else 1)
        row_block = opt[1] if len(opt) > 1 else (lambda i: i)
        operands += [lhs, rhs]
        in_specs += [pl.BlockSpec((tm, kp), lambda i, j, rb=row_block: (rb(i), 0)),
                     pl.BlockSpec((kp, width), lambda i, j, off=off: (0, j + off))]
    for vec, off, *mult in aux:
        width = tn * (mult[0] if mult else 1)
        operands.append(vec)
        in_specs.append(pl.BlockSpec((1, width), lambda i, j, off=off: (0, j + off)))
    if has_res:
        operands.append(res)
        in_specs.append(pl.BlockSpec((tm, tn), lambda i, j: (i, j)))
    out_specs = [pl.BlockSpec((tm, tn), lambda i, j: (i, j))]
    out_shape = [jax.ShapeDtypeStruct((m, n), out_dtype)]
    if has_side:
        slab = side_cast.shape[0] // ((m // tm) * nj)
        assert slab * (m // tm) * nj == side_cast.shape[0] and slab % BF16_SUBLANES == 0
        side_spec = pl.BlockSpec((slab, side_cast.shape[1]), lambda i, j: (i * nj + j, 0))
        operands.append(side_cast)
        in_specs.append(side_spec)
        out_specs.append(side_spec)
        out_shape.append(jax.ShapeDtypeStruct(side_cast.shape, BF16))
    outs = pl.pallas_call(
        functools.partial(_mm_body, n_pairs=len(pairs), n_aux=n_aux, has_res=has_res,
                          has_side=has_side, epilogue=epilogue),
        grid=(m // tm, nj),
        in_specs=in_specs,
        out_specs=out_specs,
        out_shape=out_shape,
        compiler_params=_cparams("parallel", "parallel"),
        name=name,
    )(*operands)
    return outs if has_side else outs[0]


W_IN_EDGE_ROWS = 16


def _transpose_cast_body(a_ref, b_ref, o_ref, *, shift, scale, n_scaled):
    rows = a_ref.shape[0]
    x = jnp.concatenate([a_ref[...], b_ref[...]], axis=0)[shift:shift + rows, :]
    if n_scaled:
        x = x * jnp.where(pl.program_id(0) < n_scaled, scale, 1.0)
    o_ref[...] = x.T.astype(o_ref.dtype)


def _transpose_cast(wt, start, n_cols, tile, out_block, tk, scale=1.0, n_scaled=0):
    k = wt.shape[1]
    base = start // tile * tile
    shift = start - base
    assert shift % SUBLANES == 0 and shift <= W_IN_EDGE_ROWS and n_cols % tile == 0
    assert start + n_cols + (W_IN_EDGE_ROWS - shift) <= wt.shape[0] or shift == 0
    edge_per_tile = tile // W_IN_EDGE_ROWS
    last_edge = wt.shape[0] // W_IN_EDGE_ROWS - 1
    return pl.pallas_call(
        functools.partial(_transpose_cast_body, shift=shift, scale=scale, n_scaled=n_scaled),
        grid=(n_cols // tile, k // tk),
        in_specs=[pl.BlockSpec((tile, tk), lambda j, c: (base // tile + j, c)),
                  pl.BlockSpec((W_IN_EDGE_ROWS, tk),
                               lambda j, c: (jnp.minimum((base // tile + j + 1) * edge_per_tile,
                                                         last_edge), c))],
        out_specs=pl.BlockSpec((tk, tile), lambda j, c: (c, out_block(j))),
        out_shape=jax.ShapeDtypeStruct((k, n_cols), BF16),
        compiler_params=_cparams("parallel", "parallel"),
        name="transpose_cast",
    )(wt, wt)


def _matmul_kpieces_body(*refs, n_pieces):
    res_ref, o_ref = refs[2 * n_pieces], refs[2 * n_pieces + 1]
    acc = res_ref[...].astype(F32)
    for p in range(n_pieces):
        acc = acc + jnp.dot(refs[p][...], refs[n_pieces + p][...], preferred_element_type=F32)
    o_ref[...] = acc.astype(o_ref.dtype)


def _matmul_kpieces(lhs, rhs, res, *, n_pieces, tm, tn, name):
    m, k = lhs.shape
    n = rhs.shape[1]
    kp = k // n_pieces
    tm, tn = min(tm, m), min(tn, n)
    in_specs = ([pl.BlockSpec((tm, kp), lambda i, j, p=p: (i, p)) for p in range(n_pieces)]
                + [pl.BlockSpec((kp, tn), lambda i, j, p=p: (p, j)) for p in range(n_pieces)]
                + [pl.BlockSpec((tm, tn), lambda i, j: (i, j))])
    return pl.pallas_call(
        functools.partial(_matmul_kpieces_body, n_pieces=n_pieces),
        grid=(m // tm, n // tn),
        in_specs=in_specs,
        out_specs=pl.BlockSpec((tm, tn), lambda i, j: (i, j)),
        out_shape=jax.ShapeDtypeStruct((m, n), res.dtype),
        compiler_params=_cparams("parallel", "parallel"),
        name=name,
    )(*([lhs] * n_pieces), *([rhs] * n_pieces), res)


def _ep_plain(accs, aux, res):
    return accs[0]


def _ep_add_res(accs, aux, res):
    return res + accs[0]


def _ep_gated_merge(accs, aux, res):
    tn = accs[1].shape[1]
    gates = _sigmoid(accs[0] + aux[0])
    return gates[:, :tn] * accs[1] + gates[:, tn:] * accs[2]


def _ep_relu2(accs, aux, res):
    r = jnp.maximum(accs[0], 0.0)
    return r * r


def _forget_bias_body(x_ref, o_ref, carry_ref, *, blk, nblk, heads):
    @pl.when(pl.program_id(1) == 0)
    def _():
        carry_ref[...] = jnp.zeros_like(carry_ref)

    r = lax.broadcasted_iota(jnp.int32, (blk, blk), 0)
    c = lax.broadcasted_iota(jnp.int32, (blk, blk), 1)
    tri = (c <= r).astype(F32)
    lane = lax.broadcasted_iota(jnp.int32, (blk, LANES), 1)

    def step(i, carry):
        base = pl.multiple_of(i * blk, blk)
        cs = jnp.dot(tri, x_ref[pl.ds(base, blk), :], preferred_element_type=F32,
                     precision=lax.Precision.HIGHEST) + carry
        for h in range(heads):
            bias = jnp.broadcast_to(cs[:, h:h + 1] * (-LOG2E), (blk, LANES))
            hi = bias.astype(BF16).astype(F32)
            mid = (bias - hi).astype(BF16).astype(F32)
            lo = bias - hi - mid
            pieces = jnp.where(lane == 0, hi, jnp.where(lane == 1, mid,
                                                        jnp.where(lane == 2, lo, 0.0)))
            o_ref[pl.ds(base, blk), h * LANES:(h + 1) * LANES] = pieces.astype(o_ref.dtype)
        return cs[blk - 1:blk, :]

    carry_ref[0:1, :] = lax.fori_loop(0, nblk, step, carry_ref[0:1, :])


def _forget_bias(log_f, bsz, seq, heads):
    blk = min(FORGET_BLOCK, seq)
    tt = min(TILE_MM, seq)
    nt = seq // tt
    return pl.pallas_call(
        functools.partial(_forget_bias_body, blk=blk, nblk=tt // blk, heads=heads),
        grid=(bsz, nt),
        in_specs=[pl.BlockSpec((tt, LANES), lambda b, t: (b * nt + t, 0))],
        out_specs=pl.BlockSpec((tt, heads * LANES), lambda b, t: (b * nt + t, 0)),
        out_shape=jax.ShapeDtypeStruct((bsz * seq, heads * LANES), BF16),
        scratch_shapes=[pltpu.VMEM((SUBLANES, LANES), F32)],
        compiler_params=_cparams("parallel", "arbitrary"),
        name="forget_bias",
    )(log_f)


def _fox_body(qa_ref, qb_ref, k_ref, v_ref, cp_ref, o_ref, kaug_ref, vaug_ref, s_ref,
              *, tq, seq):
    p = pl.program_id(2)
    nq = seq // tq
    dh = qa_ref.shape[1]

    @pl.when(p == 0)
    def _():
        def fill(j, _):
            rows = pl.ds(pl.multiple_of(j * tq, tq), tq)
            kaug_ref[rows, :dh] = k_ref[rows, :]
            kaug_ref[rows, dh:] = cp_ref[rows, :]
            vaug_ref[rows, :dh] = v_ref[rows, :]
            vaug_ref[rows, dh:] = jnp.ones((tq, dh), vaug_ref.dtype)
            return 0
        lax.fori_loop(0, nq, fill, 0)

    lane = lax.broadcasted_iota(jnp.int32, (tq, dh), 1)
    unit = (lane < N_BIAS_PIECES).astype(qa_ref.dtype)
    q_a = jnp.concatenate([qa_ref[...], unit], axis=1)
    q_b = jnp.concatenate([qb_ref[...], unit], axis=1)
    above_diag = (lax.broadcasted_iota(jnp.int32, (tq, tq), 1)
                  > lax.broadcasted_iota(jnp.int32, (tq, tq), 0))

    def key_rows(t):
        j = jnp.where(t <= p, t, t - (p + 1))
        return pl.ds(pl.multiple_of(j * tq, tq), tq)

    def scores(t):
        q = jnp.where(t <= p, q_a, q_b)
        s = lax.dot_general(q, kaug_ref[key_rows(t), :], (((1,), (1,)), ((), ())),
                            preferred_element_type=F32)
        if t == nq:
            s = jnp.where(above_diag, NEG_INF, s)
        elif t < nq // 2:
            s = jnp.where(jnp.logical_and(above_diag, t == p), NEG_INF, s)
        return s

    m = jnp.full((tq, 1), NEG_INF, F32)
    acc = jnp.zeros((tq, 2 * dh), F32)
    s_ref[0] = scores(0)
    for t in range(nq + 1):
        if t < nq:
            s_ref[(t + 1) % 2] = scores(t + 1)
        s = s_ref[t % 2]
        if 1 <= t <= nq // 2:
            first_b = t == p + 1
            m = jnp.where(first_b, NEG_INF, m)
            acc = jnp.where(first_b, 0.0, acc)
        m_new = jnp.maximum(m, jnp.max(s, axis=-1, keepdims=True))
        acc = jnp.exp2(m - m_new) * acc + jnp.dot(
            jnp.exp2(s - m_new).astype(vaug_ref.dtype), vaug_ref[key_rows(t), :],
            preferred_element_type=F32)
        m = m_new
        if t == 0:
            o_ref[:tq, :] = (acc[:, :dh] / acc[:, dh:]).astype(o_ref.dtype)
        elif t < nq // 2:
            out = (acc[:, :dh] / acc[:, dh:]).astype(o_ref.dtype)
            o_ref[:tq, :] = jnp.where(t <= p, out, o_ref[:tq, :])
    o_ref[tq:, :] = (acc[:, :dh] / acc[:, dh:]).astype(o_ref.dtype)


def _fox_attention(qkv, cp, bsz, seq, heads, tq):
    dh = ATTN_HEAD_DIM
    tq = min(tq, seq // 2)
    nq = seq // tq
    half = nq // 2
    assert nq % 2 == 0

    def position(i):
        b, qi = i // nq, i % nq
        return jnp.where(qi < half, (b * half + qi) * 2, (b * half + nq - 1 - qi) * 2 + 1)

    out = pl.pallas_call(
        functools.partial(_fox_body, tq=tq, seq=seq),
        grid=(bsz, heads, half),
        in_specs=[pl.BlockSpec((tq, dh), lambda b, h, p: (b * nq + p, h)),
                  pl.BlockSpec((tq, dh), lambda b, h, p: (b * nq + nq - 1 - p, h)),
                  pl.BlockSpec((seq, dh), lambda b, h, p: (b, heads + h)),
                  pl.BlockSpec((seq, dh), lambda b, h, p: (b, 2 * heads + h)),
                  pl.BlockSpec((seq, dh), lambda b, h, p: (b, h))],
        out_specs=pl.BlockSpec((2 * tq, dh), lambda b, h, p: (b * half + p, h)),
        out_shape=jax.ShapeDtypeStruct((bsz * seq, heads * dh), BF16),
        scratch_shapes=[pltpu.VMEM((seq, 2 * dh), BF16), pltpu.VMEM((seq, 2 * dh), BF16),
                        pltpu.VMEM((2, tq, tq), F32)],
        compiler_params=_cparams("parallel", "parallel", "arbitrary"),
        name="fox_attention",
    )(qkv, qkv, qkv, qkv, cp)
    return out, tq, position


def _xattn_block_body(x_ref, gx_ref, wq_ref, kv_ref, wo_ref, gn_ref, xo_ref, hn_ref):
    x = x_ref[...]
    width = wq_ref.shape[1]
    dh = width // XATTN_HEADS
    hx = _rms_scale(x, gx_ref[...]).astype(wq_ref.dtype)
    q = jnp.dot(hx, wq_ref[...], preferred_element_type=F32).astype(kv_ref.dtype)
    heads = []
    for hd in range(XATTN_HEADS):
        k = kv_ref[:, hd * dh:(hd + 1) * dh]
        v = kv_ref[:, width + hd * dh:width + (hd + 1) * dh]
        s = lax.dot_general(q[:, hd * dh:(hd + 1) * dh], k, (((1,), (1,)), ((), ())),
                            preferred_element_type=F32)
        p = jnp.exp(s - jnp.max(s, axis=-1, keepdims=True))
        l = jnp.sum(p, axis=-1, keepdims=True)
        o = jnp.dot(p.astype(v.dtype), v, preferred_element_type=F32) / l
        heads.append(o.astype(wo_ref.dtype))
    x2 = x + jnp.dot(jnp.concatenate(heads, axis=1), wo_ref[...], preferred_element_type=F32)
    xo_ref[...] = x2
    hn_ref[...] = _rms_scale(x2, gn_ref[...]).astype(hn_ref.dtype)


def _xattn_block(x, g_x, wq, kv, wo, g_next, seq, mem_len, tm):
    t, d = x.shape
    width = wq.shape[1]
    tm = min(tm, seq)
    nq = seq // tm
    row = pl.BlockSpec((tm, d), lambda i: (i, 0))
    vec = pl.BlockSpec((1, d), lambda i: (0, 0))
    resident = pl.Buffered(1)
    return pl.pallas_call(
        _xattn_block_body,
        grid=(t // tm,),
        in_specs=[row, vec,
                  pl.BlockSpec((d, width), lambda i: (0, 0), pipeline_mode=resident),
                  pl.BlockSpec((mem_len, 2 * width), lambda i: (i // nq, 0)),
                  pl.BlockSpec((width, d), lambda i: (0, 0), pipeline_mode=resident),
                  vec],
        out_specs=[row, row],
        out_shape=[jax.ShapeDtypeStruct((t, d), F32), jax.ShapeDtypeStruct((t, d), BF16)],
        compiler_params=_cparams("parallel"),
        name="xattn_block",
    )(x, g_x.reshape(1, d).astype(F32), wq, kv, wo, g_next.reshape(1, d).astype(F32))


def _s5_operators_body(bre_ref, bim_ref, cre_ref, cim_ref, p_ref, pt_ref, d_ref,
                        t_ref, w_ref, v_ref, *, L):
    hp = lax.Precision.HIGHEST
    bre, bim = bre_ref[...], bim_ref[...]
    cre, cim = cre_ref[...], cim_ref[...]
    k = bre.shape[1]
    row = lax.broadcasted_iota(jnp.int32, (LANES, LANES), 0)
    col = lax.broadcasted_iota(jnp.int32, (LANES, LANES), 1)
    lag_blocks = []
    for tau in range(L):
        pr, pi = p_ref[0, tau:tau + 1, :], p_ref[1, tau:tau + 1, :]
        xr, xi = bre * pr - bim * pi, bre * pi + bim * pr
        rows = slice((L - 1 - tau) * LANES, (L - tau) * LANES)
        w_ref[rows, :k] = xr.astype(w_ref.dtype)
        w_ref[rows, k:] = xi.astype(w_ref.dtype)
        d_tau = (jnp.dot(xr, cre, preferred_element_type=F32, precision=hp)
                 - jnp.dot(xi, cim, preferred_element_type=F32, precision=hp))
        if tau == 0:
            d_tau = d_tau + jnp.where(row == col, d_ref[...], 0.0)
        lag_blocks.append(d_tau.astype(t_ref.dtype))
    zero = jnp.zeros((LANES, LANES), t_ref.dtype)
    for s in range(L):
        for t in range(L):
            t_ref[s * LANES:(s + 1) * LANES, t * LANES:(t + 1) * LANES] = (
                lag_blocks[t - s] if t >= s else zero)
    for t in range(L):
        qr, qi = pt_ref[0, :, t + 1:t + 2], pt_ref[1, :, t + 1:t + 2]
        cols = slice(t * LANES, (t + 1) * LANES)
        v_ref[:k, cols] = (cre * qr - cim * qi).astype(v_ref.dtype)
        v_ref[k:, cols] = (-(cre * qi + cim * qr)).astype(v_ref.dtype)


def _s5_operators(A_re, A_im, log_dt, B_re, B_im, C_re, C_im, D_skip):
    n_g, n_p = A_re.shape
    n_h = B_re.shape[-1]
    L = S5_CHUNK
    gs = LANES // n_h
    n_slab = n_g // gs
    k = gs * n_p
    a_re, a_im = A_re.astype(F32), A_im.astype(F32)
    dt = jnp.exp(log_dt.astype(F32))[:, None]
    tau = jnp.arange(L + 1, dtype=F32)[:, None, None]
    mag = jnp.exp(tau * (dt * a_re))
    ang = tau * (dt * a_im)
    pw = jnp.stack([mag * jnp.cos(ang), mag * jnp.sin(ang)])
    lb_re, lb_im = pw[0, 1], pw[1, 1]
    den = a_re * a_re + a_im * a_im
    nr, ni = lb_re - 1.0, lb_im
    f_re = (nr * a_re + ni * a_im) / den
    f_im = (ni * a_re - nr * a_im) / den
    br, bi = B_re.astype(F32), B_im.astype(F32)
    bb_re = f_re[..., None] * br - f_im[..., None] * bi
    bb_im = f_re[..., None] * bi + f_im[..., None] * br

    eye = jnp.eye(gs, dtype=F32)

    def slab_b(x):
        x = x.reshape(n_slab, gs, n_p, n_h).transpose(0, 1, 3, 2)
        return (x[:, :, :, None, :] * eye[None, :, None, :, None]).reshape(n_slab, LANES, k)

    def slab_c(x):
        x = x.reshape(n_slab, gs, n_h, n_p).transpose(0, 1, 3, 2)
        return (x[:, :, :, None, :] * eye[None, :, None, :, None]).reshape(n_slab, k, LANES)

    p_tab = pw.reshape(2, L + 1, n_slab, k).transpose(2, 0, 1, 3)
    pt_tab = jnp.pad(pw.reshape(2, L + 1, n_slab, k).transpose(2, 0, 3, 1),
                     ((0, 0), (0, 0), (0, 0), (0, LANES - (L + 1))))
    mat_b = pl.BlockSpec((None, LANES, k), lambda j: (j, 0, 0))
    mat_c = pl.BlockSpec((None, k, LANES), lambda j: (j, 0, 0))
    t_op, w_op, v_op = pl.pallas_call(
        functools.partial(_s5_operators_body, L=L),
        grid=(n_slab,),
        in_specs=[mat_b, mat_b, mat_c, mat_c,
                  pl.BlockSpec((None, 2, L + 1, k), lambda j: (j, 0, 0, 0)),
                  pl.BlockSpec((None, 2, k, LANES), lambda j: (j, 0, 0, 0)),
                  pl.BlockSpec((1, LANES), lambda j: (0, j))],
        out_specs=[pl.BlockSpec((None, L * LANES, L * LANES), lambda j: (j, 0, 0)),
                   pl.BlockSpec((None, L * LANES, 2 * k), lambda j: (j, 0, 0)),
                   pl.BlockSpec((None, 2 * k, L * LANES), lambda j: (j, 0, 0))],
        out_shape=[jax.ShapeDtypeStruct((n_slab, L * LANES, L * LANES), BF16),
                   jax.ShapeDtypeStruct((n_slab, L * LANES, 2 * k), BF16),
                   jax.ShapeDtypeStruct((n_slab, 2 * k, L * LANES), BF16)],
        compiler_params=_cparams("parallel"),
        name="s5_operators",
    )(slab_b(bb_re), slab_b(bb_im), slab_c(C_re.astype(F32)), slab_c(C_im.astype(F32)),
      p_tab, pt_tab, D_skip.astype(F32).reshape(1, n_g * n_h))

    a_op = pw[:, L].reshape(2, n_slab, k).transpose(1, 0, 2).reshape(1, n_slab * 2 * k)
    return t_op, w_op, v_op, a_op


def _chunk_rows(piece_refs):
    return jnp.concatenate([r[...] for r in piece_refs], axis=1)


def _s5_increment_body(*refs):
    w_ref, z_ref = refs[-2], refs[-1]
    z_ref[...] = jnp.dot(_chunk_rows(refs[:-2]), w_ref[...], preferred_element_type=F32)


def _s5_scan_body(z_ref, a_ref, o_ref, st_ref, *, tc):
    @pl.when(pl.program_id(2) == 0)
    def _():
        st_ref[...] = jnp.zeros_like(st_ref)

    half = z_ref.shape[1] // 2
    a_re, a_im = a_ref[:, :half], a_ref[:, half:]

    def step(i, carry):
        re, im = carry
        base = pl.multiple_of(i * SUBLANES, SUBLANES)
        inc = z_ref[pl.ds(base, SUBLANES), :]
        before_re, before_im = [], []
        for r in range(SUBLANES):
            before_re.append(re)
            before_im.append(im)
            re, im = (a_re * re - a_im * im + inc[r:r + 1, :half],
                      a_re * im + a_im * re + inc[r:r + 1, half:])
        o_ref[pl.ds(base, SUBLANES), :half] = jnp.concatenate(before_re, axis=0)
        o_ref[pl.ds(base, SUBLANES), half:] = jnp.concatenate(before_im, axis=0)
        return re, im

    re, im = lax.fori_loop(0, tc // SUBLANES, step, (st_ref[0:1, :half], st_ref[0:1, half:]))
    st_ref[0:1, :half] = re
    st_ref[0:1, half:] = im


def _s5_output_body(*refs):
    x_ref, t_ref, v_ref, o_ref = refs[-4:]
    y = jnp.dot(_chunk_rows(refs[:-4]), t_ref[...], preferred_element_type=F32)
    y = y + jnp.dot(x_ref[...].astype(BF16), v_ref[...], preferred_element_type=F32)
    o_ref[...] = jax.nn.gelu(y).astype(o_ref.dtype)


def _s5_glu_body(*refs):
    w_ref, b_ref, o_ref = refs[-3:]
    y = _chunk_rows(refs[:-3])
    gate = _sigmoid(jnp.dot(y, w_ref[...], preferred_element_type=F32) + b_ref[...])
    o_ref[...] = (y.astype(F32) * gate).astype(o_ref.dtype)


def _s5_branch(u, ops, w_glu, b_glu, bsz, seq):
    t_op, w_op, v_op, a_op = ops
    n_slab = t_op.shape[0]
    L = S5_CHUNK
    width = u.shape[1]
    cw, sw = t_op.shape[1], w_op.shape[2]
    nc = bsz * seq // L
    ncb = seq // L
    uu = u.reshape(nc, L * width)
    tm = min(TILE_ROWS, nc)
    pieces = [pl.BlockSpec((tm, LANES), lambda j, i, s=s: (i, s * n_slab + j)) for s in range(L)]

    z = pl.pallas_call(
        _s5_increment_body,
        grid=(n_slab, nc // tm),
        in_specs=pieces + [pl.BlockSpec((None, cw, sw), lambda j, i: (j, 0, 0))],
        out_specs=pl.BlockSpec((tm, sw), lambda j, i: (i, j)),
        out_shape=jax.ShapeDtypeStruct((nc, n_slab * sw), F32),
        compiler_params=_cparams("parallel", "parallel"),
        name="s5_increment",
    )(*([uu] * L), w_op)

    tc = min(S5_SCAN_ROWS, ncb)
    nt = ncb // tc
    blk = pl.BlockSpec((tc, sw), lambda b, j, t: (b * nt + t, j))
    xprev = pl.pallas_call(
        functools.partial(_s5_scan_body, tc=tc),
        grid=(bsz, n_slab, nt),
        in_specs=[blk, pl.BlockSpec((1, sw), lambda b, j, t: (0, j))],
        out_specs=blk,
        out_shape=jax.ShapeDtypeStruct((nc, n_slab * sw), F32),
        scratch_shapes=[pltpu.VMEM((SUBLANES, sw), F32)],
        compiler_params=_cparams("parallel", "parallel", "arbitrary"),
        name="s5_scan",
    )(z, a_op)

    ys = pl.pallas_call(
        _s5_output_body,
        grid=(n_slab, nc // tm),
        in_specs=pieces + [pl.BlockSpec((tm, sw), lambda j, i: (i, j)),
                           pl.BlockSpec((None, cw, cw), lambda j, i: (j, 0, 0)),
                           pl.BlockSpec((None, sw, cw), lambda j, i: (j, 0, 0))],
        out_specs=pl.BlockSpec((tm, cw), lambda j, i: (i, j)),
        out_shape=jax.ShapeDtypeStruct((nc, n_slab * cw), BF16),
        compiler_params=_cparams("parallel", "parallel"),
        name="s5_output",
    )(*([uu] * L), xprev, t_op, v_op)

    tok = [pl.BlockSpec((tm, LANES), lambda i, t, j=j: (i, j * L + t)) for j in range(n_slab)]
    out = pl.pallas_call(
        _s5_glu_body,
        grid=(nc // tm, L),
        in_specs=tok + [pl.BlockSpec((width, width), lambda i, t: (0, 0)),
                        pl.BlockSpec((1, width), lambda i, t: (0, 0))],
        out_specs=pl.BlockSpec((tm, width), lambda i, t: (i, t)),
        out_shape=jax.ShapeDtypeStruct((nc, L * width), BF16),
        compiler_params=_cparams("parallel", "parallel"),
        name="s5_glu",
    )(*([ys] * n_slab), w_glu.astype(BF16), b_glu.reshape(1, width).astype(F32))
    return out.reshape(bsz * seq, width)


def _layer(x, mem_n, p, bsz, seq, mem_len):
    d = x.shape[1]
    aw = p["w_attn_up"].shape[0]
    heads = aw // ATTN_HEAD_DIM
    sw = p["w_ssm_up"].shape[0]
    off_f = 3 * aw
    off_u = off_f + heads
    off_g = off_u + sw

    w_in_t = p["w_in"].T
    tile = TILE_GATE_COLS
    w_qkv = _transpose_cast(w_in_t, 0, off_f, 2 * tile, lambda j: j, tk=d,
                            scale=LOG2E * ATTN_HEAD_DIM ** -0.5, n_scaled=aw // (2 * tile))
    w_f = jnp.pad(w_in_t[off_f:off_u].T, ((0, 0), (0, LANES - heads))).astype(BF16)
    b_f = jnp.pad(p["b_f"].astype(F32), (0, LANES - heads)).reshape(1, LANES)
    h, log_f = _rmsnorm(x, p["g_mix"], BF16, tm=TILE_ROWS, forget=(w_f, b_f))
    w_u = _transpose_cast(w_in_t, off_u, sw, tile, lambda j: j, tk=d)
    tn_mix = tile
    n_mix = d // tn_mix
    w_g = _transpose_cast(w_in_t, off_g, N_BRANCH * d, tn_mix,
                          lambda j: (j % n_mix) * N_BRANCH + j // n_mix, tk=d)

    qkv = _fused_matmul([(h, w_qkv, 0)], _ep_plain, BF16, n=3 * aw, tm=TILE_MM, tn=TILE_MM,
                        name="qkv_proj")
    fox, fox_rows, fox_position = _fox_attention(qkv, _forget_bias(log_f, bsz, seq, heads),
                                                 bsz, seq, heads, tq=TILE_MM)

    u, w_out = _fused_matmul([(h, w_u, 0)], _ep_plain, BF16, n=sw, tm=TILE_MM, tn=TILE_MM,
                             side_cast=p["w_out"], name="ssm_in_proj")
    ops = _s5_operators(p["A_re"], p["A_im"], p["log_dt"], p["B_re"], p["B_im"],
                        p["C_re"], p["C_im"], p["D_skip"])
    y = _s5_branch(u, ops, p["w_glu"], p["b_glu"], bsz, seq)

    b_gate = (p["b_gate"].astype(F32).reshape(N_BRANCH, n_mix, tn_mix).transpose(1, 0, 2)
              .reshape(1, N_BRANCH * d))
    merged, w_ff1 = _fused_matmul(
        [(h, w_g, 0, N_BRANCH), (fox, p["w_attn_up"].astype(BF16), 0, 1, fox_position),
         (y, p["w_ssm_up"].astype(BF16), 0)],
        _ep_gated_merge, BF16, n=d, tm=fox_rows, tn=tn_mix,
        aux=[(b_gate, 0, N_BRANCH)], side_cast=p["w_ff1"], name="gated_merge")
    x = _fused_matmul([(merged, w_out, 0)], _ep_add_res, F32, n=d,
                      tm=TILE_MM, tn=TILE_MM, res=x, name="mixer_out_proj")

    xw = p["wq_x"].shape[1]
    wq = (p["wq_x"] * (xw // XATTN_HEADS) ** -0.5).astype(BF16)
    w_kv = jnp.concatenate([p["wk_x"], p["wv_x"]], axis=1).astype(BF16)
    kv = _fused_matmul([(mem_n, w_kv, 0)], _ep_plain, BF16, n=2 * xw, tm=TILE_ROWS, tn=TILE_MM,
                       name="xattn_kv_proj")
    x, hm = _xattn_block(x, p["g_xattn"], wq, kv, p["wo_x"].astype(BF16), p["g_mlp"],
                         seq, mem_len, tm=TILE_XATTN_ROWS)

    dff = p["w_ff1"].shape[1]
    hid, w_ff2 = _fused_matmul([(hm, w_ff1, 0)], _ep_relu2, BF16, n=dff, tm=TILE_MM, tn=TILE_MM,
                               side_cast=p["w_ff2"], name="mlp_up")
    x = _matmul_kpieces(hid, w_ff2, x, n_pieces=MLP_DOWN_K_PIECES, tm=TILE_ROWS,
                        tn=MLP_DOWN_COLS, name="mlp_down")
    return x


_LAYER_PARAMS = ("g_mix", "w_in", "b_f", "b_gate", "A_re", "A_im", "log_dt", "B_re", "B_im",
                 "C_re", "C_im", "D_skip", "w_glu", "b_glu", "w_attn_up", "w_ssm_up", "w_out",
                 "g_xattn", "g_mem", "wq_x", "wk_x", "wv_x", "wo_x", "g_mlp", "w_ff1", "w_ff2")


def kernel(x, mem, g_mix, w_in, b_f, b_gate, A_re, A_im, log_dt, B_re, B_im, C_re, C_im, D_skip, w_glu, b_glu, w_attn_up, w_ssm_up, w_out, g_xattn, g_mem, wq_x, wk_x, wv_x, wo_x, g_mlp, w_ff1, w_ff2, g_final):
    stacked = dict(zip(_LAYER_PARAMS, (g_mix, w_in, b_f, b_gate, A_re, A_im, log_dt, B_re, B_im,
                                       C_re, C_im, D_skip, w_glu, b_glu, w_attn_up, w_ssm_up,
                                       w_out, g_xattn, g_mem, wq_x, wk_x, wv_x, wo_x, g_mlp,
                                       w_ff1, w_ff2)))
    bsz, seq, d = x.shape
    mem_len = mem.shape[1]
    xt = x.reshape(bsz * seq, d)
    mem2 = mem.reshape(bsz * mem_len, d)
    for l in range(g_mix.shape[0]):
        p = {k: v[l] for k, v in stacked.items()}
        mem_n = _rmsnorm(mem2, p["g_mem"], BF16, tm=TILE_XATTN_ROWS)
        xt = _layer(xt, mem_n, p, bsz, seq, mem_len)
    out = _rmsnorm(xt, g_final, x.dtype, tm=TILE_ROWS)
    return out.reshape(bsz, seq, d)
```

```python
import functools
import math

import jax
import jax.numpy as jnp
from jax import lax
from jax.experimental import pallas as pl
from jax.experimental.pallas import tpu as pltpu

F32 = jnp.float32
BF16 = jnp.bfloat16

V7X_VMEM_LIMIT_BYTES = 56 * 1024 * 1024
LANES = 128
SUBLANES = 8
BF16_SUBLANES = 16

EPS = 1e-6
NEG_INF = -1e30
LOG2E = math.log2(math.e)
N_BIAS_PIECES = 3
ATTN_HEAD_DIM = 128
S5_CHUNK = 16
XATTN_HEADS = 4
N_BRANCH = 2

TILE_MM = 1024
TILE_GATE_COLS = 256
TILE_ROWS = 512
TILE_XATTN_ROWS = 256
MLP_DOWN_COLS = 256
MLP_DOWN_K_PIECES = 4
FORGET_BLOCK = 256
S5_SCAN_ROWS = 128


def _cparams(*sem):
    return pltpu.CompilerParams(dimension_semantics=sem,
                                vmem_limit_bytes=V7X_VMEM_LIMIT_BYTES)


def _rms_scale(x, g):
    return (x * lax.rsqrt(jnp.mean(x * x, axis=-1, keepdims=True) + EPS)) * g


def _log_sigmoid(z):
    return jnp.minimum(z, 0.0) - jnp.log1p(jnp.exp(-jnp.abs(z)))


def _sigmoid(z):
    return 0.5 * jnp.tanh(0.5 * z) + 0.5


def _rmsnorm_body(x_ref, g_ref, *rest, has_proj):
    h = _rms_scale(x_ref[...].astype(F32), g_ref[...]).astype(rest[-1 - has_proj].dtype)
    if has_proj:
        w_ref, b_ref, o_ref, f_ref = rest
        f_ref[...] = _log_sigmoid(jnp.dot(h, w_ref[...], preferred_element_type=F32) + b_ref[...])
    else:
        o_ref, = rest
    o_ref[...] = h


def _rmsnorm(x, g, out_dtype, tm, forget=None):
    m, d = x.shape
    tm = min(tm, m)
    row = pl.BlockSpec((tm, d), lambda i: (i, 0))
    operands = [x, g.reshape(1, d).astype(F32)]
    in_specs = [row, pl.BlockSpec((1, d), lambda i: (0, 0))]
    out_specs, out_shape = [row], [jax.ShapeDtypeStruct((m, d), out_dtype)]
    if forget is not None:
        n = forget[0].shape[1]
        operands += list(forget)
        in_specs += [pl.BlockSpec((d, n), lambda i: (0, 0)), pl.BlockSpec((1, n), lambda i: (0, 0))]
        out_specs.append(pl.BlockSpec((tm, n), lambda i: (i, 0)))
        out_shape.append(jax.ShapeDtypeStruct((m, n), F32))
    outs = pl.pallas_call(
        functools.partial(_rmsnorm_body, has_proj=forget is not None),
        grid=(m // tm,),
        in_specs=in_specs,
        out_specs=out_specs,
        out_shape=out_shape,
        compiler_params=_cparams("parallel"),
        name="rmsnorm",
    )(*operands)
    return outs if forget is not None else outs[0]


def _mm_body(*refs, n_pairs, n_aux, has_res, has_side, epilogue):
    accs = [jnp.dot(refs[2 * p][...], refs[2 * p + 1][...], preferred_element_type=F32)
            for p in range(n_pairs)]
    pos = 2 * n_pairs
    aux = [refs[pos + i][...] for i in range(n_aux)]
    pos += n_aux
    res = refs[pos][...].astype(F32) if has_res else None
    pos += has_res
    if has_side:
        refs[-1][...] = refs[pos][...].astype(refs[-1].dtype)
        pos += 1
    o_ref = refs[pos]
    o_ref[...] = epilogue(accs, aux, res).astype(o_ref.dtype)


def _fused_matmul(pairs, epilogue, out_dtype, *, n, tm, tn, aux=(), res=None, side_cast=None,
                  name="fused_matmul"):
    m = pairs[0][0].shape[0]
    tm, tn = min(tm, m), min(tn, n)
    n_aux, has_res, has_side = len(aux), res is not None, side_cast is not None
    nj = n // tn
    operands, in_specs = [], []
    for lhs, rhs, off, *opt in pairs:
        kp = lhs.shape[1]
        width = tn * (opt[0] if opt else 1)
        row_block = opt[1] if len(opt) > 1 else (lambda i: i)
        operands += [lhs, rhs]
        in_specs += [pl.BlockSpec((tm, kp), lambda i, j, rb=row_block: (rb(i), 0)),
                     pl.BlockSpec((kp, width), lambda i, j, off=off: (0, j + off))]
    for vec, off, *mult in aux:
        width = tn * (mult[0] if mult else 1)
        operands.append(vec)
        in_specs.append(pl.BlockSpec((1, width), lambda i, j, off=off: (0, j + off)))
    if has_res:
        operands.append(res)
        in_specs.append(pl.BlockSpec((tm, tn), lambda i, j: (i, j)))
    out_specs = [pl.BlockSpec((tm, tn), lambda i, j: (i, j))]
    out_shape = [jax.ShapeDtypeStruct((m, n), out_dtype)]
    if has_side:
        slab = side_cast.shape[0] // ((m // tm) * nj)
        assert slab * (m // tm) * nj == side_cast.shape[0] and slab % BF16_SUBLANES == 0
        side_spec = pl.BlockSpec((slab, side_cast.shape[1]), lambda i, j: (i * nj + j, 0))
        operands.append(side_cast)
        in_specs.append(side_spec)
        out_specs.append(side_spec)
        out_shape.append(jax.ShapeDtypeStruct(side_cast.shape, BF16))
    outs = pl.pallas_call(
        functools.partial(_mm_body, n_pairs=len(pairs), n_aux=n_aux, has_res=has_res,
                          has_side=has_side, epilogue=epilogue),
        grid=(m // tm, nj),
        in_specs=in_specs,
        out_specs=out_specs,
        out_shape=out_shape,
        compiler_params=_cparams("parallel", "parallel"),
        name=name,
    )(*operands)
    return outs if has_side else outs[0]


W_IN_EDGE_ROWS = 16


def _transpose_cast_body(a_ref, b_ref, o_ref, *, shift, scale, n_scaled):
    rows = a_ref.shape[0]
    x = jnp.concatenate([a_ref[...], b_ref[...]], axis=0)[shift:shift + rows, :]
    if n_scaled:
        x = x * jnp.where(pl.program_id(0) < n_scaled, scale, 1.0)
    o_ref[...] = x.T.astype(o_ref.dtype)


def _transpose_cast(wt, start, n_cols, tile, out_block, tk, scale=1.0, n_scaled=0):
    k = wt.shape[1]
    base = start // tile * tile
    shift = start - base
    assert shift % SUBLANES == 0 and shift <= W_IN_EDGE_ROWS and n_cols % tile == 0
    assert start + n_cols + (W_IN_EDGE_ROWS - shift) <= wt.shape[0] or shift == 0
    edge_per_tile = tile // W_IN_EDGE_ROWS
    last_edge = wt.shape[0] // W_IN_EDGE_ROWS - 1
    return pl.pallas_call(
        functools.partial(_transpose_cast_body, shift=shift, scale=scale, n_scaled=n_scaled),
        grid=(n_cols // tile, k // tk),
        in_specs=[pl.BlockSpec((tile, tk), lambda j, c: (base // tile + j, c)),
                  pl.BlockSpec((W_IN_EDGE_ROWS, tk),
                               lambda j, c: (jnp.minimum((base // tile + j + 1) * edge_per_tile,
                                                         last_edge), c))],
        out_specs=pl.BlockSpec((tk, tile), lambda j, c: (c, out_block(j))),
        out_shape=jax.ShapeDtypeStruct((k, n_cols), BF16),
        compiler_params=_cparams("parallel", "parallel"),
        name="transpose_cast",
    )(wt, wt)


def _matmul_kpieces_body(*refs, n_pieces):
    res_ref, o_ref = refs[2 * n_pieces], refs[2 * n_pieces + 1]
    acc = res_ref[...].astype(F32)
    for p in range(n_pieces):
        acc = acc + jnp.dot(refs[p][...], refs[n_pieces + p][...], preferred_element_type=F32)
    o_ref[...] = acc.astype(o_ref.dtype)


def _matmul_kpieces(lhs, rhs, res, *, n_pieces, tm, tn, name):
    m, k = lhs.shape
    n = rhs.shape[1]
    kp = k // n_pieces
    tm, tn = min(tm, m), min(tn, n)
    in_specs = ([pl.BlockSpec((tm, kp), lambda i, j, p=p: (i, p)) for p in range(n_pieces)]
                + [pl.BlockSpec((kp, tn), lambda i, j, p=p: (p, j)) for p in range(n_pieces)]
                + [pl.BlockSpec((tm, tn), lambda i, j: (i, j))])
    return pl.pallas_call(
        functools.partial(_matmul_kpieces_body, n_pieces=n_pieces),
        grid=(m // tm, n // tn),
        in_specs=in_specs,
        out_specs=pl.BlockSpec((tm, tn), lambda i, j: (i, j)),
        out_shape=jax.ShapeDtypeStruct((m, n), res.dtype),
        compiler_params=_cparams("parallel", "parallel"),
        name=name,
    )(*([lhs] * n_pieces), *([rhs] * n_pieces), res)


def _ep_plain(accs, aux, res):
    return accs[0]


def _ep_add_res(accs, aux, res):
    return res + accs[0]


def _ep_gated_merge(accs, aux, res):
    tn = accs[1].shape[1]
    gates = _sigmoid(accs[0] + aux[0])
    return gates[:, :tn] * accs[1] + gates[:, tn:] * accs[2]


def _ep_relu2(accs, aux, res):
    r = jnp.maximum(accs[0], 0.0)
    return r * r


def _forget_bias_body(x_ref, o_ref, carry_ref, *, blk, nblk, heads):
    @pl.when(pl.program_id(1) == 0)
    def _():
        carry_ref[...] = jnp.zeros_like(carry_ref)

    r = lax.broadcasted_iota(jnp.int32, (blk, blk), 0)
    c = lax.broadcasted_iota(jnp.int32, (blk, blk), 1)
    tri = (c <= r).astype(F32)
    lane = lax.broadcasted_iota(jnp.int32, (blk, LANES), 1)

    def step(i, carry):
        base = pl.multiple_of(i * blk, blk)
        cs = jnp.dot(tri, x_ref[pl.ds(base, blk), :], preferred_element_type=F32,
                     precision=lax.Precision.HIGHEST) + carry
        for h in range(heads):
            bias = jnp.broadcast_to(cs[:, h:h + 1] * (-LOG2E), (blk, LANES))
            hi = bias.astype(BF16).astype(F32)
            mid = (bias - hi).astype(BF16).astype(F32)
            lo = bias - hi - mid
            pieces = jnp.where(lane == 0, hi, jnp.where(lane == 1, mid,
                                                        jnp.where(lane == 2, lo, 0.0)))
            o_ref[pl.ds(base, blk), h * LANES:(h + 1) * LANES] = pieces.astype(o_ref.dtype)
        return cs[blk - 1:blk, :]

    carry_ref[0:1, :] = lax.fori_loop(0, nblk, step, carry_ref[0:1, :])


def _forget_bias(log_f, bsz, seq, heads):
    blk = min(FORGET_BLOCK, seq)
    tt = min(TILE_MM, seq)
    nt = seq // tt
    return pl.pallas_call(
        functools.partial(_forget_bias_body, blk=blk, nblk=tt // blk, heads=heads),
        grid=(bsz, nt),
        in_specs=[pl.BlockSpec((tt, LANES), lambda b, t: (b * nt + t, 0))],
        out_specs=pl.BlockSpec((tt, heads * LANES), lambda b, t: (b * nt + t, 0)),
        out_shape=jax.ShapeDtypeStruct((bsz * seq, heads * LANES), BF16),
        scratch_shapes=[pltpu.VMEM((SUBLANES, LANES), F32)],
        compiler_params=_cparams("parallel", "arbitrary"),
        name="forget_bias",
    )(log_f)


def _fox_body(qa_ref, qb_ref, k_ref, v_ref, cp_ref, o_ref, kaug_ref, vaug_ref, s_ref,
              *, tq, seq):
    p = pl.program_id(2)
    nq = seq // tq
    dh = qa_ref.shape[1]

    @pl.when(p == 0)
    def _():
        def fill(j, _):
            rows = pl.ds(pl.multiple_of(j * tq, tq), tq)
            kaug_ref[rows, :dh] = k_ref[rows, :]
            kaug_ref[rows, dh:] = cp_ref[rows, :]
            vaug_ref[rows, :dh] = v_ref[rows, :]
            vaug_ref[rows, dh:] = jnp.ones((tq, dh), vaug_ref.dtype)
            return 0
        lax.fori_loop(0, nq, fill, 0)

    lane = lax.broadcasted_iota(jnp.int32, (tq, dh), 1)
    unit = (lane < N_BIAS_PIECES).astype(qa_ref.dtype)
    q_a = jnp.concatenate([qa_ref[...], unit], axis=1)
    q_b = jnp.concatenate([qb_ref[...], unit], axis=1)
    above_diag = (lax.broadcasted_iota(jnp.int32, (tq, tq), 1)
                  > lax.broadcasted_iota(jnp.int32, (tq, tq), 0))

    def key_rows(t):
        j = jnp.where(t <= p, t, t - (p + 1))
        return pl.ds(pl.multiple_of(j * tq, tq), tq)

    def scores(t):
        q = jnp.where(t <= p, q_a, q_b)
        s = lax.dot_general(q, kaug_ref[key_rows(t), :], (((1,), (1,)), ((), ())),
                            preferred_element_type=F32)
        if t == nq:
            s = jnp.where(above_diag, NEG_INF, s)
        elif t < nq // 2:
            s = jnp.where(jnp.logical_and(above_diag, t == p), NEG_INF, s)
        return s

    m = jnp.full((tq, 1), NEG_INF, F32)
    acc = jnp.zeros((tq, 2 * dh), F32)
    s_ref[0] = scores(0)
    for t in range(nq + 1):
        if t < nq:
            s_ref[(t + 1) % 2] = scores(t + 1)
        s = s_ref[t % 2]
        if 1 <= t <= nq // 2:
            first_b = t == p + 1
            m = jnp.where(first_b, NEG_INF, m)
            acc = jnp.where(first_b, 0.0, acc)
        m_new = jnp.maximum(m, jnp.max(s, axis=-1, keepdims=True))
        acc = jnp.exp2(m - m_new) * acc + jnp.dot(
            jnp.exp2(s - m_new).astype(vaug_ref.dtype), vaug_ref[key_rows(t), :],
            preferred_element_type=F32)
        m = m_new
        if t == 0:
            o_ref[:tq, :] = (acc[:, :dh] / acc[:, dh:]).astype(o_ref.dtype)
        elif t < nq // 2:
            out = (acc[:, :dh] / acc[:, dh:]).astype(o_ref.dtype)
            o_ref[:tq, :] = jnp.where(t <= p, out, o_ref[:tq, :])
    o_ref[tq:, :] = (acc[:, :dh] / acc[:, dh:]).astype(o_ref.dtype)


def _fox_attention(qkv, cp, bsz, seq, heads, tq):
    dh = ATTN_HEAD_DIM
    tq = min(tq, seq // 2)
    nq = seq // tq
    half = nq // 2
    assert nq % 2 == 0

    def position(i):
        b, qi = i // nq, i % nq
        return jnp.where(qi < half, (b * half + qi) * 2, (b * half + nq - 1 - qi) * 2 + 1)

    out = pl.pallas_call(
        functools.partial(_fox_body, tq=tq, seq=seq),
        grid=(bsz, heads, half),
        in_specs=[pl.BlockSpec((tq, dh), lambda b, h, p: (b * nq + p, h)),
                  pl.BlockSpec((tq, dh), lambda b, h, p: (b * nq + nq - 1 - p, h)),
                  pl.BlockSpec((seq, dh), lambda b, h, p: (b, heads + h)),
                  pl.BlockSpec((seq, dh), lambda b, h, p: (b, 2 * heads + h)),
                  pl.BlockSpec((seq, dh), lambda b, h, p: (b, h))],
        out_specs=pl.BlockSpec((2 * tq, dh), lambda b, h, p: (b * half + p, h)),
        out_shape=jax.ShapeDtypeStruct((bsz * seq, heads * dh), BF16),
        scratch_shapes=[pltpu.VMEM((seq, 2 * dh), BF16), pltpu.VMEM((seq, 2 * dh), BF16),
                        pltpu.VMEM((2, tq, tq), F32)],
        compiler_params=_cparams("parallel", "parallel", "arbitrary"),
        name="fox_attention",
    )(qkv, qkv, qkv, qkv, cp)
    return out, tq, position


def _xattn_block_body(x_ref, gx_ref, wq_ref, kv_ref, wo_ref, gn_ref, xo_ref, hn_ref):
    x = x_ref[...]
    width = wq_ref.shape[1]
    dh = width // XATTN_HEADS
    hx = _rms_scale(x, gx_ref[...]).astype(wq_ref.dtype)
    q = jnp.dot(hx, wq_ref[...], preferred_element_type=F32).astype(kv_ref.dtype)
    heads = []
    for hd in range(XATTN_HEADS):
        k = kv_ref[:, hd * dh:(hd + 1) * dh]
        v = kv_ref[:, width + hd * dh:width + (hd + 1) * dh]
        s = lax.dot_general(q[:, hd * dh:(hd + 1) * dh], k, (((1,), (1,)), ((), ())),
                            preferred_element_type=F32)
        p = jnp.exp(s - jnp.max(s, axis=-1, keepdims=True))
        l = jnp.sum(p, axis=-1, keepdims=True)
        o = jnp.dot(p.astype(v.dtype), v, preferred_element_type=F32) / l
        heads.append(o.astype(wo_ref.dtype))
    x2 = x + jnp.dot(jnp.concatenate(heads, axis=1), wo_ref[...], preferred_element_type=F32)
    xo_ref[...] = x2
    hn_ref[...] = _rms_scale(x2, gn_ref[...]).astype(hn_ref.dtype)


def _xattn_block(x, g_x, wq, kv, wo, g_next, seq, mem_len, tm):
    t, d = x.shape
    width = wq.shape[1]
    tm = min(tm, seq)
    nq = seq // tm
    row = pl.BlockSpec((tm, d), lambda i: (i, 0))
    vec = pl.BlockSpec((1, d), lambda i: (0, 0))
    resident = pl.Buffered(1)
    return pl.pallas_call(
        _xattn_block_body,
        grid=(t // tm,),
        in_specs=[row, vec,
                  pl.BlockSpec((d, width), lambda i: (0, 0), pipeline_mode=resident),
                  pl.BlockSpec((mem_len, 2 * width), lambda i: (i // nq, 0)),
                  pl.BlockSpec((width, d), lambda i: (0, 0), pipeline_mode=resident),
                  vec],
        out_specs=[row, row],
        out_shape=[jax.ShapeDtypeStruct((t, d), F32), jax.ShapeDtypeStruct((t, d), BF16)],
        compiler_params=_cparams("parallel"),
        name="xattn_block",
    )(x, g_x.reshape(1, d).astype(F32), wq, kv, wo, g_next.reshape(1, d).astype(F32))


def _s5_operators_body(bre_ref, bim_ref, cre_ref, cim_ref, p_ref, pt_ref, d_ref,
                        t_ref, w_ref, v_ref, *, L):
    hp = lax.Precision.HIGHEST
    bre, bim = bre_ref[...], bim_ref[...]
    cre, cim = cre_ref[...], cim_ref[...]
    k = bre.shape[1]
    row = lax.broadcasted_iota(jnp.int32, (LANES, LANES), 0)
    col = lax.broadcasted_iota(jnp.int32, (LANES, LANES), 1)
    lag_blocks = []
    for tau in range(L):
        pr, pi = p_ref[0, tau:tau + 1, :], p_ref[1, tau:tau + 1, :]
        xr, xi = bre * pr - bim * pi, bre * pi + bim * pr
        rows = slice((L - 1 - tau) * LANES, (L - tau) * LANES)
        w_ref[rows, :k] = xr.astype(w_ref.dtype)
        w_ref[rows, k:] = xi.astype(w_ref.dtype)
        d_tau = (jnp.dot(xr, cre, preferred_element_type=F32, precision=hp)
                 - jnp.dot(xi, cim, preferred_element_type=F32, precision=hp))
        if tau == 0:
            d_tau = d_tau + jnp.where(row == col, d_ref[...], 0.0)
        lag_blocks.append(d_tau.astype(t_ref.dtype))
    zero = jnp.zeros((LANES, LANES), t_ref.dtype)
    for s in range(L):
        for t in range(L):
            t_ref[s * LANES:(s + 1) * LANES, t * LANES:(t + 1) * LANES] = (
                lag_blocks[t - s] if t >= s else zero)
    for t in range(L):
        qr, qi = pt_ref[0, :, t + 1:t + 2], pt_ref[1, :, t + 1:t + 2]
        cols = slice(t * LANES, (t + 1) * LANES)
        v_ref[:k, cols] = (cre * qr - cim * qi).astype(v_ref.dtype)
        v_ref[k:, cols] = (-(cre * qi + cim * qr)).astype(v_ref.dtype)


def _s5_operators(A_re, A_im, log_dt, B_re, B_im, C_re, C_im, D_skip):
    n_g, n_p = A_re.shape
    n_h = B_re.shape[-1]
    L = S5_CHUNK
    gs = LANES // n_h
    n_slab = n_g // gs
    k = gs * n_p
    a_re, a_im = A_re.astype(F32), A_im.astype(F32)
    dt = jnp.exp(log_dt.astype(F32))[:, None]
    tau = jnp.arange(L + 1, dtype=F32)[:, None, None]
    mag = jnp.exp(tau * (dt * a_re))
    ang = tau * (dt * a_im)
    pw = jnp.stack([mag * jnp.cos(ang), mag * jnp.sin(ang)])
    lb_re, lb_im = pw[0, 1], pw[1, 1]
    den = a_re * a_re + a_im * a_im
    nr, ni = lb_re - 1.0, lb_im
    f_re = (nr * a_re + ni * a_im) / den
    f_im = (ni * a_re - nr * a_im) / den
    br, bi = B_re.astype(F32), B_im.astype(F32)
    bb_re = f_re[..., None] * br - f_im[..., None] * bi
    bb_im = f_re[..., None] * bi + f_im[..., None] * br

    eye = jnp.eye(gs, dtype=F32)

    def slab_b(x):
        x = x.reshape(n_slab, gs, n_p, n_h).transpose(0, 1, 3, 2)
        return (x[:, :, :, None, :] * eye[None, :, None, :, None]).reshape(n_slab, LANES, k)

    def slab_c(x):
        x = x.reshape(n_slab, gs, n_h, n_p).transpose(0, 1, 3, 2)
        return (x[:, :, :, None, :] * eye[None, :, None, :, None]).reshape(n_slab, k, LANES)

    p_tab = pw.reshape(2, L + 1, n_slab, k).transpose(2, 0, 1, 3)
    pt_tab = jnp.pad(pw.reshape(2, L + 1, n_slab, k).transpose(2, 0, 3, 1),
                     ((0, 0), (0, 0), (0, 0), (0, LANES - (L + 1))))
    mat_b = pl.BlockSpec((None, LANES, k), lambda j: (j, 0, 0))
    mat_c = pl.BlockSpec((None, k, LANES), lambda j: (j, 0, 0))
    t_op, w_op, v_op = pl.pallas_call(
        functools.partial(_s5_operators_body, L=L),
        grid=(n_slab,),
        in_specs=[mat_b, mat_b, mat_c, mat_c,
                  pl.BlockSpec((None, 2, L + 1, k), lambda j: (j, 0, 0, 0)),
                  pl.BlockSpec((None, 2, k, LANES), lambda j: (j, 0, 0, 0)),
                  pl.BlockSpec((1, LANES), lambda j: (0, j))],
        out_specs=[pl.BlockSpec((None, L * LANES, L * LANES), lambda j: (j, 0, 0)),
                   pl.BlockSpec((None, L * LANES, 2 * k), lambda j: (j, 0, 0)),
                   pl.BlockSpec((None, 2 * k, L * LANES), lambda j: (j, 0, 0))],
        out_shape=[jax.ShapeDtypeStruct((n_slab, L * LANES, L * LANES), BF16),
                   jax.ShapeDtypeStruct((n_slab, L * LANES, 2 * k), BF16),
                   jax.ShapeDtypeStruct((n_slab, 2 * k, L * LANES), BF16)],
        compiler_params=_cparams("parallel"),
        name="s5_operators",
    )(slab_b(bb_re), slab_b(bb_im), slab_c(C_re.astype(F32)), slab_c(C_im.astype(F32)),
      p_tab, pt_tab, D_skip.astype(F32).reshape(1, n_g * n_h))

    a_op = pw[:, L].reshape(2, n_slab, k).transpose(1, 0, 2).reshape(1, n_slab * 2 * k)
    return t_op, w_op, v_op, a_op


def _chunk_rows(piece_refs):
    return jnp.concatenate([r[...] for r in piece_refs], axis=1)


def _s5_increment_body(*refs):
    w_ref, z_ref = refs[-2], refs[-1]
    z_ref[...] = jnp.dot(_chunk_rows(refs[:-2]), w_ref[...], preferred_element_type=F32)


def _s5_scan_body(z_ref, a_ref, o_ref, st_ref, *, tc):
    @pl.when(pl.program_id(2) == 0)
    def _():
        st_ref[...] = jnp.zeros_like(st_ref)

    half = z_ref.shape[1] // 2
    a_re, a_im = a_ref[:, :half], a_ref[:, half:]

    def step(i, carry):
        re, im = carry
        base = pl.multiple_of(i * SUBLANES, SUBLANES)
        inc = z_ref[pl.ds(base, SUBLANES), :]
        before_re, before_im = [], []
        for r in range(SUBLANES):
            before_re.append(re)
            before_im.append(im)
            re, im = (a_re * re - a_im * im + inc[r:r + 1, :half],
                      a_re * im + a_im * re + inc[r:r + 1, half:])
        o_ref[pl.ds(base, SUBLANES), :half] = jnp.concatenate(before_re, axis=0)
        o_ref[pl.ds(base, SUBLANES), half:] = jnp.concatenate(before_im, axis=0)
        return re, im

    re, im = lax.fori_loop(0, tc // SUBLANES, step, (st_ref[0:1, :half], st_ref[0:1, half:]))
    st_ref[0:1, :half] = re
    st_ref[0:1, half:] = im


def _s5_output_body(*refs):
    x_ref, t_ref, v_ref, o_ref = refs[-4:]
    u = _chunk_rows(refs[:-4])
    half = u.shape[1] // 2
    y = jnp.concatenate(
        [jnp.dot(u[:, :half], t_ref[:half, :half], preferred_element_type=F32),
         jnp.dot(u, t_ref[:, half:], preferred_element_type=F32)], axis=1)
    y = y + jnp.dot(x_ref[...].astype(BF16), v_ref[...], preferred_element_type=F32)
    o_ref[...] = jax.nn.gelu(y).astype(o_ref.dtype)


def _s5_glu_body(*refs):
    w_ref, b_ref, o_ref = refs[-3:]
    y = _chunk_rows(refs[:-3])
    gate = _sigmoid(jnp.dot(y, w_ref[...], preferred_element_type=F32) + b_ref[...])
    o_ref[...] = (y.astype(F32) * gate).astype(o_ref.dtype)


def _chunked_proj_body(h_ref, w_ref, side_in_ref, u_ref, side_out_ref, acc_ref, *, L):
    acc = jnp.dot(h_ref[...], w_ref[...], preferred_element_type=F32)
    n_chunks, width = u_ref.shape[0], w_ref.shape[1]
    for c in range(width // LANES):
        acc_ref[c] = acc[:, c * LANES:(c + 1) * LANES]
    for tau in range(L):
        for c in range(width // LANES):
            u_ref[:, tau * width + c * LANES:tau * width + (c + 1) * LANES] = (
                acc_ref[c, pl.ds(tau, n_chunks, stride=L), :].astype(u_ref.dtype))
    side_out_ref[...] = side_in_ref[...].astype(side_out_ref.dtype)


def _chunked_proj(h, w, side_cast, tm):
    m, k = h.shape
    width = w.shape[1]
    L = S5_CHUNK
    steps = m // tm
    slab = side_cast.shape[0] // steps
    assert slab * steps == side_cast.shape[0] and slab % BF16_SUBLANES == 0 and tm % L == 0
    side_spec = pl.BlockSpec((slab, side_cast.shape[1]), lambda i: (i, 0))
    chunk_spec = pl.BlockSpec((tm // L, L * width), lambda i: (i, 0))
    return pl.pallas_call(
        functools.partial(_chunked_proj_body, L=L),
        grid=(steps,),
        in_specs=[pl.BlockSpec((tm, k), lambda i: (i, 0)), pl.BlockSpec((k, width), lambda i: (0, 0)),
                  side_spec],
        out_specs=[chunk_spec, side_spec],
        out_shape=[jax.ShapeDtypeStruct((m // L, L * width), BF16),
                   jax.ShapeDtypeStruct(side_cast.shape, BF16)],
        scratch_shapes=[pltpu.VMEM((width // LANES, tm, LANES), F32)],
        compiler_params=_cparams("parallel"),
        name="ssm_in_proj",
    )(h, w, side_cast)


def _s5_branch(uu, ops, w_glu, b_glu, bsz, seq):
    t_op, w_op, v_op, a_op = ops
    n_slab = t_op.shape[0]
    L = S5_CHUNK
    width = uu.shape[1] // L
    cw, sw = t_op.shape[1], w_op.shape[2]
    nc = bsz * seq // L
    ncb = seq // L
    tm = min(TILE_ROWS, nc)
    pieces = [pl.BlockSpec((tm, LANES), lambda j, i, s=s: (i, s * n_slab + j)) for s in range(L)]

    z = pl.pallas_call(
        _s5_increment_body,
        grid=(n_slab, nc // tm),
        in_specs=pieces + [pl.BlockSpec((None, cw, sw), lambda j, i: (j, 0, 0))],
        out_specs=pl.BlockSpec((tm, sw), lambda j, i: (i, j)),
        out_shape=jax.ShapeDtypeStruct((nc, n_slab * sw), F32),
        compiler_params=_cparams("parallel", "parallel"),
        name="s5_increment",
    )(*([uu] * L), w_op)

    tc = min(S5_SCAN_ROWS, ncb)
    nt = ncb // tc
    blk = pl.BlockSpec((tc, sw), lambda b, j, t: (b * nt + t, j))
    xprev = pl.pallas_call(
        functools.partial(_s5_scan_body, tc=tc),
        grid=(bsz, n_slab, nt),
        in_specs=[blk, pl.BlockSpec((1, sw), lambda b, j, t: (0, j))],
        out_specs=blk,
        out_shape=jax.ShapeDtypeStruct((nc, n_slab * sw), F32),
        scratch_shapes=[pltpu.VMEM((SUBLANES, sw), F32)],
        compiler_params=_cparams("parallel", "parallel", "arbitrary"),
        name="s5_scan",
    )(z, a_op)

    ys = pl.pallas_call(
        _s5_output_body,
        grid=(n_slab, nc // tm),
        in_specs=pieces + [pl.BlockSpec((tm, sw), lambda j, i: (i, j)),
                           pl.BlockSpec((None, cw, cw), lambda j, i: (j, 0, 0)),
                           pl.BlockSpec((None, sw, cw), lambda j, i: (j, 0, 0))],
        out_specs=pl.BlockSpec((tm, cw), lambda j, i: (i, j)),
        out_shape=jax.ShapeDtypeStruct((nc, n_slab * cw), BF16),
        compiler_params=_cparams("parallel", "parallel"),
        name="s5_output",
    )(*([uu] * L), xprev, t_op, v_op)

    tok = [pl.BlockSpec((tm, LANES), lambda i, t, j=j: (i, j * L + t)) for j in range(n_slab)]
    out = pl.pallas_call(
        _s5_glu_body,
        grid=(nc // tm, L),
        in_specs=tok + [pl.BlockSpec((width, width), lambda i, t: (0, 0)),
                        pl.BlockSpec((1, width), lambda i, t: (0, 0))],
        out_specs=pl.BlockSpec((tm, width), lambda i, t: (i, t)),
        out_shape=jax.ShapeDtypeStruct((nc, L * width), BF16),
        compiler_params=_cparams("parallel", "parallel"),
        name="s5_glu",
    )(*([ys] * n_slab), w_glu.astype(BF16), b_glu.reshape(1, width).astype(F32))
    return out.reshape(bsz * seq, width)


def _layer(x, mem_n, p, bsz, seq, mem_len):
    d = x.shape[1]
    aw = p["w_attn_up"].shape[0]
    heads = aw // ATTN_HEAD_DIM
    sw = p["w_ssm_up"].shape[0]
    off_f = 3 * aw
    off_u = off_f + heads
    off_g = off_u + sw

    w_in_t = p["w_in"].T
    tile = TILE_GATE_COLS
    w_qkv = _transpose_cast(w_in_t, 0, off_f, 2 * tile, lambda j: j, tk=d,
                            scale=LOG2E * ATTN_HEAD_DIM ** -0.5, n_scaled=aw // (2 * tile))
    w_f = jnp.pad(w_in_t[off_f:off_u].T, ((0, 0), (0, LANES - heads))).astype(BF16)
    b_f = jnp.pad(p["b_f"].astype(F32), (0, LANES - heads)).reshape(1, LANES)
    h, log_f = _rmsnorm(x, p["g_mix"], BF16, tm=TILE_ROWS, forget=(w_f, b_f))
    w_u = _transpose_cast(w_in_t, off_u, sw, tile, lambda j: j, tk=d)
    tn_mix = tile
    n_mix = d // tn_mix
    w_g = _transpose_cast(w_in_t, off_g, N_BRANCH * d, tn_mix,
                          lambda j: (j % n_mix) * N_BRANCH + j // n_mix, tk=d)

    qkv = _fused_matmul([(h, w_qkv, 0)], _ep_plain, BF16, n=3 * aw, tm=TILE_MM, tn=TILE_MM,
                        name="qkv_proj")
    fox, fox_rows, fox_position = _fox_attention(qkv, _forget_bias(log_f, bsz, seq, heads),
                                                 bsz, seq, heads, tq=TILE_MM)

    u, w_out = _chunked_proj(h, w_u, p["w_out"], tm=TILE_MM)
    ops = _s5_operators(p["A_re"], p["A_im"], p["log_dt"], p["B_re"], p["B_im"],
                        p["C_re"], p["C_im"], p["D_skip"])
    y = _s5_branch(u, ops, p["w_glu"], p["b_glu"], bsz, seq)

    b_gate = (p["b_gate"].astype(F32).reshape(N_BRANCH, n_mix, tn_mix).transpose(1, 0, 2)
              .reshape(1, N_BRANCH * d))
    merged, w_ff1 = _fused_matmul(
        [(h, w_g, 0, N_BRANCH), (fox, p["w_attn_up"].astype(BF16), 0, 1, fox_position),
         (y, p["w_ssm_up"].astype(BF16), 0)],
        _ep_gated_merge, BF16, n=d, tm=fox_rows, tn=tn_mix,
        aux=[(b_gate, 0, N_BRANCH)], side_cast=p["w_ff1"], name="gated_merge")
    x = _fused_matmul([(merged, w_out, 0)], _ep_add_res, F32, n=d,
                      tm=TILE_MM, tn=TILE_MM, res=x, name="mixer_out_proj")

    xw = p["wq_x"].shape[1]
    wq = (p["wq_x"] * (xw // XATTN_HEADS) ** -0.5).astype(BF16)
    w_kv = jnp.concatenate([p["wk_x"], p["wv_x"]], axis=1).astype(BF16)
    kv = _fused_matmul([(mem_n, w_kv, 0)], _ep_plain, BF16, n=2 * xw, tm=TILE_ROWS, tn=TILE_MM,
                       name="xattn_kv_proj")
    x, hm = _xattn_block(x, p["g_xattn"], wq, kv, p["wo_x"].astype(BF16), p["g_mlp"],
                         seq, mem_len, tm=TILE_XATTN_ROWS)

    dff = p["w_ff1"].shape[1]
    hid, w_ff2 = _fused_matmul([(hm, w_ff1, 0)], _ep_relu2, BF16, n=dff, tm=TILE_MM, tn=TILE_MM,
                               side_cast=p["w_ff2"], name="mlp_up")
    x = _matmul_kpieces(hid, w_ff2, x, n_pieces=MLP_DOWN_K_PIECES, tm=TILE_ROWS,
                        tn=MLP_DOWN_COLS, name="mlp_down")
    return x


_LAYER_PARAMS = ("g_mix", "w_in", "b_f", "b_gate", "A_re", "A_im", "log_dt", "B_re", "B_im",
                 "C_re", "C_im", "D_skip", "w_glu", "b_glu", "w_attn_up", "w_ssm_up", "w_out",
                 "g_xattn", "g_mem", "wq_x", "wk_x", "wv_x", "wo_x", "g_mlp", "w_ff1", "w_ff2")


def kernel(x, mem, g_mix, w_in, b_f, b_gate, A_re, A_im, log_dt, B_re, B_im, C_re, C_im, D_skip, w_glu, b_glu, w_attn_up, w_ssm_up, w_out, g_xattn, g_mem, wq_x, wk_x, wv_x, wo_x, g_mlp, w_ff1, w_ff2, g_final):
    stacked = dict(zip(_LAYER_PARAMS, (g_mix, w_in, b_f, b_gate, A_re, A_im, log_dt, B_re, B_im,
                                       C_re, C_im, D_skip, w_glu, b_glu, w_attn_up, w_ssm_up,
                                       w_out, g_xattn, g_mem, wq_x, wk_x, wv_x, wo_x, g_mlp,
                                       w_ff1, w_ff2)))
    bsz, seq, d = x.shape
    mem_len = mem.shape[1]
    xt = x.reshape(bsz * seq, d)
    mem2 = mem.reshape(bsz * mem_len, d)
    for l in range(g_mix.shape[0]):
        p = {k: v[l] for k, v in stacked.items()}
        mem_n = _rmsnorm(mem2, p["g_mem"], BF16, tm=TILE_XATTN_ROWS)
        xt = _layer(xt, mem_n, p, bsz, seq, mem_len)
    out = _rmsnorm(xt, g_final, x.dtype, tm=TILE_ROWS)
    return out.reshape(bsz, seq, d)
```

```python
import functools
import math

import jax
import jax.numpy as jnp
from jax import lax
from jax.experimental import pallas as pl
from jax.experimental.pallas import tpu as pltpu

F32 = jnp.float32
BF16 = jnp.bfloat16

V7X_VMEM_LIMIT_BYTES = 56 * 1024 * 1024
LANES = 128
SUBLANES = 8
BF16_SUBLANES = 16

EPS = 1e-6
NEG_INF = -1e30
LOG2E = math.log2(math.e)
N_BIAS_PIECES = 3
ATTN_HEAD_DIM = 128
S5_CHUNK = 16
XATTN_HEADS = 4
N_BRANCH = 2

TILE_MM = 1024
TILE_GATE_COLS = 256
TILE_ROWS = 512
TILE_XATTN_ROWS = 256
MLP_DOWN_COLS = 256
MLP_DOWN_K_PIECES = 4
FORGET_BLOCK = 256
S5_SCAN_ROWS = 128


def _cparams(*sem):
    return pltpu.CompilerParams(dimension_semantics=sem,
                                vmem_limit_bytes=V7X_VMEM_LIMIT_BYTES)


def _rms_scale(x, g):
    return (x * lax.rsqrt(jnp.mean(x * x, axis=-1, keepdims=True) + EPS)) * g


def _log_sigmoid(z):
    return jnp.minimum(z, 0.0) - jnp.log1p(jnp.exp(-jnp.abs(z)))


def _sigmoid(z):
    return 0.5 * jnp.tanh(0.5 * z) + 0.5


def _rmsnorm_body(x_ref, g_ref, *rest, has_proj):
    h = _rms_scale(x_ref[...].astype(F32), g_ref[...]).astype(rest[-1 - has_proj].dtype)
    if has_proj:
        w_ref, b_ref, o_ref, f_ref = rest
        f_ref[...] = _log_sigmoid(jnp.dot(h, w_ref[...], preferred_element_type=F32) + b_ref[...])
    else:
        o_ref, = rest
    o_ref[...] = h


def _rmsnorm(x, g, out_dtype, tm, forget=None):
    m, d = x.shape
    tm = min(tm, m)
    row = pl.BlockSpec((tm, d), lambda i: (i, 0))
    operands = [x, g.reshape(1, d).astype(F32)]
    in_specs = [row, pl.BlockSpec((1, d), lambda i: (0, 0))]
    out_specs, out_shape = [row], [jax.ShapeDtypeStruct((m, d), out_dtype)]
    if forget is not None:
        n = forget[0].shape[1]
        operands += list(forget)
        in_specs += [pl.BlockSpec((d, n), lambda i: (0, 0)), pl.BlockSpec((1, n), lambda i: (0, 0))]
        out_specs.append(pl.BlockSpec((tm, n), lambda i: (i, 0)))
        out_shape.append(jax.ShapeDtypeStruct((m, n), F32))
    outs = pl.pallas_call(
        functools.partial(_rmsnorm_body, has_proj=forget is not None),
        grid=(m // tm,),
        in_specs=in_specs,
        out_specs=out_specs,
        out_shape=out_shape,
        compiler_params=_cparams("parallel"),
        name="rmsnorm",
    )(*operands)
    return outs if forget is not None else outs[0]


def _mm_body(*refs, n_pairs, n_aux, has_res, has_side, epilogue):
    def product(p):
        return jnp.dot(refs[2 * p][...], refs[2 * p + 1][...], preferred_element_type=F32)

    pos = 2 * n_pairs
    aux = [refs[pos + i][...] for i in range(n_aux)]
    pos += n_aux
    res = refs[pos][...].astype(F32) if has_res else None
    pos += has_res
    if has_side:
        refs[-1][...] = refs[pos][...].astype(refs[-1].dtype)
        pos += 1
    o_ref = refs[pos]
    o_ref[...] = epilogue(product, aux, res).astype(o_ref.dtype)


def _fused_matmul(pairs, epilogue, out_dtype, *, n, tm, tn, aux=(), res=None, side_cast=None,
                  name="fused_matmul"):
    m = pairs[0][0].shape[0]
    tm, tn = min(tm, m), min(tn, n)
    n_aux, has_res, has_side = len(aux), res is not None, side_cast is not None
    nj = n // tn
    operands, in_specs = [], []
    for lhs, rhs, off, *opt in pairs:
        kp = lhs.shape[1]
        width = tn * (opt[0] if opt else 1)
        row_block = opt[1] if len(opt) > 1 else (lambda i: i)
        operands += [lhs, rhs]
        in_specs += [pl.BlockSpec((tm, kp), lambda i, j, rb=row_block: (rb(i), 0)),
                     pl.BlockSpec((kp, width), lambda i, j, off=off: (0, j + off))]
    for vec, off, *mult in aux:
        width = tn * (mult[0] if mult else 1)
        operands.append(vec)
        in_specs.append(pl.BlockSpec((1, width), lambda i, j, off=off: (0, j + off)))
    if has_res:
        operands.append(res)
        in_specs.append(pl.BlockSpec((tm, tn), lambda i, j: (i, j)))
    out_specs = [pl.BlockSpec((tm, tn), lambda i, j: (i, j))]
    out_shape = [jax.ShapeDtypeStruct((m, n), out_dtype)]
    if has_side:
        slab = side_cast.shape[0] // ((m // tm) * nj)
        assert slab * (m // tm) * nj == side_cast.shape[0] and slab % BF16_SUBLANES == 0
        side_spec = pl.BlockSpec((slab, side_cast.shape[1]), lambda i, j: (i * nj + j, 0))
        operands.append(side_cast)
        in_specs.append(side_spec)
        out_specs.append(side_spec)
        out_shape.append(jax.ShapeDtypeStruct(side_cast.shape, BF16))
    outs = pl.pallas_call(
        functools.partial(_mm_body, n_pairs=len(pairs), n_aux=n_aux, has_res=has_res,
                          has_side=has_side, epilogue=epilogue),
        grid=(m // tm, nj),
        in_specs=in_specs,
        out_specs=out_specs,
        out_shape=out_shape,
        compiler_params=_cparams("parallel", "parallel"),
        name=name,
    )(*operands)
    return outs if has_side else outs[0]


W_IN_EDGE_ROWS = 16


def _transpose_cast_body(a_ref, b_ref, o_ref, *, shift, scale, n_scaled):
    rows = a_ref.shape[0]
    x = jnp.concatenate([a_ref[...], b_ref[...]], axis=0)[shift:shift + rows, :]
    if n_scaled:
        x = x * jnp.where(pl.program_id(0) < n_scaled, scale, 1.0)
    o_ref[...] = x.T.astype(o_ref.dtype)


def _transpose_cast(wt, start, n_cols, tile, out_block, tk, scale=1.0, n_scaled=0):
    k = wt.shape[1]
    base = start // tile * tile
    shift = start - base
    assert shift % SUBLANES == 0 and shift <= W_IN_EDGE_ROWS and n_cols % tile == 0
    assert start + n_cols + (W_IN_EDGE_ROWS - shift) <= wt.shape[0] or shift == 0
    edge_per_tile = tile // W_IN_EDGE_ROWS
    last_edge = wt.shape[0] // W_IN_EDGE_ROWS - 1
    return pl.pallas_call(
        functools.partial(_transpose_cast_body, shift=shift, scale=scale, n_scaled=n_scaled),
        grid=(n_cols // tile, k // tk),
        in_specs=[pl.BlockSpec((tile, tk), lambda j, c: (base // tile + j, c)),
                  pl.BlockSpec((W_IN_EDGE_ROWS, tk),
                               lambda j, c: (jnp.minimum((base // tile + j + 1) * edge_per_tile,
                                                         last_edge), c))],
        out_specs=pl.BlockSpec((tk, tile), lambda j, c: (c, out_block(j))),
        out_shape=jax.ShapeDtypeStruct((k, n_cols), BF16),
        compiler_params=_cparams("parallel", "parallel"),
        name="transpose_cast",
    )(wt, wt)


def _matmul_kpieces_body(*refs, n_pieces):
    res_ref, o_ref = refs[2 * n_pieces], refs[2 * n_pieces + 1]
    acc = res_ref[...].astype(F32)
    for p in range(n_pieces):
        acc = acc + jnp.dot(refs[p][...], refs[n_pieces + p][...], preferred_element_type=F32)
    o_ref[...] = acc.astype(o_ref.dtype)


def _matmul_kpieces(lhs, rhs, res, *, n_pieces, tm, tn, name):
    m, k = lhs.shape
    n = rhs.shape[1]
    kp = k // n_pieces
    tm, tn = min(tm, m), min(tn, n)
    in_specs = ([pl.BlockSpec((tm, kp), lambda i, j, p=p: (i, p)) for p in range(n_pieces)]
                + [pl.BlockSpec((kp, tn), lambda i, j, p=p: (p, j)) for p in range(n_pieces)]
                + [pl.BlockSpec((tm, tn), lambda i, j: (i, j))])
    return pl.pallas_call(
        functools.partial(_matmul_kpieces_body, n_pieces=n_pieces),
        grid=(m // tm, n // tn),
        in_specs=in_specs,
        out_specs=pl.BlockSpec((tm, tn), lambda i, j: (i, j)),
        out_shape=jax.ShapeDtypeStruct((m, n), res.dtype),
        compiler_params=_cparams("parallel", "parallel"),
        name=name,
    )(*([lhs] * n_pieces), *([rhs] * n_pieces), res)


def _ep_plain(product, aux, res):
    return product(0)


def _ep_add_res(product, aux, res):
    return res + product(0)


def _ep_gated_merge(product, aux, res):
    gates = _sigmoid(product(0) + aux[0])
    attn = product(1)
    tn = attn.shape[1]
    return gates[:, :tn] * attn + gates[:, tn:] * product(2)


def _ep_relu2(product, aux, res):
    r = jnp.maximum(product(0), 0.0)
    return r * r


def _forget_bias_body(x_ref, o_ref, carry_ref, *, blk, nblk, heads):
    @pl.when(pl.program_id(1) == 0)
    def _():
        carry_ref[...] = jnp.zeros_like(carry_ref)

    r = lax.broadcasted_iota(jnp.int32, (blk, blk), 0)
    c = lax.broadcasted_iota(jnp.int32, (blk, blk), 1)
    tri = (c <= r).astype(F32)
    lane = lax.broadcasted_iota(jnp.int32, (blk, LANES), 1)

    def step(i, carry):
        base = pl.multiple_of(i * blk, blk)
        cs = jnp.dot(tri, x_ref[pl.ds(base, blk), :], preferred_element_type=F32,
                     precision=lax.Precision.HIGHEST) + carry
        for h in range(heads):
            bias = jnp.broadcast_to(cs[:, h:h + 1] * (-LOG2E), (blk, LANES))
            hi = bias.astype(BF16).astype(F32)
            mid = (bias - hi).astype(BF16).astype(F32)
            lo = bias - hi - mid
            pieces = jnp.where(lane == 0, hi, jnp.where(lane == 1, mid,
                                                        jnp.where(lane == 2, lo, 0.0)))
            o_ref[pl.ds(base, blk), h * LANES:(h + 1) * LANES] = pieces.astype(o_ref.dtype)
        return cs[blk - 1:blk, :]

    carry_ref[0:1, :] = lax.fori_loop(0, nblk, step, carry_ref[0:1, :])


def _forget_bias(log_f, bsz, seq, heads):
    blk = min(FORGET_BLOCK, seq)
    tt = min(TILE_MM, seq)
    nt = seq // tt
    return pl.pallas_call(
        functools.partial(_forget_bias_body, blk=blk, nblk=tt // blk, heads=heads),
        grid=(bsz, nt),
        in_specs=[pl.BlockSpec((tt, LANES), lambda b, t: (b * nt + t, 0))],
        out_specs=pl.BlockSpec((tt, heads * LANES), lambda b, t: (b * nt + t, 0)),
        out_shape=jax.ShapeDtypeStruct((bsz * seq, heads * LANES), BF16),
        scratch_shapes=[pltpu.VMEM((SUBLANES, LANES), F32)],
        compiler_params=_cparams("parallel", "arbitrary"),
        name="forget_bias",
    )(log_f)


def _fox_body(qa_ref, qb_ref, k_ref, v_ref, cp_ref, o_ref, kaug_ref, vaug_ref, s_ref,
              *, tq, seq):
    p = pl.program_id(2)
    nq = seq // tq
    dh = qa_ref.shape[1]

    @pl.when(p == 0)
    def _():
        def fill(j, _):
            rows = pl.ds(pl.multiple_of(j * tq, tq), tq)
            kaug_ref[rows, :dh] = k_ref[rows, :]
            kaug_ref[rows, dh:] = cp_ref[rows, :]
            vaug_ref[rows, :dh] = v_ref[rows, :]
            vaug_ref[rows, dh:] = jnp.ones((tq, dh), vaug_ref.dtype)
            return 0
        lax.fori_loop(0, nq, fill, 0)

    lane = lax.broadcasted_iota(jnp.int32, (tq, dh), 1)
    unit = (lane < N_BIAS_PIECES).astype(qa_ref.dtype)
    q_a = jnp.concatenate([qa_ref[...], unit], axis=1)
    q_b = jnp.concatenate([qb_ref[...], unit], axis=1)
    above_diag = (lax.broadcasted_iota(jnp.int32, (tq, tq), 1)
                  > lax.broadcasted_iota(jnp.int32, (tq, tq), 0))

    def key_rows(t):
        j = jnp.where(t <= p, t, t - (p + 1))
        return pl.ds(pl.multiple_of(j * tq, tq), tq)

    def scores(t):
        q = jnp.where(t <= p, q_a, q_b)
        s = lax.dot_general(q, kaug_ref[key_rows(t), :], (((1,), (1,)), ((), ())),
                            preferred_element_type=F32)
        if t == nq:
            s = jnp.where(above_diag, NEG_INF, s)
        elif t < nq // 2:
            s = jnp.where(jnp.logical_and(above_diag, t == p), NEG_INF, s)
        return s

    m = jnp.full((tq, 1), NEG_INF, F32)
    acc = jnp.zeros((tq, 2 * dh), F32)
    s_ref[0] = scores(0)
    for t in range(nq + 1):
        if t < nq:
            s_ref[(t + 1) % 2] = scores(t + 1)
        s = s_ref[t % 2]
        if 1 <= t <= nq // 2:
            first_b = t == p + 1
            m = jnp.where(first_b, NEG_INF, m)
            acc = jnp.where(first_b, 0.0, acc)
        m_new = jnp.maximum(m, jnp.max(s, axis=-1, keepdims=True))
        acc = jnp.exp2(m - m_new) * acc + jnp.dot(
            jnp.exp2(s - m_new).astype(vaug_ref.dtype), vaug_ref[key_rows(t), :],
            preferred_element_type=F32)
        m = m_new
        if t == 0:
            o_ref[:tq, :] = (acc[:, :dh] / acc[:, dh:]).astype(o_ref.dtype)
        elif t < nq // 2:
            out = (acc[:, :dh] / acc[:, dh:]).astype(o_ref.dtype)
            o_ref[:tq, :] = jnp.where(t <= p, out, o_ref[:tq, :])
    o_ref[tq:, :] = (acc[:, :dh] / acc[:, dh:]).astype(o_ref.dtype)


def _fox_attention(qkv, cp, bsz, seq, heads, tq):
    dh = ATTN_HEAD_DIM
    tq = min(tq, seq // 2)
    nq = seq // tq
    half = nq // 2
    assert nq % 2 == 0

    def position(i):
        b, qi = i // nq, i % nq
        return jnp.where(qi < half, (b * half + qi) * 2, (b * half + nq - 1 - qi) * 2 + 1)

    out = pl.pallas_call(
        functools.partial(_fox_body, tq=tq, seq=seq),
        grid=(bsz, heads, half),
        in_specs=[pl.BlockSpec((tq, dh), lambda b, h, p: (b * nq + p, h)),
                  pl.BlockSpec((tq, dh), lambda b, h, p: (b * nq + nq - 1 - p, h)),
                  pl.BlockSpec((seq, dh), lambda b, h, p: (b, heads + h)),
                  pl.BlockSpec((seq, dh), lambda b, h, p: (b, 2 * heads + h)),
                  pl.BlockSpec((seq, dh), lambda b, h, p: (b, h))],
        out_specs=pl.BlockSpec((2 * tq, dh), lambda b, h, p: (b * half + p, h)),
        out_shape=jax.ShapeDtypeStruct((bsz * seq, heads * dh), BF16),
        scratch_shapes=[pltpu.VMEM((seq, 2 * dh), BF16), pltpu.VMEM((seq, 2 * dh), BF16),
                        pltpu.VMEM((2, tq, tq), F32)],
        compiler_params=_cparams("parallel", "parallel", "arbitrary"),
        name="fox_attention",
    )(qkv, qkv, qkv, qkv, cp)
    return out, tq, position


def _xattn_block_body(x_ref, gx_ref, wq_ref, kv_ref, wo_ref, gn_ref, xo_ref, hn_ref):
    x = x_ref[...]
    width = wq_ref.shape[1]
    dh = width // XATTN_HEADS
    hx = _rms_scale(x, gx_ref[...]).astype(wq_ref.dtype)
    q = jnp.dot(hx, wq_ref[...], preferred_element_type=F32).astype(kv_ref.dtype)
    heads = []
    for hd in range(XATTN_HEADS):
        k = kv_ref[:, hd * dh:(hd + 1) * dh]
        v = kv_ref[:, width + hd * dh:width + (hd + 1) * dh]
        s = lax.dot_general(q[:, hd * dh:(hd + 1) * dh], k, (((1,), (1,)), ((), ())),
                            preferred_element_type=F32)
        p = jnp.exp(s - jnp.max(s, axis=-1, keepdims=True))
        l = jnp.sum(p, axis=-1, keepdims=True)
        o = jnp.dot(p.astype(v.dtype), v, preferred_element_type=F32) / l
        heads.append(o.astype(wo_ref.dtype))
    x2 = x + jnp.dot(jnp.concatenate(heads, axis=1), wo_ref[...], preferred_element_type=F32)
    xo_ref[...] = x2
    hn_ref[...] = _rms_scale(x2, gn_ref[...]).astype(hn_ref.dtype)


def _xattn_block(x, g_x, wq, kv, wo, g_next, seq, mem_len, tm):
    t, d = x.shape
    width = wq.shape[1]
    tm = min(tm, seq)
    nq = seq // tm
    row = pl.BlockSpec((tm, d), lambda i: (i, 0))
    vec = pl.BlockSpec((1, d), lambda i: (0, 0))
    resident = pl.Buffered(1)
    return pl.pallas_call(
        _xattn_block_body,
        grid=(t // tm,),
        in_specs=[row, vec,
                  pl.BlockSpec((d, width), lambda i: (0, 0), pipeline_mode=resident),
                  pl.BlockSpec((mem_len, 2 * width), lambda i: (i // nq, 0)),
                  pl.BlockSpec((width, d), lambda i: (0, 0), pipeline_mode=resident),
                  vec],
        out_specs=[row, row],
        out_shape=[jax.ShapeDtypeStruct((t, d), F32), jax.ShapeDtypeStruct((t, d), BF16)],
        compiler_params=_cparams("parallel"),
        name="xattn_block",
    )(x, g_x.reshape(1, d).astype(F32), wq, kv, wo, g_next.reshape(1, d).astype(F32))


def _s5_operators_body(bre_ref, bim_ref, cre_ref, cim_ref, p_ref, pt_ref, d_ref,
                        t_ref, w_ref, v_ref, *, L):
    hp = lax.Precision.HIGHEST
    bre, bim = bre_ref[...], bim_ref[...]
    cre, cim = cre_ref[...], cim_ref[...]
    k = bre.shape[1]
    row = lax.broadcasted_iota(jnp.int32, (LANES, LANES), 0)
    col = lax.broadcasted_iota(jnp.int32, (LANES, LANES), 1)
    lag_blocks = []
    for tau in range(L):
        pr, pi = p_ref[0, tau:tau + 1, :], p_ref[1, tau:tau + 1, :]
        xr, xi = bre * pr - bim * pi, bre * pi + bim * pr
        rows = slice((L - 1 - tau) * LANES, (L - tau) * LANES)
        w_ref[rows, :k] = xr.astype(w_ref.dtype)
        w_ref[rows, k:] = xi.astype(w_ref.dtype)
        d_tau = (jnp.dot(xr, cre, preferred_element_type=F32, precision=hp)
                 - jnp.dot(xi, cim, preferred_element_type=F32, precision=hp))
        if tau == 0:
            d_tau = d_tau + jnp.where(row == col, d_ref[...], 0.0)
        lag_blocks.append(d_tau.astype(t_ref.dtype))
    zero = jnp.zeros((LANES, LANES), t_ref.dtype)
    for s in range(L):
        for t in range(L):
            t_ref[s * LANES:(s + 1) * LANES, t * LANES:(t + 1) * LANES] = (
                lag_blocks[t - s] if t >= s else zero)
    for t in range(L):
        qr, qi = pt_ref[0, :, t + 1:t + 2], pt_ref[1, :, t + 1:t + 2]
        cols = slice(t * LANES, (t + 1) * LANES)
        v_ref[:k, cols] = (cre * qr - cim * qi).astype(v_ref.dtype)
        v_ref[k:, cols] = (-(cre * qi + cim * qr)).astype(v_ref.dtype)


def _s5_operators(A_re, A_im, log_dt, B_re, B_im, C_re, C_im, D_skip):
    n_g, n_p = A_re.shape
    n_h = B_re.shape[-1]
    L = S5_CHUNK
    gs = LANES // n_h
    n_slab = n_g // gs
    k = gs * n_p
    a_re, a_im = A_re.astype(F32), A_im.astype(F32)
    dt = jnp.exp(log_dt.astype(F32))[:, None]
    tau = jnp.arange(L + 1, dtype=F32)[:, None, None]
    mag = jnp.exp(tau * (dt * a_re))
    ang = tau * (dt * a_im)
    pw = jnp.stack([mag * jnp.cos(ang), mag * jnp.sin(ang)])
    lb_re, lb_im = pw[0, 1], pw[1, 1]
    den = a_re * a_re + a_im * a_im
    nr, ni = lb_re - 1.0, lb_im
    f_re = (nr * a_re + ni * a_im) / den
    f_im = (ni * a_re - nr * a_im) / den
    br, bi = B_re.astype(F32), B_im.astype(F32)
    bb_re = f_re[..., None] * br - f_im[..., None] * bi
    bb_im = f_re[..., None] * bi + f_im[..., None] * br

    eye = jnp.eye(gs, dtype=F32)

    def slab_b(x):
        x = x.reshape(n_slab, gs, n_p, n_h).transpose(0, 1, 3, 2)
        return (x[:, :, :, None, :] * eye[None, :, None, :, None]).reshape(n_slab, LANES, k)

    def slab_c(x):
        x = x.reshape(n_slab, gs, n_h, n_p).transpose(0, 1, 3, 2)
        return (x[:, :, :, None, :] * eye[None, :, None, :, None]).reshape(n_slab, k, LANES)

    p_tab = pw.reshape(2, L + 1, n_slab, k).transpose(2, 0, 1, 3)
    pt_tab = jnp.pad(pw.reshape(2, L + 1, n_slab, k).transpose(2, 0, 3, 1),
                     ((0, 0), (0, 0), (0, 0), (0, LANES - (L + 1))))
    mat_b = pl.BlockSpec((None, LANES, k), lambda j: (j, 0, 0))
    mat_c = pl.BlockSpec((None, k, LANES), lambda j: (j, 0, 0))
    t_op, w_op, v_op = pl.pallas_call(
        functools.partial(_s5_operators_body, L=L),
        grid=(n_slab,),
        in_specs=[mat_b, mat_b, mat_c, mat_c,
                  pl.BlockSpec((None, 2, L + 1, k), lambda j: (j, 0, 0, 0)),
                  pl.BlockSpec((None, 2, k, LANES), lambda j: (j, 0, 0, 0)),
                  pl.BlockSpec((1, LANES), lambda j: (0, j))],
        out_specs=[pl.BlockSpec((None, L * LANES, L * LANES), lambda j: (j, 0, 0)),
                   pl.BlockSpec((None, L * LANES, 2 * k), lambda j: (j, 0, 0)),
                   pl.BlockSpec((None, 2 * k, L * LANES), lambda j: (j, 0, 0))],
        out_shape=[jax.ShapeDtypeStruct((n_slab, L * LANES, L * LANES), BF16),
                   jax.ShapeDtypeStruct((n_slab, L * LANES, 2 * k), BF16),
                   jax.ShapeDtypeStruct((n_slab, 2 * k, L * LANES), BF16)],
        compiler_params=_cparams("parallel"),
        name="s5_operators",
    )(slab_b(bb_re), slab_b(bb_im), slab_c(C_re.astype(F32)), slab_c(C_im.astype(F32)),
      p_tab, pt_tab, D_skip.astype(F32).reshape(1, n_g * n_h))

    a_op = pw[:, L].reshape(2, n_slab, k).transpose(1, 0, 2).reshape(1, n_slab * 2 * k)
    return t_op, w_op, v_op, a_op


def _chunk_rows(piece_refs):
    return jnp.concatenate([r[...] for r in piece_refs], axis=1)


def _s5_increment_body(*refs):
    w_ref, z_ref = refs[-2], refs[-1]
    z_ref[...] = jnp.dot(_chunk_rows(refs[:-2]), w_ref[...], preferred_element_type=F32)


def _s5_scan_body(z_ref, a_ref, o_ref, st_ref, *, tc):
    @pl.when(pl.program_id(2) == 0)
    def _():
        st_ref[...] = jnp.zeros_like(st_ref)

    half = z_ref.shape[1] // 2
    a_re, a_im = a_ref[:, :half], a_ref[:, half:]

    def step(i, carry):
        re, im = carry
        base = pl.multiple_of(i * SUBLANES, SUBLANES)
        inc = z_ref[pl.ds(base, SUBLANES), :]
        before_re, before_im = [], []
        for r in range(SUBLANES):
            before_re.append(re)
            before_im.append(im)
            re, im = (a_re * re - a_im * im + inc[r:r + 1, :half],
                      a_re * im + a_im * re + inc[r:r + 1, half:])
        o_ref[pl.ds(base, SUBLANES), :half] = jnp.concatenate(before_re, axis=0)
        o_ref[pl.ds(base, SUBLANES), half:] = jnp.concatenate(before_im, axis=0)
        return re, im

    re, im = lax.fori_loop(0, tc // SUBLANES, step, (st_ref[0:1, :half], st_ref[0:1, half:]))
    st_ref[0:1, :half] = re
    st_ref[0:1, half:] = im


def _s5_output_body(*refs):
    x_ref, t_ref, v_ref, o_ref = refs[-4:]
    u = _chunk_rows(refs[:-4])
    half = u.shape[1] // 2
    y = jnp.concatenate(
        [jnp.dot(u[:, :half], t_ref[:half, :half], preferred_element_type=F32),
         jnp.dot(u, t_ref[:, half:], preferred_element_type=F32)], axis=1)
    y = y + jnp.dot(x_ref[...].astype(BF16), v_ref[...], preferred_element_type=F32)
    o_ref[...] = jax.nn.gelu(y).astype(o_ref.dtype)


def _s5_glu_body(*refs):
    w_ref, b_ref, o_ref = refs[-3:]
    y = _chunk_rows(refs[:-3])
    gate = _sigmoid(jnp.dot(y, w_ref[...], preferred_element_type=F32) + b_ref[...])
    o_ref[...] = (y.astype(F32) * gate).astype(o_ref.dtype)


def _chunked_proj_body(h_ref, w_ref, side_in_ref, u_ref, side_out_ref, acc_ref, *, L):
    acc = jnp.dot(h_ref[...], w_ref[...], preferred_element_type=F32)
    n_chunks, width = u_ref.shape[0], w_ref.shape[1]
    for c in range(width // LANES):
        acc_ref[c] = acc[:, c * LANES:(c + 1) * LANES]
    for tau in range(L):
        for c in range(width // LANES):
            u_ref[:, tau * width + c * LANES:tau * width + (c + 1) * LANES] = (
                acc_ref[c, pl.ds(tau, n_chunks, stride=L), :].astype(u_ref.dtype))
    side_out_ref[...] = side_in_ref[...].astype(side_out_ref.dtype)


def _chunked_proj(h, w, side_cast, tm):
    m, k = h.shape
    width = w.shape[1]
    L = S5_CHUNK
    steps = m // tm
    slab = side_cast.shape[0] // steps
    assert slab * steps == side_cast.shape[0] and slab % BF16_SUBLANES == 0 and tm % L == 0
    side_spec = pl.BlockSpec((slab, side_cast.shape[1]), lambda i: (i, 0))
    chunk_spec = pl.BlockSpec((tm // L, L * width), lambda i: (i, 0))
    return pl.pallas_call(
        functools.partial(_chunked_proj_body, L=L),
        grid=(steps,),
        in_specs=[pl.BlockSpec((tm, k), lambda i: (i, 0)), pl.BlockSpec((k, width), lambda i: (0, 0)),
                  side_spec],
        out_specs=[chunk_spec, side_spec],
        out_shape=[jax.ShapeDtypeStruct((m // L, L * width), BF16),
                   jax.ShapeDtypeStruct(side_cast.shape, BF16)],
        scratch_shapes=[pltpu.VMEM((width // LANES, tm, LANES), F32)],
        compiler_params=_cparams("parallel"),
        name="ssm_in_proj",
    )(h, w, side_cast)


def _s5_branch(uu, ops, w_glu, b_glu, bsz, seq):
    t_op, w_op, v_op, a_op = ops
    n_slab = t_op.shape[0]
    L = S5_CHUNK
    width = uu.shape[1] // L
    cw, sw = t_op.shape[1], w_op.shape[2]
    nc = bsz * seq // L
    ncb = seq // L
    tm = min(TILE_ROWS, nc)
    pieces = [pl.BlockSpec((tm, LANES), lambda j, i, s=s: (i, s * n_slab + j)) for s in range(L)]

    z = pl.pallas_call(
        _s5_increment_body,
        grid=(n_slab, nc // tm),
        in_specs=pieces + [pl.BlockSpec((None, cw, sw), lambda j, i: (j, 0, 0))],
        out_specs=pl.BlockSpec((tm, sw), lambda j, i: (i, j)),
        out_shape=jax.ShapeDtypeStruct((nc, n_slab * sw), F32),
        compiler_params=_cparams("parallel", "parallel"),
        name="s5_increment",
    )(*([uu] * L), w_op)

    tc = min(S5_SCAN_ROWS, ncb)
    nt = ncb // tc
    blk = pl.BlockSpec((tc, sw), lambda b, j, t: (b * nt + t, j))
    xprev = pl.pallas_call(
        functools.partial(_s5_scan_body, tc=tc),
        grid=(bsz, n_slab, nt),
        in_specs=[blk, pl.BlockSpec((1, sw), lambda b, j, t: (0, j))],
        out_specs=blk,
        out_shape=jax.ShapeDtypeStruct((nc, n_slab * sw), F32),
        scratch_shapes=[pltpu.VMEM((SUBLANES, sw), F32)],
        compiler_params=_cparams("parallel", "parallel", "arbitrary"),
        name="s5_scan",
    )(z, a_op)

    ys = pl.pallas_call(
        _s5_output_body,
        grid=(n_slab, nc // tm),
        in_specs=pieces + [pl.BlockSpec((tm, sw), lambda j, i: (i, j)),
                           pl.BlockSpec((None, cw, cw), lambda j, i: (j, 0, 0)),
                           pl.BlockSpec((None, sw, cw), lambda j, i: (j, 0, 0))],
        out_specs=pl.BlockSpec((tm, cw), lambda j, i: (i, j)),
        out_shape=jax.ShapeDtypeStruct((nc, n_slab * cw), BF16),
        compiler_params=_cparams("parallel", "parallel"),
        name="s5_output",
    )(*([uu] * L), xprev, t_op, v_op)

    tok = [pl.BlockSpec((tm, LANES), lambda i, t, j=j: (i, j * L + t)) for j in range(n_slab)]
    out = pl.pallas_call(
        _s5_glu_body,
        grid=(nc // tm, L),
        in_specs=tok + [pl.BlockSpec((width, width), lambda i, t: (0, 0)),
                        pl.BlockSpec((1, width), lambda i, t: (0, 0))],
        out_specs=pl.BlockSpec((tm, width), lambda i, t: (i, t)),
        out_shape=jax.ShapeDtypeStruct((nc, L * width), BF16),
        compiler_params=_cparams("parallel", "parallel"),
        name="s5_glu",
    )(*([ys] * n_slab), w_glu.astype(BF16), b_glu.reshape(1, width).astype(F32))
    return out.reshape(bsz * seq, width)


def _layer(x, mem_n, p, bsz, seq, mem_len):
    d = x.shape[1]
    aw = p["w_attn_up"].shape[0]
    heads = aw // ATTN_HEAD_DIM
    sw = p["w_ssm_up"].shape[0]
    off_f = 3 * aw
    off_u = off_f + heads
    off_g = off_u + sw

    w_in_t = p["w_in"].T
    tile = TILE_GATE_COLS
    w_qkv = _transpose_cast(w_in_t, 0, off_f, 2 * tile, lambda j: j, tk=d,
                            scale=LOG2E * ATTN_HEAD_DIM ** -0.5, n_scaled=aw // (2 * tile))
    w_f = jnp.pad(w_in_t[off_f:off_u].T, ((0, 0), (0, LANES - heads))).astype(BF16)
    b_f = jnp.pad(p["b_f"].astype(F32), (0, LANES - heads)).reshape(1, LANES)
    h, log_f = _rmsnorm(x, p["g_mix"], BF16, tm=TILE_ROWS, forget=(w_f, b_f))
    w_u = _transpose_cast(w_in_t, off_u, sw, tile, lambda j: j, tk=d)
    tn_mix = tile
    n_mix = d // tn_mix
    w_g = _transpose_cast(w_in_t, off_g, N_BRANCH * d, tn_mix,
                          lambda j: (j % n_mix) * N_BRANCH + j // n_mix, tk=d)

    qkv = _fused_matmul([(h, w_qkv, 0)], _ep_plain, BF16, n=3 * aw, tm=TILE_MM, tn=TILE_MM,
                        name="qkv_proj")
    fox, fox_rows, fox_position = _fox_attention(qkv, _forget_bias(log_f, bsz, seq, heads),
                                                 bsz, seq, heads, tq=TILE_MM)

    u, w_out = _chunked_proj(h, w_u, p["w_out"], tm=TILE_MM)
    ops = _s5_operators(p["A_re"], p["A_im"], p["log_dt"], p["B_re"], p["B_im"],
                        p["C_re"], p["C_im"], p["D_skip"])
    y = _s5_branch(u, ops, p["w_glu"], p["b_glu"], bsz, seq)

    b_gate = (p["b_gate"].astype(F32).reshape(N_BRANCH, n_mix, tn_mix).transpose(1, 0, 2)
              .reshape(1, N_BRANCH * d))
    merged, w_ff1 = _fused_matmul(
        [(h, w_g, 0, N_BRANCH), (fox, p["w_attn_up"].astype(BF16), 0, 1, fox_position),
         (y, p["w_ssm_up"].astype(BF16), 0)],
        _ep_gated_merge, BF16, n=d, tm=fox_rows, tn=tn_mix,
        aux=[(b_gate, 0, N_BRANCH)], side_cast=p["w_ff1"], name="gated_merge")
    x = _fused_matmul([(merged, w_out, 0)], _ep_add_res, F32, n=d,
                      tm=TILE_MM, tn=TILE_MM, res=x, name="mixer_out_proj")

    xw = p["wq_x"].shape[1]
    wq = (p["wq_x"] * (xw // XATTN_HEADS) ** -0.5).astype(BF16)
    w_kv = jnp.concatenate([p["wk_x"], p["wv_x"]], axis=1).astype(BF16)
    kv = _fused_matmul([(mem_n, w_kv, 0)], _ep_plain, BF16, n=2 * xw, tm=TILE_ROWS, tn=TILE_MM,
                       name="xattn_kv_proj")
    x, hm = _xattn_block(x, p["g_xattn"], wq, kv, p["wo_x"].astype(BF16), p["g_mlp"],
                         seq, mem_len, tm=TILE_XATTN_ROWS)

    dff = p["w_ff1"].shape[1]
    hid, w_ff2 = _fused_matmul([(hm, w_ff1, 0)], _ep_relu2, BF16, n=dff, tm=TILE_MM, tn=TILE_MM,
                               side_cast=p["w_ff2"], name="mlp_up")
    x = _matmul_kpieces(hid, w_ff2, x, n_pieces=MLP_DOWN_K_PIECES, tm=TILE_ROWS,
                        tn=MLP_DOWN_COLS, name="mlp_down")
    return x


_LAYER_PARAMS = ("g_mix", "w_in", "b_f", "b_gate", "A_re", "A_im", "log_dt", "B_re", "B_im",
                 "C_re", "C_im", "D_skip", "w_glu", "b_glu", "w_attn_up", "w_ssm_up", "w_out",
                 "g_xattn", "g_mem", "wq_x", "wk_x", "wv_x", "wo_x", "g_mlp", "w_ff1", "w_ff2")


def kernel(x, mem, g_mix, w_in, b_f, b_gate, A_re, A_im, log_dt, B_re, B_im, C_re, C_im, D_skip, w_glu, b_glu, w_attn_up, w_ssm_up, w_out, g_xattn, g_mem, wq_x, wk_x, wv_x, wo_x, g_mlp, w_ff1, w_ff2, g_final):
    stacked = dict(zip(_LAYER_PARAMS, (g_mix, w_in, b_f, b_gate, A_re, A_im, log_dt, B_re, B_im,
                                       C_re, C_im, D_skip, w_glu, b_glu, w_attn_up, w_ssm_up,
                                       w_out, g_xattn, g_mem, wq_x, wk_x, wv_x, wo_x, g_mlp,
                                       w_ff1, w_ff2)))
    bsz, seq, d = x.shape
    mem_len = mem.shape[1]
    xt = x.reshape(bsz * seq, d)
    mem2 = mem.reshape(bsz * mem_len, d)
    for l in range(g_mix.shape[0]):
        p = {k: v[l] for k, v in stacked.items()}
        mem_n = _rmsnorm(mem2, p["g_mem"], BF16, tm=TILE_XATTN_ROWS)
        xt = _layer(xt, mem_n, p, bsz, seq, mem_len)
    out = _rmsnorm(xt, g_final, x.dtype, tm=TILE_ROWS)
    return out.reshape(bsz, seq, d)
```

```python
import functools
import math

import jax
import jax.numpy as jnp
from jax import lax
from jax.experimental import pallas as pl
from jax.experimental.pallas import tpu as pltpu

F32 = jnp.float32
BF16 = jnp.bfloat16

V7X_VMEM_LIMIT_BYTES = 56 * 1024 * 1024
LANES = 128
SUBLANES = 8
BF16_SUBLANES = 16

EPS = 1e-6
NEG_INF = -1e30
LOG2E = math.log2(math.e)
N_BIAS_PIECES = 3
ATTN_HEAD_DIM = 128
S5_CHUNK = 16
XATTN_HEADS = 4
N_BRANCH = 2

TILE_MM = 1024
TILE_GATE_COLS = 256
TILE_ROWS = 512
TILE_XATTN_ROWS = 256
MLP_DOWN_COLS = 256
MLP_DOWN_K_PIECES = 4
FORGET_BLOCK = 256
S5_SCAN_ROWS = 128


def _cparams(*sem):
    return pltpu.CompilerParams(dimension_semantics=sem,
                                vmem_limit_bytes=V7X_VMEM_LIMIT_BYTES)


def _rms_scale(x, g):
    return (x * lax.rsqrt(jnp.mean(x * x, axis=-1, keepdims=True) + EPS)) * g


def _log_sigmoid(z):
    return jnp.minimum(z, 0.0) - jnp.log1p(jnp.exp(-jnp.abs(z)))


def _sigmoid(z):
    return 0.5 * jnp.tanh(0.5 * z) + 0.5


def _rmsnorm_body(x_ref, g_ref, *rest, has_proj):
    h = _rms_scale(x_ref[...].astype(F32), g_ref[...]).astype(rest[-1 - has_proj].dtype)
    if has_proj:
        w_ref, b_ref, o_ref, f_ref = rest
        f_ref[...] = _log_sigmoid(jnp.dot(h, w_ref[...], preferred_element_type=F32) + b_ref[...])
    else:
        o_ref, = rest
    o_ref[...] = h


def _rmsnorm(x, g, out_dtype, tm, forget=None):
    m, d = x.shape
    tm = min(tm, m)
    row = pl.BlockSpec((tm, d), lambda i: (i, 0))
    operands = [x, g.reshape(1, d).astype(F32)]
    in_specs = [row, pl.BlockSpec((1, d), lambda i: (0, 0))]
    out_specs, out_shape = [row], [jax.ShapeDtypeStruct((m, d), out_dtype)]
    if forget is not None:
        n = forget[0].shape[1]
        operands += list(forget)
        in_specs += [pl.BlockSpec((d, n), lambda i: (0, 0)), pl.BlockSpec((1, n), lambda i: (0, 0))]
        out_specs.append(pl.BlockSpec((tm, n), lambda i: (i, 0)))
        out_shape.append(jax.ShapeDtypeStruct((m, n), F32))
    outs = pl.pallas_call(
        functools.partial(_rmsnorm_body, has_proj=forget is not None),
        grid=(m // tm,),
        in_specs=in_specs,
        out_specs=out_specs,
        out_shape=out_shape,
        compiler_params=_cparams("parallel"),
        name="rmsnorm",
    )(*operands)
    return outs if forget is not None else outs[0]


def _mm_body(*refs, n_pairs, n_aux, has_res, has_side, epilogue):
    def product(p):
        return jnp.dot(refs[2 * p][...], refs[2 * p + 1][...], preferred_element_type=F32)

    pos = 2 * n_pairs
    aux = [refs[pos + i][...] for i in range(n_aux)]
    pos += n_aux
    res = refs[pos][...].astype(F32) if has_res else None
    pos += has_res
    if has_side:
        refs[-1][...] = refs[pos][...].astype(refs[-1].dtype)
        pos += 1
    o_ref = refs[pos]
    o_ref[...] = epilogue(product, aux, res).astype(o_ref.dtype)


def _fused_matmul(pairs, epilogue, out_dtype, *, n, tm, tn, aux=(), res=None, side_cast=None,
                  name="fused_matmul"):
    m = pairs[0][0].shape[0]
    tm, tn = min(tm, m), min(tn, n)
    n_aux, has_res, has_side = len(aux), res is not None, side_cast is not None
    nj = n // tn
    operands, in_specs = [], []
    for lhs, rhs, off, *opt in pairs:
        kp = lhs.shape[1]
        width = tn * (opt[0] if opt else 1)
        row_block = opt[1] if len(opt) > 1 else (lambda i: i)
        operands += [lhs, rhs]
        in_specs += [pl.BlockSpec((tm, kp), lambda i, j, rb=row_block: (rb(i), 0)),
                     pl.BlockSpec((kp, width), lambda i, j, off=off: (0, j + off))]
    for vec, off, *mult in aux:
        width = tn * (mult[0] if mult else 1)
        operands.append(vec)
        in_specs.append(pl.BlockSpec((1, width), lambda i, j, off=off: (0, j + off)))
    if has_res:
        operands.append(res)
        in_specs.append(pl.BlockSpec((tm, tn), lambda i, j: (i, j)))
    out_specs = [pl.BlockSpec((tm, tn), lambda i, j: (i, j))]
    out_shape = [jax.ShapeDtypeStruct((m, n), out_dtype)]
    if has_side:
        slab = side_cast.shape[0] // ((m // tm) * nj)
        assert slab * (m // tm) * nj == side_cast.shape[0] and slab % BF16_SUBLANES == 0
        side_spec = pl.BlockSpec((slab, side_cast.shape[1]), lambda i, j: (i * nj + j, 0))
        operands.append(side_cast)
        in_specs.append(side_spec)
        out_specs.append(side_spec)
        out_shape.append(jax.ShapeDtypeStruct(side_cast.shape, BF16))
    outs = pl.pallas_call(
        functools.partial(_mm_body, n_pairs=len(pairs), n_aux=n_aux, has_res=has_res,
                          has_side=has_side, epilogue=epilogue),
        grid=(m // tm, nj),
        in_specs=in_specs,
        out_specs=out_specs,
        out_shape=out_shape,
        compiler_params=_cparams("parallel", "parallel"),
        name=name,
    )(*operands)
    return outs if has_side else outs[0]


W_IN_EDGE_ROWS = 16


def _transpose_cast_body(a_ref, b_ref, o_ref, *, shift, scale, n_scaled):
    rows = a_ref.shape[0]
    x = jnp.concatenate([a_ref[...], b_ref[...]], axis=0)[shift:shift + rows, :]
    if n_scaled:
        x = x * jnp.where(pl.program_id(0) < n_scaled, scale, 1.0)
    o_ref[...] = x.T.astype(o_ref.dtype)


def _transpose_cast(wt, start, n_cols, tile, out_block, tk, scale=1.0, n_scaled=0):
    k = wt.shape[1]
    base = start // tile * tile
    shift = start - base
    assert shift % SUBLANES == 0 and shift <= W_IN_EDGE_ROWS and n_cols % tile == 0
    assert start + n_cols + (W_IN_EDGE_ROWS - shift) <= wt.shape[0] or shift == 0
    edge_per_tile = tile // W_IN_EDGE_ROWS
    last_edge = wt.shape[0] // W_IN_EDGE_ROWS - 1
    return pl.pallas_call(
        functools.partial(_transpose_cast_body, shift=shift, scale=scale, n_scaled=n_scaled),
        grid=(n_cols // tile, k // tk),
        in_specs=[pl.BlockSpec((tile, tk), lambda j, c: (base // tile + j, c)),
                  pl.BlockSpec((W_IN_EDGE_ROWS, tk),
                               lambda j, c: (jnp.minimum((base // tile + j + 1) * edge_per_tile,
                                                         last_edge), c))],
        out_specs=pl.BlockSpec((tk, tile), lambda j, c: (c, out_block(j))),
        out_shape=jax.ShapeDtypeStruct((k, n_cols), BF16),
        compiler_params=_cparams("parallel", "parallel"),
        name="transpose_cast",
    )(wt, wt)


def _matmul_kpieces_body(*refs, n_pieces):
    res_ref, o_ref = refs[2 * n_pieces], refs[2 * n_pieces + 1]
    acc = res_ref[...].astype(F32)
    for p in range(n_pieces):
        acc = acc + jnp.dot(refs[p][...], refs[n_pieces + p][...], preferred_element_type=F32)
    o_ref[...] = acc.astype(o_ref.dtype)


def _matmul_kpieces(lhs, rhs, res, *, n_pieces, tm, tn, name):
    m, k = lhs.shape
    n = rhs.shape[1]
    kp = k // n_pieces
    tm, tn = min(tm, m), min(tn, n)
    in_specs = ([pl.BlockSpec((tm, kp), lambda i, j, p=p: (i, p)) for p in range(n_pieces)]
                + [pl.BlockSpec((kp, tn), lambda i, j, p=p: (p, j)) for p in range(n_pieces)]
                + [pl.BlockSpec((tm, tn), lambda i, j: (i, j))])
    return pl.pallas_call(
        functools.partial(_matmul_kpieces_body, n_pieces=n_pieces),
        grid=(m // tm, n // tn),
        in_specs=in_specs,
        out_specs=pl.BlockSpec((tm, tn), lambda i, j: (i, j)),
        out_shape=jax.ShapeDtypeStruct((m, n), res.dtype),
        compiler_params=_cparams("parallel", "parallel"),
        name=name,
    )(*([lhs] * n_pieces), *([rhs] * n_pieces), res)


def _ep_plain(product, aux, res):
    return product(0)


def _ep_add_res(product, aux, res):
    return res + product(0)


def _ep_gated_merge(product, aux, res):
    gates = _sigmoid(product(0) + aux[0])
    attn = product(1)
    tn = attn.shape[1]
    return gates[:, :tn] * attn + gates[:, tn:] * product(2)


def _ep_relu2(product, aux, res):
    r = jnp.maximum(product(0), 0.0)
    return r * r


def _forget_bias_body(x_ref, o_ref, carry_ref, *, blk, nblk, heads):
    @pl.when(pl.program_id(1) == 0)
    def _():
        carry_ref[...] = jnp.zeros_like(carry_ref)

    r = lax.broadcasted_iota(jnp.int32, (blk, blk), 0)
    c = lax.broadcasted_iota(jnp.int32, (blk, blk), 1)
    tri = (c <= r).astype(F32)
    lane = lax.broadcasted_iota(jnp.int32, (blk, LANES), 1)

    def step(i, carry):
        base = pl.multiple_of(i * blk, blk)
        cs = jnp.dot(tri, x_ref[pl.ds(base, blk), :], preferred_element_type=F32,
                     precision=lax.Precision.HIGHEST) + carry
        for h in range(heads):
            bias = jnp.broadcast_to(cs[:, h:h + 1] * (-LOG2E), (blk, LANES))
            hi = bias.astype(BF16).astype(F32)
            mid = (bias - hi).astype(BF16).astype(F32)
            lo = bias - hi - mid
            pieces = jnp.where(lane == 0, hi, jnp.where(lane == 1, mid,
                                                        jnp.where(lane == 2, lo, 0.0)))
            o_ref[pl.ds(base, blk), h * LANES:(h + 1) * LANES] = pieces.astype(o_ref.dtype)
        return cs[blk - 1:blk, :]

    carry_ref[0:1, :] = lax.fori_loop(0, nblk, step, carry_ref[0:1, :])


def _forget_bias(log_f, bsz, seq, heads):
    blk = min(FORGET_BLOCK, seq)
    tt = min(TILE_MM, seq)
    nt = seq // tt
    return pl.pallas_call(
        functools.partial(_forget_bias_body, blk=blk, nblk=tt // blk, heads=heads),
        grid=(bsz, nt),
        in_specs=[pl.BlockSpec((tt, LANES), lambda b, t: (b * nt + t, 0))],
        out_specs=pl.BlockSpec((tt, heads * LANES), lambda b, t: (b * nt + t, 0)),
        out_shape=jax.ShapeDtypeStruct((bsz * seq, heads * LANES), BF16),
        scratch_shapes=[pltpu.VMEM((SUBLANES, LANES), F32)],
        compiler_params=_cparams("parallel", "arbitrary"),
        name="forget_bias",
    )(log_f)


def _fox_body(qa_ref, qb_ref, k_ref, v_ref, cp_ref, o_ref, kaug_ref, vaug_ref, s_ref,
              *, tq, seq):
    p = pl.program_id(2)
    nq = seq // tq
    dh = qa_ref.shape[1]

    @pl.when(p == 0)
    def _():
        def fill(j, _):
            rows = pl.ds(pl.multiple_of(j * tq, tq), tq)
            kaug_ref[rows, :dh] = k_ref[rows, :]
            kaug_ref[rows, dh:] = cp_ref[rows, :]
            vaug_ref[rows, :dh] = v_ref[rows, :]
            vaug_ref[rows, dh:] = jnp.ones((tq, dh), vaug_ref.dtype)
            return 0
        lax.fori_loop(0, nq, fill, 0)

    lane = lax.broadcasted_iota(jnp.int32, (tq, dh), 1)
    unit = (lane < N_BIAS_PIECES).astype(qa_ref.dtype)
    q_a = jnp.concatenate([qa_ref[...], unit], axis=1)
    q_b = jnp.concatenate([qb_ref[...], unit], axis=1)
    above_diag = (lax.broadcasted_iota(jnp.int32, (tq, tq), 1)
                  > lax.broadcasted_iota(jnp.int32, (tq, tq), 0))

    def key_rows(t):
        j = jnp.where(t <= p, t, t - (p + 1))
        return pl.ds(pl.multiple_of(j * tq, tq), tq)

    def scores(t):
        q = jnp.where(t <= p, q_a, q_b)
        s = lax.dot_general(q, kaug_ref[key_rows(t), :], (((1,), (1,)), ((), ())),
                            preferred_element_type=F32)
        if t == nq:
            s = jnp.where(above_diag, NEG_INF, s)
        elif t < nq // 2:
            s = jnp.where(jnp.logical_and(above_diag, t == p), NEG_INF, s)
        return s

    m = jnp.full((tq, 1), NEG_INF, F32)
    acc = jnp.zeros((tq, 2 * dh), F32)
    s_ref[0] = scores(0)
    for t in range(nq + 1):
        if t < nq:
            s_ref[(t + 1) % 2] = scores(t + 1)
        s = s_ref[t % 2]
        if 1 <= t <= nq // 2:
            first_b = t == p + 1
            m = jnp.where(first_b, NEG_INF, m)
            acc = jnp.where(first_b, 0.0, acc)
        v_blk = vaug_ref[key_rows(t), :]
        m_parts, acc_parts = [], []
        for r0 in (0, tq // 2):
            rows = slice(r0, r0 + tq // 2)
            m_new = jnp.maximum(m[rows], jnp.max(s[rows], axis=-1, keepdims=True))
            acc_parts.append(jnp.exp2(m[rows] - m_new) * acc[rows] + jnp.dot(
                jnp.exp2(s[rows] - m_new).astype(v_blk.dtype), v_blk,
                preferred_element_type=F32))
            m_parts.append(m_new)
        m = jnp.concatenate(m_parts, axis=0)
        acc = jnp.concatenate(acc_parts, axis=0)
        if t == 0:
            o_ref[:tq, :] = (acc[:, :dh] / acc[:, dh:]).astype(o_ref.dtype)
        elif t < nq // 2:
            out = (acc[:, :dh] / acc[:, dh:]).astype(o_ref.dtype)
            o_ref[:tq, :] = jnp.where(t <= p, out, o_ref[:tq, :])
    o_ref[tq:, :] = (acc[:, :dh] / acc[:, dh:]).astype(o_ref.dtype)


def _fox_attention(qkv, cp, bsz, seq, heads, tq):
    dh = ATTN_HEAD_DIM
    tq = min(tq, seq // 2)
    nq = seq // tq
    half = nq // 2
    assert nq % 2 == 0

    def position(i):
        b, qi = i // nq, i % nq
        return jnp.where(qi < half, (b * half + qi) * 2, (b * half + nq - 1 - qi) * 2 + 1)

    out = pl.pallas_call(
        functools.partial(_fox_body, tq=tq, seq=seq),
        grid=(bsz, heads, half),
        in_specs=[pl.BlockSpec((tq, dh), lambda b, h, p: (b * nq + p, h)),
                  pl.BlockSpec((tq, dh), lambda b, h, p: (b * nq + nq - 1 - p, h)),
                  pl.BlockSpec((seq, dh), lambda b, h, p: (b, heads + h)),
                  pl.BlockSpec((seq, dh), lambda b, h, p: (b, 2 * heads + h)),
                  pl.BlockSpec((seq, dh), lambda b, h, p: (b, h))],
        out_specs=pl.BlockSpec((2 * tq, dh), lambda b, h, p: (b * half + p, h)),
        out_shape=jax.ShapeDtypeStruct((bsz * seq, heads * dh), BF16),
        scratch_shapes=[pltpu.VMEM((seq, 2 * dh), BF16), pltpu.VMEM((seq, 2 * dh), BF16),
                        pltpu.VMEM((2, tq, tq), F32)],
        compiler_params=_cparams("parallel", "parallel", "arbitrary"),
        name="fox_attention",
    )(qkv, qkv, qkv, qkv, cp)
    return out, tq, position


def _xattn_block_body(x_ref, gx_ref, wq_ref, kv_ref, wo_ref, gn_ref, xo_ref, hn_ref):
    x = x_ref[...]
    width = wq_ref.shape[1]
    dh = width // XATTN_HEADS
    hx = _rms_scale(x, gx_ref[...]).astype(wq_ref.dtype)
    q = jnp.dot(hx, wq_ref[...], preferred_element_type=F32).astype(kv_ref.dtype)
    heads = []
    for hd in range(XATTN_HEADS):
        k = kv_ref[:, hd * dh:(hd + 1) * dh]
        v = kv_ref[:, width + hd * dh:width + (hd + 1) * dh]
        s = lax.dot_general(q[:, hd * dh:(hd + 1) * dh], k, (((1,), (1,)), ((), ())),
                            preferred_element_type=F32)
        p = jnp.exp(s - jnp.max(s, axis=-1, keepdims=True))
        l = jnp.sum(p, axis=-1, keepdims=True)
        o = jnp.dot(p.astype(v.dtype), v, preferred_element_type=F32) / l
        heads.append(o.astype(wo_ref.dtype))
    x2 = x + jnp.dot(jnp.concatenate(heads, axis=1), wo_ref[...], preferred_element_type=F32)
    xo_ref[...] = x2
    hn_ref[...] = _rms_scale(x2, gn_ref[...]).astype(hn_ref.dtype)


def _xattn_block(x, g_x, wq, kv, wo, g_next, seq, mem_len, tm):
    t, d = x.shape
    width = wq.shape[1]
    tm = min(tm, seq)
    nq = seq // tm
    row = pl.BlockSpec((tm, d), lambda i: (i, 0))
    vec = pl.BlockSpec((1, d), lambda i: (0, 0))
    resident = pl.Buffered(1)
    return pl.pallas_call(
        _xattn_block_body,
        grid=(t // tm,),
        in_specs=[row, vec,
                  pl.BlockSpec((d, width), lambda i: (0, 0), pipeline_mode=resident),
                  pl.BlockSpec((mem_len, 2 * width), lambda i: (i // nq, 0)),
                  pl.BlockSpec((width, d), lambda i: (0, 0), pipeline_mode=resident),
                  vec],
        out_specs=[row, row],
        out_shape=[jax.ShapeDtypeStruct((t, d), F32), jax.ShapeDtypeStruct((t, d), BF16)],
        compiler_params=_cparams("parallel"),
        name="xattn_block",
    )(x, g_x.reshape(1, d).astype(F32), wq, kv, wo, g_next.reshape(1, d).astype(F32))


def _s5_operators_body(bre_ref, bim_ref, cre_ref, cim_ref, p_ref, pt_ref, d_ref,
                        t_ref, w_ref, v_ref, *, L):
    hp = lax.Precision.HIGHEST
    bre, bim = bre_ref[...], bim_ref[...]
    cre, cim = cre_ref[...], cim_ref[...]
    k = bre.shape[1]
    row = lax.broadcasted_iota(jnp.int32, (LANES, LANES), 0)
    col = lax.broadcasted_iota(jnp.int32, (LANES, LANES), 1)
    lag_blocks = []
    for tau in range(L):
        pr, pi = p_ref[0, tau:tau + 1, :], p_ref[1, tau:tau + 1, :]
        xr, xi = bre * pr - bim * pi, bre * pi + bim * pr
        rows = slice((L - 1 - tau) * LANES, (L - tau) * LANES)
        w_ref[rows, :k] = xr.astype(w_ref.dtype)
        w_ref[rows, k:] = xi.astype(w_ref.dtype)
        d_tau = (jnp.dot(xr, cre, preferred_element_type=F32, precision=hp)
                 - jnp.dot(xi, cim, preferred_element_type=F32, precision=hp))
        if tau == 0:
            d_tau = d_tau + jnp.where(row == col, d_ref[...], 0.0)
        lag_blocks.append(d_tau.astype(t_ref.dtype))
    zero = jnp.zeros((LANES, LANES), t_ref.dtype)
    for s in range(L):
        for t in range(L):
            t_ref[s * LANES:(s + 1) * LANES, t * LANES:(t + 1) * LANES] = (
                lag_blocks[t - s] if t >= s else zero)
    for t in range(L):
        qr, qi = pt_ref[0, :, t + 1:t + 2], pt_ref[1, :, t + 1:t + 2]
        cols = slice(t * LANES, (t + 1) * LANES)
        v_ref[:k, cols] = (cre * qr - cim * qi).astype(v_ref.dtype)
        v_ref[k:, cols] = (-(cre * qi + cim * qr)).astype(v_ref.dtype)


def _s5_operators(A_re, A_im, log_dt, B_re, B_im, C_re, C_im, D_skip):
    n_g, n_p = A_re.shape
    n_h = B_re.shape[-1]
    L = S5_CHUNK
    gs = LANES // n_h
    n_slab = n_g // gs
    k = gs * n_p
    a_re, a_im = A_re.astype(F32), A_im.astype(F32)
    dt = jnp.exp(log_dt.astype(F32))[:, None]
    tau = jnp.arange(L + 1, dtype=F32)[:, None, None]
    mag = jnp.exp(tau * (dt * a_re))
    ang = tau * (dt * a_im)
    pw = jnp.stack([mag * jnp.cos(ang), mag * jnp.sin(ang)])
    lb_re, lb_im = pw[0, 1], pw[1, 1]
    den = a_re * a_re + a_im * a_im
    nr, ni = lb_re - 1.0, lb_im
    f_re = (nr * a_re + ni * a_im) / den
    f_im = (ni * a_re - nr * a_im) / den
    br, bi = B_re.astype(F32), B_im.astype(F32)
    bb_re = f_re[..., None] * br - f_im[..., None] * bi
    bb_im = f_re[..., None] * bi + f_im[..., None] * br

    eye = jnp.eye(gs, dtype=F32)

    def slab_b(x):
        x = x.reshape(n_slab, gs, n_p, n_h).transpose(0, 1, 3, 2)
        return (x[:, :, :, None, :] * eye[None, :, None, :, None]).reshape(n_slab, LANES, k)

    def slab_c(x):
        x = x.reshape(n_slab, gs, n_h, n_p).transpose(0, 1, 3, 2)
        return (x[:, :, :, None, :] * eye[None, :, None, :, None]).reshape(n_slab, k, LANES)

    p_tab = pw.reshape(2, L + 1, n_slab, k).transpose(2, 0, 1, 3)
    pt_tab = jnp.pad(pw.reshape(2, L + 1, n_slab, k).transpose(2, 0, 3, 1),
                     ((0, 0), (0, 0), (0, 0), (0, LANES - (L + 1))))
    mat_b = pl.BlockSpec((None, LANES, k), lambda j: (j, 0, 0))
    mat_c = pl.BlockSpec((None, k, LANES), lambda j: (j, 0, 0))
    t_op, w_op, v_op = pl.pallas_call(
        functools.partial(_s5_operators_body, L=L),
        grid=(n_slab,),
        in_specs=[mat_b, mat_b, mat_c, mat_c,
                  pl.BlockSpec((None, 2, L + 1, k), lambda j: (j, 0, 0, 0)),
                  pl.BlockSpec((None, 2, k, LANES), lambda j: (j, 0, 0, 0)),
                  pl.BlockSpec((1, LANES), lambda j: (0, j))],
        out_specs=[pl.BlockSpec((None, L * LANES, L * LANES), lambda j: (j, 0, 0)),
                   pl.BlockSpec((None, L * LANES, 2 * k), lambda j: (j, 0, 0)),
                   pl.BlockSpec((None, 2 * k, L * LANES), lambda j: (j, 0, 0))],
        out_shape=[jax.ShapeDtypeStruct((n_slab, L * LANES, L * LANES), BF16),
                   jax.ShapeDtypeStruct((n_slab, L * LANES, 2 * k), BF16),
                   jax.ShapeDtypeStruct((n_slab, 2 * k, L * LANES), BF16)],
        compiler_params=_cparams("parallel"),
        name="s5_operators",
    )(slab_b(bb_re), slab_b(bb_im), slab_c(C_re.astype(F32)), slab_c(C_im.astype(F32)),
      p_tab, pt_tab, D_skip.astype(F32).reshape(1, n_g * n_h))

    a_op = pw[:, L].reshape(2, n_slab, k).transpose(1, 0, 2).reshape(1, n_slab * 2 * k)
    return t_op, w_op, v_op, a_op


def _chunk_rows(piece_refs):
    return jnp.concatenate([r[...] for r in piece_refs], axis=1)


def _s5_increment_body(*refs):
    w_ref, z_ref = refs[-2], refs[-1]
    z_ref[...] = jnp.dot(_chunk_rows(refs[:-2]), w_ref[...], preferred_element_type=F32)


def _s5_scan_body(z_ref, a_ref, o_ref, st_ref, *, tc):
    @pl.when(pl.program_id(2) == 0)
    def _():
        st_ref[...] = jnp.zeros_like(st_ref)

    half = z_ref.shape[1] // 2
    a_re, a_im = a_ref[:, :half], a_ref[:, half:]

    def step(i, carry):
        re, im = carry
        base = pl.multiple_of(i * SUBLANES, SUBLANES)
        inc = z_ref[pl.ds(base, SUBLANES), :]
        before_re, before_im = [], []
        for r in range(SUBLANES):
            before_re.append(re)
            before_im.append(im)
            re, im = (a_re * re - a_im * im + inc[r:r + 1, :half],
                      a_re * im + a_im * re + inc[r:r + 1, half:])
        o_ref[pl.ds(base, SUBLANES), :half] = jnp.concatenate(before_re, axis=0)
        o_ref[pl.ds(base, SUBLANES), half:] = jnp.concatenate(before_im, axis=0)
        return re, im

    re, im = lax.fori_loop(0, tc // SUBLANES, step, (st_ref[0:1, :half], st_ref[0:1, half:]))
    st_ref[0:1, :half] = re
    st_ref[0:1, half:] = im


def _s5_output_body(*refs):
    x_ref, t_ref, v_ref, o_ref = refs[-4:]
    u = _chunk_rows(refs[:-4])
    half = u.shape[1] // 2
    y = jnp.concatenate(
        [jnp.dot(u[:, :half], t_ref[:half, :half], preferred_element_type=F32),
         jnp.dot(u, t_ref[:, half:], preferred_element_type=F32)], axis=1)
    y = y + jnp.dot(x_ref[...].astype(BF16), v_ref[...], preferred_element_type=F32)
    o_ref[...] = jax.nn.gelu(y).astype(o_ref.dtype)


def _s5_glu_body(*refs):
    w_ref, b_ref, o_ref = refs[-3:]
    y = _chunk_rows(refs[:-3])
    gate = _sigmoid(jnp.dot(y, w_ref[...], preferred_element_type=F32) + b_ref[...])
    o_ref[...] = (y.astype(F32) * gate).astype(o_ref.dtype)


def _chunked_proj_body(h_ref, w_ref, side_in_ref, u_ref, side_out_ref, acc_ref, *, L):
    acc = jnp.dot(h_ref[...], w_ref[...], preferred_element_type=F32)
    n_chunks, width = u_ref.shape[0], w_ref.shape[1]
    for c in range(width // LANES):
        acc_ref[c] = acc[:, c * LANES:(c + 1) * LANES]
    for tau in range(L):
        for c in range(width // LANES):
            u_ref[:, tau * width + c * LANES:tau * width + (c + 1) * LANES] = (
                acc_ref[c, pl.ds(tau, n_chunks, stride=L), :].astype(u_ref.dtype))
    side_out_ref[...] = side_in_ref[...].astype(side_out_ref.dtype)


def _chunked_proj(h, w, side_cast, tm):
    m, k = h.shape
    width = w.shape[1]
    L = S5_CHUNK
    steps = m // tm
    slab = side_cast.shape[0] // steps
    assert slab * steps == side_cast.shape[0] and slab % BF16_SUBLANES == 0 and tm % L == 0
    side_spec = pl.BlockSpec((slab, side_cast.shape[1]), lambda i: (i, 0))
    chunk_spec = pl.BlockSpec((tm // L, L * width), lambda i: (i, 0))
    return pl.pallas_call(
        functools.partial(_chunked_proj_body, L=L),
        grid=(steps,),
        in_specs=[pl.BlockSpec((tm, k), lambda i: (i, 0)), pl.BlockSpec((k, width), lambda i: (0, 0)),
                  side_spec],
        out_specs=[chunk_spec, side_spec],
        out_shape=[jax.ShapeDtypeStruct((m // L, L * width), BF16),
                   jax.ShapeDtypeStruct(side_cast.shape, BF16)],
        scratch_shapes=[pltpu.VMEM((width // LANES, tm, LANES), F32)],
        compiler_params=_cparams("parallel"),
        name="ssm_in_proj",
    )(h, w, side_cast)


def _s5_branch(uu, ops, w_glu, b_glu, bsz, seq):
    t_op, w_op, v_op, a_op = ops
    n_slab = t_op.shape[0]
    L = S5_CHUNK
    width = uu.shape[1] // L
    cw, sw = t_op.shape[1], w_op.shape[2]
    nc = bsz * seq // L
    ncb = seq // L
    tm = min(TILE_ROWS, nc)
    pieces = [pl.BlockSpec((tm, LANES), lambda j, i, s=s: (i, s * n_slab + j)) for s in range(L)]

    z = pl.pallas_call(
        _s5_increment_body,
        grid=(n_slab, nc // tm),
        in_specs=pieces + [pl.BlockSpec((None, cw, sw), lambda j, i: (j, 0, 0))],
        out_specs=pl.BlockSpec((tm, sw), lambda j, i: (i, j)),
        out_shape=jax.ShapeDtypeStruct((nc, n_slab * sw), F32),
        compiler_params=_cparams("parallel", "parallel"),
        name="s5_increment",
    )(*([uu] * L), w_op)

    tc = min(S5_SCAN_ROWS, ncb)
    nt = ncb // tc
    blk = pl.BlockSpec((tc, sw), lambda b, j, t: (b * nt + t, j))
    xprev = pl.pallas_call(
        functools.partial(_s5_scan_body, tc=tc),
        grid=(bsz, n_slab, nt),
        in_specs=[blk, pl.BlockSpec((1, sw), lambda b, j, t: (0, j))],
        out_specs=blk,
        out_shape=jax.ShapeDtypeStruct((nc, n_slab * sw), F32),
        scratch_shapes=[pltpu.VMEM((SUBLANES, sw), F32)],
        compiler_params=_cparams("parallel", "parallel", "arbitrary"),
        name="s5_scan",
    )(z, a_op)

    ys = pl.pallas_call(
        _s5_output_body,
        grid=(n_slab, nc // tm),
        in_specs=pieces + [pl.BlockSpec((tm, sw), lambda j, i: (i, j)),
                           pl.BlockSpec((None, cw, cw), lambda j, i: (j, 0, 0)),
                           pl.BlockSpec((None, sw, cw), lambda j, i: (j, 0, 0))],
        out_specs=pl.BlockSpec((tm, cw), lambda j, i: (i, j)),
        out_shape=jax.ShapeDtypeStruct((nc, n_slab * cw), BF16),
        compiler_params=_cparams("parallel", "parallel"),
        name="s5_output",
    )(*([uu] * L), xprev, t_op, v_op)

    tok = [pl.BlockSpec((tm, LANES), lambda i, t, j=j: (i, j * L + t)) for j in range(n_slab)]
    out = pl.pallas_call(
        _s5_glu_body,
        grid=(nc // tm, L),
        in_specs=tok + [pl.BlockSpec((width, width), lambda i, t: (0, 0)),
                        pl.BlockSpec((1, width), lambda i, t: (0, 0))],
        out_specs=pl.BlockSpec((tm, width), lambda i, t: (i, t)),
        out_shape=jax.ShapeDtypeStruct((nc, L * width), BF16),
        compiler_params=_cparams("parallel", "parallel"),
        name="s5_glu",
    )(*([ys] * n_slab), w_glu.astype(BF16), b_glu.reshape(1, width).astype(F32))
    return out.reshape(bsz * seq, width)


def _layer(x, mem_n, p, bsz, seq, mem_len):
    d = x.shape[1]
    aw = p["w_attn_up"].shape[0]
    heads = aw // ATTN_HEAD_DIM
    sw = p["w_ssm_up"].shape[0]
    off_f = 3 * aw
    off_u = off_f + heads
    off_g = off_u + sw

    w_in_t = p["w_in"].T
    tile = TILE_GATE_COLS
    w_qkv = _transpose_cast(w_in_t, 0, off_f, 2 * tile, lambda j: j, tk=d,
                            scale=LOG2E * ATTN_HEAD_DIM ** -0.5, n_scaled=aw // (2 * tile))
    w_f = jnp.pad(w_in_t[off_f:off_u].T, ((0, 0), (0, LANES - heads))).astype(BF16)
    b_f = jnp.pad(p["b_f"].astype(F32), (0, LANES - heads)).reshape(1, LANES)
    h, log_f = _rmsnorm(x, p["g_mix"], BF16, tm=TILE_ROWS, forget=(w_f, b_f))
    w_u = _transpose_cast(w_in_t, off_u, sw, tile, lambda j: j, tk=d)
    tn_mix = tile
    n_mix = d // tn_mix
    w_g = _transpose_cast(w_in_t, off_g, N_BRANCH * d, tn_mix,
                          lambda j: (j % n_mix) * N_BRANCH + j // n_mix, tk=d)

    qkv = _fused_matmul([(h, w_qkv, 0)], _ep_plain, BF16, n=3 * aw, tm=TILE_MM, tn=TILE_MM,
                        name="qkv_proj")
    fox, fox_rows, fox_position = _fox_attention(qkv, _forget_bias(log_f, bsz, seq, heads),
                                                 bsz, seq, heads, tq=TILE_MM)

    u, w_out = _chunked_proj(h, w_u, p["w_out"], tm=TILE_MM)
    ops = _s5_operators(p["A_re"], p["A_im"], p["log_dt"], p["B_re"], p["B_im"],
                        p["C_re"], p["C_im"], p["D_skip"])
    y = _s5_branch(u, ops, p["w_glu"], p["b_glu"], bsz, seq)

    b_gate = (p["b_gate"].astype(F32).reshape(N_BRANCH, n_mix, tn_mix).transpose(1, 0, 2)
              .reshape(1, N_BRANCH * d))
    merged, w_ff1 = _fused_matmul(
        [(h, w_g, 0, N_BRANCH), (fox, p["w_attn_up"].astype(BF16), 0, 1, fox_position),
         (y, p["w_ssm_up"].astype(BF16), 0)],
        _ep_gated_merge, BF16, n=d, tm=fox_rows, tn=tn_mix,
        aux=[(b_gate, 0, N_BRANCH)], side_cast=p["w_ff1"], name="gated_merge")
    x = _fused_matmul([(merged, w_out, 0)], _ep_add_res, F32, n=d,
                      tm=TILE_MM, tn=TILE_MM, res=x, name="mixer_out_proj")

    xw = p["wq_x"].shape[1]
    wq = (p["wq_x"] * (xw // XATTN_HEADS) ** -0.5).astype(BF16)
    w_kv = jnp.concatenate([p["wk_x"], p["wv_x"]], axis=1).astype(BF16)
    kv = _fused_matmul([(mem_n, w_kv, 0)], _ep_plain, BF16, n=2 * xw, tm=TILE_ROWS, tn=TILE_MM,
                       name="xattn_kv_proj")
    x, hm = _xattn_block(x, p["g_xattn"], wq, kv, p["wo_x"].astype(BF16), p["g_mlp"],
                         seq, mem_len, tm=TILE_XATTN_ROWS)

    dff = p["w_ff1"].shape[1]
    hid, w_ff2 = _fused_matmul([(hm, w_ff1, 0)], _ep_relu2, BF16, n=dff, tm=TILE_MM, tn=TILE_MM,
                               side_cast=p["w_ff2"], name="mlp_up")
    x = _matmul_kpieces(hid, w_ff2, x, n_pieces=MLP_DOWN_K_PIECES, tm=TILE_ROWS,
                        tn=MLP_DOWN_COLS, name="mlp_down")
    return x


_LAYER_PARAMS = ("g_mix", "w_in", "b_f", "b_gate", "A_re", "A_im", "log_dt", "B_re", "B_im",
                 "C_re", "C_im", "D_skip", "w_glu", "b_glu", "w_attn_up", "w_ssm_up", "w_out",
                 "g_xattn", "g_mem", "wq_x", "wk_x", "wv_x", "wo_x", "g_mlp", "w_ff1", "w_ff2")


def kernel(x, mem, g_mix, w_in, b_f, b_gate, A_re, A_im, log_dt, B_re, B_im, C_re, C_im, D_skip, w_glu, b_glu, w_attn_up, w_ssm_up, w_out, g_xattn, g_mem, wq_x, wk_x, wv_x, wo_x, g_mlp, w_ff1, w_ff2, g_final):
    stacked = dict(zip(_LAYER_PARAMS, (g_mix, w_in, b_f, b_gate, A_re, A_im, log_dt, B_re, B_im,
                                       C_re, C_im, D_skip, w_glu, b_glu, w_attn_up, w_ssm_up,
                                       w_out, g_xattn, g_mem, wq_x, wk_x, wv_x, wo_x, g_mlp,
                                       w_ff1, w_ff2)))
    bsz, seq, d = x.shape
    mem_len = mem.shape[1]
    xt = x.reshape(bsz * seq, d)
    mem2 = mem.reshape(bsz * mem_len, d)
    for l in range(g_mix.shape[0]):
        p = {k: v[l] for k, v in stacked.items()}
        mem_n = _rmsnorm(mem2, p["g_mem"], BF16, tm=TILE_XATTN_ROWS)
        xt = _layer(xt, mem_n, p, bsz, seq, mem_len)
    out = _rmsnorm(xt, g_final, x.dtype, tm=TILE_ROWS)
    return out.reshape(bsz, seq, d)
```

```python
import functools
import math

import jax
import jax.numpy as jnp
from jax import lax
from jax.experimental import pallas as pl
from jax.experimental.pallas import tpu as pltpu

F32 = jnp.float32
BF16 = jnp.bfloat16

V7X_VMEM_LIMIT_BYTES = 56 * 1024 * 1024
LANES = 128
SUBLANES = 8
BF16_SUBLANES = 16

EPS = 1e-6
NEG_INF = -1e30
LOG2E = math.log2(math.e)
N_BIAS_PIECES = 3
ATTN_HEAD_DIM = 128
S5_CHUNK = 16
XATTN_HEADS = 4
N_BRANCH = 2

TILE_MM = 1024
TILE_GATE_COLS = 256
TILE_ROWS = 512
TILE_XATTN_ROWS = 256
MLP_DOWN_COLS = 256
MLP_DOWN_K_PIECES = 4
FORGET_BLOCK = 256
S5_SCAN_ROWS = 128


def _cparams(*sem):
    return pltpu.CompilerParams(dimension_semantics=sem,
                                vmem_limit_bytes=V7X_VMEM_LIMIT_BYTES)


def _rms_scale(x, g):
    return (x * lax.rsqrt(jnp.mean(x * x, axis=-1, keepdims=True) + EPS)) * g


def _log_sigmoid(z):
    return jnp.minimum(z, 0.0) - jnp.log1p(jnp.exp(-jnp.abs(z)))


def _sigmoid(z):
    return 0.5 * jnp.tanh(0.5 * z) + 0.5


def _rmsnorm_body(x_ref, g_ref, *rest, has_proj):
    h = _rms_scale(x_ref[...].astype(F32), g_ref[...]).astype(rest[-1 - has_proj].dtype)
    if has_proj:
        w_ref, b_ref, o_ref, f_ref = rest
        f_ref[...] = _log_sigmoid(jnp.dot(h, w_ref[...], preferred_element_type=F32) + b_ref[...])
    else:
        o_ref, = rest
    o_ref[...] = h


def _rmsnorm(x, g, out_dtype, tm, forget=None):
    m, d = x.shape
    tm = min(tm, m)
    row = pl.BlockSpec((tm, d), lambda i: (i, 0))
    operands = [x, g.reshape(1, d).astype(F32)]
    in_specs = [row, pl.BlockSpec((1, d), lambda i: (0, 0))]
    out_specs, out_shape = [row], [jax.ShapeDtypeStruct((m, d), out_dtype)]
    if forget is not None:
        n = forget[0].shape[1]
        operands += list(forget)
        in_specs += [pl.BlockSpec((d, n), lambda i: (0, 0)), pl.BlockSpec((1, n), lambda i: (0, 0))]
        out_specs.append(pl.BlockSpec((tm, n), lambda i: (i, 0)))
        out_shape.append(jax.ShapeDtypeStruct((m, n), F32))
    outs = pl.pallas_call(
        functools.partial(_rmsnorm_body, has_proj=forget is not None),
        grid=(m // tm,),
        in_specs=in_specs,
        out_specs=out_specs,
        out_shape=out_shape,
        compiler_params=_cparams("parallel"),
        name="rmsnorm",
    )(*operands)
    return outs if forget is not None else outs[0]


def _mm_body(*refs, n_pairs, n_aux, has_res, has_side, epilogue):
    def product(p):
        return jnp.dot(refs[2 * p][...], refs[2 * p + 1][...], preferred_element_type=F32)

    pos = 2 * n_pairs
    aux = [refs[pos + i][...] for i in range(n_aux)]
    pos += n_aux
    res = refs[pos][...].astype(F32) if has_res else None
    pos += has_res
    if has_side:
        refs[-1][...] = refs[pos][...].astype(refs[-1].dtype)
        pos += 1
    o_ref = refs[pos]
    o_ref[...] = epilogue(product, aux, res).astype(o_ref.dtype)


def _fused_matmul(pairs, epilogue, out_dtype, *, n, tm, tn, aux=(), res=None, side_cast=None,
                  name="fused_matmul"):
    m = pairs[0][0].shape[0]
    tm, tn = min(tm, m), min(tn, n)
    n_aux, has_res, has_side = len(aux), res is not None, side_cast is not None
    nj = n // tn
    operands, in_specs = [], []
    for lhs, rhs, off, *opt in pairs:
        kp = lhs.shape[1]
        width = tn * (opt[0] if opt else 1)
        row_block = opt[1] if len(opt) > 1 else (lambda i: i)
        operands += [lhs, rhs]
        in_specs += [pl.BlockSpec((tm, kp), lambda i, j, rb=row_block: (rb(i), 0)),
                     pl.BlockSpec((kp, width), lambda i, j, off=off: (0, j + off))]
    for vec, off, *mult in aux:
        width = tn * (mult[0] if mult else 1)
        operands.append(vec)
        in_specs.append(pl.BlockSpec((1, width), lambda i, j, off=off: (0, j + off)))
    if has_res:
        operands.append(res)
        in_specs.append(pl.BlockSpec((tm, tn), lambda i, j: (i, j)))
    out_specs = [pl.BlockSpec((tm, tn), lambda i, j: (i, j))]
    out_shape = [jax.ShapeDtypeStruct((m, n), out_dtype)]
    if has_side:
        slab = side_cast.shape[0] // ((m // tm) * nj)
        assert slab * (m // tm) * nj == side_cast.shape[0] and slab % BF16_SUBLANES == 0
        side_spec = pl.BlockSpec((slab, side_cast.shape[1]), lambda i, j: (i * nj + j, 0))
        operands.append(side_cast)
        in_specs.append(side_spec)
        out_specs.append(side_spec)
        out_shape.append(jax.ShapeDtypeStruct(side_cast.shape, BF16))
    outs = pl.pallas_call(
        functools.partial(_mm_body, n_pairs=len(pairs), n_aux=n_aux, has_res=has_res,
                          has_side=has_side, epilogue=epilogue),
        grid=(m // tm, nj),
        in_specs=in_specs,
        out_specs=out_specs,
        out_shape=out_shape,
        compiler_params=_cparams("parallel", "parallel"),
        name=name,
    )(*operands)
    return outs if has_side else outs[0]


W_IN_EDGE_ROWS = 16


def _transpose_cast_body(a_ref, b_ref, o_ref, *, shift, scale, n_scaled):
    rows = a_ref.shape[0]
    x = jnp.concatenate([a_ref[...], b_ref[...]], axis=0)[shift:shift + rows, :]
    if n_scaled:
        x = x * jnp.where(pl.program_id(0) < n_scaled, scale, 1.0)
    o_ref[...] = x.T.astype(o_ref.dtype)


def _transpose_cast(wt, start, n_cols, tile, out_block, tk, scale=1.0, n_scaled=0):
    k = wt.shape[1]
    base = start // tile * tile
    shift = start - base
    assert shift % SUBLANES == 0 and shift <= W_IN_EDGE_ROWS and n_cols % tile == 0
    assert start + n_cols + (W_IN_EDGE_ROWS - shift) <= wt.shape[0] or shift == 0
    edge_per_tile = tile // W_IN_EDGE_ROWS
    last_edge = wt.shape[0] // W_IN_EDGE_ROWS - 1
    return pl.pallas_call(
        functools.partial(_transpose_cast_body, shift=shift, scale=scale, n_scaled=n_scaled),
        grid=(n_cols // tile, k // tk),
        in_specs=[pl.BlockSpec((tile, tk), lambda j, c: (base // tile + j, c)),
                  pl.BlockSpec((W_IN_EDGE_ROWS, tk),
                               lambda j, c: (jnp.minimum((base // tile + j + 1) * edge_per_tile,
                                                         last_edge), c))],
        out_specs=pl.BlockSpec((tk, tile), lambda j, c: (c, out_block(j))),
        out_shape=jax.ShapeDtypeStruct((k, n_cols), BF16),
        compiler_params=_cparams("parallel", "parallel"),
        name="transpose_cast",
    )(wt, wt)


def _matmul_kpieces_body(*refs, n_pieces):
    res_ref, o_ref = refs[2 * n_pieces], refs[2 * n_pieces + 1]
    acc = res_ref[...].astype(F32)
    for p in range(n_pieces):
        acc = acc + jnp.dot(refs[p][...], refs[n_pieces + p][...], preferred_element_type=F32)
    o_ref[...] = acc.astype(o_ref.dtype)


def _matmul_kpieces(lhs, rhs, res, *, n_pieces, tm, tn, name):
    m, k = lhs.shape
    n = rhs.shape[1]
    kp = k // n_pieces
    tm, tn = min(tm, m), min(tn, n)
    in_specs = ([pl.BlockSpec((tm, kp), lambda i, j, p=p: (i, p)) for p in range(n_pieces)]
                + [pl.BlockSpec((kp, tn), lambda i, j, p=p: (p, j)) for p in range(n_pieces)]
                + [pl.BlockSpec((tm, tn), lambda i, j: (i, j))])
    return pl.pallas_call(
        functools.partial(_matmul_kpieces_body, n_pieces=n_pieces),
        grid=(m // tm, n // tn),
        in_specs=in_specs,
        out_specs=pl.BlockSpec((tm, tn), lambda i, j: (i, j)),
        out_shape=jax.ShapeDtypeStruct((m, n), res.dtype),
        compiler_params=_cparams("parallel", "parallel"),
        name=name,
    )(*([lhs] * n_pieces), *([rhs] * n_pieces), res)


def _ep_plain(product, aux, res):
    return product(0)


def _ep_add_res(product, aux, res):
    return res + product(0)


def _ep_gated_merge(product, aux, res):
    gates = _sigmoid(product(0) + aux[0])
    attn = product(1)
    tn = attn.shape[1]
    return gates[:, :tn] * attn + gates[:, tn:] * product(2)


def _ep_relu2(product, aux, res):
    r = jnp.maximum(product(0), 0.0)
    return r * r


def _forget_bias_body(x_ref, o_ref, carry_ref, *, blk, nblk, heads):
    @pl.when(pl.program_id(1) == 0)
    def _():
        carry_ref[...] = jnp.zeros_like(carry_ref)

    r = lax.broadcasted_iota(jnp.int32, (blk, blk), 0)
    c = lax.broadcasted_iota(jnp.int32, (blk, blk), 1)
    tri = (c <= r).astype(F32)
    lane = lax.broadcasted_iota(jnp.int32, (blk, LANES), 1)

    def step(i, carry):
        base = pl.multiple_of(i * blk, blk)
        cs = jnp.dot(tri, x_ref[pl.ds(base, blk), :], preferred_element_type=F32,
                     precision=lax.Precision.HIGHEST) + carry
        for h in range(heads):
            bias = jnp.broadcast_to(cs[:, h:h + 1] * (-LOG2E), (blk, LANES))
            hi = bias.astype(BF16).astype(F32)
            mid = (bias - hi).astype(BF16).astype(F32)
            lo = bias - hi - mid
            pieces = jnp.where(lane == 0, hi, jnp.where(lane == 1, mid,
                                                        jnp.where(lane == 2, lo, 0.0)))
            o_ref[pl.ds(base, blk), h * LANES:(h + 1) * LANES] = pieces.astype(o_ref.dtype)
        return cs[blk - 1:blk, :]

    carry_ref[0:1, :] = lax.fori_loop(0, nblk, step, carry_ref[0:1, :])


def _forget_bias(log_f, bsz, seq, heads):
    blk = min(FORGET_BLOCK, seq)
    tt = min(TILE_MM, seq)
    nt = seq // tt
    return pl.pallas_call(
        functools.partial(_forget_bias_body, blk=blk, nblk=tt // blk, heads=heads),
        grid=(bsz, nt),
        in_specs=[pl.BlockSpec((tt, LANES), lambda b, t: (b * nt + t, 0))],
        out_specs=pl.BlockSpec((tt, heads * LANES), lambda b, t: (b * nt + t, 0)),
        out_shape=jax.ShapeDtypeStruct((bsz * seq, heads * LANES), BF16),
        scratch_shapes=[pltpu.VMEM((SUBLANES, LANES), F32)],
        compiler_params=_cparams("parallel", "arbitrary"),
        name="forget_bias",
    )(log_f)


def _fox_body(qa_ref, qb_ref, k_ref, v_ref, cp_ref, o_ref, kaug_ref, vaug_ref, s_ref,
              *, tq, seq):
    p = pl.program_id(2)
    nq = seq // tq
    dh = qa_ref.shape[1]

    @pl.when(p == 0)
    def _():
        def fill(j, _):
            rows = pl.ds(pl.multiple_of(j * tq, tq), tq)
            kaug_ref[rows, :dh] = k_ref[rows, :]
            kaug_ref[rows, dh:] = cp_ref[rows, :]
            vaug_ref[rows, :dh] = v_ref[rows, :]
            vaug_ref[rows, dh:] = jnp.ones((tq, dh), vaug_ref.dtype)
            return 0
        lax.fori_loop(0, nq, fill, 0)

    lane = lax.broadcasted_iota(jnp.int32, (tq, dh), 1)
    unit = (lane < N_BIAS_PIECES).astype(qa_ref.dtype)
    q_a = jnp.concatenate([qa_ref[...], unit], axis=1)
    q_b = jnp.concatenate([qb_ref[...], unit], axis=1)
    above_diag = (lax.broadcasted_iota(jnp.int32, (tq, tq), 1)
                  > lax.broadcasted_iota(jnp.int32, (tq, tq), 0))

    def key_rows(t):
        j = jnp.where(t <= p, t, t - (p + 1))
        return pl.ds(pl.multiple_of(j * tq, tq), tq)

    def scores(t):
        q = jnp.where(t <= p, q_a, q_b)
        s = lax.dot_general(q, kaug_ref[key_rows(t), :], (((1,), (1,)), ((), ())),
                            preferred_element_type=F32)
        if t == nq:
            s = jnp.where(above_diag, NEG_INF, s)
        elif t < nq // 2:
            s = jnp.where(jnp.logical_and(above_diag, t == p), NEG_INF, s)
        return s

    m = jnp.full((tq, 1), NEG_INF, F32)
    acc = jnp.zeros((tq, 2 * dh), F32)
    s_ref[0] = scores(0)
    for t in range(nq + 1):
        if t < nq:
            s_ref[(t + 1) % 2] = scores(t + 1)
        s = s_ref[t % 2]
        if 1 <= t <= nq // 2:
            first_b = t == p + 1
            m = jnp.where(first_b, NEG_INF, m)
            acc = jnp.where(first_b, 0.0, acc)
        m_new = jnp.maximum(m, jnp.max(s, axis=-1, keepdims=True))
        acc = jnp.exp2(m - m_new) * acc + jnp.dot(
            jnp.exp2(s - m_new).astype(vaug_ref.dtype), vaug_ref[key_rows(t), :],
            preferred_element_type=F32)
        m = m_new
        if t == 0:
            o_ref[:tq, :] = (acc[:, :dh] / acc[:, dh:]).astype(o_ref.dtype)
        elif t < nq // 2:
            out = (acc[:, :dh] / acc[:, dh:]).astype(o_ref.dtype)
            o_ref[:tq, :] = jnp.where(t <= p, out, o_ref[:tq, :])
    o_ref[tq:, :] = (acc[:, :dh] / acc[:, dh:]).astype(o_ref.dtype)


def _fox_attention(qkv, cp, bsz, seq, heads, tq):
    dh = ATTN_HEAD_DIM
    tq = min(tq, seq // 2)
    nq = seq // tq
    half = nq // 2
    assert nq % 2 == 0

    def position(i):
        b, qi = i // nq, i % nq
        return jnp.where(qi < half, (b * half + qi) * 2, (b * half + nq - 1 - qi) * 2 + 1)

    out = pl.pallas_call(
        functools.partial(_fox_body, tq=tq, seq=seq),
        grid=(bsz, heads, half),
        in_specs=[pl.BlockSpec((tq, dh), lambda b, h, p: (b * nq + p, h)),
                  pl.BlockSpec((tq, dh), lambda b, h, p: (b * nq + nq - 1 - p, h)),
                  pl.BlockSpec((seq, dh), lambda b, h, p: (b, heads + h)),
                  pl.BlockSpec((seq, dh), lambda b, h, p: (b, 2 * heads + h)),
                  pl.BlockSpec((seq, dh), lambda b, h, p: (b, h))],
        out_specs=pl.BlockSpec((2 * tq, dh), lambda b, h, p: (b * half + p, h)),
        out_shape=jax.ShapeDtypeStruct((bsz * seq, heads * dh), BF16),
        scratch_shapes=[pltpu.VMEM((seq, 2 * dh), BF16), pltpu.VMEM((seq, 2 * dh), BF16),
                        pltpu.VMEM((2, tq, tq), F32)],
        compiler_params=_cparams("parallel", "parallel", "arbitrary"),
        name="fox_attention",
    )(qkv, qkv, qkv, qkv, cp)
    return out, tq, position


def _xattn_block_body(x_ref, gx_ref, wq_ref, kv_ref, wo_ref, gn_ref, xo_ref, hn_ref):
    x = x_ref[...]
    width = wq_ref.shape[1]
    dh = width // XATTN_HEADS
    hx = _rms_scale(x, gx_ref[...]).astype(wq_ref.dtype)
    q = jnp.dot(hx, wq_ref[...], preferred_element_type=F32).astype(kv_ref.dtype)
    heads = []
    for hd in range(XATTN_HEADS):
        k = kv_ref[:, hd * dh:(hd + 1) * dh]
        v = kv_ref[:, width + hd * dh:width + (hd + 1) * dh]
        s = lax.dot_general(q[:, hd * dh:(hd + 1) * dh], k, (((1,), (1,)), ((), ())),
                            preferred_element_type=F32)
        p = jnp.exp(s - jnp.max(s, axis=-1, keepdims=True))
        l = jnp.sum(p, axis=-1, keepdims=True)
        o = jnp.dot(p.astype(v.dtype), v, preferred_element_type=F32) / l
        heads.append(o.astype(wo_ref.dtype))
    x2 = x + jnp.dot(jnp.concatenate(heads, axis=1), wo_ref[...], preferred_element_type=F32)
    xo_ref[...] = x2
    hn_ref[...] = _rms_scale(x2, gn_ref[...]).astype(hn_ref.dtype)


def _xattn_block(x, g_x, wq, kv, wo, g_next, seq, mem_len, tm):
    t, d = x.shape
    width = wq.shape[1]
    tm = min(tm, seq)
    nq = seq // tm
    row = pl.BlockSpec((tm, d), lambda i: (i, 0))
    vec = pl.BlockSpec((1, d), lambda i: (0, 0))
    resident = pl.Buffered(1)
    return pl.pallas_call(
        _xattn_block_body,
        grid=(t // tm,),
        in_specs=[row, vec,
                  pl.BlockSpec((d, width), lambda i: (0, 0), pipeline_mode=resident),
                  pl.BlockSpec((mem_len, 2 * width), lambda i: (i // nq, 0)),
                  pl.BlockSpec((width, d), lambda i: (0, 0), pipeline_mode=resident),
                  vec],
        out_specs=[row, row],
        out_shape=[jax.ShapeDtypeStruct((t, d), F32), jax.ShapeDtypeStruct((t, d), BF16)],
        compiler_params=_cparams("parallel"),
        name="xattn_block",
    )(x, g_x.reshape(1, d).astype(F32), wq, kv, wo, g_next.reshape(1, d).astype(F32))


def _s5_operators_body(bre_ref, bim_ref, cre_ref, cim_ref, p_ref, pt_ref, d_ref,
                        t_ref, w_ref, v_ref, *, L):
    hp = lax.Precision.HIGHEST
    bre, bim = bre_ref[...], bim_ref[...]
    cre, cim = cre_ref[...], cim_ref[...]
    k = bre.shape[1]
    row = lax.broadcasted_iota(jnp.int32, (LANES, LANES), 0)
    col = lax.broadcasted_iota(jnp.int32, (LANES, LANES), 1)
    lag_blocks = []
    for tau in range(L):
        pr, pi = p_ref[0, tau:tau + 1, :], p_ref[1, tau:tau + 1, :]
        xr, xi = bre * pr - bim * pi, bre * pi + bim * pr
        rows = slice((L - 1 - tau) * LANES, (L - tau) * LANES)
        w_ref[rows, :k] = xr.astype(w_ref.dtype)
        w_ref[rows, k:] = xi.astype(w_ref.dtype)
        d_tau = (jnp.dot(xr, cre, preferred_element_type=F32, precision=hp)
                 - jnp.dot(xi, cim, preferred_element_type=F32, precision=hp))
        if tau == 0:
            d_tau = d_tau + jnp.where(row == col, d_ref[...], 0.0)
        lag_blocks.append(d_tau.astype(t_ref.dtype))
    zero = jnp.zeros((LANES, LANES), t_ref.dtype)
    for s in range(L):
        for t in range(L):
            t_ref[s * LANES:(s + 1) * LANES, t * LANES:(t + 1) * LANES] = (
                lag_blocks[t - s] if t >= s else zero)
    for t in range(L):
        qr, qi = pt_ref[0, :, t + 1:t + 2], pt_ref[1, :, t + 1:t + 2]
        cols = slice(t * LANES, (t + 1) * LANES)
        v_ref[:k, cols] = (cre * qr - cim * qi).astype(v_ref.dtype)
        v_ref[k:, cols] = (-(cre * qi + cim * qr)).astype(v_ref.dtype)


def _s5_operators(A_re, A_im, log_dt, B_re, B_im, C_re, C_im, D_skip):
    n_g, n_p = A_re.shape
    n_h = B_re.shape[-1]
    L = S5_CHUNK
    gs = LANES // n_h
    n_slab = n_g // gs
    k = gs * n_p
    a_re, a_im = A_re.astype(F32), A_im.astype(F32)
    dt = jnp.exp(log_dt.astype(F32))[:, None]
    tau = jnp.arange(L + 1, dtype=F32)[:, None, None]
    mag = jnp.exp(tau * (dt * a_re))
    ang = tau * (dt * a_im)
    pw = jnp.stack([mag * jnp.cos(ang), mag * jnp.sin(ang)])
    lb_re, lb_im = pw[0, 1], pw[1, 1]
    den = a_re * a_re + a_im * a_im
    nr, ni = lb_re - 1.0, lb_im
    f_re = (nr * a_re + ni * a_im) / den
    f_im = (ni * a_re - nr * a_im) / den
    br, bi = B_re.astype(F32), B_im.astype(F32)
    bb_re = f_re[..., None] * br - f_im[..., None] * bi
    bb_im = f_re[..., None] * bi + f_im[..., None] * br

    eye = jnp.eye(gs, dtype=F32)

    def slab_b(x):
        x = x.reshape(n_slab, gs, n_p, n_h).transpose(0, 1, 3, 2)
        return (x[:, :, :, None, :] * eye[None, :, None, :, None]).reshape(n_slab, LANES, k)

    def slab_c(x):
        x = x.reshape(n_slab, gs, n_h, n_p).transpose(0, 1, 3, 2)
        return (x[:, :, :, None, :] * eye[None, :, None, :, None]).reshape(n_slab, k, LANES)

    p_tab = pw.reshape(2, L + 1, n_slab, k).transpose(2, 0, 1, 3)
    pt_tab = jnp.pad(pw.reshape(2, L + 1, n_slab, k).transpose(2, 0, 3, 1),
                     ((0, 0), (0, 0), (0, 0), (0, LANES - (L + 1))))
    mat_b = pl.BlockSpec((None, LANES, k), lambda j: (j, 0, 0))
    mat_c = pl.BlockSpec((None, k, LANES), lambda j: (j, 0, 0))
    t_op, w_op, v_op = pl.pallas_call(
        functools.partial(_s5_operators_body, L=L),
        grid=(n_slab,),
        in_specs=[mat_b, mat_b, mat_c, mat_c,
                  pl.BlockSpec((None, 2, L + 1, k), lambda j: (j, 0, 0, 0)),
                  pl.BlockSpec((None, 2, k, LANES), lambda j: (j, 0, 0, 0)),
                  pl.BlockSpec((1, LANES), lambda j: (0, j))],
        out_specs=[pl.BlockSpec((None, L * LANES, L * LANES), lambda j: (j, 0, 0)),
                   pl.BlockSpec((None, L * LANES, 2 * k), lambda j: (j, 0, 0)),
                   pl.BlockSpec((None, 2 * k, L * LANES), lambda j: (j, 0, 0))],
        out_shape=[jax.ShapeDtypeStruct((n_slab, L * LANES, L * LANES), BF16),
                   jax.ShapeDtypeStruct((n_slab, L * LANES, 2 * k), BF16),
                   jax.ShapeDtypeStruct((n_slab, 2 * k, L * LANES), BF16)],
        compiler_params=_cparams("parallel"),
        name="s5_operators",
    )(slab_b(bb_re), slab_b(bb_im), slab_c(C_re.astype(F32)), slab_c(C_im.astype(F32)),
      p_tab, pt_tab, D_skip.astype(F32).reshape(1, n_g * n_h))

    a_op = pw[:, L].reshape(2, n_slab, k).transpose(1, 0, 2).reshape(1, n_slab * 2 * k)
    return t_op, w_op, v_op, a_op


def _chunk_rows(piece_refs):
    return jnp.concatenate([r[...] for r in piece_refs], axis=1)


def _s5_increment_body(*refs):
    w_ref, z_ref = refs[-2], refs[-1]
    z_ref[...] = jnp.dot(_chunk_rows(refs[:-2]), w_ref[...], preferred_element_type=F32)


def _s5_scan_body(z_ref, a_ref, o_ref, st_ref, *, tc):
    @pl.when(pl.program_id(2) == 0)
    def _():
        st_ref[...] = jnp.zeros_like(st_ref)

    half = z_ref.shape[1] // 2
    a_re, a_im = a_ref[:, :half], a_ref[:, half:]

    def step(i, carry):
        re, im = carry
        base = pl.multiple_of(i * SUBLANES, SUBLANES)
        inc = z_ref[pl.ds(base, SUBLANES), :]
        before_re, before_im = [], []
        for r in range(SUBLANES):
            before_re.append(re)
            before_im.append(im)
            re, im = (a_re * re - a_im * im + inc[r:r + 1, :half],
                      a_re * im + a_im * re + inc[r:r + 1, half:])
        o_ref[pl.ds(base, SUBLANES), :half] = jnp.concatenate(before_re, axis=0)
        o_ref[pl.ds(base, SUBLANES), half:] = jnp.concatenate(before_im, axis=0)
        return re, im

    re, im = lax.fori_loop(0, tc // SUBLANES, step, (st_ref[0:1, :half], st_ref[0:1, half:]))
    st_ref[0:1, :half] = re
    st_ref[0:1, half:] = im


def _s5_output_body(*refs):
    x_ref, t_ref, v_ref, o_ref = refs[-4:]
    u = _chunk_rows(refs[:-4])
    half = u.shape[1] // 2
    xb = x_ref[...].astype(BF16)
    left = (jnp.dot(u[:, :half], t_ref[:half, :half], preferred_element_type=F32)
            + jnp.dot(xb, v_ref[:, :half], preferred_element_type=F32))
    o_ref[:, :half] = jax.nn.gelu(left).astype(o_ref.dtype)
    right = (jnp.dot(u, t_ref[:, half:], preferred_element_type=F32)
             + jnp.dot(xb, v_ref[:, half:], preferred_element_type=F32))
    o_ref[:, half:] = jax.nn.gelu(right).astype(o_ref.dtype)


def _s5_glu_body(*refs):
    w_ref, b_ref, o_ref = refs[-3:]
    y = _chunk_rows(refs[:-3])
    gate = _sigmoid(jnp.dot(y, w_ref[...], preferred_element_type=F32) + b_ref[...])
    o_ref[...] = (y.astype(F32) * gate).astype(o_ref.dtype)


def _chunked_proj_body(h_ref, w_ref, side_in_ref, u_ref, side_out_ref, acc_ref, *, L):
    acc = jnp.dot(h_ref[...], w_ref[...], preferred_element_type=F32)
    n_chunks, width = u_ref.shape[0], w_ref.shape[1]
    for c in range(width // LANES):
        acc_ref[c] = acc[:, c * LANES:(c + 1) * LANES]
    for tau in range(L):
        for c in range(width // LANES):
            u_ref[:, tau * width + c * LANES:tau * width + (c + 1) * LANES] = (
                acc_ref[c, pl.ds(tau, n_chunks, stride=L), :].astype(u_ref.dtype))
    side_out_ref[...] = side_in_ref[...].astype(side_out_ref.dtype)


def _chunked_proj(h, w, side_cast, tm):
    m, k = h.shape
    width = w.shape[1]
    L = S5_CHUNK
    steps = m // tm
    slab = side_cast.shape[0] // steps
    assert slab * steps == side_cast.shape[0] and slab % BF16_SUBLANES == 0 and tm % L == 0
    side_spec = pl.BlockSpec((slab, side_cast.shape[1]), lambda i: (i, 0))
    chunk_spec = pl.BlockSpec((tm // L, L * width), lambda i: (i, 0))
    return pl.pallas_call(
        functools.partial(_chunked_proj_body, L=L),
        grid=(steps,),
        in_specs=[pl.BlockSpec((tm, k), lambda i: (i, 0)), pl.BlockSpec((k, width), lambda i: (0, 0)),
                  side_spec],
        out_specs=[chunk_spec, side_spec],
        out_shape=[jax.ShapeDtypeStruct((m // L, L * width), BF16),
                   jax.ShapeDtypeStruct(side_cast.shape, BF16)],
        scratch_shapes=[pltpu.VMEM((width // LANES, tm, LANES), F32)],
        compiler_params=_cparams("parallel"),
        name="ssm_in_proj",
    )(h, w, side_cast)


def _s5_branch(uu, ops, w_glu, b_glu, bsz, seq):
    t_op, w_op, v_op, a_op = ops
    n_slab = t_op.shape[0]
    L = S5_CHUNK
    width = uu.shape[1] // L
    cw, sw = t_op.shape[1], w_op.shape[2]
    nc = bsz * seq // L
    ncb = seq // L
    tm = min(TILE_ROWS, nc)
    pieces = [pl.BlockSpec((tm, LANES), lambda j, i, s=s: (i, s * n_slab + j)) for s in range(L)]

    z = pl.pallas_call(
        _s5_increment_body,
        grid=(n_slab, nc // tm),
        in_specs=pieces + [pl.BlockSpec((None, cw, sw), lambda j, i: (j, 0, 0))],
        out_specs=pl.BlockSpec((tm, sw), lambda j, i: (i, j)),
        out_shape=jax.ShapeDtypeStruct((nc, n_slab * sw), F32),
        compiler_params=_cparams("parallel", "parallel"),
        name="s5_increment",
    )(*([uu] * L), w_op)

    tc = min(S5_SCAN_ROWS, ncb)
    nt = ncb // tc
    blk = pl.BlockSpec((tc, sw), lambda b, j, t: (b * nt + t, j))
    xprev = pl.pallas_call(
        functools.partial(_s5_scan_body, tc=tc),
        grid=(bsz, n_slab, nt),
        in_specs=[blk, pl.BlockSpec((1, sw), lambda b, j, t: (0, j))],
        out_specs=blk,
        out_shape=jax.ShapeDtypeStruct((nc, n_slab * sw), F32),
        scratch_shapes=[pltpu.VMEM((SUBLANES, sw), F32)],
        compiler_params=_cparams("parallel", "parallel", "arbitrary"),
        name="s5_scan",
    )(z, a_op)

    ys = pl.pallas_call(
        _s5_output_body,
        grid=(n_slab, nc // tm),
        in_specs=pieces + [pl.BlockSpec((tm, sw), lambda j, i: (i, j)),
                           pl.BlockSpec((None, cw, cw), lambda j, i: (j, 0, 0)),
                           pl.BlockSpec((None, sw, cw), lambda j, i: (j, 0, 0))],
        out_specs=pl.BlockSpec((tm, cw), lambda j, i: (i, j)),
        out_shape=jax.ShapeDtypeStruct((nc, n_slab * cw), BF16),
        compiler_params=_cparams("parallel", "parallel"),
        name="s5_output",
    )(*([uu] * L), xprev, t_op, v_op)

    tok = [pl.BlockSpec((tm, LANES), lambda i, t, j=j: (i, j * L + t)) for j in range(n_slab)]
    out = pl.pallas_call(
        _s5_glu_body,
        grid=(nc // tm, L),
        in_specs=tok + [pl.BlockSpec((width, width), lambda i, t: (0, 0)),
                        pl.BlockSpec((1, width), lambda i, t: (0, 0))],
        out_specs=pl.BlockSpec((tm, width), lambda i, t: (i, t)),
        out_shape=jax.ShapeDtypeStruct((nc, L * width), BF16),
        compiler_params=_cparams("parallel", "parallel"),
        name="s5_glu",
    )(*([ys] * n_slab), w_glu.astype(BF16), b_glu.reshape(1, width).astype(F32))
    return out.reshape(bsz * seq, width)


def _layer(x, mem_n, p, bsz, seq, mem_len):
    d = x.shape[1]
    aw = p["w_attn_up"].shape[0]
    heads = aw // ATTN_HEAD_DIM
    sw = p["w_ssm_up"].shape[0]
    off_f = 3 * aw
    off_u = off_f + heads
    off_g = off_u + sw

    w_in_t = p["w_in"].T
    tile = TILE_GATE_COLS
    w_qkv = _transpose_cast(w_in_t, 0, off_f, 2 * tile, lambda j: j, tk=d,
                            scale=LOG2E * ATTN_HEAD_DIM ** -0.5, n_scaled=aw // (2 * tile))
    w_f = jnp.pad(w_in_t[off_f:off_u].T, ((0, 0), (0, LANES - heads))).astype(BF16)
    b_f = jnp.pad(p["b_f"].astype(F32), (0, LANES - heads)).reshape(1, LANES)
    h, log_f = _rmsnorm(x, p["g_mix"], BF16, tm=TILE_ROWS, forget=(w_f, b_f))
    w_u = _transpose_cast(w_in_t, off_u, sw, tile, lambda j: j, tk=d)
    tn_mix = tile
    n_mix = d // tn_mix
    w_g = _transpose_cast(w_in_t, off_g, N_BRANCH * d, tn_mix,
                          lambda j: (j % n_mix) * N_BRANCH + j // n_mix, tk=d)

    qkv = _fused_matmul([(h, w_qkv, 0)], _ep_plain, BF16, n=3 * aw, tm=TILE_MM, tn=TILE_MM,
                        name="qkv_proj")
    fox, fox_rows, fox_position = _fox_attention(qkv, _forget_bias(log_f, bsz, seq, heads),
                                                 bsz, seq, heads, tq=TILE_MM)

    u, w_out = _chunked_proj(h, w_u, p["w_out"], tm=TILE_MM)
    ops = _s5_operators(p["A_re"], p["A_im"], p["log_dt"], p["B_re"], p["B_im"],
                        p["C_re"], p["C_im"], p["D_skip"])
    y = _s5_branch(u, ops, p["w_glu"], p["b_glu"], bsz, seq)

    b_gate = (p["b_gate"].astype(F32).reshape(N_BRANCH, n_mix, tn_mix).transpose(1, 0, 2)
              .reshape(1, N_BRANCH * d))
    merged, w_ff1 = _fused_matmul(
        [(h, w_g, 0, N_BRANCH), (fox, p["w_attn_up"].astype(BF16), 0, 1, fox_position),
         (y, p["w_ssm_up"].astype(BF16), 0)],
        _ep_gated_merge, BF16, n=d, tm=fox_rows, tn=tn_mix,
        aux=[(b_gate, 0, N_BRANCH)], side_cast=p["w_ff1"], name="gated_merge")
    x = _fused_matmul([(merged, w_out, 0)], _ep_add_res, F32, n=d,
                      tm=TILE_MM, tn=TILE_MM, res=x, name="mixer_out_proj")

    xw = p["wq_x"].shape[1]
    wq = (p["wq_x"] * (xw // XATTN_HEADS) ** -0.5).astype(BF16)
    w_kv = jnp.concatenate([p["wk_x"], p["wv_x"]], axis=1).astype(BF16)
    kv = _fused_matmul([(mem_n, w_kv, 0)], _ep_plain, BF16, n=2 * xw, tm=TILE_ROWS, tn=TILE_MM,
                       name="xattn_kv_proj")
    x, hm = _xattn_block(x, p["g_xattn"], wq, kv, p["wo_x"].astype(BF16), p["g_mlp"],
                         seq, mem_len, tm=TILE_XATTN_ROWS)

    dff = p["w_ff1"].shape[1]
    hid, w_ff2 = _fused_matmul([(hm, w_ff1, 0)], _ep_relu2, BF16, n=dff, tm=TILE_MM, tn=TILE_MM,
                               side_cast=p["w_ff2"], name="mlp_up")
    x = _matmul_kpieces(hid, w_ff2, x, n_pieces=MLP_DOWN_K_PIECES, tm=TILE_ROWS,
                        tn=MLP_DOWN_COLS, name="mlp_down")
    return x


_LAYER_PARAMS = ("g_mix", "w_in", "b_f", "b_gate", "A_re", "A_im", "log_dt", "B_re", "B_im",
                 "C_re", "C_im", "D_skip", "w_glu", "b_glu", "w_attn_up", "w_ssm_up", "w_out",
                 "g_xattn", "g_mem", "wq_x", "wk_x", "wv_x", "wo_x", "g_mlp", "w_ff1", "w_ff2")


def kernel(x, mem, g_mix, w_in, b_f, b_gate, A_re, A_im, log_dt, B_re, B_im, C_re, C_im, D_skip, w_glu, b_glu, w_attn_up, w_ssm_up, w_out, g_xattn, g_mem, wq_x, wk_x, wv_x, wo_x, g_mlp, w_ff1, w_ff2, g_final):
    stacked = dict(zip(_LAYER_PARAMS, (g_mix, w_in, b_f, b_gate, A_re, A_im, log_dt, B_re, B_im,
                                       C_re, C_im, D_skip, w_glu, b_glu, w_attn_up, w_ssm_up,
                                       w_out, g_xattn, g_mem, wq_x, wk_x, wv_x, wo_x, g_mlp,
                                       w_ff1, w_ff2)))
    bsz, seq, d = x.shape
    mem_len = mem.shape[1]
    xt = x.reshape(bsz * seq, d)
    mem2 = mem.reshape(bsz * mem_len, d)
    for l in range(g_mix.shape[0]):
        p = {k: v[l] for k, v in stacked.items()}
        mem_n = _rmsnorm(mem2, p["g_mem"], BF16, tm=TILE_XATTN_ROWS)
        xt = _layer(xt, mem_n, p, bsz, seq, mem_len)
    out = _rmsnorm(xt, g_final, x.dtype, tm=TILE_ROWS)
    return out.reshape(bsz, seq, d)
```
